```python
import jax
import jax.numpy as jnp
from jax import lax
import numpy as np

D_MODEL = 1024
BATCH = 8
SEQ = 8192
DEPTH = 2

CTX_LEN = 256
GRID_W = 64
HEAD_DIM = 64
GLA_HEADS = 4
GLA_DK = 32
GLA_DV = 64
GLA_GATE_RANK = 16
GLA_GATE_TAU = 16.0
GLA_CHUNK = 64
GLB_HEADS = 8
GLB_KV_HEADS = 2
WIN_HEADS = 4
WIN_KV_HEADS = 2
WINDOW = 128
Q_BLOCK = 128
FFN_HIDDEN = 2816
ROPE_BASE = 10000.0
N_MOD = 9
EPS = 1e-6
MIX_WIDTH = GLA_HEADS * GLA_DV + GLB_HEADS * HEAD_DIM + WIN_HEADS * HEAD_DIM
IN_SPLITS = (GLA_HEADS * GLA_DK, GLA_HEADS * GLA_DK, GLA_HEADS * GLA_DV, GLA_HEADS * GLA_DV, 2 * GLA_GATE_RANK,
             GLB_HEADS * HEAD_DIM, GLB_KV_HEADS * HEAD_DIM, GLB_KV_HEADS * HEAD_DIM,
             WIN_HEADS * HEAD_DIM, WIN_KV_HEADS * HEAD_DIM, WIN_KV_HEADS * HEAD_DIM)
IN_WIDTH = sum(IN_SPLITS)

kernel_name = 'hymba_style_hybrid_dit_block'


def rms_norm(x, g):
    xf = x.astype(jnp.float32)
    y = xf * lax.rsqrt(jnp.mean(xf * xf, axis=-1, keepdims=True) + EPS)
    return (y * g.astype(jnp.float32)).astype(x.dtype)


def modulate(h, shift, scale):
    return h * (1 + scale) + shift


def swiglu(h, w_in, w_out):
    a, b = jnp.split(h @ w_in, 2, axis=-1)
    return (jax.nn.silu(a) * b) @ w_out


def rope_tables(n_tokens, dtype):
    rows = n_tokens // GRID_W
    row = jnp.repeat(jnp.arange(rows, dtype=jnp.float32), GRID_W)
    col = (jnp.arange(rows * GRID_W) % GRID_W).astype(jnp.float32)
    n_freq = HEAD_DIM // 4
    inv = jnp.power(ROPE_BASE, -jnp.arange(n_freq, dtype=jnp.float32) / n_freq)
    ang = jnp.concatenate([row[:, None] * inv, col[:, None] * inv], axis=-1)
    return jnp.cos(ang)[:, None, :].astype(dtype), jnp.sin(ang)[:, None, :].astype(dtype)


def apply_rope(x, cos, sin):
    x1, x2 = jnp.split(x, 2, axis=-1)
    return jnp.concatenate([x1 * cos - x2 * sin, x1 * sin + x2 * cos], axis=-1)


def gla_scan(q, k, v, g, s0):
    bsz, nh, length, dk = q.shape
    dv = v.shape[-1]
    n = length // GLA_CHUNK
    q = q.reshape(bsz, nh, n, GLA_CHUNK, dk)
    k = k.reshape(bsz, nh, n, GLA_CHUNK, dk)
    v = v.reshape(bsz, nh, n, GLA_CHUNK, dv)
    b = jnp.cumsum(g.reshape(bsz, nh, n, GLA_CHUNK, dk), axis=3)
    gam = b[:, :, :, -1:, :]
    q_in = q * jnp.exp(b)
    a = jnp.einsum('bhncd,bhnsd->bhncs', q_in, k * jnp.exp(-b))
    a = jnp.where(jnp.tril(jnp.ones((GLA_CHUNK, GLA_CHUNK), dtype=bool)), a, 0.0)
    o = jnp.einsum('bhncs,bhnsv->bhncv', a, v)
    ds = jnp.einsum('bhncd,bhncv->bhndv', k * jnp.exp(gam - b), v)
    decay = jnp.exp(gam[:, :, :, 0, :])

    def step(s, inp):
        dec, d = inp
        return dec[..., None] * s + d, s

    s_fin, s_prev = lax.scan(step, s0, (jnp.moveaxis(decay, 2, 0), jnp.moveaxis(ds, 2, 0)))
    o = o + jnp.einsum('bhncd,nbhdv->bhncv', q_in, s_prev)
    return o.reshape(bsz, nh, length, dv), s_fin


def gla_bidir(q, k, v, gf, gb, sf0, sb0):
    of, sf = gla_scan(q, k, v, gf, sf0)
    fl = lambda t: jnp.flip(t, axis=2)
    ob, sb = gla_scan(fl(q), fl(k), fl(v), fl(gb), sb0)
    return of + fl(ob), sf, sb


def gla_prepare(q, k, v, gd, wg_f, bg_f, wg_b, bg_b):
    bsz, length, _ = q.shape

    def heads(t, d):
        return t.astype(jnp.float32).reshape(bsz, length, GLA_HEADS, d).transpose(0, 2, 1, 3)

    gdf, gdb = jnp.split(gd.astype(jnp.float32), 2, axis=-1)
    gf = jax.nn.log_sigmoid(gdf @ wg_f.astype(jnp.float32) + bg_f.astype(jnp.float32)) / GLA_GATE_TAU
    gb = jax.nn.log_sigmoid(gdb @ wg_b.astype(jnp.float32) + bg_b.astype(jnp.float32)) / GLA_GATE_TAU
    return (heads(q, GLA_DK) * GLA_DK ** -0.5, heads(k, GLA_DK), heads(v, GLA_DV),
            heads(gf, GLA_DK), heads(gb, GLA_DK))


def gla_output(o, r, gain):
    bsz, nh, length, dv = o.shape
    o = rms_norm(o.transpose(0, 2, 1, 3), gain)
    gate = jax.nn.silu(r.astype(jnp.float32)).reshape(bsz, length, nh, dv)
    return (o * gate).reshape(bsz, length, nh * dv).astype(r.dtype)


def qk_prep(q, k, q_gain, k_gain, n_q, n_kv, rope):
    bsz, length, _ = q.shape
    q = rms_norm(q.reshape(bsz, length, n_q, HEAD_DIM), q_gain)
    k = rms_norm(k.reshape(bsz, length, n_kv, HEAD_DIM), k_gain)
    if rope is not None:
        cos, sin = rope
        q = apply_rope(q, cos, sin)
        k = apply_rope(k, cos, sin)
    q = q * HEAD_DIM ** -0.5
    return q.reshape(bsz, length, n_kv, n_q // n_kv, HEAD_DIM), k


def dense_attn(q, k, v, sink):
    bsz, nq, n_kv, grp, _ = q.shape
    s = jnp.einsum('bqhgd,bkhd->bhgqk', q, k).astype(jnp.float32)
    if sink is not None:
        snk = jnp.broadcast_to(sink.astype(jnp.float32).reshape(1, n_kv, grp, 1, 1), s.shape[:-1] + (1,))
        s = jnp.concatenate([s, snk], axis=-1)
    p = jax.nn.softmax(s, axis=-1)
    if sink is not None:
        p = p[..., :-1]
    o = jnp.einsum('bhgqk,bkhd->bqhgd', p.astype(v.dtype), v)
    return o.reshape(bsz, nq, -1)


def global_attn_latent(q, k, v):
    bsz, length = q.shape[:2]
    nb = length // Q_BLOCK
    qb = q.reshape((bsz, nb, Q_BLOCK) + q.shape[2:]).swapaxes(0, 1)
    o = lax.map(lambda qi: dense_attn(qi, k, v, None), qb)
    return o.swapaxes(0, 1).reshape(bsz, length, -1)


def window_attn_latent(q, k, v, kc, vc, sink):
    bsz, length, n_kv, grp, hd = q.shape
    nb = length // Q_BLOCK
    qb = q.reshape(bsz, nb, Q_BLOCK, n_kv, grp, hd).swapaxes(0, 1)

    def band(t):
        tp = jnp.pad(t, ((0, 0), (Q_BLOCK, Q_BLOCK), (0, 0), (0, 0))).reshape(bsz, nb + 2, Q_BLOCK, n_kv, hd)
        return jnp.concatenate([tp[:, :-2], tp[:, 1:-1], tp[:, 2:]], axis=2).swapaxes(0, 1)

    rel = jnp.arange(3 * Q_BLOCK)[None, :] - jnp.arange(Q_BLOCK)[:, None]
    in_win = (rel >= Q_BLOCK - WINDOW) & (rel <= Q_BLOCK + WINDOW)
    kpos = (jnp.arange(nb)[:, None] - 1) * Q_BLOCK + jnp.arange(3 * Q_BLOCK)[None, :]
    mask = in_win[None] & ((kpos >= 0) & (kpos < length))[:, None, :]
    snk = jnp.broadcast_to(sink.astype(jnp.float32).reshape(1, n_kv, grp, 1, 1), (bsz, n_kv, grp, Q_BLOCK, 1))
    n_loc = 3 * Q_BLOCK
    n_ctx = kc.shape[1]

    def blk(args):
        qi, ki, vi, mi = args
        s_loc = jnp.einsum('bqhgd,bkhd->bhgqk', qi, ki).astype(jnp.float32)
        s_loc = jnp.where(mi, s_loc, -jnp.inf)
        s_ctx = jnp.einsum('bqhgd,bkhd->bhgqk', qi, kc).astype(jnp.float32)
        p = jax.nn.softmax(jnp.concatenate([s_loc, s_ctx, snk], axis=-1), axis=-1).astype(vi.dtype)
        o = (jnp.einsum('bhgqk,bkhd->bqhgd', p[..., :n_loc], vi)
             + jnp.einsum('bhgqk,bkhd->bqhgd', p[..., n_loc:n_loc + n_ctx], vc))
        return o.reshape(bsz, Q_BLOCK, -1)

    o = lax.map(blk, (qb, band(k), band(v), mask))
    return o.swapaxes(0, 1).reshape(bsz, length, -1)


def mixer(h, hc, rope, w_in, w_out, wg_f, bg_f, wg_b, bg_b, gla_gain,
          glb_qg, glb_kg, win_qg, win_kg, sink, need_ctx):
    bsz, length, _ = h.shape
    n_ctx = hc.shape[1]
    idx = np.cumsum(IN_SPLITS)[:-1]
    aq, ak, av, ar, ad, gq, gk, gv, wq, wk, wv = jnp.split(h @ w_in, idx, axis=-1)
    aqc, akc, avc, arc, adc, gqc, gkc, gvc, wqc, wkc, wvc = jnp.split(hc @ w_in, idx, axis=-1)

    s_zero = jnp.zeros((bsz, GLA_HEADS, GLA_DK, GLA_DV), jnp.float32)
    oc_a, s_f, s_b = gla_bidir(*gla_prepare(aqc, akc, avc, adc, wg_f, bg_f, wg_b, bg_b), s_zero, s_zero)
    o_a, _, _ = gla_bidir(*gla_prepare(aq, ak, av, ad, wg_f, bg_f, wg_b, bg_b), s_f, s_b)

    q_g, k_g = qk_prep(gq, gk, glb_qg, glb_kg, GLB_HEADS, GLB_KV_HEADS, rope)
    qc_g, kc_g = qk_prep(gqc, gkc, glb_qg, glb_kg, GLB_HEADS, GLB_KV_HEADS, None)
    v_g = gv.reshape(bsz, length, GLB_KV_HEADS, HEAD_DIM)
    vc_g = gvc.reshape(bsz, n_ctx, GLB_KV_HEADS, HEAD_DIM)
    o_b = global_attn_latent(q_g, jnp.concatenate([k_g, kc_g], axis=1), jnp.concatenate([v_g, vc_g], axis=1))

    q_w, k_w = qk_prep(wq, wk, win_qg, win_kg, WIN_HEADS, WIN_KV_HEADS, rope)
    qc_w, kc_w = qk_prep(wqc, wkc, win_qg, win_kg, WIN_HEADS, WIN_KV_HEADS, None)
    v_w = wv.reshape(bsz, length, WIN_KV_HEADS, HEAD_DIM)
    vc_w = wvc.reshape(bsz, n_ctx, WIN_KV_HEADS, HEAD_DIM)
    o_c = window_attn_latent(q_w, k_w, v_w, kc_w, vc_w, sink)

    out = jnp.concatenate([gla_output(o_a, ar, gla_gain), o_b, o_c], axis=-1) @ w_out
    if not need_ctx:
        return out, None
    oc = jnp.concatenate([gla_output(oc_a, arc, gla_gain),
                          dense_attn(qc_g, kc_g, vc_g, None),
                          dense_attn(qc_w, kc_w, vc_w, sink)], axis=-1) @ w_out
    return out, oc


def setup_inputs(seed: int = 0) -> dict:
    key = jax.random.key(seed)
    ks = jax.random.split(key, 26)
    f32 = jnp.float32

    def nrm(k, shape, scale):
        return jax.random.normal(k, shape, f32) * scale

    def gain(k, shape):
        return 1.0 + 0.02 * jax.random.normal(k, shape, f32)

    return {
        'x': nrm(ks[0], (BATCH, SEQ, D_MODEL), 1.0),
        'c': nrm(ks[1], (BATCH, D_MODEL), 1.0),
        'ctx': nrm(ks[2], (BATCH, CTX_LEN, D_MODEL), 1.0),
        'c_ctx': nrm(ks[3], (D_MODEL,), 1.0),
        'mod_w': nrm(ks[4], (DEPTH, D_MODEL, N_MOD * D_MODEL), D_MODEL ** -0.5),
        'mod_b': nrm(ks[5], (DEPTH, N_MOD * D_MODEL), 0.02),
        'norm_ffn1': gain(ks[6], (DEPTH, D_MODEL)),
        'ffn1_w_in': nrm(ks[7], (DEPTH, D_MODEL, 2 * FFN_HIDDEN), D_MODEL ** -0.5),
        'ffn1_w_out': nrm(ks[8], (DEPTH, FFN_HIDDEN, D_MODEL), FFN_HIDDEN ** -0.5),
        'norm_mix': gain(ks[9], (DEPTH, D_MODEL)),
        'mix_w_in': nrm(ks[10], (DEPTH, D_MODEL, IN_WIDTH), D_MODEL ** -0.5),
        'mix_w_out': nrm(ks[11], (DEPTH, MIX_WIDTH, D_MODEL), MIX_WIDTH ** -0.5),
        'gla_wg_f': nrm(ks[12], (DEPTH, GLA_GATE_RANK, GLA_HEADS * GLA_DK), GLA_GATE_RANK ** -0.5),
        'gla_bg_f': nrm(ks[13], (DEPTH, GLA_HEADS * GLA_DK), 0.02),
        'gla_wg_b': nrm(ks[14], (DEPTH, GLA_GATE_RANK, GLA_HEADS * GLA_DK), GLA_GATE_RANK ** -0.5),
        'gla_bg_b': nrm(ks[15], (DEPTH, GLA_HEADS * GLA_DK), 0.02),
        'gla_out_norm': gain(ks[16], (DEPTH, GLA_DV)),
        'glb_q_norm': gain(ks[17], (DEPTH, HEAD_DIM)),
        'glb_k_norm': gain(ks[18], (DEPTH, HEAD_DIM)),
        'win_q_norm': gain(ks[19], (DEPTH, HEAD_DIM)),
        'win_k_norm': gain(ks[20], (DEPTH, HEAD_DIM)),
        'win_sink': nrm(ks[21], (DEPTH, WIN_HEADS), 0.5),
        'norm_ffn2': gain(ks[22], (DEPTH, D_MODEL)),
        'ffn2_w_in': nrm(ks[23], (DEPTH, D_MODEL, 2 * FFN_HIDDEN), D_MODEL ** -0.5),
        'ffn2_w_out': nrm(ks[24], (DEPTH, FFN_HIDDEN, D_MODEL), FFN_HIDDEN ** -0.5),
    }


def reference(x, c, ctx, c_ctx, mod_w, mod_b, norm_ffn1, ffn1_w_in, ffn1_w_out, norm_mix, mix_w_in, mix_w_out,
              gla_wg_f, gla_bg_f, gla_wg_b, gla_bg_b, gla_out_norm, glb_q_norm, glb_k_norm,
              win_q_norm, win_k_norm, win_sink, norm_ffn2, ffn2_w_in, ffn2_w_out):
    length = x.shape[1]
    rope = rope_tables(length, x.dtype)
    xc = ctx
    sc = jax.nn.silu(c)
    scc = jax.nn.silu(c_ctx)
    for l in range(DEPTH):
        need_ctx = l < DEPTH - 1
        ml = jnp.split((sc @ mod_w[l] + mod_b[l])[:, None, :], N_MOD, axis=-1)
        mc = jnp.split((scc @ mod_w[l] + mod_b[l])[None, None, :], N_MOD, axis=-1)
        x = x + 0.5 * ml[2] * swiglu(modulate(rms_norm(x, norm_ffn1[l]), ml[0], ml[1]), ffn1_w_in[l], ffn1_w_out[l])
        xc = xc + 0.5 * mc[2] * swiglu(modulate(rms_norm(xc, norm_ffn1[l]), mc[0], mc[1]), ffn1_w_in[l], ffn1_w_out[l])
        h = modulate(rms_norm(x, norm_mix[l]), ml[3], ml[4])
        hc = modulate(rms_norm(xc, norm_mix[l]), mc[3], mc[4])
        o, oc = mixer(h, hc, rope, mix_w_in[l], mix_w_out[l], gla_wg_f[l], gla_bg_f[l], gla_wg_b[l], gla_bg_b[l],
                      gla_out_norm[l], glb_q_norm[l], glb_k_norm[l], win_q_norm[l], win_k_norm[l], win_sink[l],
                      need_ctx)
        x = x + ml[5] * o
        x = x + 0.5 * ml[8] * swiglu(modulate(rms_norm(x, norm_ffn2[l]), ml[6], ml[7]), ffn2_w_in[l], ffn2_w_out[l])
        if need_ctx:
            xc = xc + mc[5] * oc
            xc = xc + 0.5 * mc[8] * swiglu(modulate(rms_norm(xc, norm_ffn2[l]), mc[6], mc[7]),
                                           ffn2_w_in[l], ffn2_w_out[l])
    return x
```

```python
import functools

import numpy as np
import jax
import jax.numpy as jnp
from jax import lax
from jax.experimental import pallas as pl
from jax.experimental.pallas import tpu as pltpu

GRID_W = 64
HEAD_DIM = 64
GLA_HEADS = 4
GLA_DK = 32
GLA_DV = 64
GLA_GATE_RANK = 16
GLA_GATE_TAU = 16.0
GLA_CHUNK = 64
GLB_HEADS = 8
GLB_KV_HEADS = 2
WIN_HEADS = 4
WIN_KV_HEADS = 2
WINDOW = 128
ROPE_BASE = 10000.0
N_MOD = 9
EPS = 1e-6

LANES = 128
VMEM_LIMIT_BYTES = 56 * 1024 * 1024

GLA_QK = GLA_HEADS * GLA_DK
GLA_V = GLA_HEADS * GLA_DV
GATE_PAD = LANES
MASK_VALUE = -1e30

BF16 = jnp.bfloat16
F32 = jnp.float32


def _cparams(*sem):
    return pltpu.CompilerParams(dimension_semantics=sem, vmem_limit_bytes=VMEM_LIMIT_BYTES)


def _dot(a, b):
    return jnp.dot(a, b, preferred_element_type=F32)


def _dot_nt(a, b):
    return lax.dot_general(a, b, (((1,), (1,)), ((), ())), preferred_element_type=F32)


def _dot_tn(a, b):
    return lax.dot_general(a, b, (((0,), (0,)), ((), ())), preferred_element_type=F32)


def _lane_iota(shape):
    return lax.broadcasted_iota(jnp.int32, shape, len(shape) - 1)


def _modvec_kernel(c_ref, w_ref, b_ref, o_ref):
    c = c_ref[...]
    s = (c * jax.nn.sigmoid(c)).astype(BF16)
    o_ref[0] = _dot(s, w_ref[0].astype(BF16)) + b_ref[0]


def _modvec(c_rows, mod_w, mod_b):
    depth, d, n = mod_w.shape
    rows = c_rows.shape[0]
    tn = n // N_MOD
    return pl.pallas_call(
        _modvec_kernel,
        out_shape=jax.ShapeDtypeStruct((depth, rows, n), F32),
        grid=(depth, n // tn),
        in_specs=[
            pl.BlockSpec((rows, d), lambda l, j: (0, 0)),
            pl.BlockSpec((1, d, tn), lambda l, j: (l, 0, j)),
            pl.BlockSpec((1, 1, tn), lambda l, j: (l, 0, j)),
        ],
        out_specs=pl.BlockSpec((1, rows, tn), lambda l, j: (l, 0, j)),
        compiler_params=_cparams("arbitrary", "arbitrary"),
        name="modvec",
    )(c_rows, mod_w, mod_b.reshape(depth, 1, n))


def _norm_modulate(x, gain, mod_ref, k_shift):
    shift = mod_ref[0, k_shift:k_shift + 1, :]
    scale = mod_ref[0, k_shift + 1:k_shift + 2, :]
    y = x * lax.rsqrt(jnp.mean(x * x, axis=-1, keepdims=True) + EPS)
    return (y * gain) * (1.0 + scale) + shift


def _mod_spec(mods, tiles_per_batch):
    nb, nm, d = mods.shape
    if nb == 1:
        return pl.BlockSpec((1, nm, d), lambda i: (0, 0, 0))
    return pl.BlockSpec((1, nm, d), lambda i: (i // tiles_per_batch, 0, 0))


def _const_spec(shape):
    nd = len(shape)
    return pl.BlockSpec(shape, lambda *_: (0,) * nd, pipeline_mode=pl.Buffered(1))


def _row_tile(n_rows_per_batch, want):
    t = min(want, n_rows_per_batch)
    assert n_rows_per_batch % t == 0
    return t


def _ffn_chunks(f):
    step = 1536 if f > 1536 else f
    return tuple((lo, min(lo + step, f)) for lo in range(0, f, step))


def _ffn_kernel(x_ref, mod_ref, g_ref, win_ref, wout_ref, o_ref, *, k0, f):
    x = x_ref[...]
    hb = _norm_modulate(x, g_ref[...], mod_ref, k0).astype(BF16)
    gate = mod_ref[0, k0 + 2:k0 + 3, :]
    acc = None
    for lo, hi in _ffn_chunks(f):
        a = _dot(hb, win_ref[:, lo:hi])
        b = _dot(hb, win_ref[:, f + lo:f + hi])
        u = ((a * jax.nn.sigmoid(a)) * b).astype(BF16)
        part = _dot(u, wout_ref[lo:hi, :])
        acc = part if acc is None else acc + part
    o_ref[...] = x + (0.5 * gate) * acc


def _ffn(x, mods, k0, gain, w_in, w_out, rows_per_batch, tm_want=512):
    t, d = x.shape
    f = w_out.shape[0]
    tm = _row_tile(rows_per_batch, tm_want)
    return pl.pallas_call(
        functools.partial(_ffn_kernel, k0=k0, f=f),
        out_shape=jax.ShapeDtypeStruct((t, d), F32),
        grid=(t // tm,),
        in_specs=[
            pl.BlockSpec((tm, d), lambda i: (i, 0)),
            _mod_spec(mods, rows_per_batch // tm),
            _const_spec((1, d)),
            _const_spec((d, 2 * f)),
            _const_spec((f, d)),
        ],
        out_specs=pl.BlockSpec((tm, d), lambda i: (i, 0)),
        compiler_params=_cparams("parallel"),
        name="ffn",
    )(x, mods, gain.reshape(1, d), w_in, w_out)


_C_AQ = 0
_C_AK = _C_AQ + GLA_QK
_C_AV = _C_AK + GLA_QK
_C_AR = _C_AV + GLA_V
_C_AD = _C_AR + GLA_V
_C_GQ = _C_AD + GATE_PAD
_C_GK = _C_GQ + GLB_HEADS * HEAD_DIM
_C_GV = _C_GK + GLB_KV_HEADS * HEAD_DIM
_C_WQ = _C_GV + GLB_KV_HEADS * HEAD_DIM
_C_WK = _C_WQ + WIN_HEADS * HEAD_DIM
_C_WV = _C_WK + WIN_KV_HEADS * HEAD_DIM
_C_END = _C_WV + WIN_KV_HEADS * HEAD_DIM


def _head_norm_rope(x, gain, cos, sin_signed, bd_ref, out_scale):
    ss = _dot((x * x).astype(BF16), bd_ref[...])
    xn = x * lax.rsqrt(ss * (1.0 / HEAD_DIM) + EPS) * gain
    first_half = (_lane_iota(xn.shape) % HEAD_DIM) < (HEAD_DIM // 2)
    partner = jnp.where(first_half, pltpu.roll(xn, LANES - HEAD_DIM // 2, 1), pltpu.roll(xn, HEAD_DIM // 2, 1))
    out = xn * cos + partner * sin_signed
    if out_scale != 1.0:
        out = out * out_scale
    return out


def _dup_heads(x):
    swapped = pltpu.roll(x, HEAD_DIM, 1)
    low = _lane_iota(x.shape) < HEAD_DIM
    return jnp.where(low, x, swapped), jnp.where(low, swapped, x)


def _inproj_kernel(x_ref, mod_ref, g_ref, w_ref, wg_ref, bg_ref, qkg_ref, cos_ref, sin_ref, bd_ref,
                   aq_ref, ak_ref, av_ref, ar_ref, ag_ref, gq_ref, gk_ref, gv_ref, wq_ref, wk_ref, wv_ref):
    hb = _norm_modulate(x_ref[...], g_ref[...], mod_ref, 3).astype(BF16)
    cos = cos_ref[...]
    sin = sin_ref[...]

    def proj(lo, width):
        return _dot(hb, w_ref[:, lo:lo + width])

    aq_ref[...] = (proj(_C_AQ, GLA_QK) * (GLA_DK ** -0.5)).astype(BF16)
    ak_ref[...] = proj(_C_AK, GLA_QK).astype(BF16)
    av_ref[...] = proj(_C_AV, GLA_V).astype(BF16)
    ar_ref[...] = proj(_C_AR, GLA_V).astype(BF16)
    z = _dot(proj(_C_AD, GATE_PAD).astype(BF16), wg_ref[...]) + bg_ref[...]
    log_sig = jnp.minimum(z, 0.0) - jnp.log1p(jnp.exp(-jnp.abs(z)))
    ag_ref[...] = log_sig * (1.0 / GLA_GATE_TAU)

    q_scale = HEAD_DIM ** -0.5
    for j in range(GLB_HEADS * HEAD_DIM // LANES):
        q = _head_norm_rope(proj(_C_GQ + j * LANES, LANES), qkg_ref[0:1, :], cos, sin, bd_ref, q_scale)
        gq_ref[:, j * LANES:(j + 1) * LANES] = q.astype(BF16)
    k = _head_norm_rope(proj(_C_GK, LANES), qkg_ref[1:2, :], cos, sin, bd_ref, 1.0)
    k0, k1 = _dup_heads(k)
    gk_ref[:, 0:LANES] = k0.astype(BF16)
    gk_ref[:, LANES:2 * LANES] = k1.astype(BF16)
    v0, v1 = _dup_heads(proj(_C_GV, LANES))
    gv_ref[:, 0:LANES] = v0.astype(BF16)
    gv_ref[:, LANES:2 * LANES] = v1.astype(BF16)

    for j in range(WIN_HEADS * HEAD_DIM // LANES):
        q = _head_norm_rope(proj(_C_WQ + j * LANES, LANES), qkg_ref[2:3, :], cos, sin, bd_ref, q_scale)
        wq_ref[:, j * LANES:(j + 1) * LANES] = q.astype(BF16)
    k = _head_norm_rope(proj(_C_WK, LANES), qkg_ref[3:4, :], cos, sin, bd_ref, 1.0)
    k0, k1 = _dup_heads(k)
    wk_ref[:, 0:LANES] = k0.astype(BF16)
    wk_ref[:, LANES:2 * LANES] = k1.astype(BF16)
    v0, v1 = _dup_heads(proj(_C_WV, LANES))
    wv_ref[:, 0:LANES] = v0.astype(BF16)
    wv_ref[:, LANES:2 * LANES] = v1.astype(BF16)


def _inproj(x, mods, gain, w, wg, bg, qk_gains, cos, sin, bd, rows_per_batch, tm_want=512):
    t, d = x.shape
    tm = _row_tile(rows_per_batch, tm_want)
    tpb = rows_per_batch // tm
    widths = (GLA_QK, GLA_QK, GLA_V, GLA_V, 2 * GLA_QK, GLB_HEADS * HEAD_DIM, 2 * LANES, 2 * LANES,
              WIN_HEADS * HEAD_DIM, 2 * LANES, 2 * LANES)
    dtypes = (BF16, BF16, BF16, BF16, F32, BF16, BF16, BF16, BF16, BF16, BF16)
    return pl.pallas_call(
        _inproj_kernel,
        out_shape=tuple(jax.ShapeDtypeStruct((t, wd), dt) for wd, dt in zip(widths, dtypes)),
        grid=(t // tm,),
        in_specs=[
            pl.BlockSpec((tm, d), lambda i: (i, 0)),
            _mod_spec(mods, tpb),
            _const_spec((1, d)),
            _const_spec(w.shape),
            _const_spec(wg.shape),
            _const_spec(bg.shape),
            _const_spec(qk_gains.shape),
            pl.BlockSpec((tm, LANES), lambda i: (i % tpb, 0)),
            pl.BlockSpec((tm, LANES), lambda i: (i % tpb, 0)),
            _const_spec(bd.shape),
        ],
        out_specs=tuple(pl.BlockSpec((tm, wd), lambda i: (i, 0)) for wd in widths),
        compiler_params=_cparams("parallel"),
        name="mix_inproj",
    )(x, mods, gain.reshape(1, d), w, wg, bg, qk_gains, cos, sin, bd)


def _gla_kernel(q_ref, k_ref, v_ref, g_ref, s0_ref, o_ref, sfin_ref, st_ref, *, n_chunks):
    c_len = GLA_CHUNK
    d = pl.program_id(1)
    i = pl.program_id(2)
    sign = 1 - 2 * d

    @pl.when(i == 0)
    def _():
        st_ref[...] = s0_ref[0, 0]

    row = lax.broadcasted_iota(jnp.int32, (c_len, c_len), 0)
    col = lax.broadcasted_iota(jnp.int32, (c_len, c_len), 1)
    keep = (row - col) * sign >= 0
    cum_op = jnp.where(keep, 1.0, 0.0).astype(BF16)
    row4 = lax.broadcasted_iota(jnp.int32, (c_len, GLA_HEADS * c_len), 0)
    col4 = lax.broadcasted_iota(jnp.int32, (c_len, GLA_HEADS * c_len), 1) % c_len
    keep4 = (row4 - col4) * sign >= 0
    qk_lane_head = _lane_iota((1, GLA_QK)) // GLA_DK
    v_lane_head = _lane_iota((1, GLA_V)) // GLA_DV
    qk_head_mask = [(qk_lane_head == h).astype(F32) for h in range(GLA_HEADS)]
    v_head_mask = [(v_lane_head == h).astype(BF16) for h in range(GLA_HEADS)]
    st_row_head = lax.broadcasted_iota(jnp.int32, (GLA_V, GLA_QK), 0) // GLA_DV
    st_col_head = lax.broadcasted_iota(jnp.int32, (GLA_V, GLA_QK), 1) // GLA_DK
    st_mask = st_row_head == st_col_head

    def chunk(j, carry):
        c = j + d * (n_chunks - 1 - 2 * j)
        r0 = pl.multiple_of(c * c_len, c_len)
        q = q_ref[pl.ds(r0, c_len), :].astype(F32)
        k = k_ref[pl.ds(r0, c_len), :].astype(F32)
        v = v_ref[pl.ds(r0, c_len), :]
        g = g_ref[pl.ds(r0, c_len), :]
        g_hi = g.astype(BF16)
        r1 = g - g_hi.astype(F32)
        g_mid = r1.astype(BF16)
        g_lo = (r1 - g_mid.astype(F32)).astype(BF16)
        b = _dot(cum_op, g_hi) + _dot(cum_op, g_mid) + _dot(cum_op, g_lo)
        decay = jnp.exp(jnp.sum(g, axis=0, keepdims=True))
        q_in = (q * jnp.exp(b)).astype(BF16)
        k_out = k * jnp.exp(-b)
        k_dec = (k_out * decay).astype(BF16)
        k_stack = jnp.concatenate([(k_out * qk_head_mask[h]).astype(BF16) for h in range(GLA_HEADS)], axis=0)
        a = _dot_nt(q_in, k_stack)
        a = jnp.where(keep4, a, 0.0).astype(BF16)
        v_bd = jnp.concatenate([v * v_head_mask[h] for h in range(GLA_HEADS)], axis=0)
        st = st_ref[...]
        o = _dot(a, v_bd) + _dot_nt(q_in, st.astype(BF16))
        o_ref[0, pl.ds(r0, c_len), :] = o
        ds_t = _dot_tn(v, k_dec)
        st_ref[...] = st * decay + jnp.where(st_mask, ds_t, 0.0)
        return carry

    lax.fori_loop(0, n_chunks, chunk, 0)

    @pl.when(i == pl.num_programs(2) - 1)
    def _():
        sfin_ref[0, 0] = st_ref[...]


def _gla(q, k, v, g, s0, rows_per_batch, tt_want=1024):
    t = q.shape[0]
    nb = t // rows_per_batch
    tt = _row_tile(rows_per_batch, tt_want)
    nt = rows_per_batch // tt

    def rows(b, d, i):
        return b * nt + i + d * (nt - 1 - 2 * i)

    return pl.pallas_call(
        functools.partial(_gla_kernel, n_chunks=tt // GLA_CHUNK),
        out_shape=(jax.ShapeDtypeStruct((2, t, GLA_V), F32),
                   jax.ShapeDtypeStruct((nb, 2, GLA_V, GLA_QK), F32)),
        grid=(nb, 2, nt),
        in_specs=[
            pl.BlockSpec((tt, GLA_QK), lambda b, d, i: (rows(b, d, i), 0)),
            pl.BlockSpec((tt, GLA_QK), lambda b, d, i: (rows(b, d, i), 0)),
            pl.BlockSpec((tt, GLA_V), lambda b, d, i: (rows(b, d, i), 0)),
            pl.BlockSpec((tt, GLA_QK), lambda b, d, i: (rows(b, d, i), d)),
            pl.BlockSpec((1, 1, GLA_V, GLA_QK), lambda b, d, i: (b, d, 0, 0)),
        ],
        out_specs=(pl.BlockSpec((1, tt, GLA_V), lambda b, d, i: (d, rows(b, d, i), 0)),
                   pl.BlockSpec((1, 1, GLA_V, GLA_QK), lambda b, d, i: (b, d, 0, 0))),
        scratch_shapes=[pltpu.VMEM((GLA_V, GLA_QK), F32)],
        compiler_params=_cparams("parallel", "parallel", "arbitrary"),
        name="gla_scan",
    )(q, k, v, g, s0)


def _half_masks(dtype):
    low = _lane_iota((1, LANES)) < HEAD_DIM
    return [low.astype(dtype), (~low).astype(dtype)]


def _rep(x, width):
    n = width // LANES
    return x if n == 1 else pltpu.repeat(x, n, axis=1)


def _glb_kernel(q_ref, k_ref, v_ref, kc_ref, vc_ref, o_ref, m_ref, l_ref, acc_ref, *, tk, n_pairs):
    tq = q_ref.shape[0]
    n_lat = k_ref.shape[0] // tk
    hm = _half_masks(BF16)
    low = _lane_iota((1, LANES)) < HEAD_DIM
    qh = []
    for p in range(n_pairs):
        qp = q_ref[:, p * LANES:(p + 1) * LANES]
        qh.append([qp * hm[0], qp * hm[1]])

    m_ref[...] = jnp.full(m_ref.shape, MASK_VALUE, F32)
    l_ref[...] = jnp.zeros(l_ref.shape, F32)
    acc_ref[...] = jnp.zeros(acc_ref.shape, F32)

    def step(k, v):
        width = k.shape[0]
        v_half = [v * hm[0], v * hm[1]]
        for p in range(n_pairs):
            alphas = []
            pv = None
            for half in range(2):
                h = 2 * p + half
                s = _dot_nt(qh[p][half], k)
                m_prev = m_ref[h]
                m_new = jnp.maximum(m_prev, jnp.max(s, axis=1, keepdims=True))
                alpha = jnp.exp(m_prev - m_new)
                e = jnp.exp(s - _rep(m_new, width))
                l_ref[h] = alpha * l_ref[h] + jnp.sum(e, axis=1, keepdims=True)
                m_ref[h] = m_new
                part = _dot(e.astype(BF16), v_half[half])
                pv = part if pv is None else pv + part
                alphas.append(alpha)
            acc_ref[p] = acc_ref[p] * jnp.where(low, alphas[0], alphas[1]) + pv

    def body(j, carry):
        r0 = pl.multiple_of(j * tk, tk)
        step(k_ref[pl.ds(r0, tk), :], v_ref[pl.ds(r0, tk), :])
        return carry

    lax.fori_loop(0, n_lat, body, 0)
    step(kc_ref[...], vc_ref[...])

    for p in range(n_pairs):
        denom = jnp.where(low, l_ref[2 * p], l_ref[2 * p + 1])
        o_ref[:, p * LANES:(p + 1) * LANES] = (acc_ref[p] / denom).astype(o_ref.dtype)


def _glb_attn(q, k, v, kc, vc, seq, ctx_len, tq_want=512, tk_want=256):
    t = q.shape[0]
    nb = t // seq
    groups = GLB_KV_HEADS
    gw = q.shape[1] // groups
    n_pairs = gw // LANES
    tq = _row_tile(seq, tq_want)
    tk = _row_tile(seq, tk_want)
    nq = seq // tq
    return pl.pallas_call(
        functools.partial(_glb_kernel, tk=tk, n_pairs=n_pairs),
        out_shape=jax.ShapeDtypeStruct(q.shape, BF16),
        grid=(nb, groups, nq),
        in_specs=[
            pl.BlockSpec((tq, gw), lambda b, g, i: (b * nq + i, g)),
            pl.BlockSpec((seq, LANES), lambda b, g, i: (b, g)),
            pl.BlockSpec((seq, LANES), lambda b, g, i: (b, g)),
            pl.BlockSpec((ctx_len, LANES), lambda b, g, i: (b, g)),
            pl.BlockSpec((ctx_len, LANES), lambda b, g, i: (b, g)),
        ],
        out_specs=pl.BlockSpec((tq, gw), lambda b, g, i: (b * nq + i, g)),
        scratch_shapes=[pltpu.VMEM((2 * n_pairs, tq, LANES), F32),
                        pltpu.VMEM((2 * n_pairs, tq, LANES), F32),
                        pltpu.VMEM((n_pairs, tq, LANES), F32)],
        compiler_params=_cparams("parallel", "parallel", "arbitrary"),
        name="glb_attn",
    )(q, k, v, kc, vc)


def _softmax_pair(q_pair, keys, values, masks, sink_pair, hm, low):
    out = None
    inv = []
    for half in range(2):
        qh = q_pair * hm[half]
        ss = []
        for kk, mk in zip(keys, masks):
            s = _dot_nt(qh, kk)
            ss.append(s if mk is None else jnp.where(mk, s, MASK_VALUE))
        m = ss[0].max(axis=1, keepdims=True)
        for s in ss[1:]:
            m = jnp.maximum(m, s.max(axis=1, keepdims=True))
        if sink_pair is not None:
            m = jnp.maximum(m, sink_pair[half])
        den = None
        for s, vv in zip(ss, values):
            e = jnp.exp(s - m)
            sm = jnp.sum(e, axis=1, keepdims=True)
            den = sm if den is None else den + sm
            part = _dot(e.astype(BF16), vv * hm[half])
            out = part if out is None else out + part
        if sink_pair is not None:
            den = den + jnp.exp(sink_pair[half] - m)
        inv.append(1.0 / den)
    return out * jnp.where(low, inv[0], inv[1])


def _win_kernel(sink_ref, q_ref, k_ref, v_ref, kc_ref, vc_ref, o_ref, *, n_pairs, span):
    g = pl.program_id(1)
    i = pl.program_id(2)
    tq = q_ref.shape[0]
    seq = k_ref.shape[0]
    hm = _half_masks(BF16)
    low = _lane_iota((1, LANES)) < HEAD_DIM
    q0 = i * tq
    start = jnp.clip(q0 - WINDOW, 0, seq - span)
    start = pl.multiple_of(start, LANES)
    k_loc = k_ref[pl.ds(start, span), :]
    v_loc = v_ref[pl.ds(start, span), :]
    qpos = q0 + lax.broadcasted_iota(jnp.int32, (tq, span), 0)
    kpos = start + lax.broadcasted_iota(jnp.int32, (tq, span), 1)
    in_win = jnp.abs(qpos - kpos) <= WINDOW
    kc = kc_ref[...]
    vc = vc_ref[...]
    for p in range(n_pairs):
        heads = (g * n_pairs + p) * 2
        sink_pair = (sink_ref[heads], sink_ref[heads + 1])
        o = _softmax_pair(q_ref[:, p * LANES:(p + 1) * LANES], [k_loc, kc], [v_loc, vc], [in_win, None],
                          sink_pair, hm, low)
        o_ref[:, p * LANES:(p + 1) * LANES] = o.astype(o_ref.dtype)


def _win_attn(q, k, v, kc, vc, sink, seq, ctx_len, tq_want=256):
    t = q.shape[0]
    nb = t // seq
    groups = WIN_KV_HEADS
    gw = q.shape[1] // groups
    n_pairs = gw // LANES
    tq = _row_tile(seq, tq_want)
    span = tq + 2 * WINDOW
    assert seq >= span
    nq = seq // tq
    return pl.pallas_call(
        functools.partial(_win_kernel, n_pairs=n_pairs, span=span),
        out_shape=jax.ShapeDtypeStruct(q.shape, BF16),
        grid=(nb, groups, nq),
        in_specs=[
            pl.BlockSpec(memory_space=pltpu.SMEM),
            pl.BlockSpec((tq, gw), lambda b, g, i: (b * nq + i, g)),
            pl.BlockSpec((seq, LANES), lambda b, g, i: (b, g)),
            pl.BlockSpec((seq, LANES), lambda b, g, i: (b, g)),
            pl.BlockSpec((ctx_len, LANES), lambda b, g, i: (b, g)),
            pl.BlockSpec((ctx_len, LANES), lambda b, g, i: (b, g)),
        ],
        out_specs=pl.BlockSpec((tq, gw), lambda b, g, i: (b * nq + i, g)),
        compiler_params=_cparams("parallel", "parallel", "arbitrary"),
        name="win_attn",
    )(sink, q, k, v, kc, vc)


def _ctx_attn_kernel(sink_ref, q_ref, k_ref, v_ref, o_ref, *, n_pairs, use_sink):
    g = pl.program_id(1)
    hm = _half_masks(BF16)
    low = _lane_iota((1, LANES)) < HEAD_DIM
    k = k_ref[...]
    v = v_ref[...]
    for p in range(n_pairs):
        heads = (g * n_pairs + p) * 2
        sink_pair = (sink_ref[heads], sink_ref[heads + 1]) if use_sink else None
        o = _softmax_pair(q_ref[:, p * LANES:(p + 1) * LANES], [k], [v], [None], sink_pair, hm, low)
        o_ref[:, p * LANES:(p + 1) * LANES] = o.astype(o_ref.dtype)


def _ctx_attn(q, k, v, sink, groups, ctx_len, use_sink):
    t = q.shape[0]
    nb = t // ctx_len
    gw = q.shape[1] // groups
    n_pairs = gw // LANES
    return pl.pallas_call(
        functools.partial(_ctx_attn_kernel, n_pairs=n_pairs, use_sink=use_sink),
        out_shape=jax.ShapeDtypeStruct(q.shape, BF16),
        grid=(nb, groups),
        in_specs=[
            pl.BlockSpec(memory_space=pltpu.SMEM),
            pl.BlockSpec((ctx_len, gw), lambda b, g: (b, g)),
            pl.BlockSpec((ctx_len, LANES), lambda b, g: (b, g)),
            pl.BlockSpec((ctx_len, LANES), lambda b, g: (b, g)),
        ],
        out_specs=pl.BlockSpec((ctx_len, gw), lambda b, g: (b, g)),
        compiler_params=_cparams("parallel", "parallel"),
        name="ctx_attn",
    )(sink, q, k, v)


def _outproj_kernel(x_ref, mod_ref, of_ref, ob_ref, r_ref, og_ref, ow_ref, gain_ref, bd_ref, w_ref, o_ref):
    o = of_ref[0] + ob_ref[0]
    r = r_ref[...].astype(F32)
    gate = r * jax.nn.sigmoid(r)
    pieces = []
    for j in range(GLA_V // LANES):
        oj = o[:, j * LANES:(j + 1) * LANES]
        ss = _dot((oj * oj).astype(BF16), bd_ref[...])
        on = oj * lax.rsqrt(ss * (1.0 / GLA_DV) + EPS) * gain_ref[...]
        pieces.append((on * gate[:, j * LANES:(j + 1) * LANES]).astype(BF16))
    cat = jnp.concatenate(pieces + [og_ref[...], ow_ref[...]], axis=-1)
    y = _dot(cat, w_ref[...])
    o_ref[...] = x_ref[...] + mod_ref[0, 5:6, :] * y


def _outproj(x, mods, o_gla, r, o_glb, o_win, gla_gain, bd, w_out, rows_per_batch, tm_want=512):
    t, d = x.shape
    tm = _row_tile(rows_per_batch, tm_want)
    return pl.pallas_call(
        _outproj_kernel,
        out_shape=jax.ShapeDtypeStruct((t, d), F32),
        grid=(t // tm,),
        in_specs=[
            pl.BlockSpec((tm, d), lambda i: (i, 0)),
            _mod_spec(mods, rows_per_batch // tm),
            pl.BlockSpec((1, tm, GLA_V), lambda i: (0, i, 0)),
            pl.BlockSpec((1, tm, GLA_V), lambda i: (1, i, 0)),
            pl.BlockSpec((tm, GLA_V), lambda i: (i, 0)),
            pl.BlockSpec((tm, o_glb.shape[1]), lambda i: (i, 0)),
            pl.BlockSpec((tm, o_win.shape[1]), lambda i: (i, 0)),
            _const_spec(gla_gain.shape),
            _const_spec(bd.shape),
            _const_spec(w_out.shape),
        ],
        out_specs=pl.BlockSpec((tm, d), lambda i: (i, 0)),
        compiler_params=_cparams("parallel"),
        name="mix_outproj",
    )(x, mods, o_gla, o_gla, r, o_glb, o_win, gla_gain, bd, w_out)


def _rope_tables(seq):
    rows = seq // GRID_W
    row = jnp.repeat(jnp.arange(rows, dtype=F32), GRID_W)
    col = (jnp.arange(rows * GRID_W) % GRID_W).astype(F32)
    n_freq = HEAD_DIM // 4
    inv = jnp.power(ROPE_BASE, -jnp.arange(n_freq, dtype=F32) / n_freq)
    ang = jnp.concatenate([row[:, None] * inv, col[:, None] * inv], axis=-1)
    cos, sin = jnp.cos(ang), jnp.sin(ang)
    cos_t = jnp.concatenate([cos, cos] * (LANES // HEAD_DIM), axis=-1)
    sin_t = jnp.concatenate([-sin, sin] * (LANES // HEAD_DIM), axis=-1)
    return cos_t, sin_t


def _block_diag_ones(n, block):
    idx = np.arange(n) // block
    return jnp.asarray(idx[:, None] == idx[None, :], dtype=BF16)


def kernel(x, c, ctx, c_ctx, mod_w, mod_b, norm_ffn1, ffn1_w_in, ffn1_w_out, norm_mix, mix_w_in, mix_w_out,
           gla_wg_f, gla_bg_f, gla_wg_b, gla_bg_b, gla_out_norm, glb_q_norm, glb_k_norm,
           win_q_norm, win_k_norm, win_sink, norm_ffn2, ffn2_w_in, ffn2_w_out):
    bsz, seq, d = x.shape
    ctx_len = ctx.shape[1]
    depth = mod_w.shape[0]
    split_ad = GLA_QK * 2 + GLA_V * 2 + 2 * GLA_GATE_RANK

    n_rows = -(-(bsz + 1) // 8) * 8
    c_rows = jnp.concatenate([c, c_ctx[None, :], jnp.zeros((n_rows - bsz - 1, d), F32)], axis=0)
    mods = _modvec(c_rows, mod_w, mod_b)

    cos_l, sin_l = _rope_tables(seq)
    cos_c = jnp.ones((ctx_len, LANES), F32)
    sin_c = jnp.zeros((ctx_len, LANES), F32)
    bd = _block_diag_ones(LANES, HEAD_DIM)
    zero_state = jnp.zeros((bsz, 2, GLA_V, GLA_QK), F32)

    xl = x.reshape(bsz * seq, d)
    xc = ctx.reshape(bsz * ctx_len, d)

    for l in range(depth):
        need_ctx = l < depth - 1
        mods_l = mods[l, :bsz].reshape(bsz, N_MOD, d)
        mods_c = mods[l, bsz:bsz + 1].reshape(1, N_MOD, d)
        w1_in, w1_out = ffn1_w_in[l].astype(BF16), ffn1_w_out[l].astype(BF16)
        w2_in, w2_out = ffn2_w_in[l].astype(BF16), ffn2_w_out[l].astype(BF16)
        wm = mix_w_in[l]
        w_mix = jnp.concatenate([wm[:, :split_ad], jnp.zeros((d, GATE_PAD - 2 * GLA_GATE_RANK), F32),
                                 wm[:, split_ad:]], axis=1).astype(BF16)
        assert w_mix.shape[1] == _C_END
        wg = jnp.zeros((GATE_PAD, 2 * GLA_QK), F32)
        wg = wg.at[:GLA_GATE_RANK, :GLA_QK].set(gla_wg_f[l])
        wg = wg.at[GLA_GATE_RANK:2 * GLA_GATE_RANK, GLA_QK:].set(gla_wg_b[l]).astype(BF16)
        bg = jnp.concatenate([gla_bg_f[l], gla_bg_b[l]])[None, :]
        qk_gains = jnp.stack([jnp.tile(gn[l], LANES // HEAD_DIM)
                              for gn in (glb_q_norm, glb_k_norm, win_q_norm, win_k_norm)])
        gla_gain = jnp.tile(gla_out_norm[l], LANES // GLA_DV)[None, :]
        w_out = mix_w_out[l].astype(BF16)
        sink = win_sink[l]

        xl = _ffn(xl, mods_l, 0, norm_ffn1[l], w1_in, w1_out, seq)
        xc = _ffn(xc, mods_c, 0, norm_ffn1[l], w1_in, w1_out, ctx_len)

        pc = _inproj(xc, mods_c, norm_mix[l], w_mix, wg, bg, qk_gains, cos_c, sin_c, bd, ctx_len)
        pq = _inproj(xl, mods_l, norm_mix[l], w_mix, wg, bg, qk_gains, cos_l, sin_l, bd, seq)
        aq, ak, av, ar, ag, gq, gk, gv, wq, wk, wv = pq
        aqc, akc, avc, arc, agc, gqc, gkc, gvc, wqc, wkc, wvc = pc

        oc_gla, states = _gla(aqc, akc, avc, agc, zero_state, ctx_len)
        o_gla, _ = _gla(aq, ak, av, ag, states, seq)
        o_glb = _glb_attn(gq, gk, gv, gkc, gvc, seq, ctx_len)
        o_win = _win_attn(wq, wk, wv, wkc, wvc, sink, seq, ctx_len)
        xl = _outproj(xl, mods_l, o_gla, ar, o_glb, o_win, gla_gain, bd, w_out, seq)

        xl = _ffn(xl, mods_l, 6, norm_ffn2[l], w2_in, w2_out, seq)

        if need_ctx:
            oc_glb = _ctx_attn(gqc, gkc, gvc, sink, GLB_KV_HEADS, ctx_len, use_sink=False)
            oc_win = _ctx_attn(wqc, wkc, wvc, sink, WIN_KV_HEADS, ctx_len, use_sink=True)
            xc = _outproj(xc, mods_c, oc_gla, arc, oc_glb, oc_win, gla_gain, bd, w_out, ctx_len)
            xc = _ffn(xc, mods_c, 6, norm_ffn2[l], w2_in, w2_out, ctx_len)

    return xl.reshape(bsz, seq, d)
```

```python
import functools

import numpy as np
import jax
import jax.numpy as jnp
from jax import lax
from jax.experimental import pallas as pl
from jax.experimental.pallas import tpu as pltpu

GRID_W = 64
HEAD_DIM = 64
GLA_HEADS = 4
GLA_DK = 32
GLA_DV = 64
GLA_GATE_RANK = 16
GLA_GATE_TAU = 16.0
GLA_CHUNK = 64
GLB_HEADS = 8
GLB_KV_HEADS = 2
WIN_HEADS = 4
WIN_KV_HEADS = 2
WINDOW = 128
ROPE_BASE = 10000.0
N_MOD = 9
EPS = 1e-6

LANES = 128
VMEM_LIMIT_BYTES = 56 * 1024 * 1024

GLA_QK = GLA_HEADS * GLA_DK
GLA_V = GLA_HEADS * GLA_DV
GATE_PAD = LANES
MASK_VALUE = -1e30
LOG2_E = 1.4426950408889634
KEY_CHUNK = 256

BF16 = jnp.bfloat16
F32 = jnp.float32


def _cparams(*sem):
    return pltpu.CompilerParams(dimension_semantics=sem, vmem_limit_bytes=VMEM_LIMIT_BYTES)


def _dot(a, b):
    return jnp.dot(a, b, preferred_element_type=F32)


def _dot_nt(a, b):
    return lax.dot_general(a, b, (((1,), (1,)), ((), ())), preferred_element_type=F32)


def _dot_tn(a, b):
    return lax.dot_general(a, b, (((0,), (0,)), ((), ())), preferred_element_type=F32)


def _lane_iota(shape):
    return lax.broadcasted_iota(jnp.int32, shape, len(shape) - 1)


def _modvec_kernel(c_ref, w_ref, b_ref, o_ref):
    c = c_ref[...]
    s = (c * jax.nn.sigmoid(c)).astype(BF16)
    o_ref[0] = _dot(s, w_ref[0].astype(BF16)) + b_ref[0]


def _modvec(c_rows, mod_w, mod_b):
    depth, d, n = mod_w.shape
    rows = c_rows.shape[0]
    tn = n // N_MOD
    return pl.pallas_call(
        _modvec_kernel,
        out_shape=jax.ShapeDtypeStruct((depth, rows, n), F32),
        grid=(depth, n // tn),
        in_specs=[
            pl.BlockSpec((rows, d), lambda l, j: (0, 0)),
            pl.BlockSpec((1, d, tn), lambda l, j: (l, 0, j)),
            pl.BlockSpec((1, 1, tn), lambda l, j: (l, 0, j)),
        ],
        out_specs=pl.BlockSpec((1, rows, tn), lambda l, j: (l, 0, j)),
        compiler_params=_cparams("arbitrary", "arbitrary"),
        name="modvec",
    )(c_rows, mod_w, mod_b.reshape(depth, 1, n))


def _norm_modulate(x, gain, mod_ref, k_shift):
    shift = mod_ref[0, k_shift:k_shift + 1, :]
    scale = mod_ref[0, k_shift + 1:k_shift + 2, :]
    y = x * lax.rsqrt(jnp.mean(x * x, axis=-1, keepdims=True) + EPS)
    return (y * gain) * (1.0 + scale) + shift


def _mod_spec(mods, tiles_per_batch):
    nb, nm, d = mods.shape
    if nb == 1:
        return pl.BlockSpec((1, nm, d), lambda i: (0, 0, 0))
    return pl.BlockSpec((1, nm, d), lambda i: (i // tiles_per_batch, 0, 0))


def _const_spec(shape):
    nd = len(shape)
    return pl.BlockSpec(shape, lambda *_: (0,) * nd, pipeline_mode=pl.Buffered(1))


def _row_tile(n_rows_per_batch, want):
    t = min(want, n_rows_per_batch)
    assert n_rows_per_batch % t == 0
    return t


def _ffn_chunks(f):
    step = 1536 if f > 1536 else f
    return tuple((lo, min(lo + step, f)) for lo in range(0, f, step))


def _ffn_kernel(x_ref, mod_ref, g_ref, win_ref, wout_ref, o_ref, *, k0, f):
    x = x_ref[...]
    hb = _norm_modulate(x, g_ref[...], mod_ref, k0).astype(BF16)
    gate = mod_ref[0, k0 + 2:k0 + 3, :]
    acc = None
    for lo, hi in _ffn_chunks(f):
        a = _dot(hb, win_ref[:, lo:hi])
        b = _dot(hb, win_ref[:, f + lo:f + hi])
        u = ((a * jax.nn.sigmoid(a)) * b).astype(BF16)
        part = _dot(u, wout_ref[lo:hi, :])
        acc = part if acc is None else acc + part
    o_ref[...] = x + (0.5 * gate) * acc


def _ffn(x, mods, k0, gain, w_in, w_out, rows_per_batch, tm_want=512):
    t, d = x.shape
    f = w_out.shape[0]
    tm = _row_tile(rows_per_batch, tm_want)
    return pl.pallas_call(
        functools.partial(_ffn_kernel, k0=k0, f=f),
        out_shape=jax.ShapeDtypeStruct((t, d), F32),
        grid=(t // tm,),
        in_specs=[
            pl.BlockSpec((tm, d), lambda i: (i, 0)),
            _mod_spec(mods, rows_per_batch // tm),
            _const_spec((1, d)),
            _const_spec((d, 2 * f)),
            _const_spec((f, d)),
        ],
        out_specs=pl.BlockSpec((tm, d), lambda i: (i, 0)),
        compiler_params=_cparams("parallel"),
        name="ffn",
    )(x, mods, gain.reshape(1, d), w_in, w_out)


_C_AQ = 0
_C_AK = _C_AQ + GLA_QK
_C_AV = _C_AK + GLA_QK
_C_AR = _C_AV + GLA_V
_C_AD = _C_AR + GLA_V
_C_GQ = _C_AD + GATE_PAD
_C_GK = _C_GQ + GLB_HEADS * HEAD_DIM
_C_GV = _C_GK + GLB_KV_HEADS * HEAD_DIM
_C_WQ = _C_GV + GLB_KV_HEADS * HEAD_DIM
_C_WK = _C_WQ + WIN_HEADS * HEAD_DIM
_C_WV = _C_WK + WIN_KV_HEADS * HEAD_DIM
_C_END = _C_WV + WIN_KV_HEADS * HEAD_DIM


def _head_norm_rope(x, gain, cos, sin_signed, bd_ref, out_scale):
    ss = _dot((x * x).astype(BF16), bd_ref[...])
    xn = x * lax.rsqrt(ss * (1.0 / HEAD_DIM) + EPS) * gain
    first_half = (_lane_iota(xn.shape) % HEAD_DIM) < (HEAD_DIM // 2)
    partner = jnp.where(first_half, pltpu.roll(xn, LANES - HEAD_DIM // 2, 1), pltpu.roll(xn, HEAD_DIM // 2, 1))
    out = xn * cos + partner * sin_signed
    if out_scale != 1.0:
        out = out * out_scale
    return out


def _dup_heads(x):
    swapped = pltpu.roll(x, HEAD_DIM, 1)
    low = _lane_iota(x.shape) < HEAD_DIM
    return jnp.where(low, x, swapped), jnp.where(low, swapped, x)


def _inproj_kernel(x_ref, mod_ref, g_ref, w_ref, wg_ref, bg_ref, qkg_ref, cos_ref, sin_ref, bd_ref,
                   aq_ref, ak_ref, av_ref, ar_ref, ag_ref, gq_ref, gk_ref, gv_ref, wq_ref, wk_ref, wv_ref, gvt_ref):
    hb = _norm_modulate(x_ref[...], g_ref[...], mod_ref, 3).astype(BF16)
    cos = cos_ref[...]
    sin = sin_ref[...]

    def proj(lo, width):
        return _dot(hb, w_ref[:, lo:lo + width])

    aq_ref[...] = (proj(_C_AQ, GLA_QK) * (GLA_DK ** -0.5)).astype(BF16)
    ak_ref[...] = proj(_C_AK, GLA_QK).astype(BF16)
    av_ref[...] = proj(_C_AV, GLA_V).astype(BF16)
    ar_ref[...] = proj(_C_AR, GLA_V).astype(BF16)
    z = _dot(proj(_C_AD, GATE_PAD).astype(BF16), wg_ref[...]) + bg_ref[...]
    log_sig = jnp.minimum(z, 0.0) - jnp.log1p(jnp.exp(-jnp.abs(z)))
    ag_ref[...] = log_sig * (1.0 / GLA_GATE_TAU)

    q_scale = HEAD_DIM ** -0.5
    for j in range(GLB_HEADS * HEAD_DIM // LANES):
        q = _head_norm_rope(proj(_C_GQ + j * LANES, LANES), qkg_ref[0:1, :], cos, sin, bd_ref, q_scale)
        gq_ref[:, j * LANES:(j + 1) * LANES] = q.astype(BF16)
    k = _head_norm_rope(proj(_C_GK, LANES), qkg_ref[1:2, :], cos, sin, bd_ref, 1.0)
    k0, k1 = _dup_heads(k)
    gk_ref[:, 0:LANES] = k0.astype(BF16)
    gk_ref[:, LANES:2 * LANES] = k1.astype(BF16)
    v = proj(_C_GV, LANES)
    v0, v1 = _dup_heads(v)
    gv_ref[:, 0:LANES] = v0.astype(BF16)
    gv_ref[:, LANES:2 * LANES] = v1.astype(BF16)
    chunk = gvt_ref.shape[4]
    for c in range(gvt_ref.shape[2]):
        vt = v[c * chunk:(c + 1) * chunk, :].T
        for kv in range(GLB_KV_HEADS):
            gvt_ref[0, kv, c] = vt[kv * HEAD_DIM:(kv + 1) * HEAD_DIM].astype(BF16)

    for j in range(WIN_HEADS * HEAD_DIM // LANES):
        q = _head_norm_rope(proj(_C_WQ + j * LANES, LANES), qkg_ref[2:3, :], cos, sin, bd_ref, q_scale)
        wq_ref[:, j * LANES:(j + 1) * LANES] = q.astype(BF16)
    k = _head_norm_rope(proj(_C_WK, LANES), qkg_ref[3:4, :], cos, sin, bd_ref, 1.0)
    k0, k1 = _dup_heads(k)
    wk_ref[:, 0:LANES] = k0.astype(BF16)
    wk_ref[:, LANES:2 * LANES] = k1.astype(BF16)
    v0, v1 = _dup_heads(proj(_C_WV, LANES))
    wv_ref[:, 0:LANES] = v0.astype(BF16)
    wv_ref[:, LANES:2 * LANES] = v1.astype(BF16)


def _inproj(x, mods, gain, w, wg, bg, qk_gains, cos, sin, bd, rows_per_batch, tm_want=512):
    t, d = x.shape
    tm = _row_tile(rows_per_batch, tm_want)
    tpb = rows_per_batch // tm
    widths = (GLA_QK, GLA_QK, GLA_V, GLA_V, 2 * GLA_QK, GLB_HEADS * HEAD_DIM, 2 * LANES, 2 * LANES,
              WIN_HEADS * HEAD_DIM, 2 * LANES, 2 * LANES)
    dtypes = (BF16, BF16, BF16, BF16, F32, BF16, BF16, BF16, BF16, BF16, BF16)
    chunk = min(KEY_CHUNK, tm)
    nb = t // rows_per_batch
    vt_shape = (nb, GLB_KV_HEADS, rows_per_batch // chunk, HEAD_DIM, chunk)
    vt_block = (1, GLB_KV_HEADS, tm // chunk, HEAD_DIM, chunk)
    return pl.pallas_call(
        _inproj_kernel,
        out_shape=tuple(jax.ShapeDtypeStruct((t, wd), dt) for wd, dt in zip(widths, dtypes))
        + (jax.ShapeDtypeStruct(vt_shape, BF16),),
        grid=(t // tm,),
        in_specs=[
            pl.BlockSpec((tm, d), lambda i: (i, 0)),
            _mod_spec(mods, tpb),
            _const_spec((1, d)),
            _const_spec(w.shape),
            _const_spec(wg.shape),
            _const_spec(bg.shape),
            _const_spec(qk_gains.shape),
            pl.BlockSpec((tm, LANES), lambda i: (i % tpb, 0)),
            pl.BlockSpec((tm, LANES), lambda i: (i % tpb, 0)),
            _const_spec(bd.shape),
        ],
        out_specs=tuple(pl.BlockSpec((tm, wd), lambda i: (i, 0)) for wd in widths)
        + (pl.BlockSpec(vt_block, lambda i: (i // tpb, 0, i % tpb, 0, 0)),),
        compiler_params=_cparams("parallel"),
        name="mix_inproj",
    )(x, mods, gain.reshape(1, d), w, wg, bg, qk_gains, cos, sin, bd)


def _gla_kernel(q_ref, k_ref, v_ref, g_ref, s0_ref, o_ref, sfin_ref, st_ref, *, n_chunks):
    c_len = GLA_CHUNK
    d = pl.program_id(1)
    i = pl.program_id(2)
    sign = 1 - 2 * d

    @pl.when(i == 0)
    def _():
        st_ref[...] = s0_ref[0, 0]

    row = lax.broadcasted_iota(jnp.int32, (c_len, c_len), 0)
    col = lax.broadcasted_iota(jnp.int32, (c_len, c_len), 1)
    keep = (row - col) * sign >= 0
    cum_op = jnp.where(keep, 1.0, 0.0).astype(BF16)
    row4 = lax.broadcasted_iota(jnp.int32, (c_len, GLA_HEADS * c_len), 0)
    col4 = lax.broadcasted_iota(jnp.int32, (c_len, GLA_HEADS * c_len), 1) % c_len
    keep4 = (row4 - col4) * sign >= 0
    qk_lane_head = _lane_iota((1, GLA_QK)) // GLA_DK
    v_lane_head = _lane_iota((1, GLA_V)) // GLA_DV
    qk_head_mask = [(qk_lane_head == h).astype(F32) for h in range(GLA_HEADS)]
    v_head_mask = [(v_lane_head == h).astype(BF16) for h in range(GLA_HEADS)]
    st_row_head = lax.broadcasted_iota(jnp.int32, (GLA_V, GLA_QK), 0) // GLA_DV
    st_col_head = lax.broadcasted_iota(jnp.int32, (GLA_V, GLA_QK), 1) // GLA_DK
    st_mask = st_row_head == st_col_head

    def chunk(j, carry):
        c = j + d * (n_chunks - 1 - 2 * j)
        r0 = pl.multiple_of(c * c_len, c_len)
        q = q_ref[pl.ds(r0, c_len), :].astype(F32)
        k = k_ref[pl.ds(r0, c_len), :].astype(F32)
        v = v_ref[pl.ds(r0, c_len), :]
        g = g_ref[pl.ds(r0, c_len), :]
        g_hi = g.astype(BF16)
        r1 = g - g_hi.astype(F32)
        g_mid = r1.astype(BF16)
        g_lo = (r1 - g_mid.astype(F32)).astype(BF16)
        b = _dot(cum_op, g_hi) + _dot(cum_op, g_mid) + _dot(cum_op, g_lo)
        decay = jnp.exp(jnp.sum(g, axis=0, keepdims=True))
        q_in = (q * jnp.exp(b)).astype(BF16)
        k_out = k * jnp.exp(-b)
        k_dec = (k_out * decay).astype(BF16)
        k_stack = jnp.concatenate([(k_out * qk_head_mask[h]).astype(BF16) for h in range(GLA_HEADS)], axis=0)
        a = _dot_nt(q_in, k_stack)
        a = jnp.where(keep4, a, 0.0).astype(BF16)
        v_bd = jnp.concatenate([v * v_head_mask[h] for h in range(GLA_HEADS)], axis=0)
        st = st_ref[...]
        o = _dot(a, v_bd) + _dot_nt(q_in, st.astype(BF16))
        o_ref[0, pl.ds(r0, c_len), :] = o
        ds_t = _dot_tn(v, k_dec)
        st_ref[...] = st * decay + jnp.where(st_mask, ds_t, 0.0)
        return carry

    lax.fori_loop(0, n_chunks, chunk, 0)

    @pl.when(i == pl.num_programs(2) - 1)
    def _():
        sfin_ref[0, 0] = st_ref[...]


def _gla(q, k, v, g, s0, rows_per_batch, tt_want=1024):
    t = q.shape[0]
    nb = t // rows_per_batch
    tt = _row_tile(rows_per_batch, tt_want)
    nt = rows_per_batch // tt

    def rows(b, d, i):
        return b * nt + i + d * (nt - 1 - 2 * i)

    return pl.pallas_call(
        functools.partial(_gla_kernel, n_chunks=tt // GLA_CHUNK),
        out_shape=(jax.ShapeDtypeStruct((2, t, GLA_V), F32),
                   jax.ShapeDtypeStruct((nb, 2, GLA_V, GLA_QK), F32)),
        grid=(nb, 2, nt),
        in_specs=[
            pl.BlockSpec((tt, GLA_QK), lambda b, d, i: (rows(b, d, i), 0)),
            pl.BlockSpec((tt, GLA_QK), lambda b, d, i: (rows(b, d, i), 0)),
            pl.BlockSpec((tt, GLA_V), lambda b, d, i: (rows(b, d, i), 0)),
            pl.BlockSpec((tt, GLA_QK), lambda b, d, i: (rows(b, d, i), d)),
            pl.BlockSpec((1, 1, GLA_V, GLA_QK), lambda b, d, i: (b, d, 0, 0)),
        ],
        out_specs=(pl.BlockSpec((1, tt, GLA_V), lambda b, d, i: (d, rows(b, d, i), 0)),
                   pl.BlockSpec((1, 1, GLA_V, GLA_QK), lambda b, d, i: (b, d, 0, 0))),
        scratch_shapes=[pltpu.VMEM((GLA_V, GLA_QK), F32)],
        compiler_params=_cparams("parallel", "parallel", "arbitrary"),
        name="gla_scan",
    )(q, k, v, g, s0)


def _half_masks(dtype):
    low = _lane_iota((1, LANES)) < HEAD_DIM
    return [low.astype(dtype), (~low).astype(dtype)]


def _rep(x, width):
    n = width // LANES
    return x if n == 1 else pltpu.repeat(x, n, axis=1)


def _glb_kernel(q_ref, k_ref, vt_ref, kc_ref, vtc_ref, o_ref, qt_ref, m_ref, l_ref, acc_ref, s_ref, *,
                n_pairs, col_block, lookahead):
    tq = q_ref.shape[0]
    n_lat = vt_ref.shape[2]
    tk = vt_ref.shape[4]
    n_heads = 2 * n_pairs
    row_low = lax.broadcasted_iota(jnp.int32, (LANES, tq), 0) < HEAD_DIM
    for p in range(n_pairs):
        qt = (q_ref[:, p * LANES:(p + 1) * LANES].astype(F32) * LOG2_E).T
        qt_ref[2 * p] = jnp.where(row_low, qt, 0.0).astype(BF16)
        qt_ref[2 * p + 1] = jnp.where(row_low, 0.0, qt).astype(BF16)

    m_ref[...] = jnp.full(m_ref.shape, MASK_VALUE, F32)
    l_ref[...] = jnp.zeros(l_ref.shape, F32)
    acc_ref[...] = jnp.zeros(acc_ref.shape, F32)

    units = [(h, cb) for h in range(n_heads) for cb in range(tq // col_block)]
    assert lookahead <= len(units)

    def scores(k, u):
        h, cb = units[u]
        return _dot(k, qt_ref[h, :, cb * col_block:(cb + 1) * col_block])

    def step(k, vt, k_next):
        w = k.shape[0]
        for u, (h, cb) in enumerate(units):
            s = s_ref[u, 0:w, :]
            if u + lookahead < len(units):
                s_ref[u + lookahead, 0:w, :] = scores(k, u + lookahead)
            elif k_next is not None:
                s_ref[u + lookahead - len(units), 0:k_next.shape[0], :] = scores(k_next, u + lookahead - len(units))
            cols = slice(cb * col_block, (cb + 1) * col_block)
            m_prev = m_ref[h, :, cols]
            m_new = jnp.maximum(m_prev, jnp.max(s, axis=0, keepdims=True))
            alpha = jnp.exp2(m_prev - m_new)
            e = jnp.exp2(s - m_new)
            l_ref[h, :, cols] = alpha * l_ref[h, :, cols] + jnp.sum(e, axis=0, keepdims=True)
            m_ref[h, :, cols] = m_new
            acc_ref[h, :, cols] = acc_ref[h, :, cols] * alpha + _dot(vt, e.astype(BF16))

    def lat_tile(j):
        return k_ref[pl.ds(pl.multiple_of(j * tk, tk), tk), :]

    wc = vtc_ref.shape[4]
    ctx_tiles = [kc_ref[j * wc:(j + 1) * wc, :] for j in range(vtc_ref.shape[2])]

    for u in range(lookahead):
        s_ref[u, 0:tk, :] = scores(lat_tile(0), u)

    def body(j, carry):
        step(lat_tile(j), vt_ref[0, 0, j], lat_tile(j + 1))
        return carry

    lax.fori_loop(0, n_lat - 1, body, 0)
    step(lat_tile(n_lat - 1), vt_ref[0, 0, n_lat - 1], ctx_tiles[0])
    for j, kc in enumerate(ctx_tiles):
        step(kc, vtc_ref[0, 0, j], ctx_tiles[j + 1] if j + 1 < len(ctx_tiles) else None)

    for p in range(n_pairs):
        o_t = jnp.concatenate([acc_ref[2 * p] / l_ref[2 * p], acc_ref[2 * p + 1] / l_ref[2 * p + 1]], axis=0)
        o_ref[:, p * LANES:(p + 1) * LANES] = o_t.T.astype(o_ref.dtype)


def _glb_attn(q, k, vt, kc, vtc, seq, ctx_len, tq_want=512, col_block=512, lookahead=3):
    t = q.shape[0]
    nb = t // seq
    groups = GLB_KV_HEADS
    gw = q.shape[1] // groups
    n_pairs = gw // LANES
    tq = _row_tile(seq, tq_want)
    nq = seq // tq
    col_block = min(col_block, tq)
    n_units = 2 * n_pairs * (tq // col_block)
    max_keys = max(vt.shape[4], vtc.shape[4])
    return pl.pallas_call(
        functools.partial(_glb_kernel, n_pairs=n_pairs, col_block=col_block, lookahead=lookahead),
        out_shape=jax.ShapeDtypeStruct(q.shape, BF16),
        grid=(nb, groups, nq),
        in_specs=[
            pl.BlockSpec((tq, gw), lambda b, g, i: (b * nq + i, g)),
            pl.BlockSpec((seq, LANES), lambda b, g, i: (b, g)),
            pl.BlockSpec((1, 1) + vt.shape[2:], lambda b, g, i: (b, g, 0, 0, 0)),
            pl.BlockSpec((ctx_len, LANES), lambda b, g, i: (b, g)),
            pl.BlockSpec((1, 1) + vtc.shape[2:], lambda b, g, i: (b, g, 0, 0, 0)),
        ],
        out_specs=pl.BlockSpec((tq, gw), lambda b, g, i: (b * nq + i, g)),
        scratch_shapes=[pltpu.VMEM((2 * n_pairs, LANES, tq), BF16),
                        pltpu.VMEM((2 * n_pairs, 1, tq), F32),
                        pltpu.VMEM((2 * n_pairs, 1, tq), F32),
                        pltpu.VMEM((2 * n_pairs, HEAD_DIM, tq), F32),
                        pltpu.VMEM((n_units, max_keys, col_block), F32)],
        compiler_params=_cparams("parallel", "parallel", "arbitrary"),
        name="glb_attn",
    )(q, k, vt, kc, vtc)


def _softmax_pair(q_pair, keys, values, masks, sink_pair, hm, low):
    out = None
    inv = []
    for half in range(2):
        qh = q_pair * hm[half]
        ss = []
        for kk, mk in zip(keys, masks):
            s = _dot_nt(qh, kk)
            ss.append(s if mk is None else jnp.where(mk, s, MASK_VALUE))
        m = ss[0].max(axis=1, keepdims=True)
        for s in ss[1:]:
            m = jnp.maximum(m, s.max(axis=1, keepdims=True))
        if sink_pair is not None:
            m = jnp.maximum(m, sink_pair[half])
        den = None
        for s, vv in zip(ss, values):
            e = jnp.exp(s - m)
            sm = jnp.sum(e, axis=1, keepdims=True)
            den = sm if den is None else den + sm
            part = _dot(e.astype(BF16), vv * hm[half])
            out = part if out is None else out + part
        if sink_pair is not None:
            den = den + jnp.exp(sink_pair[half] - m)
        inv.append(1.0 / den)
    return out * jnp.where(low, inv[0], inv[1])


def _win_kernel(sink_ref, q_ref, k_ref, v_ref, kc_ref, vc_ref, o_ref, *, n_pairs, span):
    g = pl.program_id(1)
    i = pl.program_id(2)
    tq = q_ref.shape[0]
    seq = k_ref.shape[0]
    hm = _half_masks(BF16)
    low = _lane_iota((1, LANES)) < HEAD_DIM
    q0 = i * tq
    start = jnp.clip(q0 - WINDOW, 0, seq - span)
    start = pl.multiple_of(start, LANES)
    k_loc = k_ref[pl.ds(start, span), :]
    v_loc = v_ref[pl.ds(start, span), :]
    qpos = q0 + lax.broadcasted_iota(jnp.int32, (tq, span), 0)
    kpos = start + lax.broadcasted_iota(jnp.int32, (tq, span), 1)
    in_win = jnp.abs(qpos - kpos) <= WINDOW
    kc = kc_ref[...]
    vc = vc_ref[...]
    for p in range(n_pairs):
        heads = (g * n_pairs + p) * 2
        sink_pair = (sink_ref[heads], sink_ref[heads + 1])
        o = _softmax_pair(q_ref[:, p * LANES:(p + 1) * LANES], [k_loc, kc], [v_loc, vc], [in_win, None],
                          sink_pair, hm, low)
        o_ref[:, p * LANES:(p + 1) * LANES] = o.astype(o_ref.dtype)


def _win_attn(q, k, v, kc, vc, sink, seq, ctx_len, tq_want=256):
    t = q.shape[0]
    nb = t // seq
    groups = WIN_KV_HEADS
    gw = q.shape[1] // groups
    n_pairs = gw // LANES
    tq = _row_tile(seq, tq_want)
    span = tq + 2 * WINDOW
    assert seq >= span
    nq = seq // tq
    return pl.pallas_call(
        functools.partial(_win_kernel, n_pairs=n_pairs, span=span),
        out_shape=jax.ShapeDtypeStruct(q.shape, BF16),
        grid=(nb, groups, nq),
        in_specs=[
            pl.BlockSpec(memory_space=pltpu.SMEM),
            pl.BlockSpec((tq, gw), lambda b, g, i: (b * nq + i, g)),
            pl.BlockSpec((seq, LANES), lambda b, g, i: (b, g)),
            pl.BlockSpec((seq, LANES), lambda b, g, i: (b, g)),
            pl.BlockSpec((ctx_len, LANES), lambda b, g, i: (b, g)),
            pl.BlockSpec((ctx_len, LANES), lambda b, g, i: (b, g)),
        ],
        out_specs=pl.BlockSpec((tq, gw), lambda b, g, i: (b * nq + i, g)),
        compiler_params=_cparams("parallel", "parallel", "arbitrary"),
        name="win_attn",
    )(sink, q, k, v, kc, vc)


def _ctx_attn_kernel(sink_ref, q_ref, k_ref, v_ref, o_ref, *, n_pairs, use_sink):
    g = pl.program_id(1)
    hm = _half_masks(BF16)
    low = _lane_iota((1, LANES)) < HEAD_DIM
    k = k_ref[...]
    v = v_ref[...]
    for p in range(n_pairs):
        heads = (g * n_pairs + p) * 2
        sink_pair = (sink_ref[heads], sink_ref[heads + 1]) if use_sink else None
        o = _softmax_pair(q_ref[:, p * LANES:(p + 1) * LANES], [k], [v], [None], sink_pair, hm, low)
        o_ref[:, p * LANES:(p + 1) * LANES] = o.astype(o_ref.dtype)


def _ctx_attn(q, k, v, sink, groups, ctx_len, use_sink):
    t = q.shape[0]
    nb = t // ctx_len
    gw = q.shape[1] // groups
    n_pairs = gw // LANES
    return pl.pallas_call(
        functools.partial(_ctx_attn_kernel, n_pairs=n_pairs, use_sink=use_sink),
        out_shape=jax.ShapeDtypeStruct(q.shape, BF16),
        grid=(nb, groups),
        in_specs=[
            pl.BlockSpec(memory_space=pltpu.SMEM),
            pl.BlockSpec((ctx_len, gw), lambda b, g: (b, g)),
            pl.BlockSpec((ctx_len, LANES), lambda b, g: (b, g)),
            pl.BlockSpec((ctx_len, LANES), lambda b, g: (b, g)),
        ],
        out_specs=pl.BlockSpec((ctx_len, gw), lambda b, g: (b, g)),
        compiler_params=_cparams("parallel", "parallel"),
        name="ctx_attn",
    )(sink, q, k, v)


def _outproj_kernel(x_ref, mod_ref, of_ref, ob_ref, r_ref, og_ref, ow_ref, gain_ref, bd_ref, w_ref, o_ref):
    o = of_ref[0] + ob_ref[0]
    r = r_ref[...].astype(F32)
    gate = r * jax.nn.sigmoid(r)
    pieces = []
    for j in range(GLA_V // LANES):
        oj = o[:, j * LANES:(j + 1) * LANES]
        ss = _dot((oj * oj).astype(BF16), bd_ref[...])
        on = oj * lax.rsqrt(ss * (1.0 / GLA_DV) + EPS) * gain_ref[...]
        pieces.append((on * gate[:, j * LANES:(j + 1) * LANES]).astype(BF16))
    cat = jnp.concatenate(pieces + [og_ref[...], ow_ref[...]], axis=-1)
    y = _dot(cat, w_ref[...])
    o_ref[...] = x_ref[...] + mod_ref[0, 5:6, :] * y


def _outproj(x, mods, o_gla, r, o_glb, o_win, gla_gain, bd, w_out, rows_per_batch, tm_want=512):
    t, d = x.shape
    tm = _row_tile(rows_per_batch, tm_want)
    return pl.pallas_call(
        _outproj_kernel,
        out_shape=jax.ShapeDtypeStruct((t, d), F32),
        grid=(t // tm,),
        in_specs=[
            pl.BlockSpec((tm, d), lambda i: (i, 0)),
            _mod_spec(mods, rows_per_batch // tm),
            pl.BlockSpec((1, tm, GLA_V), lambda i: (0, i, 0)),
            pl.BlockSpec((1, tm, GLA_V), lambda i: (1, i, 0)),
            pl.BlockSpec((tm, GLA_V), lambda i: (i, 0)),
            pl.BlockSpec((tm, o_glb.shape[1]), lambda i: (i, 0)),
            pl.BlockSpec((tm, o_win.shape[1]), lambda i: (i, 0)),
            _const_spec(gla_gain.shape),
            _const_spec(bd.shape),
            _const_spec(w_out.shape),
        ],
        out_specs=pl.BlockSpec((tm, d), lambda i: (i, 0)),
        compiler_params=_cparams("parallel"),
        name="mix_outproj",
    )(x, mods, o_gla, o_gla, r, o_glb, o_win, gla_gain, bd, w_out)


def _rope_tables(seq):
    rows = seq // GRID_W
    row = jnp.repeat(jnp.arange(rows, dtype=F32), GRID_W)
    col = (jnp.arange(rows * GRID_W) % GRID_W).astype(F32)
    n_freq = HEAD_DIM // 4
    inv = jnp.power(ROPE_BASE, -jnp.arange(n_freq, dtype=F32) / n_freq)
    ang = jnp.concatenate([row[:, None] * inv, col[:, None] * inv], axis=-1)
    cos, sin = jnp.cos(ang), jnp.sin(ang)
    cos_t = jnp.concatenate([cos, cos] * (LANES // HEAD_DIM), axis=-1)
    sin_t = jnp.concatenate([-sin, sin] * (LANES // HEAD_DIM), axis=-1)
    return cos_t, sin_t


def _block_diag_ones(n, block):
    idx = np.arange(n) // block
    return jnp.asarray(idx[:, None] == idx[None, :], dtype=BF16)


def kernel(x, c, ctx, c_ctx, mod_w, mod_b, norm_ffn1, ffn1_w_in, ffn1_w_out, norm_mix, mix_w_in, mix_w_out,
           gla_wg_f, gla_bg_f, gla_wg_b, gla_bg_b, gla_out_norm, glb_q_norm, glb_k_norm,
           win_q_norm, win_k_norm, win_sink, norm_ffn2, ffn2_w_in, ffn2_w_out):
    bsz, seq, d = x.shape
    ctx_len = ctx.shape[1]
    depth = mod_w.shape[0]
    split_ad = GLA_QK * 2 + GLA_V * 2 + 2 * GLA_GATE_RANK

    n_rows = -(-(bsz + 1) // 8) * 8
    c_rows = jnp.concatenate([c, c_ctx[None, :], jnp.zeros((n_rows - bsz - 1, d), F32)], axis=0)
    mods = _modvec(c_rows, mod_w, mod_b)

    cos_l, sin_l = _rope_tables(seq)
    cos_c = jnp.ones((ctx_len, LANES), F32)
    sin_c = jnp.zeros((ctx_len, LANES), F32)
    bd = _block_diag_ones(LANES, HEAD_DIM)
    zero_state = jnp.zeros((bsz, 2, GLA_V, GLA_QK), F32)

    xl = x.reshape(bsz * seq, d)
    xc = ctx.reshape(bsz * ctx_len, d)

    for l in range(depth):
        need_ctx = l < depth - 1
        mods_l = mods[l, :bsz].reshape(bsz, N_MOD, d)
        mods_c = mods[l, bsz:bsz + 1].reshape(1, N_MOD, d)
        w1_in, w1_out = ffn1_w_in[l].astype(BF16), ffn1_w_out[l].astype(BF16)
        w2_in, w2_out = ffn2_w_in[l].astype(BF16), ffn2_w_out[l].astype(BF16)
        wm = mix_w_in[l]
        w_mix = jnp.concatenate([wm[:, :split_ad], jnp.zeros((d, GATE_PAD - 2 * GLA_GATE_RANK), F32),
                                 wm[:, split_ad:]], axis=1).astype(BF16)
        assert w_mix.shape[1] == _C_END
        wg = jnp.zeros((GATE_PAD, 2 * GLA_QK), F32)
        wg = wg.at[:GLA_GATE_RANK, :GLA_QK].set(gla_wg_f[l])
        wg = wg.at[GLA_GATE_RANK:2 * GLA_GATE_RANK, GLA_QK:].set(gla_wg_b[l]).astype(BF16)
        bg = jnp.concatenate([gla_bg_f[l], gla_bg_b[l]])[None, :]
        qk_gains = jnp.stack([jnp.tile(gn[l], LANES // HEAD_DIM)
                              for gn in (glb_q_norm, glb_k_norm, win_q_norm, win_k_norm)])
        gla_gain = jnp.tile(gla_out_norm[l], LANES // GLA_DV)[None, :]
        w_out = mix_w_out[l].astype(BF16)
        sink = win_sink[l]

        xl = _ffn(xl, mods_l, 0, norm_ffn1[l], w1_in, w1_out, seq)
        xc = _ffn(xc, mods_c, 0, norm_ffn1[l], w1_in, w1_out, ctx_len)

        pc = _inproj(xc, mods_c, norm_mix[l], w_mix, wg, bg, qk_gains, cos_c, sin_c, bd, ctx_len)
        pq = _inproj(xl, mods_l, norm_mix[l], w_mix, wg, bg, qk_gains, cos_l, sin_l, bd, seq)
        aq, ak, av, ar, ag, gq, gk, gv, wq, wk, wv, gvt = pq
        aqc, akc, avc, arc, agc, gqc, gkc, gvc, wqc, wkc, wvc, gvtc = pc

        oc_gla, states = _gla(aqc, akc, avc, agc, zero_state, ctx_len)
        o_gla, _ = _gla(aq, ak, av, ag, states, seq)
        o_glb = _glb_attn(gq, gk, gvt, gkc, gvtc, seq, ctx_len)
        o_win = _win_attn(wq, wk, wv, wkc, wvc, sink, seq, ctx_len)
        xl = _outproj(xl, mods_l, o_gla, ar, o_glb, o_win, gla_gain, bd, w_out, seq)

        xl = _ffn(xl, mods_l, 6, norm_ffn2[l], w2_in, w2_out, seq)

        if need_ctx:
            oc_glb = _ctx_attn(gqc, gkc, gvc, sink, GLB_KV_HEADS, ctx_len, use_sink=False)
            oc_win = _ctx_attn(wqc, wkc, wvc, sink, WIN_KV_HEADS, ctx_len, use_sink=True)
            xc = _outproj(xc, mods_c, oc_gla, arc, oc_glb, oc_win, gla_gain, bd, w_out, ctx_len)
            xc = _ffn(xc, mods_c, 6, norm_ffn2[l], w2_in, w2_out, ctx_len)

    return xl.reshape(bsz, seq, d)
```

```python
import functools

import numpy as np
import jax
import jax.numpy as jnp
from jax import lax
from jax.experimental import pallas as pl
from jax.experimental.pallas import tpu as pltpu

GRID_W = 64
HEAD_DIM = 64
GLA_HEADS = 4
GLA_DK = 32
GLA_DV = 64
GLA_GATE_RANK = 16
GLA_GATE_TAU = 16.0
GLA_CHUNK = 64
GLB_HEADS = 8
GLB_KV_HEADS = 2
WIN_HEADS = 4
WIN_KV_HEADS = 2
WINDOW = 128
ROPE_BASE = 10000.0
N_MOD = 9
EPS = 1e-6

LANES = 128
VMEM_LIMIT_BYTES = 56 * 1024 * 1024

GLA_QK = GLA_HEADS * GLA_DK
GLA_V = GLA_HEADS * GLA_DV
GATE_PAD = LANES
MASK_VALUE = -1e30
LOG2_E = 1.4426950408889634
KEY_CHUNK = 512
VT_ROWS = HEAD_DIM + 16

BF16 = jnp.bfloat16
F32 = jnp.float32


def _cparams(*sem):
    return pltpu.CompilerParams(dimension_semantics=sem, vmem_limit_bytes=VMEM_LIMIT_BYTES)


def _dot(a, b):
    return jnp.dot(a, b, preferred_element_type=F32)


def _dot_nt(a, b):
    return lax.dot_general(a, b, (((1,), (1,)), ((), ())), preferred_element_type=F32)


def _dot_tn(a, b):
    return lax.dot_general(a, b, (((0,), (0,)), ((), ())), preferred_element_type=F32)


def _lane_iota(shape):
    return lax.broadcasted_iota(jnp.int32, shape, len(shape) - 1)


def _modvec_kernel(c_ref, w_ref, b_ref, o_ref):
    c = c_ref[...]
    s = (c * jax.nn.sigmoid(c)).astype(BF16)
    o_ref[0] = _dot(s, w_ref[0].astype(BF16)) + b_ref[0]


def _modvec(c_rows, mod_w, mod_b):
    depth, d, n = mod_w.shape
    rows = c_rows.shape[0]
    tn = n // N_MOD
    return pl.pallas_call(
        _modvec_kernel,
        out_shape=jax.ShapeDtypeStruct((depth, rows, n), F32),
        grid=(depth, n // tn),
        in_specs=[
            pl.BlockSpec((rows, d), lambda l, j: (0, 0)),
            pl.BlockSpec((1, d, tn), lambda l, j: (l, 0, j)),
            pl.BlockSpec((1, 1, tn), lambda l, j: (l, 0, j)),
        ],
        out_specs=pl.BlockSpec((1, rows, tn), lambda l, j: (l, 0, j)),
        compiler_params=_cparams("arbitrary", "arbitrary"),
        name="modvec",
    )(c_rows, mod_w, mod_b.reshape(depth, 1, n))


def _norm_modulate(x, gain, mod_ref, k_shift):
    shift = mod_ref[0, k_shift:k_shift + 1, :]
    scale = mod_ref[0, k_shift + 1:k_shift + 2, :]
    y = x * lax.rsqrt(jnp.mean(x * x, axis=-1, keepdims=True) + EPS)
    return (y * gain) * (1.0 + scale) + shift


def _mod_spec(mods, tiles_per_batch):
    nb, nm, d = mods.shape
    if nb == 1:
        return pl.BlockSpec((1, nm, d), lambda i: (0, 0, 0))
    return pl.BlockSpec((1, nm, d), lambda i: (i // tiles_per_batch, 0, 0))


def _const_spec(shape):
    nd = len(shape)
    return pl.BlockSpec(shape, lambda *_: (0,) * nd, pipeline_mode=pl.Buffered(1))


def _row_tile(n_rows_per_batch, want):
    t = min(want, n_rows_per_batch)
    assert n_rows_per_batch % t == 0
    return t


def _ffn_chunks(f):
    step = 1536 if f > 1536 else f
    return tuple((lo, min(lo + step, f)) for lo in range(0, f, step))


def _ffn_kernel(x_ref, mod_ref, g_ref, win_ref, wout_ref, o_ref, *, k0, f):
    x = x_ref[...]
    hb = _norm_modulate(x, g_ref[...], mod_ref, k0).astype(BF16)
    gate = mod_ref[0, k0 + 2:k0 + 3, :]
    acc = None
    for lo, hi in _ffn_chunks(f):
        a = _dot(hb, win_ref[:, lo:hi])
        b = _dot(hb, win_ref[:, f + lo:f + hi])
        u = ((a * jax.nn.sigmoid(a)) * b).astype(BF16)
        part = _dot(u, wout_ref[lo:hi, :])
        acc = part if acc is None else acc + part
    o_ref[...] = x + (0.5 * gate) * acc


def _ffn(x, mods, k0, gain, w_in, w_out, rows_per_batch, tm_want=512):
    t, d = x.shape
    f = w_out.shape[0]
    tm = _row_tile(rows_per_batch, tm_want)
    return pl.pallas_call(
        functools.partial(_ffn_kernel, k0=k0, f=f),
        out_shape=jax.ShapeDtypeStruct((t, d), F32),
        grid=(t // tm,),
        in_specs=[
            pl.BlockSpec((tm, d), lambda i: (i, 0)),
            _mod_spec(mods, rows_per_batch // tm),
            _const_spec((1, d)),
            _const_spec((d, 2 * f)),
            _const_spec((f, d)),
        ],
        out_specs=pl.BlockSpec((tm, d), lambda i: (i, 0)),
        compiler_params=_cparams("parallel"),
        name="ffn",
    )(x, mods, gain.reshape(1, d), w_in, w_out)


_C_AQ = 0
_C_AK = _C_AQ + GLA_QK
_C_AV = _C_AK + GLA_QK
_C_AR = _C_AV + GLA_V
_C_GQ = _C_AR + GLA_V
_C_GK = _C_GQ + GLB_HEADS * HEAD_DIM
_C_WK = _C_GK + GLB_KV_HEADS * HEAD_DIM
_C_GV = _C_WK + WIN_KV_HEADS * HEAD_DIM
_C_WV = _C_GV + GLB_KV_HEADS * HEAD_DIM
_C_WQ = _C_WV + WIN_KV_HEADS * HEAD_DIM
_C_AD = _C_WQ + WIN_HEADS * HEAD_DIM
_C_END = _C_AD + GATE_PAD
SLAB = 2 * LANES


def _norm_rope_slab(x, bd_ref, gains, cos, sin_signed, out_scales):
    ss = _dot((x * x).astype(BF16), bd_ref[...])
    first_half = (_lane_iota((x.shape[0], LANES)) % HEAD_DIM) < (HEAD_DIM // 2)
    outs = []
    for j in range(2):
        sl = slice(j * LANES, (j + 1) * LANES)
        xn = x[:, sl] * lax.rsqrt(ss[:, sl] * (1.0 / HEAD_DIM) + EPS) * gains[j]
        partner = jnp.where(first_half, pltpu.roll(xn, LANES - HEAD_DIM // 2, 1), pltpu.roll(xn, HEAD_DIM // 2, 1))
        out = xn * cos + partner * sin_signed
        outs.append(out * out_scales[j] if out_scales[j] != 1.0 else out)
    return outs


def _dup_heads(x):
    swapped = pltpu.roll(x, HEAD_DIM, 1)
    low = _lane_iota(x.shape) < HEAD_DIM
    return jnp.where(low, x, swapped), jnp.where(low, swapped, x)


def _inproj_kernel(x_ref, mod_ref, g_ref, w_ref, wg_ref, bg_ref, qkg_ref, cos_ref, sin_ref, bd_ref,
                   aq_ref, ak_ref, av_ref, ar_ref, ag_ref, gq_ref, gk_ref, gv_ref, wq_ref, wk_ref, wv_ref, gvt_ref):
    hb = _norm_modulate(x_ref[...], g_ref[...], mod_ref, 3).astype(BF16)
    cos = cos_ref[...]
    sin = sin_ref[...]
    p_all = _dot(hb, w_ref[...])

    def proj(lo, width):
        return p_all[:, lo:lo + width]

    aq_ref[...] = (proj(_C_AQ, GLA_QK) * (GLA_DK ** -0.5)).astype(BF16)
    ak_ref[...] = proj(_C_AK, GLA_QK).astype(BF16)
    av_ref[...] = proj(_C_AV, GLA_V).astype(BF16)
    ar_ref[...] = proj(_C_AR, GLA_V).astype(BF16)
    z = _dot(proj(_C_AD, GATE_PAD).astype(BF16), wg_ref[...]) + bg_ref[...]
    log_sig = jnp.minimum(z, 0.0) - jnp.log1p(jnp.exp(-jnp.abs(z)))
    ag_ref[...] = log_sig * (1.0 / GLA_GATE_TAU)

    q_scale = HEAD_DIM ** -0.5
    g_glb_q, g_glb_k, g_win_q, g_win_k = (qkg_ref[r:r + 1, :] for r in range(4))

    for j in range(GLB_HEADS * HEAD_DIM // SLAB):
        halves = _norm_rope_slab(proj(_C_GQ + j * SLAB, SLAB), bd_ref, (g_glb_q, g_glb_q), cos, sin,
                                 (q_scale, q_scale))
        for i, q in enumerate(halves):
            gq_ref[:, j * SLAB + i * LANES:j * SLAB + (i + 1) * LANES] = q.astype(BF16)
    halves = _norm_rope_slab(proj(_C_WQ, SLAB), bd_ref, (g_win_q, g_win_q), cos, sin, (q_scale, q_scale))
    for i, q in enumerate(halves):
        wq_ref[:, i * LANES:(i + 1) * LANES] = q.astype(BF16)
    k_glb, k_win = _norm_rope_slab(proj(_C_GK, SLAB), bd_ref, (g_glb_k, g_win_k), cos, sin, (1.0, 1.0))
    k0, k1 = _dup_heads(k_glb)
    gk_ref[:, 0:LANES] = k0.astype(BF16)
    gk_ref[:, LANES:2 * LANES] = k1.astype(BF16)
    k0, k1 = _dup_heads(k_win)
    wk_ref[:, 0:LANES] = k0.astype(BF16)
    wk_ref[:, LANES:2 * LANES] = k1.astype(BF16)

    v = proj(_C_GV, LANES)
    v0, v1 = _dup_heads(v)
    gv_ref[:, 0:LANES] = v0.astype(BF16)
    gv_ref[:, LANES:2 * LANES] = v1.astype(BF16)
    chunk = gvt_ref.shape[4]
    for c in range(gvt_ref.shape[2]):
        vt = v[c * chunk:(c + 1) * chunk, :].T
        for kv in range(GLB_KV_HEADS):
            gvt_ref[0, kv, c, 0:HEAD_DIM, :] = vt[kv * HEAD_DIM:(kv + 1) * HEAD_DIM].astype(BF16)
            gvt_ref[0, kv, c, HEAD_DIM:VT_ROWS, :] = jnp.ones((VT_ROWS - HEAD_DIM, chunk), BF16)

    v0, v1 = _dup_heads(proj(_C_WV, LANES))
    wv_ref[:, 0:LANES] = v0.astype(BF16)
    wv_ref[:, LANES:2 * LANES] = v1.astype(BF16)


def _inproj(x, mods, gain, w, wg, bg, qk_gains, cos, sin, bd, rows_per_batch, tm_want=512):
    t, d = x.shape
    tm = _row_tile(rows_per_batch, tm_want)
    tpb = rows_per_batch // tm
    widths = (GLA_QK, GLA_QK, GLA_V, GLA_V, 2 * GLA_QK, GLB_HEADS * HEAD_DIM, 2 * LANES, 2 * LANES,
              WIN_HEADS * HEAD_DIM, 2 * LANES, 2 * LANES)
    dtypes = (BF16, BF16, BF16, BF16, F32, BF16, BF16, BF16, BF16, BF16, BF16)
    chunk = min(KEY_CHUNK, tm)
    nb = t // rows_per_batch
    vt_shape = (nb, GLB_KV_HEADS, rows_per_batch // chunk, VT_ROWS, chunk)
    vt_block = (1, GLB_KV_HEADS, tm // chunk, VT_ROWS, chunk)
    return pl.pallas_call(
        _inproj_kernel,
        out_shape=tuple(jax.ShapeDtypeStruct((t, wd), dt) for wd, dt in zip(widths, dtypes))
        + (jax.ShapeDtypeStruct(vt_shape, BF16),),
        grid=(t // tm,),
        in_specs=[
            pl.BlockSpec((tm, d), lambda i: (i, 0)),
            _mod_spec(mods, tpb),
            _const_spec((1, d)),
            _const_spec(w.shape),
            _const_spec(wg.shape),
            _const_spec(bg.shape),
            _const_spec(qk_gains.shape),
            pl.BlockSpec((tm, LANES), lambda i: (i % tpb, 0)),
            pl.BlockSpec((tm, LANES), lambda i: (i % tpb, 0)),
            _const_spec(bd.shape),
        ],
        out_specs=tuple(pl.BlockSpec((tm, wd), lambda i: (i, 0)) for wd in widths)
        + (pl.BlockSpec(vt_block, lambda i: (i // tpb, 0, i % tpb, 0, 0)),),
        compiler_params=_cparams("parallel"),
        name="mix_inproj",
    )(x, mods, gain.reshape(1, d), w, wg, bg, qk_gains, cos, sin, bd)


def _gla_kernel(q_ref, k_ref, v_ref, g_ref, s0_ref, o_ref, sfin_ref, st_ref, *, n_chunks):
    c_len = GLA_CHUNK
    d = pl.program_id(1)
    i = pl.program_id(2)
    sign = 1 - 2 * d

    @pl.when(i == 0)
    def _():
        st_ref[...] = s0_ref[0, 0]

    row = lax.broadcasted_iota(jnp.int32, (c_len, c_len), 0)
    col = lax.broadcasted_iota(jnp.int32, (c_len, c_len), 1)
    keep = (row - col) * sign >= 0
    cum_op = jnp.where(keep, 1.0, 0.0).astype(BF16)
    row4 = lax.broadcasted_iota(jnp.int32, (c_len, GLA_HEADS * c_len), 0)
    col4 = lax.broadcasted_iota(jnp.int32, (c_len, GLA_HEADS * c_len), 1) % c_len
    keep4 = (row4 - col4) * sign >= 0
    qk_lane_head = _lane_iota((1, GLA_QK)) // GLA_DK
    v_lane_head = _lane_iota((1, GLA_V)) // GLA_DV
    qk_head_mask = [(qk_lane_head == h).astype(F32) for h in range(GLA_HEADS)]
    v_head_mask = [(v_lane_head == h).astype(BF16) for h in range(GLA_HEADS)]
    st_row_head = lax.broadcasted_iota(jnp.int32, (GLA_V, GLA_QK), 0) // GLA_DV
    st_col_head = lax.broadcasted_iota(jnp.int32, (GLA_V, GLA_QK), 1) // GLA_DK
    st_mask = st_row_head == st_col_head

    def chunk(j, carry):
        c = j + d * (n_chunks - 1 - 2 * j)
        r0 = pl.multiple_of(c * c_len, c_len)
        q = q_ref[pl.ds(r0, c_len), :].astype(F32)
        k = k_ref[pl.ds(r0, c_len), :].astype(F32)
        v = v_ref[pl.ds(r0, c_len), :]
        g = g_ref[pl.ds(r0, c_len), :]
        g_hi = g.astype(BF16)
        r1 = g - g_hi.astype(F32)
        g_mid = r1.astype(BF16)
        g_lo = (r1 - g_mid.astype(F32)).astype(BF16)
        b = _dot(cum_op, g_hi) + _dot(cum_op, g_mid) + _dot(cum_op, g_lo)
        decay = jnp.exp(jnp.sum(g, axis=0, keepdims=True))
        q_in = (q * jnp.exp(b)).astype(BF16)
        k_out = k * jnp.exp(-b)
        k_dec = (k_out * decay).astype(BF16)
        k_stack = jnp.concatenate([(k_out * qk_head_mask[h]).astype(BF16) for h in range(GLA_HEADS)], axis=0)
        a = _dot_nt(q_in, k_stack)
        a = jnp.where(keep4, a, 0.0).astype(BF16)
        v_bd = jnp.concatenate([v * v_head_mask[h] for h in range(GLA_HEADS)], axis=0)
        st = st_ref[...]
        o = _dot(a, v_bd) + _dot_nt(q_in, st.astype(BF16))
        o_ref[0, pl.ds(r0, c_len), :] = o
        ds_t = _dot_tn(v, k_dec)
        st_ref[...] = st * decay + jnp.where(st_mask, ds_t, 0.0)
        return carry

    lax.fori_loop(0, n_chunks, chunk, 0)

    @pl.when(i == pl.num_programs(2) - 1)
    def _():
        sfin_ref[0, 0] = st_ref[...]


def _gla(q, k, v, g, s0, rows_per_batch, tt_want=1024):
    t = q.shape[0]
    nb = t // rows_per_batch
    tt = _row_tile(rows_per_batch, tt_want)
    nt = rows_per_batch // tt

    def rows(b, d, i):
        return b * nt + i + d * (nt - 1 - 2 * i)

    return pl.pallas_call(
        functools.partial(_gla_kernel, n_chunks=tt // GLA_CHUNK),
        out_shape=(jax.ShapeDtypeStruct((2, t, GLA_V), F32),
                   jax.ShapeDtypeStruct((nb, 2, GLA_V, GLA_QK), F32)),
        grid=(nb, 2, nt),
        in_specs=[
            pl.BlockSpec((tt, GLA_QK), lambda b, d, i: (rows(b, d, i), 0)),
            pl.BlockSpec((tt, GLA_QK), lambda b, d, i: (rows(b, d, i), 0)),
            pl.BlockSpec((tt, GLA_V), lambda b, d, i: (rows(b, d, i), 0)),
            pl.BlockSpec((tt, GLA_QK), lambda b, d, i: (rows(b, d, i), d)),
            pl.BlockSpec((1, 1, GLA_V, GLA_QK), lambda b, d, i: (b, d, 0, 0)),
        ],
        out_specs=(pl.BlockSpec((1, tt, GLA_V), lambda b, d, i: (d, rows(b, d, i), 0)),
                   pl.BlockSpec((1, 1, GLA_V, GLA_QK), lambda b, d, i: (b, d, 0, 0))),
        scratch_shapes=[pltpu.VMEM((GLA_V, GLA_QK), F32)],
        compiler_params=_cparams("parallel", "parallel", "arbitrary"),
        name="gla_scan",
    )(q, k, v, g, s0)


def _half_masks(dtype):
    low = _lane_iota((1, LANES)) < HEAD_DIM
    return [low.astype(dtype), (~low).astype(dtype)]


def _rep(x, width):
    n = width // LANES
    return x if n == 1 else pltpu.repeat(x, n, axis=1)


def _glb_kernel(q_ref, k_ref, vt_ref, kc_ref, vtc_ref, o_ref, qt_ref, m_ref, acc_ref, s_ref, *,
                n_pairs, col_block, lookahead):
    tq = q_ref.shape[0]
    n_lat = vt_ref.shape[2]
    tk = vt_ref.shape[4]
    n_heads = 2 * n_pairs
    row_low = lax.broadcasted_iota(jnp.int32, (LANES, tq), 0) < HEAD_DIM
    for p in range(n_pairs):
        qt = (q_ref[:, p * LANES:(p + 1) * LANES].astype(F32) * LOG2_E).T
        qt_ref[2 * p] = jnp.where(row_low, qt, 0.0).astype(BF16)
        qt_ref[2 * p + 1] = jnp.where(row_low, 0.0, qt).astype(BF16)

    m_ref[...] = jnp.full(m_ref.shape, MASK_VALUE, F32)
    acc_ref[...] = jnp.zeros(acc_ref.shape, F32)

    units = [(h, cb) for h in range(n_heads) for cb in range(tq // col_block)]
    assert lookahead <= len(units)

    def scores(k, u):
        h, cb = units[u]
        return _dot(k, qt_ref[h, :, cb * col_block:(cb + 1) * col_block])

    def step(k, vt, k_next):
        w = k.shape[0]
        for u, (h, cb) in enumerate(units):
            s = s_ref[u, 0:w, :]
            if u + lookahead < len(units):
                s_ref[u + lookahead, 0:w, :] = scores(k, u + lookahead)
            elif k_next is not None:
                s_ref[u + lookahead - len(units), 0:k_next.shape[0], :] = scores(k_next, u + lookahead - len(units))
            cols = slice(cb * col_block, (cb + 1) * col_block)
            m_prev = m_ref[h, :, cols]
            m_new = jnp.maximum(m_prev, jnp.max(s, axis=0, keepdims=True))
            alpha = jnp.exp2(m_prev - m_new)
            e = jnp.exp2((s - m_new).astype(BF16))
            m_ref[h, :, cols] = m_new
            acc_ref[h, :, cols] = acc_ref[h, :, cols] * alpha + _dot(vt, e)

    def lat_tile(j):
        return k_ref[pl.ds(pl.multiple_of(j * tk, tk), tk), :]

    wc = vtc_ref.shape[4]
    ctx_tiles = [kc_ref[j * wc:(j + 1) * wc, :] for j in range(vtc_ref.shape[2])]

    for u in range(lookahead):
        s_ref[u, 0:tk, :] = scores(lat_tile(0), u)

    def body(j, carry):
        step(lat_tile(j), vt_ref[0, 0, j], lat_tile(j + 1))
        return carry

    lax.fori_loop(0, n_lat - 1, body, 0)
    step(lat_tile(n_lat - 1), vt_ref[0, 0, n_lat - 1], ctx_tiles[0])
    for j, kc in enumerate(ctx_tiles):
        step(kc, vtc_ref[0, 0, j], ctx_tiles[j + 1] if j + 1 < len(ctx_tiles) else None)

    for p in range(n_pairs):
        halves = [acc_ref[h, 0:HEAD_DIM, :] / acc_ref[h, HEAD_DIM:HEAD_DIM + 1, :] for h in (2 * p, 2 * p + 1)]
        o_ref[:, p * LANES:(p + 1) * LANES] = jnp.concatenate(halves, axis=0).T.astype(o_ref.dtype)


def _glb_attn(q, k, vt, kc, vtc, seq, ctx_len, tq_want=512, col_block=512, lookahead=4):
    t = q.shape[0]
    nb = t // seq
    groups = GLB_KV_HEADS
    gw = q.shape[1] // groups
    n_pairs = gw // LANES
    tq = _row_tile(seq, tq_want)
    nq = seq // tq
    col_block = min(col_block, tq)
    n_units = 2 * n_pairs * (tq // col_block)
    max_keys = max(vt.shape[4], vtc.shape[4])
    return pl.pallas_call(
        functools.partial(_glb_kernel, n_pairs=n_pairs, col_block=col_block, lookahead=lookahead),
        out_shape=jax.ShapeDtypeStruct(q.shape, BF16),
        grid=(nb, groups, nq),
        in_specs=[
            pl.BlockSpec((tq, gw), lambda b, g, i: (b * nq + i, g)),
            pl.BlockSpec((seq, LANES), lambda b, g, i: (b, g)),
            pl.BlockSpec((1, 1) + vt.shape[2:], lambda b, g, i: (b, g, 0, 0, 0)),
            pl.BlockSpec((ctx_len, LANES), lambda b, g, i: (b, g)),
            pl.BlockSpec((1, 1) + vtc.shape[2:], lambda b, g, i: (b, g, 0, 0, 0)),
        ],
        out_specs=pl.BlockSpec((tq, gw), lambda b, g, i: (b * nq + i, g)),
        scratch_shapes=[pltpu.VMEM((2 * n_pairs, LANES, tq), BF16),
                        pltpu.VMEM((2 * n_pairs, 1, tq), F32),
                        pltpu.VMEM((2 * n_pairs, VT_ROWS, tq), F32),
                        pltpu.VMEM((n_units, max_keys, col_block), F32)],
        compiler_params=_cparams("parallel", "parallel", "arbitrary"),
        name="glb_attn",
    )(q, k, vt, kc, vtc)


def _softmax_pair(q_pair, keys, values, masks, sink_pair, hm, low):
    out = None
    inv = []
    for half in range(2):
        qh = q_pair * hm[half]
        ss = []
        for kk, mk in zip(keys, masks):
            s = _dot_nt(qh, kk)
            ss.append(s if mk is None else jnp.where(mk, s, MASK_VALUE))
        m = ss[0].max(axis=1, keepdims=True)
        for s in ss[1:]:
            m = jnp.maximum(m, s.max(axis=1, keepdims=True))
        if sink_pair is not None:
            m = jnp.maximum(m, sink_pair[half])
        den = None
        for s, vv in zip(ss, values):
            e = jnp.exp(s - m)
            sm = jnp.sum(e, axis=1, keepdims=True)
            den = sm if den is None else den + sm
            part = _dot(e.astype(BF16), vv * hm[half])
            out = part if out is None else out + part
        if sink_pair is not None:
            den = den + jnp.exp(sink_pair[half] - m)
        inv.append(1.0 / den)
    return out * jnp.where(low, inv[0], inv[1])


def _win_kernel(sink_ref, q_ref, k_ref, v_ref, kc_ref, vc_ref, o_ref, *, n_pairs, span):
    g = pl.program_id(1)
    i = pl.program_id(2)
    tq = q_ref.shape[0]
    seq = k_ref.shape[0]
    hm = _half_masks(BF16)
    low = _lane_iota((1, LANES)) < HEAD_DIM
    q0 = i * tq
    start = jnp.clip(q0 - WINDOW, 0, seq - span)
    start = pl.multiple_of(start, LANES)
    k_loc = k_ref[pl.ds(start, span), :]
    v_loc = v_ref[pl.ds(start, span), :]
    qpos = q0 + lax.broadcasted_iota(jnp.int32, (tq, span), 0)
    kpos = start + lax.broadcasted_iota(jnp.int32, (tq, span), 1)
    in_win = jnp.abs(qpos - kpos) <= WINDOW
    kc = kc_ref[...]
    vc = vc_ref[...]
    for p in range(n_pairs):
        heads = (g * n_pairs + p) * 2
        sink_pair = (sink_ref[heads], sink_ref[heads + 1])
        o = _softmax_pair(q_ref[:, p * LANES:(p + 1) * LANES], [k_loc, kc], [v_loc, vc], [in_win, None],
                          sink_pair, hm, low)
        o_ref[:, p * LANES:(p + 1) * LANES] = o.astype(o_ref.dtype)


def _win_attn(q, k, v, kc, vc, sink, seq, ctx_len, tq_want=256):
    t = q.shape[0]
    nb = t // seq
    groups = WIN_KV_HEADS
    gw = q.shape[1] // groups
    n_pairs = gw // LANES
    tq = _row_tile(seq, tq_want)
    span = tq + 2 * WINDOW
    assert seq >= span
    nq = seq // tq
    return pl.pallas_call(
        functools.partial(_win_kernel, n_pairs=n_pairs, span=span),
        out_shape=jax.ShapeDtypeStruct(q.shape, BF16),
        grid=(nb, groups, nq),
        in_specs=[
            pl.BlockSpec(memory_space=pltpu.SMEM),
            pl.BlockSpec((tq, gw), lambda b, g, i: (b * nq + i, g)),
            pl.BlockSpec((seq, LANES), lambda b, g, i: (b, g)),
            pl.BlockSpec((seq, LANES), lambda b, g, i: (b, g)),
            pl.BlockSpec((ctx_len, LANES), lambda b, g, i: (b, g)),
            pl.BlockSpec((ctx_len, LANES), lambda b, g, i: (b, g)),
        ],
        out_specs=pl.BlockSpec((tq, gw), lambda b, g, i: (b * nq + i, g)),
        compiler_params=_cparams("parallel", "parallel", "arbitrary"),
        name="win_attn",
    )(sink, q, k, v, kc, vc)


def _ctx_attn_kernel(sink_ref, q_ref, k_ref, v_ref, o_ref, *, n_pairs, use_sink):
    g = pl.program_id(1)
    hm = _half_masks(BF16)
    low = _lane_iota((1, LANES)) < HEAD_DIM
    k = k_ref[...]
    v = v_ref[...]
    for p in range(n_pairs):
        heads = (g * n_pairs + p) * 2
        sink_pair = (sink_ref[heads], sink_ref[heads + 1]) if use_sink else None
        o = _softmax_pair(q_ref[:, p * LANES:(p + 1) * LANES], [k], [v], [None], sink_pair, hm, low)
        o_ref[:, p * LANES:(p + 1) * LANES] = o.astype(o_ref.dtype)


def _ctx_attn(q, k, v, sink, groups, ctx_len, use_sink):
    t = q.shape[0]
    nb = t // ctx_len
    gw = q.shape[1] // groups
    n_pairs = gw // LANES
    return pl.pallas_call(
        functools.partial(_ctx_attn_kernel, n_pairs=n_pairs, use_sink=use_sink),
        out_shape=jax.ShapeDtypeStruct(q.shape, BF16),
        grid=(nb, groups),
        in_specs=[
            pl.BlockSpec(memory_space=pltpu.SMEM),
            pl.BlockSpec((ctx_len, gw), lambda b, g: (b, g)),
            pl.BlockSpec((ctx_len, LANES), lambda b, g: (b, g)),
            pl.BlockSpec((ctx_len, LANES), lambda b, g: (b, g)),
        ],
        out_specs=pl.BlockSpec((ctx_len, gw), lambda b, g: (b, g)),
        compiler_params=_cparams("parallel", "parallel"),
        name="ctx_attn",
    )(sink, q, k, v)


def _outproj_kernel(x_ref, mod_ref, of_ref, ob_ref, r_ref, og_ref, ow_ref, gain_ref, bd_ref, w_ref, o_ref):
    o = of_ref[0] + ob_ref[0]
    r = r_ref[...].astype(F32)
    gate = r * jax.nn.sigmoid(r)
    ss = _dot((o * o).astype(BF16), bd_ref[...])
    on = o * lax.rsqrt(ss * (1.0 / GLA_DV) + EPS) * gain_ref[...]
    cat = jnp.concatenate([(on * gate).astype(BF16), og_ref[...], ow_ref[...]], axis=-1)
    y = _dot(cat, w_ref[...])
    o_ref[...] = x_ref[...] + mod_ref[0, 5:6, :] * y


def _outproj(x, mods, o_gla, r, o_glb, o_win, gla_gain, bd, w_out, rows_per_batch, tm_want=512):
    t, d = x.shape
    tm = _row_tile(rows_per_batch, tm_want)
    return pl.pallas_call(
        _outproj_kernel,
        out_shape=jax.ShapeDtypeStruct((t, d), F32),
        grid=(t // tm,),
        in_specs=[
            pl.BlockSpec((tm, d), lambda i: (i, 0)),
            _mod_spec(mods, rows_per_batch // tm),
            pl.BlockSpec((1, tm, GLA_V), lambda i: (0, i, 0)),
            pl.BlockSpec((1, tm, GLA_V), lambda i: (1, i, 0)),
            pl.BlockSpec((tm, GLA_V), lambda i: (i, 0)),
            pl.BlockSpec((tm, o_glb.shape[1]), lambda i: (i, 0)),
            pl.BlockSpec((tm, o_win.shape[1]), lambda i: (i, 0)),
            _const_spec(gla_gain.shape),
            _const_spec(bd.shape),
            _const_spec(w_out.shape),
        ],
        out_specs=pl.BlockSpec((tm, d), lambda i: (i, 0)),
        compiler_params=_cparams("parallel"),
        name="mix_outproj",
    )(x, mods, o_gla, o_gla, r, o_glb, o_win, gla_gain, bd, w_out)


def _rope_tables(seq):
    rows = seq // GRID_W
    row = jnp.repeat(jnp.arange(rows, dtype=F32), GRID_W)
    col = (jnp.arange(rows * GRID_W) % GRID_W).astype(F32)
    n_freq = HEAD_DIM // 4
    inv = jnp.power(ROPE_BASE, -jnp.arange(n_freq, dtype=F32) / n_freq)
    ang = jnp.concatenate([row[:, None] * inv, col[:, None] * inv], axis=-1)
    cos, sin = jnp.cos(ang), jnp.sin(ang)
    cos_t = jnp.concatenate([cos, cos] * (LANES // HEAD_DIM), axis=-1)
    sin_t = jnp.concatenate([-sin, sin] * (LANES // HEAD_DIM), axis=-1)
    return cos_t, sin_t


def _block_diag_ones(n, block):
    idx = np.arange(n) // block
    return jnp.asarray(idx[:, None] == idx[None, :], dtype=BF16)


def kernel(x, c, ctx, c_ctx, mod_w, mod_b, norm_ffn1, ffn1_w_in, ffn1_w_out, norm_mix, mix_w_in, mix_w_out,
           gla_wg_f, gla_bg_f, gla_wg_b, gla_bg_b, gla_out_norm, glb_q_norm, glb_k_norm,
           win_q_norm, win_k_norm, win_sink, norm_ffn2, ffn2_w_in, ffn2_w_out):
    bsz, seq, d = x.shape
    ctx_len = ctx.shape[1]
    depth = mod_w.shape[0]
    in_splits = (GLA_QK, GLA_QK, GLA_V, GLA_V, 2 * GLA_GATE_RANK,
                 GLB_HEADS * HEAD_DIM, GLB_KV_HEADS * HEAD_DIM, GLB_KV_HEADS * HEAD_DIM,
                 WIN_HEADS * HEAD_DIM, WIN_KV_HEADS * HEAD_DIM, WIN_KV_HEADS * HEAD_DIM)

    n_rows = -(-(bsz + 1) // 8) * 8
    c_rows = jnp.concatenate([c, c_ctx[None, :], jnp.zeros((n_rows - bsz - 1, d), F32)], axis=0)
    mods = _modvec(c_rows, mod_w, mod_b)

    cos_l, sin_l = _rope_tables(seq)
    cos_c = jnp.ones((ctx_len, LANES), F32)
    sin_c = jnp.zeros((ctx_len, LANES), F32)
    bd = _block_diag_ones(SLAB, HEAD_DIM)
    zero_state = jnp.zeros((bsz, 2, GLA_V, GLA_QK), F32)

    xl = x.reshape(bsz * seq, d)
    xc = ctx.reshape(bsz * ctx_len, d)

    for l in range(depth):
        need_ctx = l < depth - 1
        mods_l = mods[l, :bsz].reshape(bsz, N_MOD, d)
        mods_c = mods[l, bsz:bsz + 1].reshape(1, N_MOD, d)
        w1_in, w1_out = ffn1_w_in[l].astype(BF16), ffn1_w_out[l].astype(BF16)
        w2_in, w2_out = ffn2_w_in[l].astype(BF16), ffn2_w_out[l].astype(BF16)
        wm = mix_w_in[l]
        src = dict(zip(("aq", "ak", "av", "ar", "ad", "gq", "gk", "gv", "wq", "wk", "wv"),
                       jnp.split(wm, np.cumsum(in_splits)[:-1], axis=1)))
        w_mix = jnp.concatenate([src[n] for n in ("aq", "ak", "av", "ar", "gq", "gk", "wk", "gv", "wv", "wq", "ad")]
                                + [jnp.zeros((d, GATE_PAD - 2 * GLA_GATE_RANK), F32)], axis=1).astype(BF16)
        assert w_mix.shape[1] == _C_END
        wg = jnp.zeros((GATE_PAD, 2 * GLA_QK), F32)
        wg = wg.at[:GLA_GATE_RANK, :GLA_QK].set(gla_wg_f[l])
        wg = wg.at[GLA_GATE_RANK:2 * GLA_GATE_RANK, GLA_QK:].set(gla_wg_b[l]).astype(BF16)
        bg = jnp.concatenate([gla_bg_f[l], gla_bg_b[l]])[None, :]
        qk_gains = jnp.stack([jnp.tile(gn[l], LANES // HEAD_DIM)
                              for gn in (glb_q_norm, glb_k_norm, win_q_norm, win_k_norm)])
        gla_gain = jnp.tile(gla_out_norm[l], GLA_HEADS)[None, :]
        w_out = mix_w_out[l].astype(BF16)
        sink = win_sink[l]

        xl = _ffn(xl, mods_l, 0, norm_ffn1[l], w1_in, w1_out, seq)
        xc = _ffn(xc, mods_c, 0, norm_ffn1[l], w1_in, w1_out, ctx_len)

        pc = _inproj(xc, mods_c, norm_mix[l], w_mix, wg, bg, qk_gains, cos_c, sin_c, bd, ctx_len)
        pq = _inproj(xl, mods_l, norm_mix[l], w_mix, wg, bg, qk_gains, cos_l, sin_l, bd, seq)
        aq, ak, av, ar, ag, gq, gk, gv, wq, wk, wv, gvt = pq
        aqc, akc, avc, arc, agc, gqc, gkc, gvc, wqc, wkc, wvc, gvtc = pc

        oc_gla, states = _gla(aqc, akc, avc, agc, zero_state, ctx_len)
        o_gla, _ = _gla(aq, ak, av, ag, states, seq)
        o_glb = _glb_attn(gq, gk, gvt, gkc, gvtc, seq, ctx_len)
        o_win = _win_attn(wq, wk, wv, wkc, wvc, sink, seq, ctx_len)
        xl = _outproj(xl, mods_l, o_gla, ar, o_glb, o_win, gla_gain, bd, w_out, seq)

        xl = _ffn(xl, mods_l, 6, norm_ffn2[l], w2_in, w2_out, seq)

        if need_ctx:
            oc_glb = _ctx_attn(gqc, gkc, gvc, sink, GLB_KV_HEADS, ctx_len, use_sink=False)
            oc_win = _ctx_attn(wqc, wkc, wvc, sink, WIN_KV_HEADS, ctx_len, use_sink=True)
            xc = _outproj(xc, mods_c, oc_gla, arc, oc_glb, oc_win, gla_gain, bd, w_out, ctx_len)
            xc = _ffn(xc, mods_c, 6, norm_ffn2[l], w2_in, w2_out, ctx_len)

    return xl.reshape(bsz, seq, d)
```

```python
import functools

import numpy as np
import jax
import jax.numpy as jnp
from jax import lax
from jax.experimental import pallas as pl
from jax.experimental.pallas import tpu as pltpu

GRID_W = 64
HEAD_DIM = 64
GLA_HEADS = 4
GLA_DK = 32
GLA_DV = 64
GLA_GATE_RANK = 16
GLA_GATE_TAU = 16.0
GLA_CHUNK = 64
GLB_HEADS = 8
GLB_KV_HEADS = 2
WIN_HEADS = 4
WIN_KV_HEADS = 2
WINDOW = 128
ROPE_BASE = 10000.0
N_MOD = 9
EPS = 1e-6

LANES = 128
VMEM_LIMIT_BYTES = 56 * 1024 * 1024

GLA_QK = GLA_HEADS * GLA_DK
GLA_V = GLA_HEADS * GLA_DV
GATE_PAD = LANES
MASK_VALUE = -1e30
LOG2_E = 1.4426950408889634
KEY_CHUNK = 512
GLA_GROUP = 4
WIN_CHUNK = 128
VT_ROWS = HEAD_DIM + 16

BF16 = jnp.bfloat16
F32 = jnp.float32


def _cparams(*sem):
    return pltpu.CompilerParams(dimension_semantics=sem, vmem_limit_bytes=VMEM_LIMIT_BYTES)


def _dot(a, b):
    return jnp.dot(a, b, preferred_element_type=F32)


def _dot_nt(a, b):
    return lax.dot_general(a, b, (((1,), (1,)), ((), ())), preferred_element_type=F32)


def _dot_tn(a, b):
    return lax.dot_general(a, b, (((0,), (0,)), ((), ())), preferred_element_type=F32)


def _lane_iota(shape):
    return lax.broadcasted_iota(jnp.int32, shape, len(shape) - 1)


def _modvec_kernel(c_ref, w_ref, b_ref, o_ref):
    c = c_ref[...]
    s = (c * jax.nn.sigmoid(c)).astype(BF16)
    o_ref[0] = _dot(s, w_ref[0].astype(BF16)) + b_ref[0]


def _modvec(c_rows, mod_w, mod_b):
    depth, d, n = mod_w.shape
    rows = c_rows.shape[0]
    tn = n // N_MOD
    return pl.pallas_call(
        _modvec_kernel,
        out_shape=jax.ShapeDtypeStruct((depth, rows, n), F32),
        grid=(depth, n // tn),
        in_specs=[
            pl.BlockSpec((rows, d), lambda l, j: (0, 0)),
            pl.BlockSpec((1, d, tn), lambda l, j: (l, 0, j)),
            pl.BlockSpec((1, 1, tn), lambda l, j: (l, 0, j)),
        ],
        out_specs=pl.BlockSpec((1, rows, tn), lambda l, j: (l, 0, j)),
        compiler_params=_cparams("arbitrary", "arbitrary"),
        name="modvec",
    )(c_rows, mod_w, mod_b.reshape(depth, 1, n))


def _norm_modulate(x, gain, mod_ref, k_shift):
    shift = mod_ref[0, k_shift:k_shift + 1, :]
    scale = mod_ref[0, k_shift + 1:k_shift + 2, :]
    y = x * lax.rsqrt(jnp.mean(x * x, axis=-1, keepdims=True) + EPS)
    return (y * gain) * (1.0 + scale) + shift


def _mod_spec(mods, tiles_per_batch):
    nb, nm, d = mods.shape
    if nb == 1:
        return pl.BlockSpec((1, nm, d), lambda i: (0, 0, 0))
    return pl.BlockSpec((1, nm, d), lambda i: (i // tiles_per_batch, 0, 0))


def _const_spec(shape):
    nd = len(shape)
    return pl.BlockSpec(shape, lambda *_: (0,) * nd, pipeline_mode=pl.Buffered(1))


def _row_tile(n_rows_per_batch, want):
    t = min(want, n_rows_per_batch)
    assert n_rows_per_batch % t == 0
    return t


def _ffn_chunks(f):
    step = 1536 if f > 1536 else f
    return tuple((lo, min(lo + step, f)) for lo in range(0, f, step))


def _ffn_kernel(x_ref, mod_ref, g_ref, win_ref, wout_ref, o_ref, *, k0, f):
    x = x_ref[...]
    hb = _norm_modulate(x, g_ref[...], mod_ref, k0).astype(BF16)
    gate = mod_ref[0, k0 + 2:k0 + 3, :]
    acc = None
    for lo, hi in _ffn_chunks(f):
        a = _dot(hb, win_ref[:, lo:hi])
        b = _dot(hb, win_ref[:, f + lo:f + hi])
        u = ((a * jax.nn.sigmoid(a)) * b).astype(BF16)
        part = _dot(u, wout_ref[lo:hi, :])
        acc = part if acc is None else acc + part
    o_ref[...] = x + (0.5 * gate) * acc


def _ffn(x, mods, k0, gain, w_in, w_out, rows_per_batch, tm_want=512):
    t, d = x.shape
    f = w_out.shape[0]
    tm = _row_tile(rows_per_batch, tm_want)
    return pl.pallas_call(
        functools.partial(_ffn_kernel, k0=k0, f=f),
        out_shape=jax.ShapeDtypeStruct((t, d), F32),
        grid=(t // tm,),
        in_specs=[
            pl.BlockSpec((tm, d), lambda i: (i, 0)),
            _mod_spec(mods, rows_per_batch // tm),
            _const_spec((1, d)),
            _const_spec((d, 2 * f)),
            _const_spec((f, d)),
        ],
        out_specs=pl.BlockSpec((tm, d), lambda i: (i, 0)),
        compiler_params=_cparams("parallel"),
        name="ffn",
    )(x, mods, gain.reshape(1, d), w_in, w_out)


_C_AQ = 0
_C_AK = _C_AQ + GLA_QK
_C_AV = _C_AK + GLA_QK
_C_AR = _C_AV + GLA_V
_C_GQ = _C_AR + GLA_V
_C_GK = _C_GQ + GLB_HEADS * HEAD_DIM
_C_WK = _C_GK + GLB_KV_HEADS * HEAD_DIM
_C_GV = _C_WK + WIN_KV_HEADS * HEAD_DIM
_C_WV = _C_GV + GLB_KV_HEADS * HEAD_DIM
_C_WQ = _C_WV + WIN_KV_HEADS * HEAD_DIM
_C_AD = _C_WQ + WIN_HEADS * HEAD_DIM
_C_END = _C_AD + GATE_PAD
SLAB = 2 * LANES


def _norm_rope_slab(x, bd_ref, gains, cos, sin_signed, out_scales):
    ss = _dot((x * x).astype(BF16), bd_ref[...])
    first_half = (_lane_iota((x.shape[0], LANES)) % HEAD_DIM) < (HEAD_DIM // 2)
    outs = []
    for j in range(2):
        sl = slice(j * LANES, (j + 1) * LANES)
        xn = x[:, sl] * lax.rsqrt(ss[:, sl] * (1.0 / HEAD_DIM) + EPS) * gains[j]
        partner = jnp.where(first_half, pltpu.roll(xn, LANES - HEAD_DIM // 2, 1), pltpu.roll(xn, HEAD_DIM // 2, 1))
        out = xn * cos + partner * sin_signed
        outs.append(out * out_scales[j] if out_scales[j] != 1.0 else out)
    return outs


def _dup_heads(x):
    swapped = pltpu.roll(x, HEAD_DIM, 1)
    low = _lane_iota(x.shape) < HEAD_DIM
    return jnp.where(low, x, swapped), jnp.where(low, swapped, x)


def _store_vt(vt_ref, v):
    chunk = vt_ref.shape[4]
    for c in range(vt_ref.shape[2]):
        vt = v[c * chunk:(c + 1) * chunk, :].T
        for kv in range(vt_ref.shape[1]):
            vt_ref[0, kv, c, 0:HEAD_DIM, :] = vt[kv * HEAD_DIM:(kv + 1) * HEAD_DIM].astype(BF16)
            vt_ref[0, kv, c, HEAD_DIM:VT_ROWS, :] = jnp.ones((VT_ROWS - HEAD_DIM, chunk), BF16)


def _inproj_kernel(x_ref, mod_ref, g_ref, w_ref, wg_ref, bg_ref, qkg_ref, cos_ref, sin_ref, bd_ref,
                   aq_ref, ak_ref, av_ref, ar_ref, ag_ref, gq_ref, gk_ref, gv_ref, wq_ref, wk_ref, wv_ref,
                   gvt_ref, wvt_ref):
    hb = _norm_modulate(x_ref[...], g_ref[...], mod_ref, 3).astype(BF16)
    cos = cos_ref[...]
    sin = sin_ref[...]
    p_all = _dot(hb, w_ref[...])

    def proj(lo, width):
        return p_all[:, lo:lo + width]

    aq_ref[...] = (proj(_C_AQ, GLA_QK) * (GLA_DK ** -0.5)).astype(BF16)
    ak_ref[...] = proj(_C_AK, GLA_QK).astype(BF16)
    av_ref[...] = proj(_C_AV, GLA_V).astype(BF16)
    ar_ref[...] = proj(_C_AR, GLA_V).astype(BF16)
    z = _dot(proj(_C_AD, GATE_PAD).astype(BF16), wg_ref[...]) + bg_ref[...]
    log_sig = jnp.minimum(z, 0.0) - jnp.log1p(jnp.exp(-jnp.abs(z)))
    ag_ref[...] = log_sig * (1.0 / GLA_GATE_TAU)

    q_scale = HEAD_DIM ** -0.5
    g_glb_q, g_glb_k, g_win_q, g_win_k = (qkg_ref[r:r + 1, :] for r in range(4))

    for j in range(GLB_HEADS * HEAD_DIM // SLAB):
        halves = _norm_rope_slab(proj(_C_GQ + j * SLAB, SLAB), bd_ref, (g_glb_q, g_glb_q), cos, sin,
                                 (q_scale, q_scale))
        for i, q in enumerate(halves):
            gq_ref[:, j * SLAB + i * LANES:j * SLAB + (i + 1) * LANES] = q.astype(BF16)
    halves = _norm_rope_slab(proj(_C_WQ, SLAB), bd_ref, (g_win_q, g_win_q), cos, sin, (q_scale, q_scale))
    for i, q in enumerate(halves):
        wq_ref[:, i * LANES:(i + 1) * LANES] = q.astype(BF16)
    k_glb, k_win = _norm_rope_slab(proj(_C_GK, SLAB), bd_ref, (g_glb_k, g_win_k), cos, sin, (1.0, 1.0))
    k0, k1 = _dup_heads(k_glb)
    gk_ref[:, 0:LANES] = k0.astype(BF16)
    gk_ref[:, LANES:2 * LANES] = k1.astype(BF16)
    k0, k1 = _dup_heads(k_win)
    wk_ref[:, 0:LANES] = k0.astype(BF16)
    wk_ref[:, LANES:2 * LANES] = k1.astype(BF16)

    v = proj(_C_GV, LANES)
    v0, v1 = _dup_heads(v)
    gv_ref[:, 0:LANES] = v0.astype(BF16)
    gv_ref[:, LANES:2 * LANES] = v1.astype(BF16)
    _store_vt(gvt_ref, v)

    v = proj(_C_WV, LANES)
    v0, v1 = _dup_heads(v)
    wv_ref[:, 0:LANES] = v0.astype(BF16)
    wv_ref[:, LANES:2 * LANES] = v1.astype(BF16)
    _store_vt(wvt_ref, v)


def _inproj(x, mods, gain, w, wg, bg, qk_gains, cos, sin, bd, rows_per_batch, tm_want=512):
    t, d = x.shape
    tm = _row_tile(rows_per_batch, tm_want)
    tpb = rows_per_batch // tm
    widths = (GLA_QK, GLA_QK, GLA_V, GLA_V, 2 * GLA_QK, GLB_HEADS * HEAD_DIM, 2 * LANES, 2 * LANES,
              WIN_HEADS * HEAD_DIM, 2 * LANES, 2 * LANES)
    dtypes = (BF16, BF16, BF16, BF16, F32, BF16, BF16, BF16, BF16, BF16, BF16)
    nb = t // rows_per_batch
    vt_shapes, vt_specs = [], []
    for kv_heads, want in ((GLB_KV_HEADS, KEY_CHUNK), (WIN_KV_HEADS, WIN_CHUNK)):
        chunk = min(want, tm)
        vt_shapes.append(jax.ShapeDtypeStruct((nb, kv_heads, rows_per_batch // chunk, VT_ROWS, chunk), BF16))
        vt_specs.append(pl.BlockSpec((1, kv_heads, tm // chunk, VT_ROWS, chunk),
                                     lambda i: (i // tpb, 0, i % tpb, 0, 0)))
    return pl.pallas_call(
        _inproj_kernel,
        out_shape=tuple(jax.ShapeDtypeStruct((t, wd), dt) for wd, dt in zip(widths, dtypes)) + tuple(vt_shapes),
        grid=(t // tm,),
        in_specs=[
            pl.BlockSpec((tm, d), lambda i: (i, 0)),
            _mod_spec(mods, tpb),
            _const_spec((1, d)),
            _const_spec(w.shape),
            _const_spec(wg.shape),
            _const_spec(bg.shape),
            _const_spec(qk_gains.shape),
            pl.BlockSpec((tm, LANES), lambda i: (i % tpb, 0)),
            pl.BlockSpec((tm, LANES), lambda i: (i % tpb, 0)),
            _const_spec(bd.shape),
        ],
        out_specs=tuple(pl.BlockSpec((tm, wd), lambda i: (i, 0)) for wd in widths) + tuple(vt_specs),
        compiler_params=_cparams("parallel"),
        name="mix_inproj",
    )(x, mods, gain.reshape(1, d), w, wg, bg, qk_gains, cos, sin, bd)


def _gla_kernel(q_ref, k_ref, v_ref, g_ref, s0_ref, o_ref, sfin_ref, st_ref, *, n_chunks):
    c_len = GLA_CHUNK
    d = pl.program_id(1)
    i = pl.program_id(2)
    sign = 1 - 2 * d

    @pl.when(i == 0)
    def _():
        st_ref[...] = s0_ref[0, 0]

    group = min(GLA_GROUP, n_chunks)
    g_len = group * c_len
    row = lax.broadcasted_iota(jnp.int32, (g_len, g_len), 0)
    col = lax.broadcasted_iota(jnp.int32, (g_len, g_len), 1)
    keep = ((row - col) * sign >= 0) & (row // c_len == col // c_len)
    cum_op = jnp.where(keep, 1.0, 0.0).astype(BF16)
    row4 = lax.broadcasted_iota(jnp.int32, (c_len, GLA_HEADS * c_len), 0)
    col4 = lax.broadcasted_iota(jnp.int32, (c_len, GLA_HEADS * c_len), 1) % c_len
    keep4 = (row4 - col4) * sign >= 0
    qk_lane_head = _lane_iota((1, GLA_QK)) // GLA_DK
    v_lane_head = _lane_iota((1, GLA_V)) // GLA_DV
    qk_head_mask = [(qk_lane_head == h).astype(F32) for h in range(GLA_HEADS)]
    v_head_mask = [(v_lane_head == h).astype(BF16) for h in range(GLA_HEADS)]
    st_row_head = lax.broadcasted_iota(jnp.int32, (GLA_V, GLA_QK), 0) // GLA_DV
    st_col_head = lax.broadcasted_iota(jnp.int32, (GLA_V, GLA_QK), 1) // GLA_DK
    st_mask = st_row_head == st_col_head

    def chunk_group(j, carry):
        starts = []
        for p in range(group):
            c = j * group + p
            c = c + d * (n_chunks - 1 - 2 * c)
            starts.append(pl.multiple_of(c * c_len, c_len))
        q = jnp.concatenate([q_ref[pl.ds(r, c_len), :] for r in starts], axis=0).astype(F32)
        k = jnp.concatenate([k_ref[pl.ds(r, c_len), :] for r in starts], axis=0).astype(F32)
        g = jnp.concatenate([g_ref[pl.ds(r, c_len), :] for r in starts], axis=0)
        vs = [v_ref[pl.ds(r, c_len), :] for r in starts]
        g_hi = g.astype(BF16)
        r1 = g - g_hi.astype(F32)
        g_mid = r1.astype(BF16)
        g_lo = (r1 - g_mid.astype(F32)).astype(BF16)
        b = _dot(cum_op, g_hi) + _dot(cum_op, g_mid) + _dot(cum_op, g_lo)
        q_in = (q * jnp.exp(b)).astype(BF16)
        k_out = k * jnp.exp(-b)
        st = st_ref[...]
        for p in range(group):
            rows = slice(p * c_len, (p + 1) * c_len)
            decay = jnp.exp(jnp.sum(g[rows], axis=0, keepdims=True))
            k_out_p = k_out[rows]
            k_dec = (k_out_p * decay).astype(BF16)
            k_stack = jnp.concatenate([(k_out_p * qk_head_mask[h]).astype(BF16) for h in range(GLA_HEADS)], axis=0)
            a = _dot_nt(q_in[rows], k_stack)
            a = jnp.where(keep4, a, 0.0).astype(BF16)
            v_bd = jnp.concatenate([vs[p] * v_head_mask[h] for h in range(GLA_HEADS)], axis=0)
            o = _dot(a, v_bd) + _dot_nt(q_in[rows], st.astype(BF16))
            o_ref[0, pl.ds(starts[p], c_len), :] = o
            ds_t = _dot_tn(vs[p], k_dec)
            st = st * decay + jnp.where(st_mask, ds_t, 0.0)
        st_ref[...] = st
        return carry

    lax.fori_loop(0, n_chunks // group, chunk_group, 0)

    @pl.when(i == pl.num_programs(2) - 1)
    def _():
        sfin_ref[0, 0] = st_ref[...]


def _gla(q, k, v, g, s0, rows_per_batch, tt_want=1024):
    t = q.shape[0]
    nb = t // rows_per_batch
    tt = _row_tile(rows_per_batch, tt_want)
    nt = rows_per_batch // tt

    def rows(b, d, i):
        return b * nt + i + d * (nt - 1 - 2 * i)

    return pl.pallas_call(
        functools.partial(_gla_kernel, n_chunks=tt // GLA_CHUNK),
        out_shape=(jax.ShapeDtypeStruct((2, t, GLA_V), F32),
                   jax.ShapeDtypeStruct((nb, 2, GLA_V, GLA_QK), F32)),
        grid=(nb, 2, nt),
        in_specs=[
            pl.BlockSpec((tt, GLA_QK), lambda b, d, i: (rows(b, d, i), 0)),
            pl.BlockSpec((tt, GLA_QK), lambda b, d, i: (rows(b, d, i), 0)),
            pl.BlockSpec((tt, GLA_V), lambda b, d, i: (rows(b, d, i), 0)),
            pl.BlockSpec((tt, GLA_QK), lambda b, d, i: (rows(b, d, i), d)),
            pl.BlockSpec((1, 1, GLA_V, GLA_QK), lambda b, d, i: (b, d, 0, 0)),
        ],
        out_specs=(pl.BlockSpec((1, tt, GLA_V), lambda b, d, i: (d, rows(b, d, i), 0)),
                   pl.BlockSpec((1, 1, GLA_V, GLA_QK), lambda b, d, i: (b, d, 0, 0))),
        scratch_shapes=[pltpu.VMEM((GLA_V, GLA_QK), F32)],
        compiler_params=_cparams("parallel", "parallel", "arbitrary"),
        name="gla_scan",
    )(q, k, v, g, s0)


def _half_masks(dtype):
    low = _lane_iota((1, LANES)) < HEAD_DIM
    return [low.astype(dtype), (~low).astype(dtype)]


def _rep(x, width):
    n = width // LANES
    return x if n == 1 else pltpu.repeat(x, n, axis=1)


def _glb_kernel(q_ref, k_ref, vt_ref, kc_ref, vtc_ref, o_ref, qt_ref, m_ref, acc_ref, s_ref, *,
                n_pairs, col_block, lookahead, chunks_per_tile):
    tq = q_ref.shape[0]
    n_lat = vt_ref.shape[2] // chunks_per_tile
    tk = vt_ref.shape[4] * chunks_per_tile
    n_heads = 2 * n_pairs
    row_low = lax.broadcasted_iota(jnp.int32, (LANES, tq), 0) < HEAD_DIM
    for p in range(n_pairs):
        qt = (q_ref[:, p * LANES:(p + 1) * LANES].astype(F32) * LOG2_E).T
        qt_ref[2 * p] = jnp.where(row_low, qt, 0.0).astype(BF16)
        qt_ref[2 * p + 1] = jnp.where(row_low, 0.0, qt).astype(BF16)

    m_ref[...] = jnp.full(m_ref.shape, MASK_VALUE, F32)
    acc_ref[...] = jnp.zeros(acc_ref.shape, F32)

    units = [(h, cb) for h in range(n_heads) for cb in range(tq // col_block)]
    assert lookahead <= len(units)

    def scores(k, u):
        h, cb = units[u]
        return _dot(k, qt_ref[h, :, cb * col_block:(cb + 1) * col_block])

    def step(k, vt, k_next):
        w = k.shape[0]
        for u, (h, cb) in enumerate(units):
            s = s_ref[u, 0:w, :]
            if u + lookahead < len(units):
                s_ref[u + lookahead, 0:w, :] = scores(k, u + lookahead)
            elif k_next is not None:
                s_ref[u + lookahead - len(units), 0:k_next.shape[0], :] = scores(k_next, u + lookahead - len(units))
            cols = slice(cb * col_block, (cb + 1) * col_block)
            m_prev = m_ref[h, :, cols]
            m_new = jnp.maximum(m_prev, jnp.max(s, axis=0, keepdims=True))
            alpha = jnp.exp2(m_prev - m_new)
            e = jnp.exp2((s - m_new).astype(BF16))
            m_ref[h, :, cols] = m_new
            ch = w // len(vt)
            pv = _dot(vt[0], e[0:ch, :])
            for c in range(1, len(vt)):
                pv = pv + _dot(vt[c], e[c * ch:(c + 1) * ch, :])
            acc_ref[h, :, cols] = acc_ref[h, :, cols] * alpha + pv

    def lat_tile(j):
        return k_ref[pl.ds(pl.multiple_of(j * tk, tk), tk), :]

    def lat_vt(j):
        return [vt_ref[0, 0, j * chunks_per_tile + c] for c in range(chunks_per_tile)]

    wc = vtc_ref.shape[4]
    ctx_tiles = [kc_ref[j * wc:(j + 1) * wc, :] for j in range(vtc_ref.shape[2])]

    for u in range(lookahead):
        s_ref[u, 0:tk, :] = scores(lat_tile(0), u)

    def body(j, carry):
        step(lat_tile(j), lat_vt(j), lat_tile(j + 1))
        return carry

    lax.fori_loop(0, n_lat - 1, body, 0)
    step(lat_tile(n_lat - 1), lat_vt(n_lat - 1), ctx_tiles[0])
    for j, kc in enumerate(ctx_tiles):
        step(kc, [vtc_ref[0, 0, j]], ctx_tiles[j + 1] if j + 1 < len(ctx_tiles) else None)

    for p in range(n_pairs):
        halves = [acc_ref[h, 0:HEAD_DIM, :] / acc_ref[h, HEAD_DIM:HEAD_DIM + 1, :] for h in (2 * p, 2 * p + 1)]
        o_ref[:, p * LANES:(p + 1) * LANES] = jnp.concatenate(halves, axis=0).T.astype(o_ref.dtype)


def _glb_attn(q, k, vt, kc, vtc, seq, ctx_len, tq_want=512, col_block=512, lookahead=4, tk_want=512):
    t = q.shape[0]
    nb = t // seq
    groups = GLB_KV_HEADS
    gw = q.shape[1] // groups
    n_pairs = gw // LANES
    tq = _row_tile(seq, tq_want)
    nq = seq // tq
    col_block = min(col_block, tq)
    n_units = 2 * n_pairs * (tq // col_block)
    chunks_per_tile = max(1, min(tk_want, seq) // vt.shape[4])
    assert vt.shape[2] % chunks_per_tile == 0
    max_keys = max(vt.shape[4] * chunks_per_tile, vtc.shape[4])
    return pl.pallas_call(
        functools.partial(_glb_kernel, n_pairs=n_pairs, col_block=col_block, lookahead=lookahead,
                          chunks_per_tile=chunks_per_tile),
        out_shape=jax.ShapeDtypeStruct(q.shape, BF16),
        grid=(nb, groups, nq),
        in_specs=[
            pl.BlockSpec((tq, gw), lambda b, g, i: (b * nq + i, g)),
            pl.BlockSpec((seq, LANES), lambda b, g, i: (b, g)),
            pl.BlockSpec((1, 1) + vt.shape[2:], lambda b, g, i: (b, g, 0, 0, 0)),
            pl.BlockSpec((ctx_len, LANES), lambda b, g, i: (b, g)),
            pl.BlockSpec((1, 1) + vtc.shape[2:], lambda b, g, i: (b, g, 0, 0, 0)),
        ],
        out_specs=pl.BlockSpec((tq, gw), lambda b, g, i: (b * nq + i, g)),
        scratch_shapes=[pltpu.VMEM((2 * n_pairs, LANES, tq), BF16),
                        pltpu.VMEM((2 * n_pairs, 1, tq), F32),
                        pltpu.VMEM((2 * n_pairs, VT_ROWS, tq), F32),
                        pltpu.VMEM((n_units, max_keys, col_block), F32)],
        compiler_params=_cparams("parallel", "parallel", "arbitrary"),
        name="glb_attn",
    )(q, k, vt, kc, vtc)


def _softmax_pair(q_pair, keys, values, masks, sink_pair, hm, low):
    out = None
    inv = []
    for half in range(2):
        qh = q_pair * hm[half]
        ss = []
        for kk, mk in zip(keys, masks):
            s = _dot_nt(qh, kk)
            ss.append(s if mk is None else jnp.where(mk, s, MASK_VALUE))
        m = ss[0].max(axis=1, keepdims=True)
        for s in ss[1:]:
            m = jnp.maximum(m, s.max(axis=1, keepdims=True))
        if sink_pair is not None:
            m = jnp.maximum(m, sink_pair[half])
        den = None
        for s, vv in zip(ss, values):
            e = jnp.exp(s - m)
            sm = jnp.sum(e, axis=1, keepdims=True)
            den = sm if den is None else den + sm
            part = _dot(e.astype(BF16), vv * hm[half])
            out = part if out is None else out + part
        if sink_pair is not None:
            den = den + jnp.exp(sink_pair[half] - m)
        inv.append(1.0 / den)
    return out * jnp.where(low, inv[0], inv[1])


def _win_kernel(sink_ref, q_ref, k_ref, vt_ref, kc_ref, vtc_ref, o_ref, *, span):
    g = pl.program_id(1)
    i = pl.program_id(2)
    tq = q_ref.shape[0]
    seq = k_ref.shape[0]
    wchunk = vt_ref.shape[4]
    q0 = i * tq
    start = pl.multiple_of(jnp.clip(q0 - WINDOW, 0, seq - span), wchunk)
    c0 = start // wchunk
    k_loc = k_ref[pl.ds(start, span), :]
    kc = kc_ref[...]
    kpos = start + lax.broadcasted_iota(jnp.int32, (span, tq), 0)
    qpos = q0 + lax.broadcasted_iota(jnp.int32, (span, tq), 1)
    bias = jnp.where(jnp.abs(qpos - kpos) <= WINDOW, 0.0, MASK_VALUE)
    qt = (q_ref[...].astype(F32) * LOG2_E).T
    row_low = lax.broadcasted_iota(jnp.int32, (LANES, tq), 0) < HEAD_DIM
    halves = []
    for half in range(2):
        qt_h = (jnp.where(row_low, qt, 0.0) if half == 0 else jnp.where(row_low, 0.0, qt)).astype(BF16)
        sink2 = sink_ref[2 * g + half] * LOG2_E
        s_loc = _dot(k_loc, qt_h) + bias
        s_ctx = _dot(kc, qt_h)
        m = jnp.maximum(jnp.max(s_loc, axis=0, keepdims=True), jnp.max(s_ctx, axis=0, keepdims=True))
        m = jnp.maximum(m, sink2)
        e_loc = jnp.exp2((s_loc - m).astype(BF16))
        e_ctx = jnp.exp2((s_ctx - m).astype(BF16))
        vt_loc = jnp.concatenate([vt_ref[0, 0, c0 + c] for c in range(span // wchunk)], axis=1)
        vt_ctx = jnp.concatenate([vtc_ref[0, 0, c] for c in range(vtc_ref.shape[2])], axis=1)
        acc = _dot(vt_loc, e_loc) + _dot(vt_ctx, e_ctx)
        den = acc[HEAD_DIM:HEAD_DIM + 1, :] + jnp.exp2(sink2 - m)
        halves.append(acc[0:HEAD_DIM, :] / den)
    o_ref[...] = jnp.concatenate(halves, axis=0).T.astype(o_ref.dtype)


def _win_attn(q, k, vt, kc, vtc, sink, seq, ctx_len, tq_want=512):
    t = q.shape[0]
    nb = t // seq
    groups = WIN_KV_HEADS
    assert q.shape[1] == groups * LANES
    tq = _row_tile(seq, min(tq_want, max(seq - 2 * WINDOW, WINDOW)))
    span = tq + 2 * WINDOW
    assert seq >= span and span % vt.shape[4] == 0 and WINDOW % vt.shape[4] == 0
    nq = seq // tq
    return pl.pallas_call(
        functools.partial(_win_kernel, span=span),
        out_shape=jax.ShapeDtypeStruct(q.shape, BF16),
        grid=(nb, groups, nq),
        in_specs=[
            pl.BlockSpec(memory_space=pltpu.SMEM),
            pl.BlockSpec((tq, LANES), lambda b, g, i: (b * nq + i, g)),
            pl.BlockSpec((seq, LANES), lambda b, g, i: (b, g)),
            pl.BlockSpec((1, 1) + vt.shape[2:], lambda b, g, i: (b, g, 0, 0, 0)),
            pl.BlockSpec((ctx_len, LANES), lambda b, g, i: (b, g)),
            pl.BlockSpec((1, 1) + vtc.shape[2:], lambda b, g, i: (b, g, 0, 0, 0)),
        ],
        out_specs=pl.BlockSpec((tq, LANES), lambda b, g, i: (b * nq + i, g)),
        compiler_params=_cparams("parallel", "parallel", "arbitrary"),
        name="win_attn",
    )(sink, q, k, vt, kc, vtc)


def _ctx_attn_kernel(sink_ref, q_ref, k_ref, v_ref, o_ref, *, n_pairs, use_sink):
    g = pl.program_id(1)
    hm = _half_masks(BF16)
    low = _lane_iota((1, LANES)) < HEAD_DIM
    k = k_ref[...]
    v = v_ref[...]
    for p in range(n_pairs):
        heads = (g * n_pairs + p) * 2
        sink_pair = (sink_ref[heads], sink_ref[heads + 1]) if use_sink else None
        o = _softmax_pair(q_ref[:, p * LANES:(p + 1) * LANES], [k], [v], [None], sink_pair, hm, low)
        o_ref[:, p * LANES:(p + 1) * LANES] = o.astype(o_ref.dtype)


def _ctx_attn(q, k, v, sink, groups, ctx_len, use_sink):
    t = q.shape[0]
    nb = t // ctx_len
    gw = q.shape[1] // groups
    n_pairs = gw // LANES
    return pl.pallas_call(
        functools.partial(_ctx_attn_kernel, n_pairs=n_pairs, use_sink=use_sink),
        out_shape=jax.ShapeDtypeStruct(q.shape, BF16),
        grid=(nb, groups),
        in_specs=[
            pl.BlockSpec(memory_space=pltpu.SMEM),
            pl.BlockSpec((ctx_len, gw), lambda b, g: (b, g)),
            pl.BlockSpec((ctx_len, LANES), lambda b, g: (b, g)),
            pl.BlockSpec((ctx_len, LANES), lambda b, g: (b, g)),
        ],
        out_specs=pl.BlockSpec((ctx_len, gw), lambda b, g: (b, g)),
        compiler_params=_cparams("parallel", "parallel"),
        name="ctx_attn",
    )(sink, q, k, v)


def _outproj_kernel(x_ref, mod_ref, of_ref, ob_ref, r_ref, og_ref, ow_ref, gain_ref, bd_ref, w_ref, o_ref):
    o = of_ref[0] + ob_ref[0]
    r = r_ref[...].astype(F32)
    gate = r * jax.nn.sigmoid(r)
    ss = _dot((o * o).astype(BF16), bd_ref[...])
    on = o * lax.rsqrt(ss * (1.0 / GLA_DV) + EPS) * gain_ref[...]
    cat = jnp.concatenate([(on * gate).astype(BF16), og_ref[...], ow_ref[...]], axis=-1)
    y = _dot(cat, w_ref[...])
    o_ref[...] = x_ref[...] + mod_ref[0, 5:6, :] * y


def _outproj(x, mods, o_gla, r, o_glb, o_win, gla_gain, bd, w_out, rows_per_batch, tm_want=512):
    t, d = x.shape
    tm = _row_tile(rows_per_batch, tm_want)
    return pl.pallas_call(
        _outproj_kernel,
        out_shape=jax.ShapeDtypeStruct((t, d), F32),
        grid=(t // tm,),
        in_specs=[
            pl.BlockSpec((tm, d), lambda i: (i, 0)),
            _mod_spec(mods, rows_per_batch // tm),
            pl.BlockSpec((1, tm, GLA_V), lambda i: (0, i, 0)),
            pl.BlockSpec((1, tm, GLA_V), lambda i: (1, i, 0)),
            pl.BlockSpec((tm, GLA_V), lambda i: (i, 0)),
            pl.BlockSpec((tm, o_glb.shape[1]), lambda i: (i, 0)),
            pl.BlockSpec((tm, o_win.shape[1]), lambda i: (i, 0)),
            _const_spec(gla_gain.shape),
            _const_spec(bd.shape),
            _const_spec(w_out.shape),
        ],
        out_specs=pl.BlockSpec((tm, d), lambda i: (i, 0)),
        compiler_params=_cparams("parallel"),
        name="mix_outproj",
    )(x, mods, o_gla, o_gla, r, o_glb, o_win, gla_gain, bd, w_out)


def _rope_tables(seq):
    rows = seq // GRID_W
    row = jnp.repeat(jnp.arange(rows, dtype=F32), GRID_W)
    col = (jnp.arange(rows * GRID_W) % GRID_W).astype(F32)
    n_freq = HEAD_DIM // 4
    inv = jnp.power(ROPE_BASE, -jnp.arange(n_freq, dtype=F32) / n_freq)
    ang = jnp.concatenate([row[:, None] * inv, col[:, None] * inv], axis=-1)
    cos, sin = jnp.cos(ang), jnp.sin(ang)
    cos_t = jnp.concatenate([cos, cos] * (LANES // HEAD_DIM), axis=-1)
    sin_t = jnp.concatenate([-sin, sin] * (LANES // HEAD_DIM), axis=-1)
    return cos_t, sin_t


def _block_diag_ones(n, block):
    idx = np.arange(n) // block
    return jnp.asarray(idx[:, None] == idx[None, :], dtype=BF16)


def kernel(x, c, ctx, c_ctx, mod_w, mod_b, norm_ffn1, ffn1_w_in, ffn1_w_out, norm_mix, mix_w_in, mix_w_out,
           gla_wg_f, gla_bg_f, gla_wg_b, gla_bg_b, gla_out_norm, glb_q_norm, glb_k_norm,
           win_q_norm, win_k_norm, win_sink, norm_ffn2, ffn2_w_in, ffn2_w_out):
    bsz, seq, d = x.shape
    ctx_len = ctx.shape[1]
    depth = mod_w.shape[0]
    in_splits = (GLA_QK, GLA_QK, GLA_V, GLA_V, 2 * GLA_GATE_RANK,
                 GLB_HEADS * HEAD_DIM, GLB_KV_HEADS * HEAD_DIM, GLB_KV_HEADS * HEAD_DIM,
                 WIN_HEADS * HEAD_DIM, WIN_KV_HEADS * HEAD_DIM, WIN_KV_HEADS * HEAD_DIM)

    n_rows = -(-(bsz + 1) // 8) * 8
    c_rows = jnp.concatenate([c, c_ctx[None, :], jnp.zeros((n_rows - bsz - 1, d), F32)], axis=0)
    mods = _modvec(c_rows, mod_w, mod_b)

    cos_l, sin_l = _rope_tables(seq)
    cos_c = jnp.ones((ctx_len, LANES), F32)
    sin_c = jnp.zeros((ctx_len, LANES), F32)
    bd = _block_diag_ones(SLAB, HEAD_DIM)
    zero_state = jnp.zeros((bsz, 2, GLA_V, GLA_QK), F32)

    xl = x.reshape(bsz * seq, d)
    xc = ctx.reshape(bsz * ctx_len, d)

    for l in range(depth):
        need_ctx = l < depth - 1
        mods_l = mods[l, :bsz].reshape(bsz, N_MOD, d)
        mods_c = mods[l, bsz:bsz + 1].reshape(1, N_MOD, d)
        w1_in, w1_out = ffn1_w_in[l].astype(BF16), ffn1_w_out[l].astype(BF16)
        w2_in, w2_out = ffn2_w_in[l].astype(BF16), ffn2_w_out[l].astype(BF16)
        wm = mix_w_in[l]
        src = dict(zip(("aq", "ak", "av", "ar", "ad", "gq", "gk", "gv", "wq", "wk", "wv"),
                       jnp.split(wm, np.cumsum(in_splits)[:-1], axis=1)))
        w_mix = jnp.concatenate([src[n] for n in ("aq", "ak", "av", "ar", "gq", "gk", "wk", "gv", "wv", "wq", "ad")]
                                + [jnp.zeros((d, GATE_PAD - 2 * GLA_GATE_RANK), F32)], axis=1).astype(BF16)
        assert w_mix.shape[1] == _C_END
        wg = jnp.zeros((GATE_PAD, 2 * GLA_QK), F32)
        wg = wg.at[:GLA_GATE_RANK, :GLA_QK].set(gla_wg_f[l])
        wg = wg.at[GLA_GATE_RANK:2 * GLA_GATE_RANK, GLA_QK:].set(gla_wg_b[l]).astype(BF16)
        bg = jnp.concatenate([gla_bg_f[l], gla_bg_b[l]])[None, :]
        qk_gains = jnp.stack([jnp.tile(gn[l], LANES // HEAD_DIM)
                              for gn in (glb_q_norm, glb_k_norm, win_q_norm, win_k_norm)])
        gla_gain = jnp.tile(gla_out_norm[l], GLA_HEADS)[None, :]
        w_out = mix_w_out[l].astype(BF16)
        sink = win_sink[l]

        xl = _ffn(xl, mods_l, 0, norm_ffn1[l], w1_in, w1_out, seq)
        xc = _ffn(xc, mods_c, 0, norm_ffn1[l], w1_in, w1_out, ctx_len)

        pc = _inproj(xc, mods_c, norm_mix[l], w_mix, wg, bg, qk_gains, cos_c, sin_c, bd, ctx_len)
        pq = _inproj(xl, mods_l, norm_mix[l], w_mix, wg, bg, qk_gains, cos_l, sin_l, bd, seq)
        aq, ak, av, ar, ag, gq, gk, gv, wq, wk, wv, gvt, wvt = pq
        aqc, akc, avc, arc, agc, gqc, gkc, gvc, wqc, wkc, wvc, gvtc, wvtc = pc

        oc_gla, states = _gla(aqc, akc, avc, agc, zero_state, ctx_len)
        o_gla, _ = _gla(aq, ak, av, ag, states, seq)
        o_glb = _glb_attn(gq, gk, gvt, gkc, gvtc, seq, ctx_len)
        o_win = _win_attn(wq, wk, wvt, wkc, wvtc, sink, seq, ctx_len)
        xl = _outproj(xl, mods_l, o_gla, ar, o_glb, o_win, gla_gain, bd, w_out, seq)

        xl = _ffn(xl, mods_l, 6, norm_ffn2[l], w2_in, w2_out, seq)

        if need_ctx:
            oc_glb = _ctx_attn(gqc, gkc, gvc, sink, GLB_KV_HEADS, ctx_len, use_sink=False)
            oc_win = _ctx_attn(wqc, wkc, wvc, sink, WIN_KV_HEADS, ctx_len, use_sink=True)
            xc = _outproj(xc, mods_c, oc_gla, arc, oc_glb, oc_win, gla_gain, bd, w_out, ctx_len)
            xc = _ffn(xc, mods_c, 6, norm_ffn2[l], w2_in, w2_out, ctx_len)

    return xl.reshape(bsz, seq, d)
```

```python
import functools

import numpy as np
import jax
import jax.numpy as jnp
from jax import lax
from jax.experimental import pallas as pl
from jax.experimental.pallas import tpu as pltpu

GRID_W = 64
HEAD_DIM = 64
GLA_HEADS = 4
GLA_DK = 32
GLA_DV = 64
GLA_GATE_RANK = 16
GLA_GATE_TAU = 16.0
GLA_CHUNK = 64
GLB_HEADS = 8
GLB_KV_HEADS = 2
WIN_HEADS = 4
WIN_KV_HEADS = 2
WINDOW = 128
ROPE_BASE = 10000.0
N_MOD = 9
EPS = 1e-6

LANES = 128
VMEM_LIMIT_BYTES = 56 * 1024 * 1024

GLA_QK = GLA_HEADS * GLA_DK
GLA_V = GLA_HEADS * GLA_DV
GATE_PAD = LANES
MASK_VALUE = -1e30
LOG2_E = 1.4426950408889634
KEY_CHUNK = 512
GLA_GROUP = 4
WIN_CHUNK = 128
VT_ROWS = HEAD_DIM + 16

BF16 = jnp.bfloat16
F32 = jnp.float32


def _cparams(*sem):
    return pltpu.CompilerParams(dimension_semantics=sem, vmem_limit_bytes=VMEM_LIMIT_BYTES)


def _dot(a, b):
    return jnp.dot(a, b, preferred_element_type=F32)


def _dot_nt(a, b):
    return lax.dot_general(a, b, (((1,), (1,)), ((), ())), preferred_element_type=F32)


def _dot_tn(a, b):
    return lax.dot_general(a, b, (((0,), (0,)), ((), ())), preferred_element_type=F32)


def _lane_iota(shape):
    return lax.broadcasted_iota(jnp.int32, shape, len(shape) - 1)


def _modvec_kernel(c_ref, w_ref, b_ref, o_ref):
    c = c_ref[...]
    s = (c * jax.nn.sigmoid(c)).astype(BF16)
    o_ref[0] = _dot(s, w_ref[0].astype(BF16)) + b_ref[0]


def _modvec(c_rows, mod_w, mod_b):
    depth, d, n = mod_w.shape
    rows = c_rows.shape[0]
    tn = n // N_MOD
    return pl.pallas_call(
        _modvec_kernel,
        out_shape=jax.ShapeDtypeStruct((depth, rows, n), F32),
        grid=(depth, n // tn),
        in_specs=[
            pl.BlockSpec((rows, d), lambda l, j: (0, 0)),
            pl.BlockSpec((1, d, tn), lambda l, j: (l, 0, j)),
            pl.BlockSpec((1, 1, tn), lambda l, j: (l, 0, j)),
        ],
        out_specs=pl.BlockSpec((1, rows, tn), lambda l, j: (l, 0, j)),
        compiler_params=_cparams("arbitrary", "arbitrary"),
        name="modvec",
    )(c_rows, mod_w, mod_b.reshape(depth, 1, n))


def _norm_modulate(x, gain, mod_ref, k_shift):
    shift = mod_ref[0, k_shift:k_shift + 1, :]
    scale = mod_ref[0, k_shift + 1:k_shift + 2, :]
    y = x * lax.rsqrt(jnp.mean(x * x, axis=-1, keepdims=True) + EPS)
    return (y * gain) * (1.0 + scale) + shift


def _mod_spec(mods, tiles_per_batch):
    nb, nm, d = mods.shape
    if nb == 1:
        return pl.BlockSpec((1, nm, d), lambda i: (0, 0, 0))
    return pl.BlockSpec((1, nm, d), lambda i: (i // tiles_per_batch, 0, 0))


def _const_spec(shape):
    nd = len(shape)
    return pl.BlockSpec(shape, lambda *_: (0,) * nd, pipeline_mode=pl.Buffered(1))


def _row_tile(n_rows_per_batch, want):
    t = min(want, n_rows_per_batch)
    assert n_rows_per_batch % t == 0
    return t


def _ffn_chunks(f):
    step = 1536 if f > 1536 else f
    return tuple((lo, min(lo + step, f)) for lo in range(0, f, step))


def _ffn_kernel(x_ref, mod_ref, g_ref, win_ref, wout_ref, o_ref, *, k0, f):
    o_ref[...] = _ffn_body(x_ref[...], mod_ref, g_ref, win_ref, wout_ref, k0, f)


def _ffn_body(x, mod_ref, g_ref, win_ref, wout_ref, k0, f):
    hb = _norm_modulate(x, g_ref[...], mod_ref, k0).astype(BF16)
    gate = mod_ref[0, k0 + 2:k0 + 3, :]
    acc = None
    for lo, hi in _ffn_chunks(f):
        a = _dot(hb, win_ref[:, lo:hi])
        b = _dot(hb, win_ref[:, f + lo:f + hi])
        u = ((a * jax.nn.sigmoid(a)) * b).astype(BF16)
        part = _dot(u, wout_ref[lo:hi, :])
        acc = part if acc is None else acc + part
    return x + (0.5 * gate) * acc


def _ffn(x, mods, k0, gain, w_in, w_out, rows_per_batch, tm_want=512):
    t, d = x.shape
    f = w_out.shape[0]
    tm = _row_tile(rows_per_batch, tm_want)
    return pl.pallas_call(
        functools.partial(_ffn_kernel, k0=k0, f=f),
        out_shape=jax.ShapeDtypeStruct((t, d), F32),
        grid=(t // tm,),
        in_specs=[
            pl.BlockSpec((tm, d), lambda i: (i, 0)),
            _mod_spec(mods, rows_per_batch // tm),
            _const_spec((1, d)),
            _const_spec((d, 2 * f)),
            _const_spec((f, d)),
        ],
        out_specs=pl.BlockSpec((tm, d), lambda i: (i, 0)),
        compiler_params=_cparams("parallel"),
        name="ffn",
    )(x, mods, gain.reshape(1, d), w_in, w_out)


_C_AQ = 0
_C_AK = _C_AQ + GLA_QK
_C_AV = _C_AK + GLA_QK
_C_AR = _C_AV + GLA_V
_C_GQ = _C_AR + GLA_V
_C_GK = _C_GQ + GLB_HEADS * HEAD_DIM
_C_WK = _C_GK + GLB_KV_HEADS * HEAD_DIM
_C_GV = _C_WK + WIN_KV_HEADS * HEAD_DIM
_C_WV = _C_GV + GLB_KV_HEADS * HEAD_DIM
_C_WQ = _C_WV + WIN_KV_HEADS * HEAD_DIM
_C_AD = _C_WQ + WIN_HEADS * HEAD_DIM
_C_END = _C_AD + GATE_PAD
SLAB = 2 * LANES


def _norm_rope_slab(x, bd_ref, gains, cos, sin_signed, out_scales):
    ss = _dot((x * x).astype(BF16), bd_ref[...])
    first_half = (_lane_iota((x.shape[0], LANES)) % HEAD_DIM) < (HEAD_DIM // 2)
    outs = []
    for j in range(2):
        sl = slice(j * LANES, (j + 1) * LANES)
        xn = x[:, sl] * lax.rsqrt(ss[:, sl] * (1.0 / HEAD_DIM) + EPS) * gains[j]
        partner = jnp.where(first_half, pltpu.roll(xn, LANES - HEAD_DIM // 2, 1), pltpu.roll(xn, HEAD_DIM // 2, 1))
        out = xn * cos + partner * sin_signed
        outs.append(out * out_scales[j] if out_scales[j] != 1.0 else out)
    return outs


def _dup_heads(x):
    swapped = pltpu.roll(x, HEAD_DIM, 1)
    low = _lane_iota(x.shape) < HEAD_DIM
    return jnp.where(low, x, swapped), jnp.where(low, swapped, x)


def _store_vt(vt_ref, v):
    chunk = vt_ref.shape[4]
    for c in range(vt_ref.shape[2]):
        vt = v[c * chunk:(c + 1) * chunk, :].T
        for kv in range(vt_ref.shape[1]):
            vt_ref[0, kv, c, 0:HEAD_DIM, :] = vt[kv * HEAD_DIM:(kv + 1) * HEAD_DIM].astype(BF16)
            vt_ref[0, kv, c, HEAD_DIM:VT_ROWS, :] = jnp.ones((VT_ROWS - HEAD_DIM, chunk), BF16)


def _inproj_kernel(x_ref, mod_ref, g_ref, w_ref, wg_ref, bg_ref, qkg_ref, cos_ref, sin_ref, bd_ref,
                   aq_ref, ak_ref, av_ref, ar_ref, ag_ref, gq_ref, gk_ref, gv_ref, wq_ref, wk_ref, wv_ref,
                   gvt_ref, wvt_ref):
    hb = _norm_modulate(x_ref[...], g_ref[...], mod_ref, 3).astype(BF16)
    cos = cos_ref[...]
    sin = sin_ref[...]
    tiles = {}

    def proj(lo, width):
        j, off = divmod(lo, SLAB)
        assert off + width <= SLAB
        if j not in tiles:
            tiles[j] = _dot(hb, w_ref[:, j * SLAB:min((j + 1) * SLAB, _C_END)])
        return tiles[j][:, off:off + width]

    q_scale = HEAD_DIM ** -0.5
    g_glb_q, g_glb_k, g_win_q, g_win_k = (qkg_ref[r:r + 1, :] for r in range(4))

    order = (_C_GQ, _C_GQ + SLAB, _C_WQ, _C_GK, _C_GV, _C_AD, _C_AQ, _C_AV, _C_AR)

    def issue_ahead(n):
        for lo in order[:n + 3]:
            proj(lo, LANES)

    for j in range(GLB_HEADS * HEAD_DIM // SLAB):
        issue_ahead(j)
        halves = _norm_rope_slab(proj(_C_GQ + j * SLAB, SLAB), bd_ref, (g_glb_q, g_glb_q), cos, sin,
                                 (q_scale, q_scale))
        for i, q in enumerate(halves):
            gq_ref[:, j * SLAB + i * LANES:j * SLAB + (i + 1) * LANES] = q.astype(BF16)
    issue_ahead(2)
    halves = _norm_rope_slab(proj(_C_WQ, SLAB), bd_ref, (g_win_q, g_win_q), cos, sin, (q_scale, q_scale))
    for i, q in enumerate(halves):
        wq_ref[:, i * LANES:(i + 1) * LANES] = q.astype(BF16)
    issue_ahead(3)
    k_glb, k_win = _norm_rope_slab(proj(_C_GK, SLAB), bd_ref, (g_glb_k, g_win_k), cos, sin, (1.0, 1.0))
    k0, k1 = _dup_heads(k_glb)
    gk_ref[:, 0:LANES] = k0.astype(BF16)
    gk_ref[:, LANES:2 * LANES] = k1.astype(BF16)
    k0, k1 = _dup_heads(k_win)
    wk_ref[:, 0:LANES] = k0.astype(BF16)
    wk_ref[:, LANES:2 * LANES] = k1.astype(BF16)

    issue_ahead(4)
    v = proj(_C_GV, LANES)
    v0, v1 = _dup_heads(v)
    gv_ref[:, 0:LANES] = v0.astype(BF16)
    gv_ref[:, LANES:2 * LANES] = v1.astype(BF16)
    _store_vt(gvt_ref, v)

    v = proj(_C_WV, LANES)
    v0, v1 = _dup_heads(v)
    wv_ref[:, 0:LANES] = v0.astype(BF16)
    wv_ref[:, LANES:2 * LANES] = v1.astype(BF16)
    _store_vt(wvt_ref, v)

    issue_ahead(len(order))
    z = _dot(proj(_C_AD, GATE_PAD).astype(BF16), wg_ref[...]) + bg_ref[...]
    log_sig = jnp.minimum(z, 0.0) - jnp.log1p(jnp.exp(-jnp.abs(z)))
    ag_ref[...] = log_sig * (1.0 / GLA_GATE_TAU)
    aq_ref[...] = (proj(_C_AQ, GLA_QK) * (GLA_DK ** -0.5)).astype(BF16)
    ak_ref[...] = proj(_C_AK, GLA_QK).astype(BF16)
    av_ref[...] = proj(_C_AV, GLA_V).astype(BF16)
    ar_ref[...] = proj(_C_AR, GLA_V).astype(BF16)


def _inproj(x, mods, gain, w, wg, bg, qk_gains, cos, sin, bd, rows_per_batch, tm_want=512):
    t, d = x.shape
    tm = _row_tile(rows_per_batch, tm_want)
    tpb = rows_per_batch // tm
    widths = (GLA_QK, GLA_QK, GLA_V, GLA_V, 2 * GLA_QK, GLB_HEADS * HEAD_DIM, 2 * LANES, 2 * LANES,
              WIN_HEADS * HEAD_DIM, 2 * LANES, 2 * LANES)
    dtypes = (BF16, BF16, BF16, BF16, F32, BF16, BF16, BF16, BF16, BF16, BF16)
    nb = t // rows_per_batch
    vt_shapes, vt_specs = [], []
    for kv_heads, want in ((GLB_KV_HEADS, KEY_CHUNK), (WIN_KV_HEADS, WIN_CHUNK)):
        chunk = min(want, tm)
        vt_shapes.append(jax.ShapeDtypeStruct((nb, kv_heads, rows_per_batch // chunk, VT_ROWS, chunk), BF16))
        vt_specs.append(pl.BlockSpec((1, kv_heads, tm // chunk, VT_ROWS, chunk),
                                     lambda i: (i // tpb, 0, i % tpb, 0, 0)))
    return pl.pallas_call(
        _inproj_kernel,
        out_shape=tuple(jax.ShapeDtypeStruct((t, wd), dt) for wd, dt in zip(widths, dtypes)) + tuple(vt_shapes),
        grid=(t // tm,),
        in_specs=[
            pl.BlockSpec((tm, d), lambda i: (i, 0)),
            _mod_spec(mods, tpb),
            _const_spec((1, d)),
            _const_spec(w.shape),
            _const_spec(wg.shape),
            _const_spec(bg.shape),
            _const_spec(qk_gains.shape),
            pl.BlockSpec((tm, LANES), lambda i: (i % tpb, 0)),
            pl.BlockSpec((tm, LANES), lambda i: (i % tpb, 0)),
            _const_spec(bd.shape),
        ],
        out_specs=tuple(pl.BlockSpec((tm, wd), lambda i: (i, 0)) for wd in widths) + tuple(vt_specs),
        compiler_params=_cparams("parallel"),
        name="mix_inproj",
    )(x, mods, gain.reshape(1, d), w, wg, bg, qk_gains, cos, sin, bd)


def _gla_kernel(q_ref, k_ref, v_ref, g_ref, s0_ref, o_ref, sfin_ref, st_ref, *, n_chunks):
    c_len = GLA_CHUNK
    d = pl.program_id(1)
    i = pl.program_id(2)
    sign = 1 - 2 * d

    @pl.when(i == 0)
    def _():
        st_ref[...] = s0_ref[0, 0]

    group = min(GLA_GROUP, n_chunks)
    g_len = group * c_len
    row = lax.broadcasted_iota(jnp.int32, (g_len, g_len), 0)
    col = lax.broadcasted_iota(jnp.int32, (g_len, g_len), 1)
    keep = ((row - col) * sign >= 0) & (row // c_len == col // c_len)
    cum_op = jnp.where(keep, 1.0, 0.0).astype(BF16)
    row4 = lax.broadcasted_iota(jnp.int32, (c_len, GLA_HEADS * c_len), 0)
    col4 = lax.broadcasted_iota(jnp.int32, (c_len, GLA_HEADS * c_len), 1) % c_len
    keep4 = (row4 - col4) * sign >= 0
    qk_lane_head = _lane_iota((1, GLA_QK)) // GLA_DK
    v_lane_head = _lane_iota((1, GLA_V)) // GLA_DV
    qk_head_mask = [(qk_lane_head == h).astype(F32) for h in range(GLA_HEADS)]
    v_head_mask = [(v_lane_head == h).astype(BF16) for h in range(GLA_HEADS)]
    st_row_head = lax.broadcasted_iota(jnp.int32, (GLA_V, GLA_QK), 0) // GLA_DV
    st_col_head = lax.broadcasted_iota(jnp.int32, (GLA_V, GLA_QK), 1) // GLA_DK
    st_mask = st_row_head == st_col_head

    n_groups = n_chunks // group

    def group_starts(j):
        starts = []
        for p in range(group):
            c = j * group + p
            c = c + d * (n_chunks - 1 - 2 * c)
            starts.append(pl.multiple_of(c * c_len, c_len))
        return starts

    def decay_stage(j):
        starts = group_starts(j)
        q = jnp.concatenate([q_ref[pl.ds(r, c_len), :] for r in starts], axis=0).astype(F32)
        k = jnp.concatenate([k_ref[pl.ds(r, c_len), :] for r in starts], axis=0).astype(F32)
        g = jnp.concatenate([g_ref[pl.ds(r, c_len), :] for r in starts], axis=0)
        g_hi = g.astype(BF16)
        r1 = g - g_hi.astype(F32)
        g_mid = r1.astype(BF16)
        g_lo = (r1 - g_mid.astype(F32)).astype(BF16)
        b = _dot(cum_op, g_hi) + _dot(cum_op, g_mid) + _dot(cum_op, g_lo)
        q_in = (q * jnp.exp(b)).astype(BF16)
        k_out = k * jnp.exp(-b)
        decays = jnp.concatenate([jnp.exp(jnp.sum(g[p * c_len:(p + 1) * c_len], axis=0, keepdims=True))
                                  for p in range(group)], axis=0)
        return q_in, k_out, decays

    def chunk_group(j, staged):
        staged_next = decay_stage(jnp.minimum(j + 1, n_groups - 1))
        q_in, k_out, decays = staged
        starts = group_starts(j)
        vs = [v_ref[pl.ds(r, c_len), :] for r in starts]
        st = st_ref[...]
        for p in range(group):
            rows = slice(p * c_len, (p + 1) * c_len)
            decay = decays[p:p + 1, :]
            k_out_p = k_out[rows]
            k_dec = (k_out_p * decay).astype(BF16)
            k_stack = jnp.concatenate([(k_out_p * qk_head_mask[h]).astype(BF16) for h in range(GLA_HEADS)], axis=0)
            a = _dot_nt(q_in[rows], k_stack)
            a = jnp.where(keep4, a, 0.0).astype(BF16)
            v_bd = jnp.concatenate([vs[p] * v_head_mask[h] for h in range(GLA_HEADS)], axis=0)
            o = _dot(a, v_bd) + _dot_nt(q_in[rows], st.astype(BF16))
            o_ref[0, pl.ds(starts[p], c_len), :] = o
            ds_t = _dot_tn(vs[p], k_dec)
            st = st * decay + jnp.where(st_mask, ds_t, 0.0)
        st_ref[...] = st
        return staged_next

    lax.fori_loop(0, n_groups, chunk_group, decay_stage(0))

    @pl.when(i == pl.num_programs(2) - 1)
    def _():
        sfin_ref[0, 0] = st_ref[...]


def _gla(q, k, v, g, s0, rows_per_batch, tt_want=2048):
    t = q.shape[0]
    nb = t // rows_per_batch
    tt = _row_tile(rows_per_batch, tt_want)
    nt = rows_per_batch // tt

    def rows(b, d, i):
        return b * nt + i + d * (nt - 1 - 2 * i)

    return pl.pallas_call(
        functools.partial(_gla_kernel, n_chunks=tt // GLA_CHUNK),
        out_shape=(jax.ShapeDtypeStruct((2, t, GLA_V), F32),
                   jax.ShapeDtypeStruct((nb, 2, GLA_V, GLA_QK), F32)),
        grid=(nb, 2, nt),
        in_specs=[
            pl.BlockSpec((tt, GLA_QK), lambda b, d, i: (rows(b, d, i), 0)),
            pl.BlockSpec((tt, GLA_QK), lambda b, d, i: (rows(b, d, i), 0)),
            pl.BlockSpec((tt, GLA_V), lambda b, d, i: (rows(b, d, i), 0)),
            pl.BlockSpec((tt, GLA_QK), lambda b, d, i: (rows(b, d, i), d)),
            pl.BlockSpec((1, 1, GLA_V, GLA_QK), lambda b, d, i: (b, d, 0, 0)),
        ],
        out_specs=(pl.BlockSpec((1, tt, GLA_V), lambda b, d, i: (d, rows(b, d, i), 0)),
                   pl.BlockSpec((1, 1, GLA_V, GLA_QK), lambda b, d, i: (b, d, 0, 0))),
        scratch_shapes=[pltpu.VMEM((GLA_V, GLA_QK), F32)],
        compiler_params=_cparams("parallel", "parallel", "arbitrary"),
        name="gla_scan",
    )(q, k, v, g, s0)


def _half_masks(dtype):
    low = _lane_iota((1, LANES)) < HEAD_DIM
    return [low.astype(dtype), (~low).astype(dtype)]


def _rep(x, width):
    n = width // LANES
    return x if n == 1 else pltpu.repeat(x, n, axis=1)


def _glb_kernel(q_ref, k_ref, vt_ref, kc_ref, vtc_ref, o_ref, qt_ref, m_ref, acc_ref, s_ref, *,
                n_pairs, col_block, lookahead, chunks_per_tile):
    tq = q_ref.shape[0]
    n_lat = vt_ref.shape[2] // chunks_per_tile
    tk = vt_ref.shape[4] * chunks_per_tile
    n_heads = 2 * n_pairs
    for p in range(n_pairs):
        qt = (q_ref[:, p * LANES:(p + 1) * LANES].astype(F32) * LOG2_E).T
        qt_ref[2 * p] = qt[0:HEAD_DIM].astype(BF16)
        qt_ref[2 * p + 1] = qt[HEAD_DIM:2 * HEAD_DIM].astype(BF16)

    m_ref[...] = jnp.full(m_ref.shape, MASK_VALUE, F32)
    acc_ref[...] = jnp.zeros(acc_ref.shape, F32)

    units = [(h, cb) for h in range(n_heads) for cb in range(tq // col_block)]
    assert lookahead <= len(units)

    def scores(k, u):
        h, cb = units[u]
        return _dot(k[:, 0:HEAD_DIM], qt_ref[h, :, cb * col_block:(cb + 1) * col_block])

    def step(k, vt, k_next):
        w = k.shape[0]
        for u, (h, cb) in enumerate(units):
            s = s_ref[u, 0:w, :]
            if u + lookahead < len(units):
                s_ref[u + lookahead, 0:w, :] = scores(k, u + lookahead)
            elif k_next is not None:
                s_ref[u + lookahead - len(units), 0:k_next.shape[0], :] = scores(k_next, u + lookahead - len(units))
            cols = slice(cb * col_block, (cb + 1) * col_block)
            m_prev = m_ref[h, :, cols]
            m_new = jnp.maximum(m_prev, jnp.max(s, axis=0, keepdims=True))
            alpha = jnp.exp2(m_prev - m_new)
            e = jnp.exp2((s - m_new).astype(BF16))
            m_ref[h, :, cols] = m_new
            ch = w // len(vt)
            pv = _dot(vt[0], e[0:ch, :])
            for c in range(1, len(vt)):
                pv = pv + _dot(vt[c], e[c * ch:(c + 1) * ch, :])
            acc_ref[h, :, cols] = acc_ref[h, :, cols] * alpha + pv

    def lat_tile(j):
        return k_ref[pl.ds(pl.multiple_of(j * tk, tk), tk), :]

    def lat_vt(j):
        return [vt_ref[0, 0, j * chunks_per_tile + c] for c in range(chunks_per_tile)]

    wc = vtc_ref.shape[4]
    ctx_tiles = [kc_ref[j * wc:(j + 1) * wc, :] for j in range(vtc_ref.shape[2])]

    for u in range(lookahead):
        s_ref[u, 0:tk, :] = scores(lat_tile(0), u)

    def body(j, carry):
        step(lat_tile(j), lat_vt(j), lat_tile(j + 1))
        return carry

    lax.fori_loop(0, n_lat - 1, body, 0)
    step(lat_tile(n_lat - 1), lat_vt(n_lat - 1), ctx_tiles[0])
    for j, kc in enumerate(ctx_tiles):
        step(kc, [vtc_ref[0, 0, j]], ctx_tiles[j + 1] if j + 1 < len(ctx_tiles) else None)

    for p in range(n_pairs):
        halves = [acc_ref[h, 0:HEAD_DIM, :] / acc_ref[h, HEAD_DIM:HEAD_DIM + 1, :] for h in (2 * p, 2 * p + 1)]
        o_ref[:, p * LANES:(p + 1) * LANES] = jnp.concatenate(halves, axis=0).T.astype(o_ref.dtype)


def _glb_attn(q, k, vt, kc, vtc, seq, ctx_len, tq_want=512, col_block=512, lookahead=4, tk_want=512):
    t = q.shape[0]
    nb = t // seq
    groups = GLB_KV_HEADS
    gw = q.shape[1] // groups
    n_pairs = gw // LANES
    tq = _row_tile(seq, tq_want)
    nq = seq // tq
    col_block = min(col_block, tq)
    n_units = 2 * n_pairs * (tq // col_block)
    chunks_per_tile = max(1, min(tk_want, seq) // vt.shape[4])
    assert vt.shape[2] % chunks_per_tile == 0
    max_keys = max(vt.shape[4] * chunks_per_tile, vtc.shape[4])
    return pl.pallas_call(
        functools.partial(_glb_kernel, n_pairs=n_pairs, col_block=col_block, lookahead=lookahead,
                          chunks_per_tile=chunks_per_tile),
        out_shape=jax.ShapeDtypeStruct(q.shape, BF16),
        grid=(nb, groups, nq),
        in_specs=[
            pl.BlockSpec((tq, gw), lambda b, g, i: (b * nq + i, g)),
            pl.BlockSpec((seq, LANES), lambda b, g, i: (b, g)),
            pl.BlockSpec((1, 1) + vt.shape[2:], lambda b, g, i: (b, g, 0, 0, 0)),
            pl.BlockSpec((ctx_len, LANES), lambda b, g, i: (b, g)),
            pl.BlockSpec((1, 1) + vtc.shape[2:], lambda b, g, i: (b, g, 0, 0, 0)),
        ],
        out_specs=pl.BlockSpec((tq, gw), lambda b, g, i: (b * nq + i, g)),
        scratch_shapes=[pltpu.VMEM((2 * n_pairs, HEAD_DIM, tq), BF16),
                        pltpu.VMEM((2 * n_pairs, 1, tq), F32),
                        pltpu.VMEM((2 * n_pairs, VT_ROWS, tq), F32),
                        pltpu.VMEM((n_units, max_keys, col_block), F32)],
        compiler_params=_cparams("parallel", "parallel", "arbitrary"),
        name="glb_attn",
    )(q, k, vt, kc, vtc)


def _softmax_pair(q_pair, keys, values, masks, sink_pair, hm, low):
    out = None
    inv = []
    for half in range(2):
        qh = q_pair * hm[half]
        ss = []
        for kk, mk in zip(keys, masks):
            s = _dot_nt(qh, kk)
            ss.append(s if mk is None else jnp.where(mk, s, MASK_VALUE))
        m = ss[0].max(axis=1, keepdims=True)
        for s in ss[1:]:
            m = jnp.maximum(m, s.max(axis=1, keepdims=True))
        if sink_pair is not None:
            m = jnp.maximum(m, sink_pair[half])
        den = None
        for s, vv in zip(ss, values):
            e = jnp.exp(s - m)
            sm = jnp.sum(e, axis=1, keepdims=True)
            den = sm if den is None else den + sm
            part = _dot(e.astype(BF16), vv * hm[half])
            out = part if out is None else out + part
        if sink_pair is not None:
            den = den + jnp.exp(sink_pair[half] - m)
        inv.append(1.0 / den)
    return out * jnp.where(low, inv[0], inv[1])


def _win_kernel(sink_ref, q_ref, k_ref, vt_ref, kc_ref, vtc_ref, o_ref, *, span):
    g = pl.program_id(1)
    i = pl.program_id(2)
    tq = q_ref.shape[0]
    seq = k_ref.shape[0]
    wchunk = vt_ref.shape[4]
    q0 = i * tq
    start = pl.multiple_of(jnp.clip(q0 - WINDOW, 0, seq - span), wchunk)
    c0 = start // wchunk
    k_loc = k_ref[pl.ds(start, span), :]
    kc = kc_ref[...]
    kpos = start + lax.broadcasted_iota(jnp.int32, (span, tq), 0)
    qpos = q0 + lax.broadcasted_iota(jnp.int32, (span, tq), 1)
    bias = jnp.where(jnp.abs(qpos - kpos) <= WINDOW, 0.0, MASK_VALUE)
    qt = (q_ref[...].astype(F32) * LOG2_E).T
    row_low = lax.broadcasted_iota(jnp.int32, (LANES, tq), 0) < HEAD_DIM
    halves = []
    for half in range(2):
        qt_h = (jnp.where(row_low, qt, 0.0) if half == 0 else jnp.where(row_low, 0.0, qt)).astype(BF16)
        sink2 = sink_ref[2 * g + half] * LOG2_E
        s_loc = _dot(k_loc, qt_h) + bias
        s_ctx = _dot(kc, qt_h)
        m = jnp.maximum(jnp.max(s_loc, axis=0, keepdims=True), jnp.max(s_ctx, axis=0, keepdims=True))
        m = jnp.maximum(m, sink2)
        e_loc = jnp.exp2((s_loc - m).astype(BF16))
        e_ctx = jnp.exp2((s_ctx - m).astype(BF16))
        vt_loc = jnp.concatenate([vt_ref[0, 0, c0 + c] for c in range(span // wchunk)], axis=1)
        vt_ctx = jnp.concatenate([vtc_ref[0, 0, c] for c in range(vtc_ref.shape[2])], axis=1)
        acc = _dot(vt_loc, e_loc) + _dot(vt_ctx, e_ctx)
        den = acc[HEAD_DIM:HEAD_DIM + 1, :] + jnp.exp2(sink2 - m)
        halves.append(acc[0:HEAD_DIM, :] / den)
    o_ref[...] = jnp.concatenate(halves, axis=0).T.astype(o_ref.dtype)


def _win_attn(q, k, vt, kc, vtc, sink, seq, ctx_len, tq_want=512):
    t = q.shape[0]
    nb = t // seq
    groups = WIN_KV_HEADS
    assert q.shape[1] == groups * LANES
    tq = _row_tile(seq, min(tq_want, max(seq - 2 * WINDOW, WINDOW)))
    span = tq + 2 * WINDOW
    assert seq >= span and span % vt.shape[4] == 0 and WINDOW % vt.shape[4] == 0
    nq = seq // tq
    return pl.pallas_call(
        functools.partial(_win_kernel, span=span),
        out_shape=jax.ShapeDtypeStruct(q.shape, BF16),
        grid=(nb, groups, nq),
        in_specs=[
            pl.BlockSpec(memory_space=pltpu.SMEM),
            pl.BlockSpec((tq, LANES), lambda b, g, i: (b * nq + i, g)),
            pl.BlockSpec((seq, LANES), lambda b, g, i: (b, g)),
            pl.BlockSpec((1, 1) + vt.shape[2:], lambda b, g, i: (b, g, 0, 0, 0)),
            pl.BlockSpec((ctx_len, LANES), lambda b, g, i: (b, g)),
            pl.BlockSpec((1, 1) + vtc.shape[2:], lambda b, g, i: (b, g, 0, 0, 0)),
        ],
        out_specs=pl.BlockSpec((tq, LANES), lambda b, g, i: (b * nq + i, g)),
        compiler_params=_cparams("parallel", "parallel", "arbitrary"),
        name="win_attn",
    )(sink, q, k, vt, kc, vtc)


def _ctx_attn_kernel(sink_ref, q_ref, k_ref, v_ref, o_ref, *, n_pairs, use_sink):
    g = pl.program_id(1)
    hm = _half_masks(BF16)
    low = _lane_iota((1, LANES)) < HEAD_DIM
    k = k_ref[...]
    v = v_ref[...]
    for p in range(n_pairs):
        heads = (g * n_pairs + p) * 2
        sink_pair = (sink_ref[heads], sink_ref[heads + 1]) if use_sink else None
        o = _softmax_pair(q_ref[:, p * LANES:(p + 1) * LANES], [k], [v], [None], sink_pair, hm, low)
        o_ref[:, p * LANES:(p + 1) * LANES] = o.astype(o_ref.dtype)


def _ctx_attn(q, k, v, sink, groups, ctx_len, use_sink):
    t = q.shape[0]
    nb = t // ctx_len
    gw = q.shape[1] // groups
    n_pairs = gw // LANES
    return pl.pallas_call(
        functools.partial(_ctx_attn_kernel, n_pairs=n_pairs, use_sink=use_sink),
        out_shape=jax.ShapeDtypeStruct(q.shape, BF16),
        grid=(nb, groups),
        in_specs=[
            pl.BlockSpec(memory_space=pltpu.SMEM),
            pl.BlockSpec((ctx_len, gw), lambda b, g: (b, g)),
            pl.BlockSpec((ctx_len, LANES), lambda b, g: (b, g)),
            pl.BlockSpec((ctx_len, LANES), lambda b, g: (b, g)),
        ],
        out_specs=pl.BlockSpec((ctx_len, gw), lambda b, g: (b, g)),
        compiler_params=_cparams("parallel", "parallel"),
        name="ctx_attn",
    )(sink, q, k, v)


def _outproj_ffn_kernel(x_ref, mod_ref, of_ref, ob_ref, r_ref, og_ref, ow_ref, gain_ref, bd_ref, w_ref,
                        g2_ref, win_ref, wout_ref, o_ref, *, f):
    o = of_ref[0] + ob_ref[0]
    r = r_ref[...].astype(F32)
    gate = r * jax.nn.sigmoid(r)
    ss = _dot((o * o).astype(BF16), bd_ref[...])
    on = o * lax.rsqrt(ss * (1.0 / GLA_DV) + EPS) * gain_ref[...]
    cat = jnp.concatenate([(on * gate).astype(BF16), og_ref[...], ow_ref[...]], axis=-1)
    x = x_ref[...] + mod_ref[0, 5:6, :] * _dot(cat, w_ref[...])
    o_ref[...] = _ffn_body(x, mod_ref, g2_ref, win_ref, wout_ref, 6, f)


def _outproj_ffn(x, mods, o_gla, r, o_glb, o_win, gla_gain, bd, w_out, gain2, w2_in, w2_out, rows_per_batch,
                 tm_want=512):
    t, d = x.shape
    f = w2_out.shape[0]
    tm = _row_tile(rows_per_batch, tm_want)
    return pl.pallas_call(
        functools.partial(_outproj_ffn_kernel, f=f),
        out_shape=jax.ShapeDtypeStruct((t, d), F32),
        grid=(t // tm,),
        in_specs=[
            pl.BlockSpec((tm, d), lambda i: (i, 0)),
            _mod_spec(mods, rows_per_batch // tm),
            pl.BlockSpec((1, tm, GLA_V), lambda i: (0, i, 0)),
            pl.BlockSpec((1, tm, GLA_V), lambda i: (1, i, 0)),
            pl.BlockSpec((tm, GLA_V), lambda i: (i, 0)),
            pl.BlockSpec((tm, o_glb.shape[1]), lambda i: (i, 0)),
            pl.BlockSpec((tm, o_win.shape[1]), lambda i: (i, 0)),
            _const_spec(gla_gain.shape),
            _const_spec(bd.shape),
            _const_spec(w_out.shape),
            _const_spec((1, d)),
            _const_spec((d, 2 * f)),
            _const_spec((f, d)),
        ],
        out_specs=pl.BlockSpec((tm, d), lambda i: (i, 0)),
        compiler_params=_cparams("parallel"),
        name="mix_outproj_ffn",
    )(x, mods, o_gla, o_gla, r, o_glb, o_win, gla_gain, bd, w_out, gain2.reshape(1, d), w2_in, w2_out)


def _rope_tables(seq):
    rows = seq // GRID_W
    row = jnp.repeat(jnp.arange(rows, dtype=F32), GRID_W)
    col = (jnp.arange(rows * GRID_W) % GRID_W).astype(F32)
    n_freq = HEAD_DIM // 4
    inv = jnp.power(ROPE_BASE, -jnp.arange(n_freq, dtype=F32) / n_freq)
    ang = jnp.concatenate([row[:, None] * inv, col[:, None] * inv], axis=-1)
    cos, sin = jnp.cos(ang), jnp.sin(ang)
    cos_t = jnp.concatenate([cos, cos] * (LANES // HEAD_DIM), axis=-1)
    sin_t = jnp.concatenate([-sin, sin] * (LANES // HEAD_DIM), axis=-1)
    return cos_t, sin_t


def _block_diag_ones(n, block):
    idx = np.arange(n) // block
    return jnp.asarray(idx[:, None] == idx[None, :], dtype=BF16)


def kernel(x, c, ctx, c_ctx, mod_w, mod_b, norm_ffn1, ffn1_w_in, ffn1_w_out, norm_mix, mix_w_in, mix_w_out,
           gla_wg_f, gla_bg_f, gla_wg_b, gla_bg_b, gla_out_norm, glb_q_norm, glb_k_norm,
           win_q_norm, win_k_norm, win_sink, norm_ffn2, ffn2_w_in, ffn2_w_out):
    bsz, seq, d = x.shape
    ctx_len = ctx.shape[1]
    depth = mod_w.shape[0]
    in_splits = (GLA_QK, GLA_QK, GLA_V, GLA_V, 2 * GLA_GATE_RANK,
                 GLB_HEADS * HEAD_DIM, GLB_KV_HEADS * HEAD_DIM, GLB_KV_HEADS * HEAD_DIM,
                 WIN_HEADS * HEAD_DIM, WIN_KV_HEADS * HEAD_DIM, WIN_KV_HEADS * HEAD_DIM)

    n_rows = -(-(bsz + 1) // 8) * 8
    c_rows = jnp.concatenate([c, c_ctx[None, :], jnp.zeros((n_rows - bsz - 1, d), F32)], axis=0)
    mods = _modvec(c_rows, mod_w, mod_b)

    cos_l, sin_l = _rope_tables(seq)
    cos_c = jnp.ones((ctx_len, LANES), F32)
    sin_c = jnp.zeros((ctx_len, LANES), F32)
    bd = _block_diag_ones(SLAB, HEAD_DIM)
    zero_state = jnp.zeros((bsz, 2, GLA_V, GLA_QK), F32)

    xl = x.reshape(bsz * seq, d)
    xc = ctx.reshape(bsz * ctx_len, d)

    for l in range(depth):
        need_ctx = l < depth - 1
        mods_l = mods[l, :bsz].reshape(bsz, N_MOD, d)
        mods_c = mods[l, bsz:bsz + 1].reshape(1, N_MOD, d)
        w1_in, w1_out = ffn1_w_in[l].astype(BF16), ffn1_w_out[l].astype(BF16)
        w2_in, w2_out = ffn2_w_in[l].astype(BF16), ffn2_w_out[l].astype(BF16)
        wm = mix_w_in[l]
        src = dict(zip(("aq", "ak", "av", "ar", "ad", "gq", "gk", "gv", "wq", "wk", "wv"),
                       jnp.split(wm, np.cumsum(in_splits)[:-1], axis=1)))
        w_mix = jnp.concatenate([src[n] for n in ("aq", "ak", "av", "ar", "gq", "gk", "wk", "gv", "wv", "wq", "ad")]
                                + [jnp.zeros((d, GATE_PAD - 2 * GLA_GATE_RANK), F32)], axis=1).astype(BF16)
        assert w_mix.shape[1] == _C_END
        wg = jnp.zeros((GATE_PAD, 2 * GLA_QK), F32)
        wg = wg.at[:GLA_GATE_RANK, :GLA_QK].set(gla_wg_f[l])
        wg = wg.at[GLA_GATE_RANK:2 * GLA_GATE_RANK, GLA_QK:].set(gla_wg_b[l]).astype(BF16)
        bg = jnp.concatenate([gla_bg_f[l], gla_bg_b[l]])[None, :]
        qk_gains = jnp.stack([jnp.tile(gn[l], LANES // HEAD_DIM)
                              for gn in (glb_q_norm, glb_k_norm, win_q_norm, win_k_norm)])
        gla_gain = jnp.tile(gla_out_norm[l], GLA_HEADS)[None, :]
        w_out = mix_w_out[l].astype(BF16)
        sink = win_sink[l]

        xl = _ffn(xl, mods_l, 0, norm_ffn1[l], w1_in, w1_out, seq)
        xc = _ffn(xc, mods_c, 0, norm_ffn1[l], w1_in, w1_out, ctx_len)

        pc = _inproj(xc, mods_c, norm_mix[l], w_mix, wg, bg, qk_gains, cos_c, sin_c, bd, ctx_len)
        pq = _inproj(xl, mods_l, norm_mix[l], w_mix, wg, bg, qk_gains, cos_l, sin_l, bd, seq)
        aq, ak, av, ar, ag, gq, gk, gv, wq, wk, wv, gvt, wvt = pq
        aqc, akc, avc, arc, agc, gqc, gkc, gvc, wqc, wkc, wvc, gvtc, wvtc = pc

        oc_gla, states = _gla(aqc, akc, avc, agc, zero_state, ctx_len)
        o_gla, _ = _gla(aq, ak, av, ag, states, seq)
        o_glb = _glb_attn(gq, gk, gvt, gkc, gvtc, seq, ctx_len)
        o_win = _win_attn(wq, wk, wvt, wkc, wvtc, sink, seq, ctx_len)
        xl = _outproj_ffn(xl, mods_l, o_gla, ar, o_glb, o_win, gla_gain, bd, w_out, norm_ffn2[l], w2_in, w2_out, seq)

        if need_ctx:
            oc_glb = _ctx_attn(gqc, gkc, gvc, sink, GLB_KV_HEADS, ctx_len, use_sink=False)
            oc_win = _ctx_attn(wqc, wkc, wvc, sink, WIN_KV_HEADS, ctx_len, use_sink=True)
            xc = _outproj_ffn(xc, mods_c, oc_gla, arc, oc_glb, oc_win, gla_gain, bd, w_out, norm_ffn2[l],
                              w2_in, w2_out, ctx_len)

    return xl.reshape(bsz, seq, d)
```

```python
import functools

import numpy as np
import jax
import jax.numpy as jnp
from jax import lax
from jax.experimental import pallas as pl
from jax.experimental.pallas import tpu as pltpu

GRID_W = 64
HEAD_DIM = 64
GLA_HEADS = 4
GLA_DK = 32
GLA_DV = 64
GLA_GATE_RANK = 16
GLA_GATE_TAU = 16.0
GLA_CHUNK = 64
GLB_HEADS = 8
GLB_KV_HEADS = 2
WIN_HEADS = 4
WIN_KV_HEADS = 2
WINDOW = 128
ROPE_BASE = 10000.0
N_MOD = 9
EPS = 1e-6

LANES = 128
VMEM_LIMIT_BYTES = 56 * 1024 * 1024

GLA_QK = GLA_HEADS * GLA_DK
GLA_V = GLA_HEADS * GLA_DV
GATE_PAD = LANES
MASK_VALUE = -1e30
LOG2_E = 1.4426950408889634
KEY_CHUNK = 512
GLA_GROUP = 4
WIN_CHUNK = 128
VT_ROWS = HEAD_DIM + 16

BF16 = jnp.bfloat16
F32 = jnp.float32


def _cparams(*sem):
    return pltpu.CompilerParams(dimension_semantics=sem, vmem_limit_bytes=VMEM_LIMIT_BYTES)


def _dot(a, b):
    return jnp.dot(a, b, preferred_element_type=F32)


def _dot_nt(a, b):
    return lax.dot_general(a, b, (((1,), (1,)), ((), ())), preferred_element_type=F32)


def _dot_tn(a, b):
    return lax.dot_general(a, b, (((0,), (0,)), ((), ())), preferred_element_type=F32)


def _lane_iota(shape):
    return lax.broadcasted_iota(jnp.int32, shape, len(shape) - 1)


def _modvec_kernel(c_ref, w_ref, b_ref, o_ref):
    c = c_ref[...]
    s = (c * jax.nn.sigmoid(c)).astype(BF16)
    o_ref[0] = _dot(s, w_ref[0].astype(BF16)) + b_ref[0]


def _modvec(c_rows, mod_w, mod_b):
    depth, d, n = mod_w.shape
    rows = c_rows.shape[0]
    tn = n // N_MOD
    return pl.pallas_call(
        _modvec_kernel,
        out_shape=jax.ShapeDtypeStruct((depth, rows, n), F32),
        grid=(depth, n // tn),
        in_specs=[
            pl.BlockSpec((rows, d), lambda l, j: (0, 0)),
            pl.BlockSpec((1, d, tn), lambda l, j: (l, 0, j)),
            pl.BlockSpec((1, 1, tn), lambda l, j: (l, 0, j)),
        ],
        out_specs=pl.BlockSpec((1, rows, tn), lambda l, j: (l, 0, j)),
        compiler_params=_cparams("arbitrary", "arbitrary"),
        name="modvec",
    )(c_rows, mod_w, mod_b.reshape(depth, 1, n))


def _norm_modulate(x, gain, mod_ref, k_shift):
    shift = mod_ref[0, k_shift:k_shift + 1, :]
    scale = mod_ref[0, k_shift + 1:k_shift + 2, :]
    y = x * lax.rsqrt(jnp.mean(x * x, axis=-1, keepdims=True) + EPS)
    return (y * gain) * (1.0 + scale) + shift


def _mod_spec(mods, tiles_per_batch):
    nb, nm, d = mods.shape
    if nb == 1:
        return pl.BlockSpec((1, nm, d), lambda i: (0, 0, 0))
    return pl.BlockSpec((1, nm, d), lambda i: (i // tiles_per_batch, 0, 0))


def _const_spec(shape):
    nd = len(shape)
    return pl.BlockSpec(shape, lambda *_: (0,) * nd, pipeline_mode=pl.Buffered(1))


def _row_tile(n_rows_per_batch, want):
    t = min(want, n_rows_per_batch)
    assert n_rows_per_batch % t == 0
    return t


def _ffn_chunks(f):
    step = 1536 if f > 1536 else f
    return tuple((lo, min(lo + step, f)) for lo in range(0, f, step))


def _ffn_kernel(x_ref, mod_ref, g_ref, win_ref, wout_ref, o_ref, *, k0, f):
    o_ref[...] = _ffn_body(x_ref[...], mod_ref, g_ref, win_ref, wout_ref, k0, f)


def _ffn_body(x, mod_ref, g_ref, win_ref, wout_ref, k0, f):
    hb = _norm_modulate(x, g_ref[...], mod_ref, k0).astype(BF16)
    gate = mod_ref[0, k0 + 2:k0 + 3, :]
    acc = None
    for lo, hi in _ffn_chunks(f):
        a = _dot(hb, win_ref[:, lo:hi])
        b = _dot(hb, win_ref[:, f + lo:f + hi])
        u = ((a * jax.nn.sigmoid(a)) * b).astype(BF16)
        part = _dot(u, wout_ref[lo:hi, :])
        acc = part if acc is None else acc + part
    return x + (0.5 * gate) * acc


def _ffn(x, mods, k0, gain, w_in, w_out, rows_per_batch, tm_want=512):
    t, d = x.shape
    f = w_out.shape[0]
    tm = _row_tile(rows_per_batch, tm_want)
    return pl.pallas_call(
        functools.partial(_ffn_kernel, k0=k0, f=f),
        out_shape=jax.ShapeDtypeStruct((t, d), F32),
        grid=(t // tm,),
        in_specs=[
            pl.BlockSpec((tm, d), lambda i: (i, 0)),
            _mod_spec(mods, rows_per_batch // tm),
            _const_spec((1, d)),
            _const_spec((d, 2 * f)),
            _const_spec((f, d)),
        ],
        out_specs=pl.BlockSpec((tm, d), lambda i: (i, 0)),
        compiler_params=_cparams("parallel"),
        name="ffn",
    )(x, mods, gain.reshape(1, d), w_in, w_out)


_C_AQ = 0
_C_AK = _C_AQ + GLA_QK
_C_AV = _C_AK + GLA_QK
_C_AR = _C_AV + GLA_V
_C_GQ = _C_AR + GLA_V
_C_GK = _C_GQ + GLB_HEADS * HEAD_DIM
_C_WK = _C_GK + GLB_KV_HEADS * HEAD_DIM
_C_GV = _C_WK + WIN_KV_HEADS * HEAD_DIM
_C_WV = _C_GV + GLB_KV_HEADS * HEAD_DIM
_C_WQ = _C_WV + WIN_KV_HEADS * HEAD_DIM
_C_AD = _C_WQ + WIN_HEADS * HEAD_DIM
_C_END = _C_AD + GATE_PAD
SLAB = 2 * LANES


def _norm_rope_slab(x, bd_ref, gains, cos, sin_signed, out_scales):
    ss = _dot((x * x).astype(BF16), bd_ref[...])
    first_half = (_lane_iota((x.shape[0], LANES)) % HEAD_DIM) < (HEAD_DIM // 2)
    outs = []
    for j in range(2):
        sl = slice(j * LANES, (j + 1) * LANES)
        xn = x[:, sl] * lax.rsqrt(ss[:, sl] * (1.0 / HEAD_DIM) + EPS) * gains[j]
        partner = jnp.where(first_half, pltpu.roll(xn, LANES - HEAD_DIM // 2, 1), pltpu.roll(xn, HEAD_DIM // 2, 1))
        out = xn * cos + partner * sin_signed
        outs.append(out * out_scales[j] if out_scales[j] != 1.0 else out)
    return outs


def _dup_heads(x):
    swapped = pltpu.roll(x, HEAD_DIM, 1)
    low = _lane_iota(x.shape) < HEAD_DIM
    return jnp.where(low, x, swapped), jnp.where(low, swapped, x)


def _store_vt(vt_ref, v):
    chunk = vt_ref.shape[4]
    for c in range(vt_ref.shape[2]):
        vt = v[c * chunk:(c + 1) * chunk, :].T
        for kv in range(vt_ref.shape[1]):
            vt_ref[0, kv, c, 0:HEAD_DIM, :] = vt[kv * HEAD_DIM:(kv + 1) * HEAD_DIM].astype(BF16)
            vt_ref[0, kv, c, HEAD_DIM:VT_ROWS, :] = jnp.ones((VT_ROWS - HEAD_DIM, chunk), BF16)


def _inproj_kernel(x_ref, mod_ref, g_ref, w_ref, wg_ref, bg_ref, qkg_ref, cos_ref, sin_ref, bd_ref,
                   aq_ref, ak_ref, av_ref, ar_ref, ag_ref, gq_ref, gk_ref, gv_ref, wq_ref, wk_ref, wv_ref,
                   gvt_ref, wvt_ref):
    hb = _norm_modulate(x_ref[...], g_ref[...], mod_ref, 3).astype(BF16)
    cos = cos_ref[...]
    sin = sin_ref[...]
    tiles = {}

    def proj(lo, width):
        j, off = divmod(lo, SLAB)
        assert off + width <= SLAB
        if j not in tiles:
            tiles[j] = _dot(hb, w_ref[:, j * SLAB:min((j + 1) * SLAB, _C_END)])
        return tiles[j][:, off:off + width]

    q_scale = HEAD_DIM ** -0.5
    g_glb_q, g_glb_k, g_win_q, g_win_k = (qkg_ref[r:r + 1, :] for r in range(4))

    order = (_C_GQ, _C_GQ + SLAB, _C_WQ, _C_GK, _C_GV, _C_AD, _C_AQ, _C_AV, _C_AR)

    def issue_ahead(n):
        for lo in order[:n + 3]:
            proj(lo, LANES)

    for j in range(GLB_HEADS * HEAD_DIM // SLAB):
        issue_ahead(j)
        halves = _norm_rope_slab(proj(_C_GQ + j * SLAB, SLAB), bd_ref, (g_glb_q, g_glb_q), cos, sin,
                                 (q_scale, q_scale))
        for i, q in enumerate(halves):
            gq_ref[:, j * SLAB + i * LANES:j * SLAB + (i + 1) * LANES] = q.astype(BF16)
    issue_ahead(2)
    halves = _norm_rope_slab(proj(_C_WQ, SLAB), bd_ref, (g_win_q, g_win_q), cos, sin, (q_scale, q_scale))
    for i, q in enumerate(halves):
        wq_ref[:, i * LANES:(i + 1) * LANES] = q.astype(BF16)
    issue_ahead(3)
    k_glb, k_win = _norm_rope_slab(proj(_C_GK, SLAB), bd_ref, (g_glb_k, g_win_k), cos, sin, (1.0, 1.0))
    k0, k1 = _dup_heads(k_glb)
    gk_ref[:, 0:LANES] = k0.astype(BF16)
    gk_ref[:, LANES:2 * LANES] = k1.astype(BF16)
    k0, k1 = _dup_heads(k_win)
    wk_ref[:, 0:LANES] = k0.astype(BF16)
    wk_ref[:, LANES:2 * LANES] = k1.astype(BF16)

    issue_ahead(4)
    v = proj(_C_GV, LANES)
    v0, v1 = _dup_heads(v)
    gv_ref[:, 0:LANES] = v0.astype(BF16)
    gv_ref[:, LANES:2 * LANES] = v1.astype(BF16)
    _store_vt(gvt_ref, v)

    v = proj(_C_WV, LANES)
    v0, v1 = _dup_heads(v)
    wv_ref[:, 0:LANES] = v0.astype(BF16)
    wv_ref[:, LANES:2 * LANES] = v1.astype(BF16)
    _store_vt(wvt_ref, v)

    issue_ahead(len(order))
    z = _dot(proj(_C_AD, GATE_PAD).astype(BF16), wg_ref[...]) + bg_ref[...]
    log_sig = jnp.minimum(z, 0.0) - jnp.log1p(jnp.exp(-jnp.abs(z)))
    ag_ref[...] = log_sig * (1.0 / GLA_GATE_TAU)
    aq_ref[...] = (proj(_C_AQ, GLA_QK) * (GLA_DK ** -0.5)).astype(BF16)
    ak_ref[...] = proj(_C_AK, GLA_QK).astype(BF16)
    av_ref[...] = proj(_C_AV, GLA_V).astype(BF16)
    ar_ref[...] = proj(_C_AR, GLA_V).astype(BF16)


def _inproj(x, mods, gain, w, wg, bg, qk_gains, cos, sin, bd, rows_per_batch, tm_want=512):
    t, d = x.shape
    tm = _row_tile(rows_per_batch, tm_want)
    tpb = rows_per_batch // tm
    widths = (GLA_QK, GLA_QK, GLA_V, GLA_V, 2 * GLA_QK, GLB_HEADS * HEAD_DIM, 2 * LANES, 2 * LANES,
              WIN_HEADS * HEAD_DIM, 2 * LANES, 2 * LANES)
    dtypes = (BF16, BF16, BF16, BF16, F32, BF16, BF16, BF16, BF16, BF16, BF16)
    nb = t // rows_per_batch
    vt_shapes, vt_specs = [], []
    for kv_heads, want in ((GLB_KV_HEADS, KEY_CHUNK), (WIN_KV_HEADS, WIN_CHUNK)):
        chunk = min(want, tm)
        vt_shapes.append(jax.ShapeDtypeStruct((nb, kv_heads, rows_per_batch // chunk, VT_ROWS, chunk), BF16))
        vt_specs.append(pl.BlockSpec((1, kv_heads, tm // chunk, VT_ROWS, chunk),
                                     lambda i: (i // tpb, 0, i % tpb, 0, 0)))
    return pl.pallas_call(
        _inproj_kernel,
        out_shape=tuple(jax.ShapeDtypeStruct((t, wd), dt) for wd, dt in zip(widths, dtypes)) + tuple(vt_shapes),
        grid=(t // tm,),
        in_specs=[
            pl.BlockSpec((tm, d), lambda i: (i, 0)),
            _mod_spec(mods, tpb),
            _const_spec((1, d)),
            _const_spec(w.shape),
            _const_spec(wg.shape),
            _const_spec(bg.shape),
            _const_spec(qk_gains.shape),
            pl.BlockSpec((tm, LANES), lambda i: (i % tpb, 0)),
            pl.BlockSpec((tm, LANES), lambda i: (i % tpb, 0)),
            _const_spec(bd.shape),
        ],
        out_specs=tuple(pl.BlockSpec((tm, wd), lambda i: (i, 0)) for wd in widths) + tuple(vt_specs),
        compiler_params=_cparams("parallel"),
        name="mix_inproj",
    )(x, mods, gain.reshape(1, d), w, wg, bg, qk_gains, cos, sin, bd)


def _gla_kernel(q_ref, k_ref, v_ref, g_ref, s0_ref, o_ref, sfin_ref, st_ref, *, n_chunks):
    c_len = GLA_CHUNK
    d = pl.program_id(1)
    i = pl.program_id(2)
    sign = 1 - 2 * d

    @pl.when(i == 0)
    def _():
        st_ref[...] = s0_ref[0, 0]

    group = min(GLA_GROUP, n_chunks)
    g_len = group * c_len
    row = lax.broadcasted_iota(jnp.int32, (g_len, g_len), 0)
    col = lax.broadcasted_iota(jnp.int32, (g_len, g_len), 1)
    keep = ((row - col) * sign >= 0) & (row // c_len == col // c_len)
    cum_op = jnp.where(keep, 1.0, 0.0).astype(BF16)
    row4 = lax.broadcasted_iota(jnp.int32, (c_len, GLA_HEADS * c_len), 0)
    col4 = lax.broadcasted_iota(jnp.int32, (c_len, GLA_HEADS * c_len), 1) % c_len
    keep4 = (row4 - col4) * sign >= 0
    qk_lane_head = _lane_iota((1, GLA_QK)) // GLA_DK
    v_lane_head = _lane_iota((1, GLA_V)) // GLA_DV
    qk_head_mask = [(qk_lane_head == h).astype(F32) for h in range(GLA_HEADS)]
    v_head_mask = [(v_lane_head == h).astype(BF16) for h in range(GLA_HEADS)]
    st_row_head = lax.broadcasted_iota(jnp.int32, (GLA_V, GLA_QK), 0) // GLA_DV
    st_col_head = lax.broadcasted_iota(jnp.int32, (GLA_V, GLA_QK), 1) // GLA_DK
    st_mask = st_row_head == st_col_head

    n_groups = n_chunks // group

    def group_starts(j):
        starts = []
        for p in range(group):
            c = j * group + p
            c = c + d * (n_chunks - 1 - 2 * c)
            starts.append(pl.multiple_of(c * c_len, c_len))
        return starts

    def decay_stage(j):
        starts = group_starts(j)
        q = jnp.concatenate([q_ref[pl.ds(r, c_len), :] for r in starts], axis=0).astype(F32)
        k = jnp.concatenate([k_ref[pl.ds(r, c_len), :] for r in starts], axis=0).astype(F32)
        g = jnp.concatenate([g_ref[pl.ds(r, c_len), :] for r in starts], axis=0)
        g_hi = g.astype(BF16)
        r1 = g - g_hi.astype(F32)
        g_mid = r1.astype(BF16)
        g_lo = (r1 - g_mid.astype(F32)).astype(BF16)
        b = _dot(cum_op, g_hi) + _dot(cum_op, g_mid) + _dot(cum_op, g_lo)
        q_in = (q * jnp.exp(b)).astype(BF16)
        k_out = k * jnp.exp(-b)
        decays = jnp.concatenate([jnp.exp(jnp.sum(g[p * c_len:(p + 1) * c_len], axis=0, keepdims=True))
                                  for p in range(group)], axis=0)
        return q_in, k_out, decays

    def chunk_group(j, staged):
        staged_next = decay_stage(jnp.minimum(j + 1, n_groups - 1))
        q_in, k_out, decays = staged
        starts = group_starts(j)
        vs = [v_ref[pl.ds(r, c_len), :] for r in starts]
        st = st_ref[...]
        for p in range(group):
            rows = slice(p * c_len, (p + 1) * c_len)
            decay = decays[p:p + 1, :]
            k_out_p = k_out[rows]
            k_dec = (k_out_p * decay).astype(BF16)
            k_stack = jnp.concatenate([(k_out_p * qk_head_mask[h]).astype(BF16) for h in range(GLA_HEADS)], axis=0)
            a = _dot_nt(q_in[rows], k_stack)
            a = jnp.where(keep4, a, 0.0).astype(BF16)
            v_bd = jnp.concatenate([vs[p] * v_head_mask[h] for h in range(GLA_HEADS)], axis=0)
            o = _dot(a, v_bd) + _dot_nt(q_in[rows], st.astype(BF16))
            o_ref[0, pl.ds(starts[p], c_len), :] = o
            ds_t = _dot_tn(vs[p], k_dec)
            st = st * decay + jnp.where(st_mask, ds_t, 0.0)
        st_ref[...] = st
        return staged_next

    lax.fori_loop(0, n_groups, chunk_group, decay_stage(0))

    @pl.when(i == pl.num_programs(2) - 1)
    def _():
        sfin_ref[0, 0] = st_ref[...]


def _gla(q, k, v, g, s0, rows_per_batch, tt_want=2048):
    t = q.shape[0]
    nb = t // rows_per_batch
    tt = _row_tile(rows_per_batch, tt_want)
    nt = rows_per_batch // tt

    def rows(b, d, i):
        return b * nt + i + d * (nt - 1 - 2 * i)

    return pl.pallas_call(
        functools.partial(_gla_kernel, n_chunks=tt // GLA_CHUNK),
        out_shape=(jax.ShapeDtypeStruct((2, t, GLA_V), F32),
                   jax.ShapeDtypeStruct((nb, 2, GLA_V, GLA_QK), F32)),
        grid=(nb, 2, nt),
        in_specs=[
            pl.BlockSpec((tt, GLA_QK), lambda b, d, i: (rows(b, d, i), 0)),
            pl.BlockSpec((tt, GLA_QK), lambda b, d, i: (rows(b, d, i), 0)),
            pl.BlockSpec((tt, GLA_V), lambda b, d, i: (rows(b, d, i), 0)),
            pl.BlockSpec((tt, GLA_QK), lambda b, d, i: (rows(b, d, i), d)),
            pl.BlockSpec((1, 1, GLA_V, GLA_QK), lambda b, d, i: (b, d, 0, 0)),
        ],
        out_specs=(pl.BlockSpec((1, tt, GLA_V), lambda b, d, i: (d, rows(b, d, i), 0)),
                   pl.BlockSpec((1, 1, GLA_V, GLA_QK), lambda b, d, i: (b, d, 0, 0))),
        scratch_shapes=[pltpu.VMEM((GLA_V, GLA_QK), F32)],
        compiler_params=_cparams("parallel", "parallel", "arbitrary"),
        name="gla_scan",
    )(q, k, v, g, s0)


def _half_masks(dtype):
    low = _lane_iota((1, LANES)) < HEAD_DIM
    return [low.astype(dtype), (~low).astype(dtype)]


def _rep(x, width):
    n = width // LANES
    return x if n == 1 else pltpu.repeat(x, n, axis=1)


def _glb_kernel(q_ref, k_ref, vt_ref, kc_ref, vtc_ref, o_ref, qt_ref, m_ref, acc_ref, s_ref, *,
                n_pairs, col_block, lookahead, chunks_per_tile):
    tq = q_ref.shape[0]
    n_lat = vt_ref.shape[2] // chunks_per_tile
    tk = vt_ref.shape[4] * chunks_per_tile
    n_heads = 2 * n_pairs
    for p in range(n_pairs):
        qt = (q_ref[:, p * LANES:(p + 1) * LANES].astype(F32) * LOG2_E).T
        qt_ref[2 * p] = qt[0:HEAD_DIM].astype(BF16)
        qt_ref[2 * p + 1] = qt[HEAD_DIM:2 * HEAD_DIM].astype(BF16)

    m_ref[...] = jnp.full(m_ref.shape, MASK_VALUE, F32)
    acc_ref[...] = jnp.zeros(acc_ref.shape, F32)

    units = [(h, cb) for h in range(n_heads) for cb in range(tq // col_block)]
    assert lookahead <= len(units)

    def scores(k, u):
        h, cb = units[u]
        return _dot(k[:, 0:HEAD_DIM], qt_ref[h, :, cb * col_block:(cb + 1) * col_block])

    def step(k, vt, k_next):
        w = k.shape[0]
        for u, (h, cb) in enumerate(units):
            s = s_ref[u, 0:w, :]
            if u + lookahead < len(units):
                s_ref[u + lookahead, 0:w, :] = scores(k, u + lookahead)
            elif k_next is not None:
                s_ref[u + lookahead - len(units), 0:k_next.shape[0], :] = scores(k_next, u + lookahead - len(units))
            cols = slice(cb * col_block, (cb + 1) * col_block)
            m_prev = m_ref[h, :, cols]
            m_new = jnp.maximum(m_prev, jnp.max(s, axis=0, keepdims=True))
            alpha = jnp.exp2(m_prev - m_new)
            e = jnp.exp2((s - m_new).astype(BF16))
            m_ref[h, :, cols] = m_new
            ch = w // len(vt)
            pv = _dot(vt[0], e[0:ch, :])
            for c in range(1, len(vt)):
                pv = pv + _dot(vt[c], e[c * ch:(c + 1) * ch, :])
            acc_ref[h, :, cols] = acc_ref[h, :, cols] * alpha + pv

    def lat_tile(j):
        return k_ref[pl.ds(pl.multiple_of(j * tk, tk), tk), :]

    def lat_vt(j):
        return [vt_ref[0, 0, j * chunks_per_tile + c] for c in range(chunks_per_tile)]

    wc = vtc_ref.shape[4]
    ctx_tiles = [kc_ref[j * wc:(j + 1) * wc, :] for j in range(vtc_ref.shape[2])]

    for u in range(lookahead):
        s_ref[u, 0:tk, :] = scores(lat_tile(0), u)

    def body(j, carry):
        step(lat_tile(j), lat_vt(j), lat_tile(j + 1))
        return carry

    lax.fori_loop(0, n_lat - 1, body, 0)
    step(lat_tile(n_lat - 1), lat_vt(n_lat - 1), ctx_tiles[0])
    for j, kc in enumerate(ctx_tiles):
        step(kc, [vtc_ref[0, 0, j]], ctx_tiles[j + 1] if j + 1 < len(ctx_tiles) else None)

    for p in range(n_pairs):
        halves = [acc_ref[h, 0:HEAD_DIM, :] / acc_ref[h, HEAD_DIM:HEAD_DIM + 1, :] for h in (2 * p, 2 * p + 1)]
        o_ref[:, p * LANES:(p + 1) * LANES] = jnp.concatenate(halves, axis=0).T.astype(o_ref.dtype)


def _glb_attn(q, k, vt, kc, vtc, seq, ctx_len, tq_want=512, col_block=512, lookahead=4, tk_want=512):
    t = q.shape[0]
    nb = t // seq
    groups = GLB_KV_HEADS
    gw = q.shape[1] // groups
    n_pairs = gw // LANES
    tq = _row_tile(seq, tq_want)
    nq = seq // tq
    col_block = min(col_block, tq)
    n_units = 2 * n_pairs * (tq // col_block)
    chunks_per_tile = max(1, min(tk_want, seq) // vt.shape[4])
    assert vt.shape[2] % chunks_per_tile == 0
    max_keys = max(vt.shape[4] * chunks_per_tile, vtc.shape[4])
    return pl.pallas_call(
        functools.partial(_glb_kernel, n_pairs=n_pairs, col_block=col_block, lookahead=lookahead,
                          chunks_per_tile=chunks_per_tile),
        out_shape=jax.ShapeDtypeStruct(q.shape, BF16),
        grid=(nb, groups, nq),
        in_specs=[
            pl.BlockSpec((tq, gw), lambda b, g, i: (b * nq + i, g)),
            pl.BlockSpec((seq, LANES), lambda b, g, i: (b, g)),
            pl.BlockSpec((1, 1) + vt.shape[2:], lambda b, g, i: (b, g, 0, 0, 0)),
            pl.BlockSpec((ctx_len, LANES), lambda b, g, i: (b, g)),
            pl.BlockSpec((1, 1) + vtc.shape[2:], lambda b, g, i: (b, g, 0, 0, 0)),
        ],
        out_specs=pl.BlockSpec((tq, gw), lambda b, g, i: (b * nq + i, g)),
        scratch_shapes=[pltpu.VMEM((2 * n_pairs, HEAD_DIM, tq), BF16),
                        pltpu.VMEM((2 * n_pairs, 1, tq), F32),
                        pltpu.VMEM((2 * n_pairs, VT_ROWS, tq), F32),
                        pltpu.VMEM((n_units, max_keys, col_block), F32)],
        compiler_params=_cparams("parallel", "parallel", "arbitrary"),
        name="glb_attn",
    )(q, k, vt, kc, vtc)


def _softmax_pair(q_pair, keys, values, masks, sink_pair, hm, low):
    out = None
    inv = []
    for half in range(2):
        qh = q_pair * hm[half]
        ss = []
        for kk, mk in zip(keys, masks):
            s = _dot_nt(qh, kk)
            ss.append(s if mk is None else jnp.where(mk, s, MASK_VALUE))
        m = ss[0].max(axis=1, keepdims=True)
        for s in ss[1:]:
            m = jnp.maximum(m, s.max(axis=1, keepdims=True))
        if sink_pair is not None:
            m = jnp.maximum(m, sink_pair[half])
        den = None
        for s, vv in zip(ss, values):
            e = jnp.exp(s - m)
            sm = jnp.sum(e, axis=1, keepdims=True)
            den = sm if den is None else den + sm
            part = _dot(e.astype(BF16), vv * hm[half])
            out = part if out is None else out + part
        if sink_pair is not None:
            den = den + jnp.exp(sink_pair[half] - m)
        inv.append(1.0 / den)
    return out * jnp.where(low, inv[0], inv[1])


def _win_kernel(sink_ref, q_ref, k_ref, vt_ref, kc_ref, vtc_ref, o_ref, sloc_ref, sctx_ref, *, tq, span):
    g = pl.program_id(1)
    seq = k_ref.shape[0]
    n_tiles = seq // tq
    wchunk = vt_ref.shape[4]
    kc = kc_ref[...][:, 0:HEAD_DIM]
    vt_ctx = jnp.concatenate([vtc_ref[0, 0, c] for c in range(vtc_ref.shape[2])], axis=1)
    col_minus_row = (lax.broadcasted_iota(jnp.int32, (span, tq), 1)
                     - lax.broadcasted_iota(jnp.int32, (span, tq), 0))

    def band_start(i):
        return pl.multiple_of(jnp.clip(i * tq - WINDOW, 0, seq - span), wchunk)

    def store_scores(i):
        k_loc = k_ref[pl.ds(band_start(i), span), :][:, 0:HEAD_DIM]
        qt = (q_ref[pl.ds(pl.multiple_of(i * tq, tq), tq), :].astype(F32) * LOG2_E).T
        for half in range(2):
            qt_h = qt[half * HEAD_DIM:(half + 1) * HEAD_DIM].astype(BF16)
            yield half, _dot(k_loc, qt_h), _dot(kc, qt_h)

    for half, s_loc, s_ctx in store_scores(0):
        sloc_ref[half] = s_loc
        sctx_ref[half] = s_ctx

    def tile(i, carry):
        q0 = pl.multiple_of(i * tq, tq)
        start = band_start(i)
        c0 = start // wchunk
        bias = jnp.where(jnp.abs(col_minus_row + (q0 - start)) <= WINDOW, 0.0, MASK_VALUE)
        vt_loc = jnp.concatenate([vt_ref[0, 0, c0 + c] for c in range(span // wchunk)], axis=1)
        ahead = store_scores(jnp.minimum(i + 1, n_tiles - 1))
        halves = []
        for half in range(2):
            sink2 = sink_ref[2 * g + half] * LOG2_E
            s_loc = sloc_ref[half] + bias
            s_ctx = sctx_ref[half]
            _, n_loc, n_ctx = next(ahead)
            sloc_ref[half] = n_loc
            sctx_ref[half] = n_ctx
            m = jnp.maximum(jnp.max(s_loc, axis=0, keepdims=True), jnp.max(s_ctx, axis=0, keepdims=True))
            m = jnp.maximum(m, sink2)
            e_loc = jnp.exp2((s_loc - m).astype(BF16))
            e_ctx = jnp.exp2((s_ctx - m).astype(BF16))
            acc = _dot(vt_loc, e_loc) + _dot(vt_ctx, e_ctx)
            den = acc[HEAD_DIM:HEAD_DIM + 1, :] + jnp.exp2(sink2 - m)
            halves.append(acc[0:HEAD_DIM, :] / den)
        o_ref[pl.ds(q0, tq), :] = jnp.concatenate(halves, axis=0).T.astype(o_ref.dtype)
        return carry

    lax.fori_loop(0, n_tiles, tile, 0)


def _win_attn(q, k, vt, kc, vtc, sink, seq, ctx_len, tq_want=512):
    t = q.shape[0]
    nb = t // seq
    groups = WIN_KV_HEADS
    assert q.shape[1] == groups * LANES
    tq = _row_tile(seq, min(tq_want, max(seq - 2 * WINDOW, WINDOW)))
    span = tq + 2 * WINDOW
    assert seq >= span and span % vt.shape[4] == 0 and WINDOW % vt.shape[4] == 0
    return pl.pallas_call(
        functools.partial(_win_kernel, tq=tq, span=span),
        out_shape=jax.ShapeDtypeStruct(q.shape, BF16),
        grid=(nb, groups),
        in_specs=[
            pl.BlockSpec(memory_space=pltpu.SMEM),
            pl.BlockSpec((seq, LANES), lambda b, g: (b, g)),
            pl.BlockSpec((seq, LANES), lambda b, g: (b, g)),
            pl.BlockSpec((1, 1) + vt.shape[2:], lambda b, g: (b, g, 0, 0, 0)),
            pl.BlockSpec((ctx_len, LANES), lambda b, g: (b, g)),
            pl.BlockSpec((1, 1) + vtc.shape[2:], lambda b, g: (b, g, 0, 0, 0)),
        ],
        out_specs=pl.BlockSpec((seq, LANES), lambda b, g: (b, g)),
        scratch_shapes=[pltpu.VMEM((2, span, tq), F32), pltpu.VMEM((2, ctx_len, tq), F32)],
        compiler_params=_cparams("parallel", "parallel"),
        name="win_attn",
    )(sink, q, k, vt, kc, vtc)


def _ctx_attn_kernel(sink_ref, q_ref, k_ref, v_ref, o_ref, *, n_pairs, use_sink):
    g = pl.program_id(1)
    hm = _half_masks(BF16)
    low = _lane_iota((1, LANES)) < HEAD_DIM
    k = k_ref[...]
    v = v_ref[...]
    for p in range(n_pairs):
        heads = (g * n_pairs + p) * 2
        sink_pair = (sink_ref[heads], sink_ref[heads + 1]) if use_sink else None
        o = _softmax_pair(q_ref[:, p * LANES:(p + 1) * LANES], [k], [v], [None], sink_pair, hm, low)
        o_ref[:, p * LANES:(p + 1) * LANES] = o.astype(o_ref.dtype)


def _ctx_attn(q, k, v, sink, groups, ctx_len, use_sink):
    t = q.shape[0]
    nb = t // ctx_len
    gw = q.shape[1] // groups
    n_pairs = gw // LANES
    return pl.pallas_call(
        functools.partial(_ctx_attn_kernel, n_pairs=n_pairs, use_sink=use_sink),
        out_shape=jax.ShapeDtypeStruct(q.shape, BF16),
        grid=(nb, groups),
        in_specs=[
            pl.BlockSpec(memory_space=pltpu.SMEM),
            pl.BlockSpec((ctx_len, gw), lambda b, g: (b, g)),
            pl.BlockSpec((ctx_len, LANES), lambda b, g: (b, g)),
            pl.BlockSpec((ctx_len, LANES), lambda b, g: (b, g)),
        ],
        out_specs=pl.BlockSpec((ctx_len, gw), lambda b, g: (b, g)),
        compiler_params=_cparams("parallel", "parallel"),
        name="ctx_attn",
    )(sink, q, k, v)


def _outproj_ffn_kernel(x_ref, mod_ref, of_ref, ob_ref, r_ref, og_ref, ow_ref, gain_ref, bd_ref, w_ref,
                        g2_ref, win_ref, wout_ref, o_ref, *, f):
    o = of_ref[0] + ob_ref[0]
    r = r_ref[...].astype(F32)
    gate = r * jax.nn.sigmoid(r)
    ss = _dot((o * o).astype(BF16), bd_ref[...])
    on = o * lax.rsqrt(ss * (1.0 / GLA_DV) + EPS) * gain_ref[...]
    cat = jnp.concatenate([(on * gate).astype(BF16), og_ref[...], ow_ref[...]], axis=-1)
    x = x_ref[...] + mod_ref[0, 5:6, :] * _dot(cat, w_ref[...])
    o_ref[...] = _ffn_body(x, mod_ref, g2_ref, win_ref, wout_ref, 6, f)


def _outproj_ffn(x, mods, o_gla, r, o_glb, o_win, gla_gain, bd, w_out, gain2, w2_in, w2_out, rows_per_batch,
                 tm_want=512):
    t, d = x.shape
    f = w2_out.shape[0]
    tm = _row_tile(rows_per_batch, tm_want)
    return pl.pallas_call(
        functools.partial(_outproj_ffn_kernel, f=f),
        out_shape=jax.ShapeDtypeStruct((t, d), F32),
        grid=(t // tm,),
        in_specs=[
            pl.BlockSpec((tm, d), lambda i: (i, 0)),
            _mod_spec(mods, rows_per_batch // tm),
            pl.BlockSpec((1, tm, GLA_V), lambda i: (0, i, 0)),
            pl.BlockSpec((1, tm, GLA_V), lambda i: (1, i, 0)),
            pl.BlockSpec((tm, GLA_V), lambda i: (i, 0)),
            pl.BlockSpec((tm, o_glb.shape[1]), lambda i: (i, 0)),
            pl.BlockSpec((tm, o_win.shape[1]), lambda i: (i, 0)),
            _const_spec(gla_gain.shape),
            _const_spec(bd.shape),
            _const_spec(w_out.shape),
            _const_spec((1, d)),
            _const_spec((d, 2 * f)),
            _const_spec((f, d)),
        ],
        out_specs=pl.BlockSpec((tm, d), lambda i: (i, 0)),
        compiler_params=_cparams("parallel"),
        name="mix_outproj_ffn",
    )(x, mods, o_gla, o_gla, r, o_glb, o_win, gla_gain, bd, w_out, gain2.reshape(1, d), w2_in, w2_out)


def _rope_tables(seq):
    rows = seq // GRID_W
    row = jnp.repeat(jnp.arange(rows, dtype=F32), GRID_W)
    col = (jnp.arange(rows * GRID_W) % GRID_W).astype(F32)
    n_freq = HEAD_DIM // 4
    inv = jnp.power(ROPE_BASE, -jnp.arange(n_freq, dtype=F32) / n_freq)
    ang = jnp.concatenate([row[:, None] * inv, col[:, None] * inv], axis=-1)
    cos, sin = jnp.cos(ang), jnp.sin(ang)
    cos_t = jnp.concatenate([cos, cos] * (LANES // HEAD_DIM), axis=-1)
    sin_t = jnp.concatenate([-sin, sin] * (LANES // HEAD_DIM), axis=-1)
    return cos_t, sin_t


def _block_diag_ones(n, block):
    idx = np.arange(n) // block
    return jnp.asarray(idx[:, None] == idx[None, :], dtype=BF16)


def kernel(x, c, ctx, c_ctx, mod_w, mod_b, norm_ffn1, ffn1_w_in, ffn1_w_out, norm_mix, mix_w_in, mix_w_out,
           gla_wg_f, gla_bg_f, gla_wg_b, gla_bg_b, gla_out_norm, glb_q_norm, glb_k_norm,
           win_q_norm, win_k_norm, win_sink, norm_ffn2, ffn2_w_in, ffn2_w_out):
    bsz, seq, d = x.shape
    ctx_len = ctx.shape[1]
    depth = mod_w.shape[0]
    in_splits = (GLA_QK, GLA_QK, GLA_V, GLA_V, 2 * GLA_GATE_RANK,
                 GLB_HEADS * HEAD_DIM, GLB_KV_HEADS * HEAD_DIM, GLB_KV_HEADS * HEAD_DIM,
                 WIN_HEADS * HEAD_DIM, WIN_KV_HEADS * HEAD_DIM, WIN_KV_HEADS * HEAD_DIM)

    n_rows = -(-(bsz + 1) // 8) * 8
    c_rows = jnp.concatenate([c, c_ctx[None, :], jnp.zeros((n_rows - bsz - 1, d), F32)], axis=0)
    mods = _modvec(c_rows, mod_w, mod_b)

    cos_l, sin_l = _rope_tables(seq)
    cos_c = jnp.ones((ctx_len, LANES), F32)
    sin_c = jnp.zeros((ctx_len, LANES), F32)
    bd = _block_diag_ones(SLAB, HEAD_DIM)
    zero_state = jnp.zeros((bsz, 2, GLA_V, GLA_QK), F32)

    xl = x.reshape(bsz * seq, d)
    xc = ctx.reshape(bsz * ctx_len, d)

    for l in range(depth):
        need_ctx = l < depth - 1
        mods_l = mods[l, :bsz].reshape(bsz, N_MOD, d)
        mods_c = mods[l, bsz:bsz + 1].reshape(1, N_MOD, d)
        w1_in, w1_out = ffn1_w_in[l].astype(BF16), ffn1_w_out[l].astype(BF16)
        w2_in, w2_out = ffn2_w_in[l].astype(BF16), ffn2_w_out[l].astype(BF16)
        wm = mix_w_in[l]
        src = dict(zip(("aq", "ak", "av", "ar", "ad", "gq", "gk", "gv", "wq", "wk", "wv"),
                       jnp.split(wm, np.cumsum(in_splits)[:-1], axis=1)))
        w_mix = jnp.concatenate([src[n] for n in ("aq", "ak", "av", "ar", "gq", "gk", "wk", "gv", "wv", "wq", "ad")]
                                + [jnp.zeros((d, GATE_PAD - 2 * GLA_GATE_RANK), F32)], axis=1).astype(BF16)
        assert w_mix.shape[1] == _C_END
        wg = jnp.zeros((GATE_PAD, 2 * GLA_QK), F32)
        wg = wg.at[:GLA_GATE_RANK, :GLA_QK].set(gla_wg_f[l])
        wg = wg.at[GLA_GATE_RANK:2 * GLA_GATE_RANK, GLA_QK:].set(gla_wg_b[l]).astype(BF16)
        bg = jnp.concatenate([gla_bg_f[l], gla_bg_b[l]])[None, :]
        qk_gains = jnp.stack([jnp.tile(gn[l], LANES // HEAD_DIM)
                              for gn in (glb_q_norm, glb_k_norm, win_q_norm, win_k_norm)])
        gla_gain = jnp.tile(gla_out_norm[l], GLA_HEADS)[None, :]
        w_out = mix_w_out[l].astype(BF16)
        sink = win_sink[l]

        xl = _ffn(xl, mods_l, 0, norm_ffn1[l], w1_in, w1_out, seq)
        xc = _ffn(xc, mods_c, 0, norm_ffn1[l], w1_in, w1_out, ctx_len)

        pc = _inproj(xc, mods_c, norm_mix[l], w_mix, wg, bg, qk_gains, cos_c, sin_c, bd, ctx_len)
        pq = _inproj(xl, mods_l, norm_mix[l], w_mix, wg, bg, qk_gains, cos_l, sin_l, bd, seq)
        aq, ak, av, ar, ag, gq, gk, gv, wq, wk, wv, gvt, wvt = pq
        aqc, akc, avc, arc, agc, gqc, gkc, gvc, wqc, wkc, wvc, gvtc, wvtc = pc

        oc_gla, states = _gla(aqc, akc, avc, agc, zero_state, ctx_len)
        o_gla, _ = _gla(aq, ak, av, ag, states, seq)
        o_glb = _glb_attn(gq, gk, gvt, gkc, gvtc, seq, ctx_len)
        o_win = _win_attn(wq, wk, wvt, wkc, wvtc, sink, seq, ctx_len)
        xl = _outproj_ffn(xl, mods_l, o_gla, ar, o_glb, o_win, gla_gain, bd, w_out, norm_ffn2[l], w2_in, w2_out, seq)

        if need_ctx:
            oc_glb = _ctx_attn(gqc, gkc, gvc, sink, GLB_KV_HEADS, ctx_len, use_sink=False)
            oc_win = _ctx_attn(wqc, wkc, wvc, sink, WIN_KV_HEADS, ctx_len, use_sink=True)
            xc = _outproj_ffn(xc, mods_c, oc_gla, arc, oc_glb, oc_win, gla_gain, bd, w_out, norm_ffn2[l],
                              w2_in, w2_out, ctx_len)

    return xl.reshape(bsz, seq, d)
```

```python
import functools

import numpy as np
import jax
import jax.numpy as jnp
from jax import lax
from jax.experimental import pallas as pl
from jax.experimental.pallas import tpu as pltpu

GRID_W = 64
HEAD_DIM = 64
GLA_HEADS = 4
GLA_DK = 32
GLA_DV = 64
GLA_GATE_RANK = 16
GLA_GATE_TAU = 16.0
GLA_CHUNK = 64
GLB_HEADS = 8
GLB_KV_HEADS = 2
WIN_HEADS = 4
WIN_KV_HEADS = 2
WINDOW = 128
ROPE_BASE = 10000.0
N_MOD = 9
EPS = 1e-6

LANES = 128
VMEM_LIMIT_BYTES = 56 * 1024 * 1024

GLA_QK = GLA_HEADS * GLA_DK
GLA_V = GLA_HEADS * GLA_DV
GATE_PAD = LANES
MASK_VALUE = -1e30
LOG2_E = 1.4426950408889634
KEY_CHUNK = 256
GLA_GROUP = 4
WIN_CHUNK = 128
VT_ROWS = HEAD_DIM + 16

BF16 = jnp.bfloat16
F32 = jnp.float32


def _cparams(*sem):
    return pltpu.CompilerParams(dimension_semantics=sem, vmem_limit_bytes=VMEM_LIMIT_BYTES)


def _dot(a, b):
    return jnp.dot(a, b, preferred_element_type=F32)


def _dot_nt(a, b):
    return lax.dot_general(a, b, (((1,), (1,)), ((), ())), preferred_element_type=F32)


def _dot_tn(a, b):
    return lax.dot_general(a, b, (((0,), (0,)), ((), ())), preferred_element_type=F32)


def _lane_iota(shape):
    return lax.broadcasted_iota(jnp.int32, shape, len(shape) - 1)


def _modvec_kernel(c_ref, w_ref, b_ref, o_ref):
    c = c_ref[...]
    s = (c * jax.nn.sigmoid(c)).astype(BF16)
    o_ref[0] = _dot(s, w_ref[0].astype(BF16)) + b_ref[0]


def _modvec(c_rows, mod_w, mod_b):
    depth, d, n = mod_w.shape
    rows = c_rows.shape[0]
    tn = n // N_MOD
    return pl.pallas_call(
        _modvec_kernel,
        out_shape=jax.ShapeDtypeStruct((depth, rows, n), F32),
        grid=(depth, n // tn),
        in_specs=[
            pl.BlockSpec((rows, d), lambda l, j: (0, 0)),
            pl.BlockSpec((1, d, tn), lambda l, j: (l, 0, j)),
            pl.BlockSpec((1, 1, tn), lambda l, j: (l, 0, j)),
        ],
        out_specs=pl.BlockSpec((1, rows, tn), lambda l, j: (l, 0, j)),
        compiler_params=_cparams("arbitrary", "arbitrary"),
        name="modvec",
    )(c_rows, mod_w, mod_b.reshape(depth, 1, n))


def _norm_modulate(x, gain, mod_ref, k_shift):
    shift = mod_ref[0, k_shift:k_shift + 1, :]
    scale = mod_ref[0, k_shift + 1:k_shift + 2, :]
    y = x * lax.rsqrt(jnp.mean(x * x, axis=-1, keepdims=True) + EPS)
    return (y * gain) * (1.0 + scale) + shift


def _mod_spec(mods, tiles_per_batch):
    nb, nm, d = mods.shape
    if nb == 1:
        return pl.BlockSpec((1, nm, d), lambda i: (0, 0, 0))
    return pl.BlockSpec((1, nm, d), lambda i: (i // tiles_per_batch, 0, 0))


def _const_spec(shape):
    nd = len(shape)
    return pl.BlockSpec(shape, lambda *_: (0,) * nd, pipeline_mode=pl.Buffered(1))


def _row_tile(n_rows_per_batch, want):
    t = min(want, n_rows_per_batch)
    assert n_rows_per_batch % t == 0
    return t


def _ffn_chunks(f):
    step = 1536 if f > 1536 else f
    return tuple((lo, min(lo + step, f)) for lo in range(0, f, step))


def _ffn_kernel(x_ref, mod_ref, g_ref, win_ref, wout_ref, o_ref, *, k0, f):
    o_ref[...] = _ffn_body(x_ref[...], mod_ref, g_ref, win_ref, wout_ref, k0, f)


def _ffn_body(x, mod_ref, g_ref, win_ref, wout_ref, k0, f):
    hb = _norm_modulate(x, g_ref[...], mod_ref, k0).astype(BF16)
    gate = mod_ref[0, k0 + 2:k0 + 3, :]
    acc = None
    for lo, hi in _ffn_chunks(f):
        a = _dot(hb, win_ref[:, lo:hi])
        b = _dot(hb, win_ref[:, f + lo:f + hi])
        u = ((a * jax.nn.sigmoid(a)) * b).astype(BF16)
        part = _dot(u, wout_ref[lo:hi, :])
        acc = part if acc is None else acc + part
    return x + (0.5 * gate) * acc


def _ffn(x, mods, k0, gain, w_in, w_out, rows_per_batch, tm_want=512):
    t, d = x.shape
    f = w_out.shape[0]
    tm = _row_tile(rows_per_batch, tm_want)
    return pl.pallas_call(
        functools.partial(_ffn_kernel, k0=k0, f=f),
        out_shape=jax.ShapeDtypeStruct((t, d), F32),
        grid=(t // tm,),
        in_specs=[
            pl.BlockSpec((tm, d), lambda i: (i, 0)),
            _mod_spec(mods, rows_per_batch // tm),
            _const_spec((1, d)),
            _const_spec((d, 2 * f)),
            _const_spec((f, d)),
        ],
        out_specs=pl.BlockSpec((tm, d), lambda i: (i, 0)),
        compiler_params=_cparams("parallel"),
        name="ffn",
    )(x, mods, gain.reshape(1, d), w_in, w_out)


_C_AQ = 0
_C_AK = _C_AQ + GLA_QK
_C_AV = _C_AK + GLA_QK
_C_AR = _C_AV + GLA_V
_C_GQ = _C_AR + GLA_V
_C_GK = _C_GQ + GLB_HEADS * HEAD_DIM
_C_WK = _C_GK + GLB_KV_HEADS * HEAD_DIM
_C_GV = _C_WK + WIN_KV_HEADS * HEAD_DIM
_C_WV = _C_GV + GLB_KV_HEADS * HEAD_DIM
_C_WQ = _C_WV + WIN_KV_HEADS * HEAD_DIM
_C_AD = _C_WQ + WIN_HEADS * HEAD_DIM
_C_END = _C_AD + GATE_PAD
SLAB = 2 * LANES


def _norm_rope_slab(x, bd_ref, gains, cos, sin_signed, out_scales):
    ss = _dot((x * x).astype(BF16), bd_ref[...])
    first_half = (_lane_iota((x.shape[0], LANES)) % HEAD_DIM) < (HEAD_DIM // 2)
    outs = []
    for j in range(2):
        sl = slice(j * LANES, (j + 1) * LANES)
        xn = x[:, sl] * lax.rsqrt(ss[:, sl] * (1.0 / HEAD_DIM) + EPS) * gains[j]
        partner = jnp.where(first_half, pltpu.roll(xn, LANES - HEAD_DIM // 2, 1), pltpu.roll(xn, HEAD_DIM // 2, 1))
        out = xn * cos + partner * sin_signed
        outs.append(out * out_scales[j] if out_scales[j] != 1.0 else out)
    return outs


def _dup_heads(x):
    swapped = pltpu.roll(x, HEAD_DIM, 1)
    low = _lane_iota(x.shape) < HEAD_DIM
    return jnp.where(low, x, swapped), jnp.where(low, swapped, x)


def _store_vt(vt_ref, v):
    chunk = vt_ref.shape[4]
    for c in range(vt_ref.shape[2]):
        vt = v[c * chunk:(c + 1) * chunk, :].T
        for kv in range(vt_ref.shape[1]):
            vt_ref[0, kv, c, 0:HEAD_DIM, :] = vt[kv * HEAD_DIM:(kv + 1) * HEAD_DIM].astype(BF16)
            vt_ref[0, kv, c, HEAD_DIM:VT_ROWS, :] = jnp.ones((VT_ROWS - HEAD_DIM, chunk), BF16)


def _inproj_kernel(x_ref, mod_ref, g_ref, w_ref, wg_ref, bg_ref, qkg_ref, cos_ref, sin_ref, bd_ref,
                   aq_ref, ak_ref, av_ref, ar_ref, ag_ref, gq_ref, gk_ref, gv_ref, wq_ref, wk_ref, wv_ref,
                   gvt_ref, wvt_ref):
    hb = _norm_modulate(x_ref[...], g_ref[...], mod_ref, 3).astype(BF16)
    cos = cos_ref[...]
    sin = sin_ref[...]
    tiles = {}

    def proj(lo, width):
        j, off = divmod(lo, SLAB)
        assert off + width <= SLAB
        if j not in tiles:
            tiles[j] = _dot(hb, w_ref[:, j * SLAB:min((j + 1) * SLAB, _C_END)])
        return tiles[j][:, off:off + width]

    q_scale = HEAD_DIM ** -0.5
    g_glb_q, g_glb_k, g_win_q, g_win_k = (qkg_ref[r:r + 1, :] for r in range(4))

    order = (_C_GQ, _C_GQ + SLAB, _C_WQ, _C_GK, _C_GV, _C_AD, _C_AQ, _C_AV, _C_AR)

    def issue_ahead(n):
        for lo in order[:n + 3]:
            proj(lo, LANES)

    for j in range(GLB_HEADS * HEAD_DIM // SLAB):
        issue_ahead(j)
        halves = _norm_rope_slab(proj(_C_GQ + j * SLAB, SLAB), bd_ref, (g_glb_q, g_glb_q), cos, sin,
                                 (q_scale, q_scale))
        for i, q in enumerate(halves):
            gq_ref[:, j * SLAB + i * LANES:j * SLAB + (i + 1) * LANES] = q.astype(BF16)
    issue_ahead(2)
    halves = _norm_rope_slab(proj(_C_WQ, SLAB), bd_ref, (g_win_q, g_win_q), cos, sin, (q_scale, q_scale))
    for i, q in enumerate(halves):
        wq_ref[:, i * LANES:(i + 1) * LANES] = q.astype(BF16)
    issue_ahead(3)
    k_glb, k_win = _norm_rope_slab(proj(_C_GK, SLAB), bd_ref, (g_glb_k, g_win_k), cos, sin, (1.0, 1.0))
    k0, k1 = _dup_heads(k_glb)
    gk_ref[:, 0:LANES] = k0.astype(BF16)
    gk_ref[:, LANES:2 * LANES] = k1.astype(BF16)
    k0, k1 = _dup_heads(k_win)
    wk_ref[:, 0:LANES] = k0.astype(BF16)
    wk_ref[:, LANES:2 * LANES] = k1.astype(BF16)

    issue_ahead(4)
    v = proj(_C_GV, LANES)
    v0, v1 = _dup_heads(v)
    gv_ref[:, 0:LANES] = v0.astype(BF16)
    gv_ref[:, LANES:2 * LANES] = v1.astype(BF16)
    _store_vt(gvt_ref, v)

    v = proj(_C_WV, LANES)
    v0, v1 = _dup_heads(v)
    wv_ref[:, 0:LANES] = v0.astype(BF16)
    wv_ref[:, LANES:2 * LANES] = v1.astype(BF16)
    _store_vt(wvt_ref, v)

    issue_ahead(len(order))
    z = _dot(proj(_C_AD, GATE_PAD).astype(BF16), wg_ref[...]) + bg_ref[...]
    log_sig = jnp.minimum(z, 0.0) - jnp.log1p(jnp.exp(-jnp.abs(z)))
    ag_ref[...] = log_sig * (1.0 / GLA_GATE_TAU)
    aq_ref[...] = (proj(_C_AQ, GLA_QK) * (GLA_DK ** -0.5)).astype(BF16)
    ak_ref[...] = proj(_C_AK, GLA_QK).astype(BF16)
    av_ref[...] = proj(_C_AV, GLA_V).astype(BF16)
    ar_ref[...] = proj(_C_AR, GLA_V).astype(BF16)


def _inproj(x, mods, gain, w, wg, bg, qk_gains, cos, sin, bd, rows_per_batch, tm_want=512):
    t, d = x.shape
    tm = _row_tile(rows_per_batch, tm_want)
    tpb = rows_per_batch // tm
    widths = (GLA_QK, GLA_QK, GLA_V, GLA_V, 2 * GLA_QK, GLB_HEADS * HEAD_DIM, 2 * LANES, 2 * LANES,
              WIN_HEADS * HEAD_DIM, 2 * LANES, 2 * LANES)
    dtypes = (BF16, BF16, BF16, BF16, F32, BF16, BF16, BF16, BF16, BF16, BF16)
    nb = t // rows_per_batch
    vt_shapes, vt_specs = [], []
    for kv_heads, want in ((GLB_KV_HEADS, KEY_CHUNK), (WIN_KV_HEADS, WIN_CHUNK)):
        chunk = min(want, tm)
        vt_shapes.append(jax.ShapeDtypeStruct((nb, kv_heads, rows_per_batch // chunk, VT_ROWS, chunk), BF16))
        vt_specs.append(pl.BlockSpec((1, kv_heads, tm // chunk, VT_ROWS, chunk),
                                     lambda i: (i // tpb, 0, i % tpb, 0, 0)))
    return pl.pallas_call(
        _inproj_kernel,
        out_shape=tuple(jax.ShapeDtypeStruct((t, wd), dt) for wd, dt in zip(widths, dtypes)) + tuple(vt_shapes),
        grid=(t // tm,),
        in_specs=[
            pl.BlockSpec((tm, d), lambda i: (i, 0)),
            _mod_spec(mods, tpb),
            _const_spec((1, d)),
            _const_spec(w.shape),
            _const_spec(wg.shape),
            _const_spec(bg.shape),
            _const_spec(qk_gains.shape),
            pl.BlockSpec((tm, LANES), lambda i: (i % tpb, 0)),
            pl.BlockSpec((tm, LANES), lambda i: (i % tpb, 0)),
            _const_spec(bd.shape),
        ],
        out_specs=tuple(pl.BlockSpec((tm, wd), lambda i: (i, 0)) for wd in widths) + tuple(vt_specs),
        compiler_params=_cparams("parallel"),
        name="mix_inproj",
    )(x, mods, gain.reshape(1, d), w, wg, bg, qk_gains, cos, sin, bd)


def _gla_kernel(q_ref, k_ref, v_ref, g_ref, s0_ref, o_ref, sfin_ref, st_ref, *, n_chunks):
    c_len = GLA_CHUNK
    d = pl.program_id(1)
    i = pl.program_id(2)
    sign = 1 - 2 * d

    @pl.when(i == 0)
    def _():
        st_ref[...] = s0_ref[0, 0]

    group = min(GLA_GROUP, n_chunks)
    g_len = group * c_len
    row = lax.broadcasted_iota(jnp.int32, (g_len, g_len), 0)
    col = lax.broadcasted_iota(jnp.int32, (g_len, g_len), 1)
    keep = ((row - col) * sign >= 0) & (row // c_len == col // c_len)
    cum_op = jnp.where(keep, 1.0, 0.0).astype(BF16)
    row4 = lax.broadcasted_iota(jnp.int32, (c_len, GLA_HEADS * c_len), 0)
    col4 = lax.broadcasted_iota(jnp.int32, (c_len, GLA_HEADS * c_len), 1) % c_len
    keep4 = (row4 - col4) * sign >= 0
    qk_lane_head = _lane_iota((1, GLA_QK)) // GLA_DK
    v_lane_head = _lane_iota((1, GLA_V)) // GLA_DV
    qk_head_mask = [(qk_lane_head == h).astype(F32) for h in range(GLA_HEADS)]
    v_head_mask = [(v_lane_head == h).astype(BF16) for h in range(GLA_HEADS)]
    st_row_head = lax.broadcasted_iota(jnp.int32, (GLA_V, GLA_QK), 0) // GLA_DV
    st_col_head = lax.broadcasted_iota(jnp.int32, (GLA_V, GLA_QK), 1) // GLA_DK
    st_mask = st_row_head == st_col_head

    n_groups = n_chunks // group

    def group_starts(j):
        starts = []
        for p in range(group):
            c = j * group + p
            c = c + d * (n_chunks - 1 - 2 * c)
            starts.append(pl.multiple_of(c * c_len, c_len))
        return starts

    def decay_stage(j):
        starts = group_starts(j)
        q = jnp.concatenate([q_ref[pl.ds(r, c_len), :] for r in starts], axis=0).astype(F32)
        k = jnp.concatenate([k_ref[pl.ds(r, c_len), :] for r in starts], axis=0).astype(F32)
        g = jnp.concatenate([g_ref[pl.ds(r, c_len), :] for r in starts], axis=0)
        g_hi = g.astype(BF16)
        r1 = g - g_hi.astype(F32)
        g_mid = r1.astype(BF16)
        g_lo = (r1 - g_mid.astype(F32)).astype(BF16)
        b = _dot(cum_op, g_hi) + _dot(cum_op, g_mid) + _dot(cum_op, g_lo)
        q_in = (q * jnp.exp(b)).astype(BF16)
        k_out = k * jnp.exp(-b)
        decays = jnp.concatenate([jnp.exp(jnp.sum(g[p * c_len:(p + 1) * c_len], axis=0, keepdims=True))
                                  for p in range(group)], axis=0)
        return q_in, k_out, decays

    def chunk_group(j, staged):
        staged_next = decay_stage(jnp.minimum(j + 1, n_groups - 1))
        q_in, k_out, decays = staged
        starts = group_starts(j)
        vs = [v_ref[pl.ds(r, c_len), :] for r in starts]
        st = st_ref[...]
        for p in range(group):
            rows = slice(p * c_len, (p + 1) * c_len)
            decay = decays[p:p + 1, :]
            k_out_p = k_out[rows]
            k_dec = (k_out_p * decay).astype(BF16)
            k_stack = jnp.concatenate([(k_out_p * qk_head_mask[h]).astype(BF16) for h in range(GLA_HEADS)], axis=0)
            a = _dot_nt(q_in[rows], k_stack)
            a = jnp.where(keep4, a, 0.0).astype(BF16)
            v_bd = jnp.concatenate([vs[p] * v_head_mask[h] for h in range(GLA_HEADS)], axis=0)
            o = _dot(a, v_bd) + _dot_nt(q_in[rows], st.astype(BF16))
            o_ref[0, pl.ds(starts[p], c_len), :] = o
            ds_t = _dot_tn(vs[p], k_dec)
            st = st * decay + jnp.where(st_mask, ds_t, 0.0)
        st_ref[...] = st
        return staged_next

    lax.fori_loop(0, n_groups, chunk_group, decay_stage(0))

    @pl.when(i == pl.num_programs(2) - 1)
    def _():
        sfin_ref[0, 0] = st_ref[...]


def _gla(q, k, v, g, s0, rows_per_batch, tt_want=2048):
    t = q.shape[0]
    nb = t // rows_per_batch
    tt = _row_tile(rows_per_batch, tt_want)
    nt = rows_per_batch // tt

    def rows(b, d, i):
        return b * nt + i + d * (nt - 1 - 2 * i)

    return pl.pallas_call(
        functools.partial(_gla_kernel, n_chunks=tt // GLA_CHUNK),
        out_shape=(jax.ShapeDtypeStruct((2, t, GLA_V), F32),
                   jax.ShapeDtypeStruct((nb, 2, GLA_V, GLA_QK), F32)),
        grid=(nb, 2, nt),
        in_specs=[
            pl.BlockSpec((tt, GLA_QK), lambda b, d, i: (rows(b, d, i), 0)),
            pl.BlockSpec((tt, GLA_QK), lambda b, d, i: (rows(b, d, i), 0)),
            pl.BlockSpec((tt, GLA_V), lambda b, d, i: (rows(b, d, i), 0)),
            pl.BlockSpec((tt, GLA_QK), lambda b, d, i: (rows(b, d, i), d)),
            pl.BlockSpec((1, 1, GLA_V, GLA_QK), lambda b, d, i: (b, d, 0, 0)),
        ],
        out_specs=(pl.BlockSpec((1, tt, GLA_V), lambda b, d, i: (d, rows(b, d, i), 0)),
                   pl.BlockSpec((1, 1, GLA_V, GLA_QK), lambda b, d, i: (b, d, 0, 0))),
        scratch_shapes=[pltpu.VMEM((GLA_V, GLA_QK), F32)],
        compiler_params=_cparams("parallel", "parallel", "arbitrary"),
        name="gla_scan",
    )(q, k, v, g, s0)


def _half_masks(dtype):
    low = _lane_iota((1, LANES)) < HEAD_DIM
    return [low.astype(dtype), (~low).astype(dtype)]


def _rep(x, width):
    n = width // LANES
    return x if n == 1 else pltpu.repeat(x, n, axis=1)


def _glb_kernel(q_ref, k_ref, vt_ref, kc_ref, vtc_ref, o_ref, qt_ref, m_ref, acc_ref, s_ref, *,
                n_pairs, col_block, lookahead, chunks_per_tile):
    tq = q_ref.shape[0]
    n_lat = vt_ref.shape[2] // chunks_per_tile
    tk = vt_ref.shape[4] * chunks_per_tile
    n_heads = 2 * n_pairs
    for p in range(n_pairs):
        qt = (q_ref[:, p * LANES:(p + 1) * LANES].astype(F32) * LOG2_E).T
        qt_ref[2 * p] = qt[0:HEAD_DIM].astype(BF16)
        qt_ref[2 * p + 1] = qt[HEAD_DIM:2 * HEAD_DIM].astype(BF16)

    m_ref[...] = jnp.full(m_ref.shape, MASK_VALUE, F32)
    acc_ref[...] = jnp.zeros(acc_ref.shape, F32)

    units = [(h, cb) for h in range(n_heads) for cb in range(tq // col_block)]
    assert lookahead <= len(units)

    def scores(k, u):
        h, cb = units[u]
        return _dot(k[:, 0:HEAD_DIM], qt_ref[h, :, cb * col_block:(cb + 1) * col_block])

    def step(k, vt, k_next):
        w = k.shape[0]
        for u, (h, cb) in enumerate(units):
            s = s_ref[u, 0:w, :]
            if u + lookahead < len(units):
                s_ref[u + lookahead, 0:w, :] = scores(k, u + lookahead)
            elif k_next is not None:
                s_ref[u + lookahead - len(units), 0:k_next.shape[0], :] = scores(k_next, u + lookahead - len(units))
            cols = slice(cb * col_block, (cb + 1) * col_block)
            m_prev = m_ref[h, :, cols]
            m_new = jnp.maximum(m_prev, jnp.max(s, axis=0, keepdims=True))
            alpha = jnp.exp2(m_prev - m_new)
            e = jnp.exp2((s - m_new).astype(BF16))
            m_ref[h, :, cols] = m_new
            ch = w // len(vt)
            pv = _dot(vt[0], e[0:ch, :])
            for c in range(1, len(vt)):
                pv = pv + _dot(vt[c], e[c * ch:(c + 1) * ch, :])
            acc_ref[h, :, cols] = acc_ref[h, :, cols] * alpha + pv

    def lat_tile(j):
        return k_ref[pl.ds(pl.multiple_of(j * tk, tk), tk), :]

    def lat_vt(j):
        return [vt_ref[0, 0, j * chunks_per_tile + c] for c in range(chunks_per_tile)]

    wc = vtc_ref.shape[4]
    ctx_tiles = [kc_ref[j * wc:(j + 1) * wc, :] for j in range(vtc_ref.shape[2])]

    for u in range(lookahead):
        s_ref[u, 0:tk, :] = scores(lat_tile(0), u)

    def body(j, carry):
        step(lat_tile(j), lat_vt(j), lat_tile(j + 1))
        return carry

    lax.fori_loop(0, n_lat - 1, body, 0)
    step(lat_tile(n_lat - 1), lat_vt(n_lat - 1), ctx_tiles[0])
    for j, kc in enumerate(ctx_tiles):
        step(kc, [vtc_ref[0, 0, j]], ctx_tiles[j + 1] if j + 1 < len(ctx_tiles) else None)

    for p in range(n_pairs):
        halves = [acc_ref[h, 0:HEAD_DIM, :] / acc_ref[h, HEAD_DIM:HEAD_DIM + 1, :] for h in (2 * p, 2 * p + 1)]
        o_ref[:, p * LANES:(p + 1) * LANES] = jnp.concatenate(halves, axis=0).T.astype(o_ref.dtype)


def _glb_attn(q, k, vt, kc, vtc, seq, ctx_len, tq_want=4096, col_block=256, lookahead=64, tk_want=256):
    t = q.shape[0]
    nb = t // seq
    groups = GLB_KV_HEADS
    gw = q.shape[1] // groups
    n_pairs = gw // LANES
    tq = _row_tile(seq, tq_want)
    nq = seq // tq
    col_block = min(col_block, tq)
    n_units = 2 * n_pairs * (tq // col_block)
    chunks_per_tile = max(1, min(tk_want, seq) // vt.shape[4])
    assert vt.shape[2] % chunks_per_tile == 0
    max_keys = max(vt.shape[4] * chunks_per_tile, vtc.shape[4])
    return pl.pallas_call(
        functools.partial(_glb_kernel, n_pairs=n_pairs, col_block=col_block, lookahead=lookahead,
                          chunks_per_tile=chunks_per_tile),
        out_shape=jax.ShapeDtypeStruct(q.shape, BF16),
        grid=(nb, groups, nq),
        in_specs=[
            pl.BlockSpec((tq, gw), lambda b, g, i: (b * nq + i, g)),
            pl.BlockSpec((seq, LANES), lambda b, g, i: (b, g)),
            pl.BlockSpec((1, 1) + vt.shape[2:], lambda b, g, i: (b, g, 0, 0, 0)),
            pl.BlockSpec((ctx_len, LANES), lambda b, g, i: (b, g)),
            pl.BlockSpec((1, 1) + vtc.shape[2:], lambda b, g, i: (b, g, 0, 0, 0)),
        ],
        out_specs=pl.BlockSpec((tq, gw), lambda b, g, i: (b * nq + i, g)),
        scratch_shapes=[pltpu.VMEM((2 * n_pairs, HEAD_DIM, tq), BF16),
                        pltpu.VMEM((2 * n_pairs, 1, tq), F32),
                        pltpu.VMEM((2 * n_pairs, VT_ROWS, tq), F32),
                        pltpu.VMEM((n_units, max_keys, col_block), F32)],
        compiler_params=_cparams("parallel", "parallel", "arbitrary"),
        name="glb_attn",
    )(q, k, vt, kc, vtc)


def _softmax_pair(q_pair, keys, values, masks, sink_pair, hm, low):
    out = None
    inv = []
    for half in range(2):
        qh = q_pair * hm[half]
        ss = []
        for kk, mk in zip(keys, masks):
            s = _dot_nt(qh, kk)
            ss.append(s if mk is None else jnp.where(mk, s, MASK_VALUE))
        m = ss[0].max(axis=1, keepdims=True)
        for s in ss[1:]:
            m = jnp.maximum(m, s.max(axis=1, keepdims=True))
        if sink_pair is not None:
            m = jnp.maximum(m, sink_pair[half])
        den = None
        for s, vv in zip(ss, values):
            e = jnp.exp(s - m)
            sm = jnp.sum(e, axis=1, keepdims=True)
            den = sm if den is None else den + sm
            part = _dot(e.astype(BF16), vv * hm[half])
            out = part if out is None else out + part
        if sink_pair is not None:
            den = den + jnp.exp(sink_pair[half] - m)
        inv.append(1.0 / den)
    return out * jnp.where(low, inv[0], inv[1])


def _win_kernel(sink_ref, q_ref, k_ref, vt_ref, kc_ref, vtc_ref, o_ref, sloc_ref, sctx_ref, *, tq, span):
    g = pl.program_id(1)
    seq = k_ref.shape[0]
    n_tiles = seq // tq
    wchunk = vt_ref.shape[4]
    kc = kc_ref[...][:, 0:HEAD_DIM]
    vt_ctx = jnp.concatenate([vtc_ref[0, 0, c] for c in range(vtc_ref.shape[2])], axis=1)
    col_minus_row = (lax.broadcasted_iota(jnp.int32, (span, tq), 1)
                     - lax.broadcasted_iota(jnp.int32, (span, tq), 0))

    def band_start(i):
        return pl.multiple_of(jnp.clip(i * tq - WINDOW, 0, seq - span), wchunk)

    def store_scores(i):
        k_loc = k_ref[pl.ds(band_start(i), span), :][:, 0:HEAD_DIM]
        qt = (q_ref[pl.ds(pl.multiple_of(i * tq, tq), tq), :].astype(F32) * LOG2_E).T
        for half in range(2):
            qt_h = qt[half * HEAD_DIM:(half + 1) * HEAD_DIM].astype(BF16)
            yield half, _dot(k_loc, qt_h), _dot(kc, qt_h)

    for half, s_loc, s_ctx in store_scores(0):
        sloc_ref[half] = s_loc
        sctx_ref[half] = s_ctx

    def tile(i, carry):
        q0 = pl.multiple_of(i * tq, tq)
        start = band_start(i)
        c0 = start // wchunk
        bias = jnp.where(jnp.abs(col_minus_row + (q0 - start)) <= WINDOW, 0.0, MASK_VALUE)
        vt_loc = jnp.concatenate([vt_ref[0, 0, c0 + c] for c in range(span // wchunk)], axis=1)
        ahead = store_scores(jnp.minimum(i + 1, n_tiles - 1))
        halves = []
        for half in range(2):
            sink2 = sink_ref[2 * g + half] * LOG2_E
            s_loc = sloc_ref[half] + bias
            s_ctx = sctx_ref[half]
            _, n_loc, n_ctx = next(ahead)
            sloc_ref[half] = n_loc
            sctx_ref[half] = n_ctx
            m = jnp.maximum(jnp.max(s_loc, axis=0, keepdims=True), jnp.max(s_ctx, axis=0, keepdims=True))
            m = jnp.maximum(m, sink2)
            e_loc = jnp.exp2((s_loc - m).astype(BF16))
            e_ctx = jnp.exp2((s_ctx - m).astype(BF16))
            acc = _dot(vt_loc, e_loc) + _dot(vt_ctx, e_ctx)
            den = acc[HEAD_DIM:HEAD_DIM + 1, :] + jnp.exp2(sink2 - m)
            halves.append(acc[0:HEAD_DIM, :] / den)
        o_ref[pl.ds(q0, tq), :] = jnp.concatenate(halves, axis=0).T.astype(o_ref.dtype)
        return carry

    lax.fori_loop(0, n_tiles, tile, 0)


def _win_attn(q, k, vt, kc, vtc, sink, seq, ctx_len, tq_want=512):
    t = q.shape[0]
    nb = t // seq
    groups = WIN_KV_HEADS
    assert q.shape[1] == groups * LANES
    tq = _row_tile(seq, min(tq_want, max(seq - 2 * WINDOW, WINDOW)))
    span = tq + 2 * WINDOW
    assert seq >= span and span % vt.shape[4] == 0 and WINDOW % vt.shape[4] == 0
    return pl.pallas_call(
        functools.partial(_win_kernel, tq=tq, span=span),
        out_shape=jax.ShapeDtypeStruct(q.shape, BF16),
        grid=(nb, groups),
        in_specs=[
            pl.BlockSpec(memory_space=pltpu.SMEM),
            pl.BlockSpec((seq, LANES), lambda b, g: (b, g)),
            pl.BlockSpec((seq, LANES), lambda b, g: (b, g)),
            pl.BlockSpec((1, 1) + vt.shape[2:], lambda b, g: (b, g, 0, 0, 0)),
            pl.BlockSpec((ctx_len, LANES), lambda b, g: (b, g)),
            pl.BlockSpec((1, 1) + vtc.shape[2:], lambda b, g: (b, g, 0, 0, 0)),
        ],
        out_specs=pl.BlockSpec((seq, LANES), lambda b, g: (b, g)),
        scratch_shapes=[pltpu.VMEM((2, span, tq), F32), pltpu.VMEM((2, ctx_len, tq), F32)],
        compiler_params=_cparams("parallel", "parallel"),
        name="win_attn",
    )(sink, q, k, vt, kc, vtc)


def _ctx_attn_kernel(sink_ref, q_ref, k_ref, v_ref, o_ref, *, n_pairs, use_sink):
    g = pl.program_id(1)
    hm = _half_masks(BF16)
    low = _lane_iota((1, LANES)) < HEAD_DIM
    k = k_ref[...]
    v = v_ref[...]
    for p in range(n_pairs):
        heads = (g * n_pairs + p) * 2
        sink_pair = (sink_ref[heads], sink_ref[heads + 1]) if use_sink else None
        o = _softmax_pair(q_ref[:, p * LANES:(p + 1) * LANES], [k], [v], [None], sink_pair, hm, low)
        o_ref[:, p * LANES:(p + 1) * LANES] = o.astype(o_ref.dtype)


def _ctx_attn(q, k, v, sink, groups, ctx_len, use_sink):
    t = q.shape[0]
    nb = t // ctx_len
    gw = q.shape[1] // groups
    n_pairs = gw // LANES
    return pl.pallas_call(
        functools.partial(_ctx_attn_kernel, n_pairs=n_pairs, use_sink=use_sink),
        out_shape=jax.ShapeDtypeStruct(q.shape, BF16),
        grid=(nb, groups),
        in_specs=[
            pl.BlockSpec(memory_space=pltpu.SMEM),
            pl.BlockSpec((ctx_len, gw), lambda b, g: (b, g)),
            pl.BlockSpec((ctx_len, LANES), lambda b, g: (b, g)),
            pl.BlockSpec((ctx_len, LANES), lambda b, g: (b, g)),
        ],
        out_specs=pl.BlockSpec((ctx_len, gw), lambda b, g: (b, g)),
        compiler_params=_cparams("parallel", "parallel"),
        name="ctx_attn",
    )(sink, q, k, v)


def _outproj_ffn_kernel(x_ref, mod_ref, of_ref, ob_ref, r_ref, og_ref, ow_ref, gain_ref, bd_ref, w_ref,
                        g2_ref, win_ref, wout_ref, o_ref, *, f):
    o = of_ref[0] + ob_ref[0]
    r = r_ref[...].astype(F32)
    gate = r * jax.nn.sigmoid(r)
    ss = _dot((o * o).astype(BF16), bd_ref[...])
    on = o * lax.rsqrt(ss * (1.0 / GLA_DV) + EPS) * gain_ref[...]
    cat = jnp.concatenate([(on * gate).astype(BF16), og_ref[...], ow_ref[...]], axis=-1)
    x = x_ref[...] + mod_ref[0, 5:6, :] * _dot(cat, w_ref[...])
    o_ref[...] = _ffn_body(x, mod_ref, g2_ref, win_ref, wout_ref, 6, f)


def _outproj_ffn(x, mods, o_gla, r, o_glb, o_win, gla_gain, bd, w_out, gain2, w2_in, w2_out, rows_per_batch,
                 tm_want=512):
    t, d = x.shape
    f = w2_out.shape[0]
    tm = _row_tile(rows_per_batch, tm_want)
    return pl.pallas_call(
        functools.partial(_outproj_ffn_kernel, f=f),
        out_shape=jax.ShapeDtypeStruct((t, d), F32),
        grid=(t // tm,),
        in_specs=[
            pl.BlockSpec((tm, d), lambda i: (i, 0)),
            _mod_spec(mods, rows_per_batch // tm),
            pl.BlockSpec((1, tm, GLA_V), lambda i: (0, i, 0)),
            pl.BlockSpec((1, tm, GLA_V), lambda i: (1, i, 0)),
            pl.BlockSpec((tm, GLA_V), lambda i: (i, 0)),
            pl.BlockSpec((tm, o_glb.shape[1]), lambda i: (i, 0)),
            pl.BlockSpec((tm, o_win.shape[1]), lambda i: (i, 0)),
            _const_spec(gla_gain.shape),
            _const_spec(bd.shape),
            _const_spec(w_out.shape),
            _const_spec((1, d)),
            _const_spec((d, 2 * f)),
            _const_spec((f, d)),
        ],
        out_specs=pl.BlockSpec((tm, d), lambda i: (i, 0)),
        compiler_params=_cparams("parallel"),
        name="mix_outproj_ffn",
    )(x, mods, o_gla, o_gla, r, o_glb, o_win, gla_gain, bd, w_out, gain2.reshape(1, d), w2_in, w2_out)


def _rope_tables(seq):
    rows = seq // GRID_W
    row = jnp.repeat(jnp.arange(rows, dtype=F32), GRID_W)
    col = (jnp.arange(rows * GRID_W) % GRID_W).astype(F32)
    n_freq = HEAD_DIM // 4
    inv = jnp.power(ROPE_BASE, -jnp.arange(n_freq, dtype=F32) / n_freq)
    ang = jnp.concatenate([row[:, None] * inv, col[:, None] * inv], axis=-1)
    cos, sin = jnp.cos(ang), jnp.sin(ang)
    cos_t = jnp.concatenate([cos, cos] * (LANES // HEAD_DIM), axis=-1)
    sin_t = jnp.concatenate([-sin, sin] * (LANES // HEAD_DIM), axis=-1)
    return cos_t, sin_t


def _block_diag_ones(n, block):
    idx = np.arange(n) // block
    return jnp.asarray(idx[:, None] == idx[None, :], dtype=BF16)


def kernel(x, c, ctx, c_ctx, mod_w, mod_b, norm_ffn1, ffn1_w_in, ffn1_w_out, norm_mix, mix_w_in, mix_w_out,
           gla_wg_f, gla_bg_f, gla_wg_b, gla_bg_b, gla_out_norm, glb_q_norm, glb_k_norm,
           win_q_norm, win_k_norm, win_sink, norm_ffn2, ffn2_w_in, ffn2_w_out):
    bsz, seq, d = x.shape
    ctx_len = ctx.shape[1]
    depth = mod_w.shape[0]
    in_splits = (GLA_QK, GLA_QK, GLA_V, GLA_V, 2 * GLA_GATE_RANK,
                 GLB_HEADS * HEAD_DIM, GLB_KV_HEADS * HEAD_DIM, GLB_KV_HEADS * HEAD_DIM,
                 WIN_HEADS * HEAD_DIM, WIN_KV_HEADS * HEAD_DIM, WIN_KV_HEADS * HEAD_DIM)

    n_rows = -(-(bsz + 1) // 8) * 8
    c_rows = jnp.concatenate([c, c_ctx[None, :], jnp.zeros((n_rows - bsz - 1, d), F32)], axis=0)
    mods = _modvec(c_rows, mod_w, mod_b)

    cos_l, sin_l = _rope_tables(seq)
    cos_c = jnp.ones((ctx_len, LANES), F32)
    sin_c = jnp.zeros((ctx_len, LANES), F32)
    bd = _block_diag_ones(SLAB, HEAD_DIM)
    zero_state = jnp.zeros((bsz, 2, GLA_V, GLA_QK), F32)

    xl = x.reshape(bsz * seq, d)
    xc = ctx.reshape(bsz * ctx_len, d)

    for l in range(depth):
        need_ctx = l < depth - 1
        mods_l = mods[l, :bsz].reshape(bsz, N_MOD, d)
        mods_c = mods[l, bsz:bsz + 1].reshape(1, N_MOD, d)
        w1_in, w1_out = ffn1_w_in[l].astype(BF16), ffn1_w_out[l].astype(BF16)
        w2_in, w2_out = ffn2_w_in[l].astype(BF16), ffn2_w_out[l].astype(BF16)
        wm = mix_w_in[l]
        src = dict(zip(("aq", "ak", "av", "ar", "ad", "gq", "gk", "gv", "wq", "wk", "wv"),
                       jnp.split(wm, np.cumsum(in_splits)[:-1], axis=1)))
        w_mix = jnp.concatenate([src[n] for n in ("aq", "ak", "av", "ar", "gq", "gk", "wk", "gv", "wv", "wq", "ad")]
                                + [jnp.zeros((d, GATE_PAD - 2 * GLA_GATE_RANK), F32)], axis=1).astype(BF16)
        assert w_mix.shape[1] == _C_END
        wg = jnp.zeros((GATE_PAD, 2 * GLA_QK), F32)
        wg = wg.at[:GLA_GATE_RANK, :GLA_QK].set(gla_wg_f[l])
        wg = wg.at[GLA_GATE_RANK:2 * GLA_GATE_RANK, GLA_QK:].set(gla_wg_b[l]).astype(BF16)
        bg = jnp.concatenate([gla_bg_f[l], gla_bg_b[l]])[None, :]
        qk_gains = jnp.stack([jnp.tile(gn[l], LANES // HEAD_DIM)
                              for gn in (glb_q_norm, glb_k_norm, win_q_norm, win_k_norm)])
        gla_gain = jnp.tile(gla_out_norm[l], GLA_HEADS)[None, :]
        w_out = mix_w_out[l].astype(BF16)
        sink = win_sink[l]

        xl = _ffn(xl, mods_l, 0, norm_ffn1[l], w1_in, w1_out, seq)
        xc = _ffn(xc, mods_c, 0, norm_ffn1[l], w1_in, w1_out, ctx_len)

        pc = _inproj(xc, mods_c, norm_mix[l], w_mix, wg, bg, qk_gains, cos_c, sin_c, bd, ctx_len)
        pq = _inproj(xl, mods_l, norm_mix[l], w_mix, wg, bg, qk_gains, cos_l, sin_l, bd, seq)
        aq, ak, av, ar, ag, gq, gk, gv, wq, wk, wv, gvt, wvt = pq
        aqc, akc, avc, arc, agc, gqc, gkc, gvc, wqc, wkc, wvc, gvtc, wvtc = pc

        oc_gla, states = _gla(aqc, akc, avc, agc, zero_state, ctx_len)
        o_gla, _ = _gla(aq, ak, av, ag, states, seq)
        o_glb = _glb_attn(gq, gk, gvt, gkc, gvtc, seq, ctx_len)
        o_win = _win_attn(wq, wk, wvt, wkc, wvtc, sink, seq, ctx_len)
        xl = _outproj_ffn(xl, mods_l, o_gla, ar, o_glb, o_win, gla_gain, bd, w_out, norm_ffn2[l], w2_in, w2_out, seq)

        if need_ctx:
            oc_glb = _ctx_attn(gqc, gkc, gvc, sink, GLB_KV_HEADS, ctx_len, use_sink=False)
            oc_win = _ctx_attn(wqc, wkc, wvc, sink, WIN_KV_HEADS, ctx_len, use_sink=True)
            xc = _outproj_ffn(xc, mods_c, oc_gla, arc, oc_glb, oc_win, gla_gain, bd, w_out, norm_ffn2[l],
                              w2_in, w2_out, ctx_len)

    return xl.reshape(bsz, seq, d)
```

```python
import functools

import numpy as np
import jax
import jax.numpy as jnp
from jax import lax
from jax.experimental import pallas as pl
from jax.experimental.pallas import tpu as pltpu

GRID_W = 64
HEAD_DIM = 64
GLA_HEADS = 4
GLA_DK = 32
GLA_DV = 64
GLA_GATE_RANK = 16
GLA_GATE_TAU = 16.0
GLA_CHUNK = 64
GLB_HEADS = 8
GLB_KV_HEADS = 2
WIN_HEADS = 4
WIN_KV_HEADS = 2
WINDOW = 128
ROPE_BASE = 10000.0
N_MOD = 9
EPS = 1e-6

LANES = 128
VMEM_LIMIT_BYTES = 56 * 1024 * 1024

GLA_QK = GLA_HEADS * GLA_DK
GLA_V = GLA_HEADS * GLA_DV
GATE_PAD = LANES
MASK_VALUE = -1e30
LOG2_E = 1.4426950408889634
KEY_CHUNK = 256
GLA_GROUP = 4
WIN_CHUNK = 128
VT_ROWS = HEAD_DIM + 16

BF16 = jnp.bfloat16
F32 = jnp.float32


def _cparams(*sem):
    return pltpu.CompilerParams(dimension_semantics=sem, vmem_limit_bytes=VMEM_LIMIT_BYTES)


def _dot(a, b):
    return jnp.dot(a, b, preferred_element_type=F32)


def _dot_nt(a, b):
    return lax.dot_general(a, b, (((1,), (1,)), ((), ())), preferred_element_type=F32)


def _dot_tn(a, b):
    return lax.dot_general(a, b, (((0,), (0,)), ((), ())), preferred_element_type=F32)


def _lane_iota(shape):
    return lax.broadcasted_iota(jnp.int32, shape, len(shape) - 1)


def _modvec_kernel(c_ref, w_ref, b_ref, o_ref):
    c = c_ref[...]
    s = (c * jax.nn.sigmoid(c)).astype(BF16)
    o_ref[0] = _dot(s, w_ref[0].astype(BF16)) + b_ref[0]


def _modvec(c_rows, mod_w, mod_b):
    depth, d, n = mod_w.shape
    rows = c_rows.shape[0]
    tn = n // N_MOD
    return pl.pallas_call(
        _modvec_kernel,
        out_shape=jax.ShapeDtypeStruct((depth, rows, n), F32),
        grid=(depth, n // tn),
        in_specs=[
            pl.BlockSpec((rows, d), lambda l, j: (0, 0)),
            pl.BlockSpec((1, d, tn), lambda l, j: (l, 0, j)),
            pl.BlockSpec((1, 1, tn), lambda l, j: (l, 0, j)),
        ],
        out_specs=pl.BlockSpec((1, rows, tn), lambda l, j: (l, 0, j)),
        compiler_params=_cparams("arbitrary", "arbitrary"),
        name="modvec",
    )(c_rows, mod_w, mod_b.reshape(depth, 1, n))


def _norm_modulate(x, gain, mod_ref, k_shift):
    shift = mod_ref[0, k_shift:k_shift + 1, :]
    scale = mod_ref[0, k_shift + 1:k_shift + 2, :]
    y = x * lax.rsqrt(jnp.mean(x * x, axis=-1, keepdims=True) + EPS)
    return (y * gain) * (1.0 + scale) + shift


def _mod_spec(mods, tiles_per_batch):
    nb, nm, d = mods.shape
    if nb == 1:
        return pl.BlockSpec((1, nm, d), lambda i: (0, 0, 0))
    return pl.BlockSpec((1, nm, d), lambda i: (i // tiles_per_batch, 0, 0))


def _const_spec(shape):
    nd = len(shape)
    return pl.BlockSpec(shape, lambda *_: (0,) * nd, pipeline_mode=pl.Buffered(1))


def _row_tile(n_rows_per_batch, want):
    t = min(want, n_rows_per_batch)
    assert n_rows_per_batch % t == 0
    return t


def _ffn_chunks(f):
    step = 1536 if f > 1536 else f
    return tuple((lo, min(lo + step, f)) for lo in range(0, f, step))


def _ffn_kernel(x_ref, mod_ref, g_ref, win_ref, wout_ref, o_ref, *, k0, f):
    o_ref[...] = _ffn_body(x_ref[...], mod_ref, g_ref, win_ref, wout_ref, k0, f)


def _ffn_body(x, mod_ref, g_ref, win_ref, wout_ref, k0, f):
    hb = _norm_modulate(x, g_ref[...], mod_ref, k0).astype(BF16)
    gate = mod_ref[0, k0 + 2:k0 + 3, :]
    acc = None
    for lo, hi in _ffn_chunks(f):
        a = _dot(hb, win_ref[:, lo:hi])
        b = _dot(hb, win_ref[:, f + lo:f + hi])
        u = ((a * jax.nn.sigmoid(a)) * b).astype(BF16)
        part = _dot(u, wout_ref[lo:hi, :])
        acc = part if acc is None else acc + part
    return x + (0.5 * gate) * acc


def _ffn(x, mods, k0, gain, w_in, w_out, rows_per_batch, tm_want=512):
    t, d = x.shape
    f = w_out.shape[0]
    tm = _row_tile(rows_per_batch, tm_want)
    return pl.pallas_call(
        functools.partial(_ffn_kernel, k0=k0, f=f),
        out_shape=jax.ShapeDtypeStruct((t, d), F32),
        grid=(t // tm,),
        in_specs=[
            pl.BlockSpec((tm, d), lambda i: (i, 0)),
            _mod_spec(mods, rows_per_batch // tm),
            _const_spec((1, d)),
            _const_spec((d, 2 * f)),
            _const_spec((f, d)),
        ],
        out_specs=pl.BlockSpec((tm, d), lambda i: (i, 0)),
        compiler_params=_cparams("parallel"),
        name="ffn",
    )(x, mods, gain.reshape(1, d), w_in, w_out)


_C_AQ = 0
_C_AK = _C_AQ + GLA_QK
_C_AV = _C_AK + GLA_QK
_C_AR = _C_AV + GLA_V
_C_GQ = _C_AR + GLA_V
_C_GK = _C_GQ + GLB_HEADS * HEAD_DIM
_C_WK = _C_GK + GLB_KV_HEADS * HEAD_DIM
_C_GV = _C_WK + WIN_KV_HEADS * HEAD_DIM
_C_WV = _C_GV + GLB_KV_HEADS * HEAD_DIM
_C_WQ = _C_WV + WIN_KV_HEADS * HEAD_DIM
_C_AD = _C_WQ + WIN_HEADS * HEAD_DIM
_C_END = _C_AD + GATE_PAD
SLAB = 2 * LANES


def _norm_rope_slab(x, bd_ref, gains, cos, sin_signed, out_scales):
    ss = _dot((x * x).astype(BF16), bd_ref[...])
    first_half = (_lane_iota((x.shape[0], LANES)) % HEAD_DIM) < (HEAD_DIM // 2)
    outs = []
    for j in range(2):
        sl = slice(j * LANES, (j + 1) * LANES)
        xn = x[:, sl] * lax.rsqrt(ss[:, sl] * (1.0 / HEAD_DIM) + EPS) * gains[j]
        partner = jnp.where(first_half, pltpu.roll(xn, LANES - HEAD_DIM // 2, 1), pltpu.roll(xn, HEAD_DIM // 2, 1))
        out = xn * cos + partner * sin_signed
        outs.append(out * out_scales[j] if out_scales[j] != 1.0 else out)
    return outs


def _dup_heads(x):
    swapped = pltpu.roll(x, HEAD_DIM, 1)
    low = _lane_iota(x.shape) < HEAD_DIM
    return jnp.where(low, x, swapped), jnp.where(low, swapped, x)


def _store_vt(vt_ref, v):
    chunk = vt_ref.shape[4]
    for c in range(vt_ref.shape[2]):
        vt = v[c * chunk:(c + 1) * chunk, :].T
        for kv in range(vt_ref.shape[1]):
            vt_ref[0, kv, c, 0:HEAD_DIM, :] = vt[kv * HEAD_DIM:(kv + 1) * HEAD_DIM].astype(BF16)
            vt_ref[0, kv, c, HEAD_DIM:VT_ROWS, :] = jnp.ones((VT_ROWS - HEAD_DIM, chunk), BF16)


def _inproj_kernel(x_ref, mod_ref, g_ref, w_ref, wg_ref, bg_ref, qkg_ref, cos_ref, sin_ref, bd_ref,
                   aq_ref, ak_ref, av_ref, ar_ref, ag_ref, gq_ref, gk_ref, gv_ref, wq_ref, wk_ref, wv_ref,
                   gvt_ref, wvt_ref):
    hb = _norm_modulate(x_ref[...], g_ref[...], mod_ref, 3).astype(BF16)
    cos = cos_ref[...]
    sin = sin_ref[...]
    tiles = {}

    def proj(lo, width):
        j, off = divmod(lo, SLAB)
        assert off + width <= SLAB
        if j not in tiles:
            tiles[j] = _dot(hb, w_ref[:, j * SLAB:min((j + 1) * SLAB, _C_END)])
        return tiles[j][:, off:off + width]

    q_scale = HEAD_DIM ** -0.5
    g_glb_q, g_glb_k, g_win_q, g_win_k = (qkg_ref[r:r + 1, :] for r in range(4))

    order = (_C_GQ, _C_GQ + SLAB, _C_WQ, _C_GK, _C_GV, _C_AD, _C_AQ, _C_AV, _C_AR)

    def issue_ahead(n):
        for lo in order[:n + 3]:
            proj(lo, LANES)

    for j in range(GLB_HEADS * HEAD_DIM // SLAB):
        issue_ahead(j)
        halves = _norm_rope_slab(proj(_C_GQ + j * SLAB, SLAB), bd_ref, (g_glb_q, g_glb_q), cos, sin,
                                 (q_scale, q_scale))
        for i, q in enumerate(halves):
            gq_ref[:, j * SLAB + i * LANES:j * SLAB + (i + 1) * LANES] = q.astype(BF16)
    issue_ahead(2)
    halves = _norm_rope_slab(proj(_C_WQ, SLAB), bd_ref, (g_win_q, g_win_q), cos, sin, (q_scale, q_scale))
    for i, q in enumerate(halves):
        wq_ref[:, i * LANES:(i + 1) * LANES] = q.astype(BF16)
    issue_ahead(3)
    k_glb, k_win = _norm_rope_slab(proj(_C_GK, SLAB), bd_ref, (g_glb_k, g_win_k), cos, sin, (1.0, 1.0))
    k0, k1 = _dup_heads(k_glb)
    gk_ref[:, 0:LANES] = k0.astype(BF16)
    gk_ref[:, LANES:2 * LANES] = k1.astype(BF16)
    k0, k1 = _dup_heads(k_win)
    wk_ref[:, 0:LANES] = k0.astype(BF16)
    wk_ref[:, LANES:2 * LANES] = k1.astype(BF16)

    issue_ahead(4)
    v = proj(_C_GV, LANES)
    v0, v1 = _dup_heads(v)
    gv_ref[:, 0:LANES] = v0.astype(BF16)
    gv_ref[:, LANES:2 * LANES] = v1.astype(BF16)
    _store_vt(gvt_ref, v)

    v = proj(_C_WV, LANES)
    v0, v1 = _dup_heads(v)
    wv_ref[:, 0:LANES] = v0.astype(BF16)
    wv_ref[:, LANES:2 * LANES] = v1.astype(BF16)
    _store_vt(wvt_ref, v)

    issue_ahead(len(order))
    z = _dot(proj(_C_AD, GATE_PAD).astype(BF16), wg_ref[...]) + bg_ref[...]
    log_sig = jnp.minimum(z, 0.0) - jnp.log1p(jnp.exp(-jnp.abs(z)))
    ag_ref[...] = log_sig * (1.0 / GLA_GATE_TAU)
    aq_ref[...] = (proj(_C_AQ, GLA_QK) * (GLA_DK ** -0.5)).astype(BF16)
    ak_ref[...] = proj(_C_AK, GLA_QK).astype(BF16)
    av_ref[...] = proj(_C_AV, GLA_V).astype(BF16)
    ar_ref[...] = proj(_C_AR, GLA_V).astype(BF16)


def _inproj(x, mods, gain, w, wg, bg, qk_gains, cos, sin, bd, rows_per_batch, tm_want=512):
    t, d = x.shape
    tm = _row_tile(rows_per_batch, tm_want)
    tpb = rows_per_batch // tm
    widths = (GLA_QK, GLA_QK, GLA_V, GLA_V, 2 * GLA_QK, GLB_HEADS * HEAD_DIM, 2 * LANES, 2 * LANES,
              WIN_HEADS * HEAD_DIM, 2 * LANES, 2 * LANES)
    dtypes = (BF16, BF16, BF16, BF16, F32, BF16, BF16, BF16, BF16, BF16, BF16)
    nb = t // rows_per_batch
    vt_shapes, vt_specs = [], []
    for kv_heads, want in ((GLB_KV_HEADS, KEY_CHUNK), (WIN_KV_HEADS, WIN_CHUNK)):
        chunk = min(want, tm)
        vt_shapes.append(jax.ShapeDtypeStruct((nb, kv_heads, rows_per_batch // chunk, VT_ROWS, chunk), BF16))
        vt_specs.append(pl.BlockSpec((1, kv_heads, tm // chunk, VT_ROWS, chunk),
                                     lambda i: (i // tpb, 0, i % tpb, 0, 0)))
    return pl.pallas_call(
        _inproj_kernel,
        out_shape=tuple(jax.ShapeDtypeStruct((t, wd), dt) for wd, dt in zip(widths, dtypes)) + tuple(vt_shapes),
        grid=(t // tm,),
        in_specs=[
            pl.BlockSpec((tm, d), lambda i: (i, 0)),
            _mod_spec(mods, tpb),
            _const_spec((1, d)),
            _const_spec(w.shape),
            _const_spec(wg.shape),
            _const_spec(bg.shape),
            _const_spec(qk_gains.shape),
            pl.BlockSpec((tm, LANES), lambda i: (i % tpb, 0)),
            pl.BlockSpec((tm, LANES), lambda i: (i % tpb, 0)),
            _const_spec(bd.shape),
        ],
        out_specs=tuple(pl.BlockSpec((tm, wd), lambda i: (i, 0)) for wd in widths) + tuple(vt_specs),
        compiler_params=_cparams("parallel"),
        name="mix_inproj",
    )(x, mods, gain.reshape(1, d), w, wg, bg, qk_gains, cos, sin, bd)


def _gla_kernel(q_ref, k_ref, v_ref, g_ref, s0_ref, o_ref, sfin_ref, st_ref, *, n_chunks):
    c_len = GLA_CHUNK
    d = pl.program_id(1)
    i = pl.program_id(2)
    sign = 1 - 2 * d

    @pl.when(i == 0)
    def _():
        st_ref[...] = s0_ref[0, 0]

    group = min(GLA_GROUP, n_chunks)
    g_len = group * c_len
    row = lax.broadcasted_iota(jnp.int32, (g_len, g_len), 0)
    col = lax.broadcasted_iota(jnp.int32, (g_len, g_len), 1)
    keep = ((row - col) * sign >= 0) & (row // c_len == col // c_len)
    cum_op = jnp.where(keep, 1.0, 0.0).astype(BF16)
    row4 = lax.broadcasted_iota(jnp.int32, (c_len, GLA_HEADS * c_len), 0)
    col4 = lax.broadcasted_iota(jnp.int32, (c_len, GLA_HEADS * c_len), 1) % c_len
    keep4 = (row4 - col4) * sign >= 0
    qk_lane_head = _lane_iota((1, GLA_QK)) // GLA_DK
    v_lane_head = _lane_iota((1, GLA_V)) // GLA_DV
    qk_head_mask = [(qk_lane_head == h).astype(F32) for h in range(GLA_HEADS)]
    v_head_mask = [(v_lane_head == h).astype(BF16) for h in range(GLA_HEADS)]
    st_row_head = lax.broadcasted_iota(jnp.int32, (GLA_V, GLA_QK), 0) // GLA_DV
    st_col_head = lax.broadcasted_iota(jnp.int32, (GLA_V, GLA_QK), 1) // GLA_DK
    st_mask = st_row_head == st_col_head

    n_groups = n_chunks // group

    def group_starts(j):
        starts = []
        for p in range(group):
            c = j * group + p
            c = c + d * (n_chunks - 1 - 2 * c)
            starts.append(pl.multiple_of(c * c_len, c_len))
        return starts

    def decay_stage(j):
        starts = group_starts(j)
        q = jnp.concatenate([q_ref[pl.ds(r, c_len), :] for r in starts], axis=0).astype(F32)
        k = jnp.concatenate([k_ref[pl.ds(r, c_len), :] for r in starts], axis=0).astype(F32)
        g = jnp.concatenate([g_ref[pl.ds(r, c_len), :] for r in starts], axis=0)
        g_hi = g.astype(BF16)
        r1 = g - g_hi.astype(F32)
        g_mid = r1.astype(BF16)
        g_lo = (r1 - g_mid.astype(F32)).astype(BF16)
        b = _dot(cum_op, g_hi) + _dot(cum_op, g_mid) + _dot(cum_op, g_lo)
        q_in = (q * jnp.exp(b)).astype(BF16)
        k_out = k * jnp.exp(-b)
        decays = jnp.concatenate([jnp.exp(jnp.sum(g[p * c_len:(p + 1) * c_len], axis=0, keepdims=True))
                                  for p in range(group)], axis=0)
        return q_in, k_out, decays

    def chunk_group(j, staged):
        staged_next = decay_stage(jnp.minimum(j + 1, n_groups - 1))
        q_in, k_out, decays = staged
        starts = group_starts(j)
        vs = [v_ref[pl.ds(r, c_len), :] for r in starts]
        st = st_ref[...]
        for p in range(group):
            rows = slice(p * c_len, (p + 1) * c_len)
            decay = decays[p:p + 1, :]
            k_out_p = k_out[rows]
            k_dec = (k_out_p * decay).astype(BF16)
            k_stack = jnp.concatenate([(k_out_p * qk_head_mask[h]).astype(BF16) for h in range(GLA_HEADS)], axis=0)
            a = _dot_nt(q_in[rows], k_stack)
            a = jnp.where(keep4, a, 0.0).astype(BF16)
            v_bd = jnp.concatenate([vs[p] * v_head_mask[h] for h in range(GLA_HEADS)], axis=0)
            o = _dot(a, v_bd) + _dot_nt(q_in[rows], st.astype(BF16))
            o_ref[0, pl.ds(starts[p], c_len), :] = o
            ds_t = _dot_tn(vs[p], k_dec)
            st = st * decay + jnp.where(st_mask, ds_t, 0.0)
        st_ref[...] = st
        return staged_next

    lax.fori_loop(0, n_groups, chunk_group, decay_stage(0))

    @pl.when(i == pl.num_programs(2) - 1)
    def _():
        sfin_ref[0, 0] = st_ref[...]


def _gla(q, k, v, g, s0, rows_per_batch, tt_want=2048):
    t = q.shape[0]
    nb = t // rows_per_batch
    tt = _row_tile(rows_per_batch, tt_want)
    nt = rows_per_batch // tt

    def rows(b, d, i):
        return b * nt + i + d * (nt - 1 - 2 * i)

    return pl.pallas_call(
        functools.partial(_gla_kernel, n_chunks=tt // GLA_CHUNK),
        out_shape=(jax.ShapeDtypeStruct((2, t, GLA_V), F32),
                   jax.ShapeDtypeStruct((nb, 2, GLA_V, GLA_QK), F32)),
        grid=(nb, 2, nt),
        in_specs=[
            pl.BlockSpec((tt, GLA_QK), lambda b, d, i: (rows(b, d, i), 0)),
            pl.BlockSpec((tt, GLA_QK), lambda b, d, i: (rows(b, d, i), 0)),
            pl.BlockSpec((tt, GLA_V), lambda b, d, i: (rows(b, d, i), 0)),
            pl.BlockSpec((tt, GLA_QK), lambda b, d, i: (rows(b, d, i), d)),
            pl.BlockSpec((1, 1, GLA_V, GLA_QK), lambda b, d, i: (b, d, 0, 0)),
        ],
        out_specs=(pl.BlockSpec((1, tt, GLA_V), lambda b, d, i: (d, rows(b, d, i), 0)),
                   pl.BlockSpec((1, 1, GLA_V, GLA_QK), lambda b, d, i: (b, d, 0, 0))),
        scratch_shapes=[pltpu.VMEM((GLA_V, GLA_QK), F32)],
        compiler_params=_cparams("parallel", "parallel", "arbitrary"),
        name="gla_scan",
    )(q, k, v, g, s0)


def _half_masks(dtype):
    low = _lane_iota((1, LANES)) < HEAD_DIM
    return [low.astype(dtype), (~low).astype(dtype)]


def _rep(x, width):
    n = width // LANES
    return x if n == 1 else pltpu.repeat(x, n, axis=1)


def _glb_kernel(q_ref, k_ref, vt_ref, kc_ref, vtc_ref, o_ref, qt_ref, m_ref, acc_ref, s_ref, *,
                n_pairs, col_block, chunks_per_tile):
    tq = q_ref.shape[0]
    n_lat = vt_ref.shape[2] // chunks_per_tile
    tk = vt_ref.shape[4] * chunks_per_tile
    n_heads = 2 * n_pairs
    for p in range(n_pairs):
        qt = (q_ref[:, p * LANES:(p + 1) * LANES].astype(F32) * LOG2_E).T
        qt_ref[2 * p] = qt[0:HEAD_DIM].astype(BF16)
        qt_ref[2 * p + 1] = qt[HEAD_DIM:2 * HEAD_DIM].astype(BF16)

    m_ref[...] = jnp.full(m_ref.shape, MASK_VALUE, F32)
    acc_ref[...] = jnp.zeros(acc_ref.shape, F32)

    units = [(h, cb) for h in range(n_heads) for cb in range(tq // col_block)]

    def scores(k, u):
        h, cb = units[u]
        return _dot(k[:, 0:HEAD_DIM], qt_ref[h, :, cb * col_block:(cb + 1) * col_block])

    def step(k, vt, k_next):
        w = k.shape[0]
        for u, (h, cb) in enumerate(units):
            s = s_ref[u, 0:w, :]
            if k_next is not None:
                s_ref[u, 0:k_next.shape[0], :] = scores(k_next, u)
            cols = slice(cb * col_block, (cb + 1) * col_block)
            m_prev = m_ref[h, :, cols]
            m_new = jnp.maximum(m_prev, jnp.max(s, axis=0, keepdims=True))
            alpha = jnp.exp2(m_prev - m_new)
            e = jnp.exp2((s - m_new).astype(BF16))
            m_ref[h, :, cols] = m_new
            ch = w // len(vt)
            pv = _dot(vt[0], e[0:ch, :])
            for c in range(1, len(vt)):
                pv = pv + _dot(vt[c], e[c * ch:(c + 1) * ch, :])
            acc_ref[h, :, cols] = acc_ref[h, :, cols] * alpha + pv

    def lat_tile(j):
        return k_ref[pl.ds(pl.multiple_of(j * tk, tk), tk), :]

    def lat_vt(j):
        return [vt_ref[0, 0, j * chunks_per_tile + c] for c in range(chunks_per_tile)]

    wc = vtc_ref.shape[4]
    ctx_tiles = [kc_ref[j * wc:(j + 1) * wc, :] for j in range(vtc_ref.shape[2])]

    for u in range(len(units)):
        s_ref[u, 0:tk, :] = scores(lat_tile(0), u)

    def body(j, carry):
        step(lat_tile(j), lat_vt(j), lat_tile(j + 1))
        return carry

    lax.fori_loop(0, n_lat - 1, body, 0)
    step(lat_tile(n_lat - 1), lat_vt(n_lat - 1), ctx_tiles[0])
    for j, kc in enumerate(ctx_tiles):
        step(kc, [vtc_ref[0, 0, j]], ctx_tiles[j + 1] if j + 1 < len(ctx_tiles) else None)

    for p in range(n_pairs):
        halves = [acc_ref[h, 0:HEAD_DIM, :] / acc_ref[h, HEAD_DIM:HEAD_DIM + 1, :] for h in (2 * p, 2 * p + 1)]
        o_ref[:, p * LANES:(p + 1) * LANES] = jnp.concatenate(halves, axis=0).T.astype(o_ref.dtype)


def _glb_attn(q, k, vt, kc, vtc, seq, ctx_len, tq_want=4096, col_block=256, tk_want=256):
    t = q.shape[0]
    nb = t // seq
    groups = GLB_KV_HEADS
    gw = q.shape[1] // groups
    n_pairs = gw // LANES
    tq = _row_tile(seq, tq_want)
    nq = seq // tq
    col_block = min(col_block, tq)
    n_units = 2 * n_pairs * (tq // col_block)
    chunks_per_tile = max(1, min(tk_want, seq) // vt.shape[4])
    assert vt.shape[2] % chunks_per_tile == 0
    max_keys = max(vt.shape[4] * chunks_per_tile, vtc.shape[4])
    return pl.pallas_call(
        functools.partial(_glb_kernel, n_pairs=n_pairs, col_block=col_block, chunks_per_tile=chunks_per_tile),
        out_shape=jax.ShapeDtypeStruct(q.shape, BF16),
        grid=(nb, groups, nq),
        in_specs=[
            pl.BlockSpec((tq, gw), lambda b, g, i: (b * nq + i, g)),
            pl.BlockSpec((seq, LANES), lambda b, g, i: (b, g)),
            pl.BlockSpec((1, 1) + vt.shape[2:], lambda b, g, i: (b, g, 0, 0, 0)),
            pl.BlockSpec((ctx_len, LANES), lambda b, g, i: (b, g)),
            pl.BlockSpec((1, 1) + vtc.shape[2:], lambda b, g, i: (b, g, 0, 0, 0)),
        ],
        out_specs=pl.BlockSpec((tq, gw), lambda b, g, i: (b * nq + i, g)),
        scratch_shapes=[pltpu.VMEM((2 * n_pairs, HEAD_DIM, tq), BF16),
                        pltpu.VMEM((2 * n_pairs, 1, tq), F32),
                        pltpu.VMEM((2 * n_pairs, VT_ROWS, tq), F32),
                        pltpu.VMEM((n_units, max_keys, col_block), F32)],
        compiler_params=_cparams("parallel", "parallel", "arbitrary"),
        name="glb_attn",
    )(q, k, vt, kc, vtc)


def _softmax_pair(q_pair, keys, values, masks, sink_pair, hm, low):
    out = None
    inv = []
    for half in range(2):
        qh = q_pair * hm[half]
        ss = []
        for kk, mk in zip(keys, masks):
            s = _dot_nt(qh, kk)
            ss.append(s if mk is None else jnp.where(mk, s, MASK_VALUE))
        m = ss[0].max(axis=1, keepdims=True)
        for s in ss[1:]:
            m = jnp.maximum(m, s.max(axis=1, keepdims=True))
        if sink_pair is not None:
            m = jnp.maximum(m, sink_pair[half])
        den = None
        for s, vv in zip(ss, values):
            e = jnp.exp(s - m)
            sm = jnp.sum(e, axis=1, keepdims=True)
            den = sm if den is None else den + sm
            part = _dot(e.astype(BF16), vv * hm[half])
            out = part if out is None else out + part
        if sink_pair is not None:
            den = den + jnp.exp(sink_pair[half] - m)
        inv.append(1.0 / den)
    return out * jnp.where(low, inv[0], inv[1])


def _win_kernel(sink_ref, q_ref, k_ref, vt_ref, kc_ref, vtc_ref, o_ref, sloc_ref, sctx_ref, *,
                tq, span, blocks_per_step):
    g = pl.program_id(1)
    seq = k_ref.shape[0]
    n_blocks = seq // tq
    n_steps = n_blocks // blocks_per_step
    wchunk = vt_ref.shape[4]
    kc = kc_ref[...][:, 0:HEAD_DIM]
    vt_ctx = jnp.concatenate([vtc_ref[0, 0, c] for c in range(vtc_ref.shape[2])], axis=1)
    col_minus_row = (lax.broadcasted_iota(jnp.int32, (span, tq), 1)
                     - lax.broadcasted_iota(jnp.int32, (span, tq), 0))

    def band_start(qb):
        return pl.multiple_of(jnp.clip(qb * tq - WINDOW, 0, seq - span), wchunk)

    def block_scores(qb):
        k_loc = k_ref[pl.ds(band_start(qb), span), :][:, 0:HEAD_DIM]
        qt = (q_ref[pl.ds(pl.multiple_of(qb * tq, tq), tq), :].astype(F32) * LOG2_E).T
        for half in range(2):
            qt_h = qt[half * HEAD_DIM:(half + 1) * HEAD_DIM].astype(BF16)
            yield _dot(k_loc, qt_h), _dot(kc, qt_h)

    for j in range(blocks_per_step):
        for half, (s_loc, s_ctx) in enumerate(block_scores(j)):
            sloc_ref[2 * j + half] = s_loc
            sctx_ref[2 * j + half] = s_ctx

    def step(i, carry):
        for j in range(blocks_per_step):
            qb = i * blocks_per_step + j
            q0 = pl.multiple_of(qb * tq, tq)
            start = band_start(qb)
            c0 = start // wchunk
            bias = jnp.where(jnp.abs(col_minus_row + (q0 - start)) <= WINDOW, 0.0, MASK_VALUE)
            vt_loc = jnp.concatenate([vt_ref[0, 0, c0 + c] for c in range(span // wchunk)], axis=1)
            ahead = block_scores(jnp.minimum(qb + blocks_per_step, n_blocks - 1))
            halves = []
            for half in range(2):
                u = 2 * j + half
                sink2 = sink_ref[2 * g + half] * LOG2_E
                s_loc = sloc_ref[u] + bias
                s_ctx = sctx_ref[u]
                sloc_ref[u], sctx_ref[u] = next(ahead)
                m = jnp.maximum(jnp.max(s_loc, axis=0, keepdims=True), jnp.max(s_ctx, axis=0, keepdims=True))
                m = jnp.maximum(m, sink2)
                e_loc = jnp.exp2((s_loc - m).astype(BF16))
                e_ctx = jnp.exp2((s_ctx - m).astype(BF16))
                acc = _dot(vt_loc, e_loc) + _dot(vt_ctx, e_ctx)
                den = acc[HEAD_DIM:HEAD_DIM + 1, :] + jnp.exp2(sink2 - m)
                halves.append(acc[0:HEAD_DIM, :] / den)
            o_ref[pl.ds(q0, tq), :] = jnp.concatenate(halves, axis=0).T.astype(o_ref.dtype)
        return carry

    lax.fori_loop(0, n_steps, step, 0)


def _win_attn(q, k, vt, kc, vtc, sink, seq, ctx_len, tq_want=256, blocks_per_step=4):
    t = q.shape[0]
    nb = t // seq
    groups = WIN_KV_HEADS
    assert q.shape[1] == groups * LANES
    tq = _row_tile(seq, min(tq_want, max(seq - 2 * WINDOW, WINDOW)))
    span = tq + 2 * WINDOW
    assert seq >= span and span % vt.shape[4] == 0 and WINDOW % vt.shape[4] == 0
    blocks_per_step = min(blocks_per_step, seq // tq)
    assert (seq // tq) % blocks_per_step == 0
    n_units = 2 * blocks_per_step
    return pl.pallas_call(
        functools.partial(_win_kernel, tq=tq, span=span, blocks_per_step=blocks_per_step),
        out_shape=jax.ShapeDtypeStruct(q.shape, BF16),
        grid=(nb, groups),
        in_specs=[
            pl.BlockSpec(memory_space=pltpu.SMEM),
            pl.BlockSpec((seq, LANES), lambda b, g: (b, g)),
            pl.BlockSpec((seq, LANES), lambda b, g: (b, g)),
            pl.BlockSpec((1, 1) + vt.shape[2:], lambda b, g: (b, g, 0, 0, 0)),
            pl.BlockSpec((ctx_len, LANES), lambda b, g: (b, g)),
            pl.BlockSpec((1, 1) + vtc.shape[2:], lambda b, g: (b, g, 0, 0, 0)),
        ],
        out_specs=pl.BlockSpec((seq, LANES), lambda b, g: (b, g)),
        scratch_shapes=[pltpu.VMEM((n_units, span, tq), F32), pltpu.VMEM((n_units, ctx_len, tq), F32)],
        compiler_params=_cparams("parallel", "parallel"),
        name="win_attn",
    )(sink, q, k, vt, kc, vtc)


def _ctx_attn_kernel(sink_ref, q_ref, k_ref, v_ref, o_ref, *, n_pairs, use_sink):
    g = pl.program_id(1)
    hm = _half_masks(BF16)
    low = _lane_iota((1, LANES)) < HEAD_DIM
    k = k_ref[...]
    v = v_ref[...]
    for p in range(n_pairs):
        heads = (g * n_pairs + p) * 2
        sink_pair = (sink_ref[heads], sink_ref[heads + 1]) if use_sink else None
        o = _softmax_pair(q_ref[:, p * LANES:(p + 1) * LANES], [k], [v], [None], sink_pair, hm, low)
        o_ref[:, p * LANES:(p + 1) * LANES] = o.astype(o_ref.dtype)


def _ctx_attn(q, k, v, sink, groups, ctx_len, use_sink):
    t = q.shape[0]
    nb = t // ctx_len
    gw = q.shape[1] // groups
    n_pairs = gw // LANES
    return pl.pallas_call(
        functools.partial(_ctx_attn_kernel, n_pairs=n_pairs, use_sink=use_sink),
        out_shape=jax.ShapeDtypeStruct(q.shape, BF16),
        grid=(nb, groups),
        in_specs=[
            pl.BlockSpec(memory_space=pltpu.SMEM),
            pl.BlockSpec((ctx_len, gw), lambda b, g: (b, g)),
            pl.BlockSpec((ctx_len, LANES), lambda b, g: (b, g)),
            pl.BlockSpec((ctx_len, LANES), lambda b, g: (b, g)),
        ],
        out_specs=pl.BlockSpec((ctx_len, gw), lambda b, g: (b, g)),
        compiler_params=_cparams("parallel", "parallel"),
        name="ctx_attn",
    )(sink, q, k, v)


def _outproj_ffn_kernel(x_ref, mod_ref, of_ref, ob_ref, r_ref, og_ref, ow_ref, gain_ref, bd_ref, w_ref,
                        g2_ref, win_ref, wout_ref, o_ref, *, f):
    o = of_ref[0] + ob_ref[0]
    r = r_ref[...].astype(F32)
    gate = r * jax.nn.sigmoid(r)
    ss = _dot((o * o).astype(BF16), bd_ref[...])
    on = o * lax.rsqrt(ss * (1.0 / GLA_DV) + EPS) * gain_ref[...]
    cat = jnp.concatenate([(on * gate).astype(BF16), og_ref[...], ow_ref[...]], axis=-1)
    x = x_ref[...] + mod_ref[0, 5:6, :] * _dot(cat, w_ref[...])
    o_ref[...] = _ffn_body(x, mod_ref, g2_ref, win_ref, wout_ref, 6, f)


def _outproj_ffn(x, mods, o_gla, r, o_glb, o_win, gla_gain, bd, w_out, gain2, w2_in, w2_out, rows_per_batch,
                 tm_want=512):
    t, d = x.shape
    f = w2_out.shape[0]
    tm = _row_tile(rows_per_batch, tm_want)
    return pl.pallas_call(
        functools.partial(_outproj_ffn_kernel, f=f),
        out_shape=jax.ShapeDtypeStruct((t, d), F32),
        grid=(t // tm,),
        in_specs=[
            pl.BlockSpec((tm, d), lambda i: (i, 0)),
            _mod_spec(mods, rows_per_batch // tm),
            pl.BlockSpec((1, tm, GLA_V), lambda i: (0, i, 0)),
            pl.BlockSpec((1, tm, GLA_V), lambda i: (1, i, 0)),
            pl.BlockSpec((tm, GLA_V), lambda i: (i, 0)),
            pl.BlockSpec((tm, o_glb.shape[1]), lambda i: (i, 0)),
            pl.BlockSpec((tm, o_win.shape[1]), lambda i: (i, 0)),
            _const_spec(gla_gain.shape),
            _const_spec(bd.shape),
            _const_spec(w_out.shape),
            _const_spec((1, d)),
            _const_spec((d, 2 * f)),
            _const_spec((f, d)),
        ],
        out_specs=pl.BlockSpec((tm, d), lambda i: (i, 0)),
        compiler_params=_cparams("parallel"),
        name="mix_outproj_ffn",
    )(x, mods, o_gla, o_gla, r, o_glb, o_win, gla_gain, bd, w_out, gain2.reshape(1, d), w2_in, w2_out)


def _rope_tables(seq):
    rows = seq // GRID_W
    row = jnp.repeat(jnp.arange(rows, dtype=F32), GRID_W)
    col = (jnp.arange(rows * GRID_W) % GRID_W).astype(F32)
    n_freq = HEAD_DIM // 4
    inv = jnp.power(ROPE_BASE, -jnp.arange(n_freq, dtype=F32) / n_freq)
    ang = jnp.concatenate([row[:, None] * inv, col[:, None] * inv], axis=-1)
    cos, sin = jnp.cos(ang), jnp.sin(ang)
    cos_t = jnp.concatenate([cos, cos] * (LANES // HEAD_DIM), axis=-1)
    sin_t = jnp.concatenate([-sin, sin] * (LANES // HEAD_DIM), axis=-1)
    return cos_t, sin_t


def _block_diag_ones(n, block):
    idx = np.arange(n) // block
    return jnp.asarray(idx[:, None] == idx[None, :], dtype=BF16)


def kernel(x, c, ctx, c_ctx, mod_w, mod_b, norm_ffn1, ffn1_w_in, ffn1_w_out, norm_mix, mix_w_in, mix_w_out,
           gla_wg_f, gla_bg_f, gla_wg_b, gla_bg_b, gla_out_norm, glb_q_norm, glb_k_norm,
           win_q_norm, win_k_norm, win_sink, norm_ffn2, ffn2_w_in, ffn2_w_out):
    bsz, seq, d = x.shape
    ctx_len = ctx.shape[1]
    depth = mod_w.shape[0]
    in_splits = (GLA_QK, GLA_QK, GLA_V, GLA_V, 2 * GLA_GATE_RANK,
                 GLB_HEADS * HEAD_DIM, GLB_KV_HEADS * HEAD_DIM, GLB_KV_HEADS * HEAD_DIM,
                 WIN_HEADS * HEAD_DIM, WIN_KV_HEADS * HEAD_DIM, WIN_KV_HEADS * HEAD_DIM)

    n_rows = -(-(bsz + 1) // 8) * 8
    c_rows = jnp.concatenate([c, c_ctx[None, :], jnp.zeros((n_rows - bsz - 1, d), F32)], axis=0)
    mods = _modvec(c_rows, mod_w, mod_b)

    cos_l, sin_l = _rope_tables(seq)
    cos_c = jnp.ones((ctx_len, LANES), F32)
    sin_c = jnp.zeros((ctx_len, LANES), F32)
    bd = _block_diag_ones(SLAB, HEAD_DIM)
    zero_state = jnp.zeros((bsz, 2, GLA_V, GLA_QK), F32)

    xl = x.reshape(bsz * seq, d)
    xc = ctx.reshape(bsz * ctx_len, d)

    for l in range(depth):
        need_ctx = l < depth - 1
        mods_l = mods[l, :bsz].reshape(bsz, N_MOD, d)
        mods_c = mods[l, bsz:bsz + 1].reshape(1, N_MOD, d)
        w1_in, w1_out = ffn1_w_in[l].astype(BF16), ffn1_w_out[l].astype(BF16)
        w2_in, w2_out = ffn2_w_in[l].astype(BF16), ffn2_w_out[l].astype(BF16)
        wm = mix_w_in[l]
        src = dict(zip(("aq", "ak", "av", "ar", "ad", "gq", "gk", "gv", "wq", "wk", "wv"),
                       jnp.split(wm, np.cumsum(in_splits)[:-1], axis=1)))
        w_mix = jnp.concatenate([src[n] for n in ("aq", "ak", "av", "ar", "gq", "gk", "wk", "gv", "wv", "wq", "ad")]
                                + [jnp.zeros((d, GATE_PAD - 2 * GLA_GATE_RANK), F32)], axis=1).astype(BF16)
        assert w_mix.shape[1] == _C_END
        wg = jnp.zeros((GATE_PAD, 2 * GLA_QK), F32)
        wg = wg.at[:GLA_GATE_RANK, :GLA_QK].set(gla_wg_f[l])
        wg = wg.at[GLA_GATE_RANK:2 * GLA_GATE_RANK, GLA_QK:].set(gla_wg_b[l]).astype(BF16)
        bg = jnp.concatenate([gla_bg_f[l], gla_bg_b[l]])[None, :]
        qk_gains = jnp.stack([jnp.tile(gn[l], LANES // HEAD_DIM)
                              for gn in (glb_q_norm, glb_k_norm, win_q_norm, win_k_norm)])
        gla_gain = jnp.tile(gla_out_norm[l], GLA_HEADS)[None, :]
        w_out = mix_w_out[l].astype(BF16)
        sink = win_sink[l]

        xl = _ffn(xl, mods_l, 0, norm_ffn1[l], w1_in, w1_out, seq)
        xc = _ffn(xc, mods_c, 0, norm_ffn1[l], w1_in, w1_out, ctx_len)

        pc = _inproj(xc, mods_c, norm_mix[l], w_mix, wg, bg, qk_gains, cos_c, sin_c, bd, ctx_len)
        pq = _inproj(xl, mods_l, norm_mix[l], w_mix, wg, bg, qk_gains, cos_l, sin_l, bd, seq)
        aq, ak, av, ar, ag, gq, gk, gv, wq, wk, wv, gvt, wvt = pq
        aqc, akc, avc, arc, agc, gqc, gkc, gvc, wqc, wkc, wvc, gvtc, wvtc = pc

        oc_gla, states = _gla(aqc, akc, avc, agc, zero_state, ctx_len)
        o_gla, _ = _gla(aq, ak, av, ag, states, seq)
        o_glb = _glb_attn(gq, gk, gvt, gkc, gvtc, seq, ctx_len)
        o_win = _win_attn(wq, wk, wvt, wkc, wvtc, sink, seq, ctx_len)
        xl = _outproj_ffn(xl, mods_l, o_gla, ar, o_glb, o_win, gla_gain, bd, w_out, norm_ffn2[l], w2_in, w2_out, seq)

        if need_ctx:
            oc_glb = _ctx_attn(gqc, gkc, gvc, sink, GLB_KV_HEADS, ctx_len, use_sink=False)
            oc_win = _ctx_attn(wqc, wkc, wvc, sink, WIN_KV_HEADS, ctx_len, use_sink=True)
            xc = _outproj_ffn(xc, mods_c, oc_gla, arc, oc_glb, oc_win, gla_gain, bd, w_out, norm_ffn2[l],
                              w2_in, w2_out, ctx_len)

    return xl.reshape(bsz, seq, d)
```

```python
import functools

import numpy as np
import jax
import jax.numpy as jnp
from jax import lax
from jax.experimental import pallas as pl
from jax.experimental.pallas import tpu as pltpu

GRID_W = 64
HEAD_DIM = 64
GLA_HEADS = 4
GLA_DK = 32
GLA_DV = 64
GLA_GATE_RANK = 16
GLA_GATE_TAU = 16.0
GLA_CHUNK = 64
GLB_HEADS = 8
GLB_KV_HEADS = 2
WIN_HEADS = 4
WIN_KV_HEADS = 2
WINDOW = 128
ROPE_BASE = 10000.0
N_MOD = 9
EPS = 1e-6

LANES = 128
VMEM_LIMIT_BYTES = 56 * 1024 * 1024

GLA_QK = GLA_HEADS * GLA_DK
GLA_V = GLA_HEADS * GLA_DV
GATE_PAD = LANES
MASK_VALUE = -1e30
LOG2_E = 1.4426950408889634
KEY_CHUNK = 256
GLA_GROUP = 4
WIN_CHUNK = 128
VT_ROWS = HEAD_DIM + 16

BF16 = jnp.bfloat16
F32 = jnp.float32


def _cparams(*sem):
    return pltpu.CompilerParams(dimension_semantics=sem, vmem_limit_bytes=VMEM_LIMIT_BYTES)


def _dot(a, b):
    return jnp.dot(a, b, preferred_element_type=F32)


def _dot_nt(a, b):
    return lax.dot_general(a, b, (((1,), (1,)), ((), ())), preferred_element_type=F32)


def _dot_tn(a, b):
    return lax.dot_general(a, b, (((0,), (0,)), ((), ())), preferred_element_type=F32)


def _lane_iota(shape):
    return lax.broadcasted_iota(jnp.int32, shape, len(shape) - 1)


def _modvec_kernel(c_ref, w_ref, b_ref, o_ref):
    c = c_ref[...]
    s = (c * jax.nn.sigmoid(c)).astype(BF16)
    o_ref[0] = _dot(s, w_ref[0].astype(BF16)) + b_ref[0]


def _modvec(c_rows, mod_w, mod_b):
    depth, d, n = mod_w.shape
    rows = c_rows.shape[0]
    tn = n // N_MOD
    return pl.pallas_call(
        _modvec_kernel,
        out_shape=jax.ShapeDtypeStruct((depth, rows, n), F32),
        grid=(depth, n // tn),
        in_specs=[
            pl.BlockSpec((rows, d), lambda l, j: (0, 0)),
            pl.BlockSpec((1, d, tn), lambda l, j: (l, 0, j)),
            pl.BlockSpec((1, 1, tn), lambda l, j: (l, 0, j)),
        ],
        out_specs=pl.BlockSpec((1, rows, tn), lambda l, j: (l, 0, j)),
        compiler_params=_cparams("arbitrary", "arbitrary"),
        name="modvec",
    )(c_rows, mod_w, mod_b.reshape(depth, 1, n))


def _norm_modulate(x, gain, mod_ref, k_shift):
    shift = mod_ref[0, k_shift:k_shift + 1, :]
    scale = mod_ref[0, k_shift + 1:k_shift + 2, :]
    y = x * lax.rsqrt(jnp.mean(x * x, axis=-1, keepdims=True) + EPS)
    return (y * gain) * (1.0 + scale) + shift


def _mod_spec(mods, tiles_per_batch):
    nb, nm, d = mods.shape
    if nb == 1:
        return pl.BlockSpec((1, nm, d), lambda i: (0, 0, 0))
    return pl.BlockSpec((1, nm, d), lambda i: (i // tiles_per_batch, 0, 0))


def _const_spec(shape):
    nd = len(shape)
    return pl.BlockSpec(shape, lambda *_: (0,) * nd, pipeline_mode=pl.Buffered(1))


def _row_tile(n_rows_per_batch, want):
    t = min(want, n_rows_per_batch)
    assert n_rows_per_batch % t == 0
    return t


def _ffn_chunks(f):
    step = 1536 if f > 1536 else f
    return tuple((lo, min(lo + step, f)) for lo in range(0, f, step))


def _ffn_kernel(x_ref, mod_ref, g_ref, win_ref, wout_ref, o_ref, *, k0, f):
    o_ref[...] = _ffn_body(x_ref[...], mod_ref, g_ref, win_ref, wout_ref, k0, f)


def _ffn_body(x, mod_ref, g_ref, win_ref, wout_ref, k0, f):
    hb = _norm_modulate(x, g_ref[...], mod_ref, k0).astype(BF16)
    gate = mod_ref[0, k0 + 2:k0 + 3, :]
    acc = None
    for lo, hi in _ffn_chunks(f):
        a = _dot(hb, win_ref[:, lo:hi])
        b = _dot(hb, win_ref[:, f + lo:f + hi])
        u = ((a * jax.nn.sigmoid(a)) * b).astype(BF16)
        part = _dot(u, wout_ref[lo:hi, :])
        acc = part if acc is None else acc + part
    return x + (0.5 * gate) * acc


def _ffn(x, mods, k0, gain, w_in, w_out, rows_per_batch, tm_want=512):
    t, d = x.shape
    f = w_out.shape[0]
    tm = _row_tile(rows_per_batch, tm_want)
    return pl.pallas_call(
        functools.partial(_ffn_kernel, k0=k0, f=f),
        out_shape=jax.ShapeDtypeStruct((t, d), F32),
        grid=(t // tm,),
        in_specs=[
            pl.BlockSpec((tm, d), lambda i: (i, 0)),
            _mod_spec(mods, rows_per_batch // tm),
            _const_spec((1, d)),
            _const_spec((d, 2 * f)),
            _const_spec((f, d)),
        ],
        out_specs=pl.BlockSpec((tm, d), lambda i: (i, 0)),
        compiler_params=_cparams("parallel"),
        name="ffn",
    )(x, mods, gain.reshape(1, d), w_in, w_out)


_C_AQ = 0
_C_AK = _C_AQ + GLA_QK
_C_AV = _C_AK + GLA_QK
_C_AR = _C_AV + GLA_V
_C_GQ = _C_AR + GLA_V
_C_GK = _C_GQ + GLB_HEADS * HEAD_DIM
_C_WK = _C_GK + GLB_KV_HEADS * HEAD_DIM
_C_GV = _C_WK + WIN_KV_HEADS * HEAD_DIM
_C_WV = _C_GV + GLB_KV_HEADS * HEAD_DIM
_C_WQ = _C_WV + WIN_KV_HEADS * HEAD_DIM
_C_AD = _C_WQ + WIN_HEADS * HEAD_DIM
_C_END = _C_AD + GATE_PAD
SLAB = 2 * LANES


def _norm_rope_slab(x, bd_ref, gains, cos, sin_signed, out_scales):
    ss = _dot((x * x).astype(BF16), bd_ref[...])
    first_half = (_lane_iota((x.shape[0], LANES)) % HEAD_DIM) < (HEAD_DIM // 2)
    outs = []
    for j in range(2):
        sl = slice(j * LANES, (j + 1) * LANES)
        xn = x[:, sl] * lax.rsqrt(ss[:, sl] * (1.0 / HEAD_DIM) + EPS) * gains[j]
        partner = jnp.where(first_half, pltpu.roll(xn, LANES - HEAD_DIM // 2, 1), pltpu.roll(xn, HEAD_DIM // 2, 1))
        out = xn * cos + partner * sin_signed
        outs.append(out * out_scales[j] if out_scales[j] != 1.0 else out)
    return outs


def _dup_heads(x):
    swapped = pltpu.roll(x, HEAD_DIM, 1)
    low = _lane_iota(x.shape) < HEAD_DIM
    return jnp.where(low, x, swapped), jnp.where(low, swapped, x)


def _store_vt(vt_ref, v):
    chunk = vt_ref.shape[4]
    for c in range(vt_ref.shape[2]):
        vt = v[c * chunk:(c + 1) * chunk, :].T
        for kv in range(vt_ref.shape[1]):
            vt_ref[0, kv, c, 0:HEAD_DIM, :] = vt[kv * HEAD_DIM:(kv + 1) * HEAD_DIM].astype(BF16)
            vt_ref[0, kv, c, HEAD_DIM:VT_ROWS, :] = jnp.ones((VT_ROWS - HEAD_DIM, chunk), BF16)


def _inproj_kernel(x_ref, mod_ref, g_ref, w_ref, wg_ref, bg_ref, qkg_ref, cos_ref, sin_ref, bd_ref,
                   aq_ref, ak_ref, av_ref, ar_ref, ag_ref, gq_ref, gk_ref, gv_ref, wq_ref, wk_ref, wv_ref,
                   gvt_ref, wvt_ref):
    hb = _norm_modulate(x_ref[...], g_ref[...], mod_ref, 3).astype(BF16)
    cos = cos_ref[...]
    sin = sin_ref[...]
    tiles = {}

    def proj(lo, width):
        j, off = divmod(lo, SLAB)
        assert off + width <= SLAB
        if j not in tiles:
            tiles[j] = _dot(hb, w_ref[:, j * SLAB:min((j + 1) * SLAB, _C_END)])
        return tiles[j][:, off:off + width]

    q_scale = HEAD_DIM ** -0.5
    g_glb_q, g_glb_k, g_win_q, g_win_k = (qkg_ref[r:r + 1, :] for r in range(4))

    order = (_C_GQ, _C_GQ + SLAB, _C_WQ, _C_GK, _C_GV, _C_AD, _C_AQ, _C_AV, _C_AR)

    def issue_ahead(n):
        for lo in order[:n + 3]:
            proj(lo, LANES)

    for j in range(GLB_HEADS * HEAD_DIM // SLAB):
        issue_ahead(j)
        halves = _norm_rope_slab(proj(_C_GQ + j * SLAB, SLAB), bd_ref, (g_glb_q, g_glb_q), cos, sin,
                                 (q_scale, q_scale))
        for i, q in enumerate(halves):
            gq_ref[:, j * SLAB + i * LANES:j * SLAB + (i + 1) * LANES] = q.astype(BF16)
    issue_ahead(2)
    halves = _norm_rope_slab(proj(_C_WQ, SLAB), bd_ref, (g_win_q, g_win_q), cos, sin, (q_scale, q_scale))
    for i, q in enumerate(halves):
        wq_ref[:, i * LANES:(i + 1) * LANES] = q.astype(BF16)
    issue_ahead(3)
    k_glb, k_win = _norm_rope_slab(proj(_C_GK, SLAB), bd_ref, (g_glb_k, g_win_k), cos, sin, (1.0, 1.0))
    k0, k1 = _dup_heads(k_glb)
    gk_ref[:, 0:LANES] = k0.astype(BF16)
    gk_ref[:, LANES:2 * LANES] = k1.astype(BF16)
    k0, k1 = _dup_heads(k_win)
    wk_ref[:, 0:LANES] = k0.astype(BF16)
    wk_ref[:, LANES:2 * LANES] = k1.astype(BF16)

    issue_ahead(4)
    v = proj(_C_GV, LANES)
    v0, v1 = _dup_heads(v)
    gv_ref[:, 0:LANES] = v0.astype(BF16)
    gv_ref[:, LANES:2 * LANES] = v1.astype(BF16)
    _store_vt(gvt_ref, v)

    v = proj(_C_WV, LANES)
    v0, v1 = _dup_heads(v)
    wv_ref[:, 0:LANES] = v0.astype(BF16)
    wv_ref[:, LANES:2 * LANES] = v1.astype(BF16)
    _store_vt(wvt_ref, v)

    issue_ahead(len(order))
    z = _dot(proj(_C_AD, GATE_PAD).astype(BF16), wg_ref[...]) + bg_ref[...]
    log_sig = jnp.minimum(z, 0.0) - jnp.log1p(jnp.exp(-jnp.abs(z)))
    ag_ref[...] = log_sig * (1.0 / GLA_GATE_TAU)
    aq_ref[...] = (proj(_C_AQ, GLA_QK) * (GLA_DK ** -0.5)).astype(BF16)
    ak_ref[...] = proj(_C_AK, GLA_QK).astype(BF16)
    av_ref[...] = proj(_C_AV, GLA_V).astype(BF16)
    ar_ref[...] = proj(_C_AR, GLA_V).astype(BF16)


def _inproj(x, mods, gain, w, wg, bg, qk_gains, cos, sin, bd, rows_per_batch, tm_want=512):
    t, d = x.shape
    tm = _row_tile(rows_per_batch, tm_want)
    tpb = rows_per_batch // tm
    widths = (GLA_QK, GLA_QK, GLA_V, GLA_V, 2 * GLA_QK, GLB_HEADS * HEAD_DIM, 2 * LANES, 2 * LANES,
              WIN_HEADS * HEAD_DIM, 2 * LANES, 2 * LANES)
    dtypes = (BF16, BF16, BF16, BF16, F32, BF16, BF16, BF16, BF16, BF16, BF16)
    nb = t // rows_per_batch
    vt_shapes, vt_specs = [], []
    for kv_heads, want in ((GLB_KV_HEADS, KEY_CHUNK), (WIN_KV_HEADS, WIN_CHUNK)):
        chunk = min(want, tm)
        vt_shapes.append(jax.ShapeDtypeStruct((nb, kv_heads, rows_per_batch // chunk, VT_ROWS, chunk), BF16))
        vt_specs.append(pl.BlockSpec((1, kv_heads, tm // chunk, VT_ROWS, chunk),
                                     lambda i: (i // tpb, 0, i % tpb, 0, 0)))
    return pl.pallas_call(
        _inproj_kernel,
        out_shape=tuple(jax.ShapeDtypeStruct((t, wd), dt) for wd, dt in zip(widths, dtypes)) + tuple(vt_shapes),
        grid=(t // tm,),
        in_specs=[
            pl.BlockSpec((tm, d), lambda i: (i, 0)),
            _mod_spec(mods, tpb),
            _const_spec((1, d)),
            _const_spec(w.shape),
            _const_spec(wg.shape),
            _const_spec(bg.shape),
            _const_spec(qk_gains.shape),
            pl.BlockSpec((tm, LANES), lambda i: (i % tpb, 0)),
            pl.BlockSpec((tm, LANES), lambda i: (i % tpb, 0)),
            _const_spec(bd.shape),
        ],
        out_specs=tuple(pl.BlockSpec((tm, wd), lambda i: (i, 0)) for wd in widths) + tuple(vt_specs),
        compiler_params=_cparams("parallel"),
        name="mix_inproj",
    )(x, mods, gain.reshape(1, d), w, wg, bg, qk_gains, cos, sin, bd)


def _gla_kernel(qf_ref, kf_ref, vf_ref, gf_ref, qb_ref, kb_ref, vb_ref, gb_ref, s0_ref,
                of_ref, ob_ref, sfin_ref, st_ref, *, n_chunks):
    c_len = GLA_CHUNK
    i = pl.program_id(1)
    dir_refs = ((qf_ref, kf_ref, vf_ref, gf_ref, of_ref), (qb_ref, kb_ref, vb_ref, gb_ref, ob_ref))
    signs = (1, -1)

    @pl.when(i == 0)
    def _():
        st_ref[...] = s0_ref[0]

    group = min(GLA_GROUP, n_chunks)
    g_len = group * c_len
    row = lax.broadcasted_iota(jnp.int32, (g_len, g_len), 0)
    col = lax.broadcasted_iota(jnp.int32, (g_len, g_len), 1)
    same_chunk = row // c_len == col // c_len
    cum_ops = [jnp.where(((row - col) * sg >= 0) & same_chunk, 1.0, 0.0).astype(BF16) for sg in signs]
    row4 = lax.broadcasted_iota(jnp.int32, (c_len, GLA_HEADS * c_len), 0)
    col4 = lax.broadcasted_iota(jnp.int32, (c_len, GLA_HEADS * c_len), 1) % c_len
    keep4s = [(row4 - col4) * sg >= 0 for sg in signs]
    qk_lane_head = _lane_iota((1, GLA_QK)) // GLA_DK
    v_lane_head = _lane_iota((1, GLA_V)) // GLA_DV
    qk_head_mask = [(qk_lane_head == h).astype(F32) for h in range(GLA_HEADS)]
    v_head_mask = [(v_lane_head == h).astype(BF16) for h in range(GLA_HEADS)]
    st_row_head = lax.broadcasted_iota(jnp.int32, (GLA_V, GLA_QK), 0) // GLA_DV
    st_col_head = lax.broadcasted_iota(jnp.int32, (GLA_V, GLA_QK), 1) // GLA_DK
    st_mask = st_row_head == st_col_head

    n_groups = n_chunks // group

    def group_starts(d, j):
        starts = []
        for p in range(group):
            c = j * group + p
            if d == 1:
                c = n_chunks - 1 - c
            starts.append(pl.multiple_of(c * c_len, c_len))
        return starts

    def decay_stage(d, j):
        q_ref, k_ref, _, g_ref, _ = dir_refs[d]
        starts = group_starts(d, j)
        q = jnp.concatenate([q_ref[pl.ds(r, c_len), :] for r in starts], axis=0).astype(F32)
        k = jnp.concatenate([k_ref[pl.ds(r, c_len), :] for r in starts], axis=0).astype(F32)
        g = jnp.concatenate([g_ref[pl.ds(r, c_len), :] for r in starts], axis=0)
        g_hi = g.astype(BF16)
        r1 = g - g_hi.astype(F32)
        g_mid = r1.astype(BF16)
        g_lo = (r1 - g_mid.astype(F32)).astype(BF16)
        b = _dot(cum_ops[d], g_hi) + _dot(cum_ops[d], g_mid) + _dot(cum_ops[d], g_lo)
        q_in = (q * jnp.exp(b)).astype(BF16)
        k_out = k * jnp.exp(-b)
        decays = jnp.concatenate([jnp.exp(jnp.sum(g[p * c_len:(p + 1) * c_len], axis=0, keepdims=True))
                                  for p in range(group)], axis=0)
        return q_in, k_out, decays

    def chunk_group(j, staged):
        j_next = jnp.minimum(j + 1, n_groups - 1)
        staged_next = tuple(decay_stage(d, j_next) for d in range(2))
        starts = [group_starts(d, j) for d in range(2)]
        vs = [[dir_refs[d][2][pl.ds(r, c_len), :] for r in starts[d]] for d in range(2)]
        sts = [st_ref[0], st_ref[1]]
        for p in range(group):
            rows = slice(p * c_len, (p + 1) * c_len)
            for d in range(2):
                q_in, k_out, decays = staged[d]
                v_p = vs[d][p]
                decay = decays[p:p + 1, :]
                k_out_p = k_out[rows]
                k_dec = (k_out_p * decay).astype(BF16)
                k_stack = jnp.concatenate([(k_out_p * qk_head_mask[h]).astype(BF16) for h in range(GLA_HEADS)],
                                          axis=0)
                a = _dot_nt(q_in[rows], k_stack)
                a = jnp.where(keep4s[d], a, 0.0).astype(BF16)
                v_bd = jnp.concatenate([v_p * v_head_mask[h] for h in range(GLA_HEADS)], axis=0)
                o = _dot(a, v_bd) + _dot_nt(q_in[rows], sts[d].astype(BF16))
                dir_refs[d][4][pl.ds(starts[d][p], c_len), :] = o
                ds_t = _dot_tn(v_p, k_dec)
                sts[d] = sts[d] * decay + jnp.where(st_mask, ds_t, 0.0)
        st_ref[0] = sts[0]
        st_ref[1] = sts[1]
        return staged_next

    lax.fori_loop(0, n_groups, chunk_group, tuple(decay_stage(d, 0) for d in range(2)))

    @pl.when(i == pl.num_programs(1) - 1)
    def _():
        sfin_ref[0] = st_ref[...]


def _gla(q, k, v, g, s0, rows_per_batch, tt_want=2048):
    t = q.shape[0]
    nb = t // rows_per_batch
    tt = _row_tile(rows_per_batch, tt_want)
    nt = rows_per_batch // tt

    def fwd(lane_block):
        return lambda b, i: (b * nt + i, lane_block)

    def bwd(lane_block):
        return lambda b, i: (b * nt + nt - 1 - i, lane_block)

    state_spec = pl.BlockSpec((1, 2, GLA_V, GLA_QK), lambda b, i: (b, 0, 0, 0))
    return pl.pallas_call(
        functools.partial(_gla_kernel, n_chunks=tt // GLA_CHUNK),
        out_shape=(jax.ShapeDtypeStruct((t, GLA_V), F32), jax.ShapeDtypeStruct((t, GLA_V), F32),
                   jax.ShapeDtypeStruct((nb, 2, GLA_V, GLA_QK), F32)),
        grid=(nb, nt),
        in_specs=[
            pl.BlockSpec((tt, GLA_QK), fwd(0)), pl.BlockSpec((tt, GLA_QK), fwd(0)),
            pl.BlockSpec((tt, GLA_V), fwd(0)), pl.BlockSpec((tt, GLA_QK), fwd(0)),
            pl.BlockSpec((tt, GLA_QK), bwd(0)), pl.BlockSpec((tt, GLA_QK), bwd(0)),
            pl.BlockSpec((tt, GLA_V), bwd(0)), pl.BlockSpec((tt, GLA_QK), bwd(1)),
            state_spec,
        ],
        out_specs=(pl.BlockSpec((tt, GLA_V), fwd(0)), pl.BlockSpec((tt, GLA_V), bwd(0)), state_spec),
        scratch_shapes=[pltpu.VMEM((2, GLA_V, GLA_QK), F32)],
        compiler_params=_cparams("parallel", "arbitrary"),
        name="gla_scan",
    )(q, k, v, g, q, k, v, g, s0)


def _half_masks(dtype):
    low = _lane_iota((1, LANES)) < HEAD_DIM
    return [low.astype(dtype), (~low).astype(dtype)]


def _rep(x, width):
    n = width // LANES
    return x if n == 1 else pltpu.repeat(x, n, axis=1)


def _glb_kernel(q_ref, k_ref, vt_ref, kc_ref, vtc_ref, o_ref, qt_ref, m_ref, acc_ref, s_ref, *,
                n_pairs, col_block, chunks_per_tile):
    tq = q_ref.shape[0]
    n_lat = vt_ref.shape[2] // chunks_per_tile
    tk = vt_ref.shape[4] * chunks_per_tile
    n_heads = 2 * n_pairs
    for p in range(n_pairs):
        qt = (q_ref[:, p * LANES:(p + 1) * LANES].astype(F32) * LOG2_E).T
        qt_ref[2 * p] = qt[0:HEAD_DIM].astype(BF16)
        qt_ref[2 * p + 1] = qt[HEAD_DIM:2 * HEAD_DIM].astype(BF16)

    m_ref[...] = jnp.full(m_ref.shape, MASK_VALUE, F32)
    acc_ref[...] = jnp.zeros(acc_ref.shape, F32)

    units = [(h, cb) for h in range(n_heads) for cb in range(tq // col_block)]

    def scores(k, u):
        h, cb = units[u]
        return _dot(k[:, 0:HEAD_DIM], qt_ref[h, :, cb * col_block:(cb + 1) * col_block])

    def step(k, vt, k_next):
        w = k.shape[0]
        for u, (h, cb) in enumerate(units):
            s = s_ref[u, 0:w, :]
            if k_next is not None:
                s_ref[u, 0:k_next.shape[0], :] = scores(k_next, u)
            cols = slice(cb * col_block, (cb + 1) * col_block)
            m_prev = m_ref[h, :, cols]
            m_new = jnp.maximum(m_prev, jnp.max(s, axis=0, keepdims=True))
            alpha = jnp.exp2(m_prev - m_new)
            e = jnp.exp2((s - m_new).astype(BF16))
            m_ref[h, :, cols] = m_new
            ch = w // len(vt)
            pv = _dot(vt[0], e[0:ch, :])
            for c in range(1, len(vt)):
                pv = pv + _dot(vt[c], e[c * ch:(c + 1) * ch, :])
            acc_ref[h, :, cols] = acc_ref[h, :, cols] * alpha + pv

    def lat_tile(j):
        return k_ref[pl.ds(pl.multiple_of(j * tk, tk), tk), :]

    def lat_vt(j):
        return [vt_ref[0, 0, j * chunks_per_tile + c] for c in range(chunks_per_tile)]

    wc = vtc_ref.shape[4]
    ctx_tiles = [kc_ref[j * wc:(j + 1) * wc, :] for j in range(vtc_ref.shape[2])]

    for u in range(len(units)):
        s_ref[u, 0:tk, :] = scores(lat_tile(0), u)

    def body(j, carry):
        step(lat_tile(j), lat_vt(j), lat_tile(j + 1))
        return carry

    lax.fori_loop(0, n_lat - 1, body, 0)
    step(lat_tile(n_lat - 1), lat_vt(n_lat - 1), ctx_tiles[0])
    for j, kc in enumerate(ctx_tiles):
        step(kc, [vtc_ref[0, 0, j]], ctx_tiles[j + 1] if j + 1 < len(ctx_tiles) else None)

    for p in range(n_pairs):
        halves = [acc_ref[h, 0:HEAD_DIM, :] / acc_ref[h, HEAD_DIM:HEAD_DIM + 1, :] for h in (2 * p, 2 * p + 1)]
        o_ref[:, p * LANES:(p + 1) * LANES] = jnp.concatenate(halves, axis=0).T.astype(o_ref.dtype)


def _glb_attn(q, k, vt, kc, vtc, seq, ctx_len, tq_want=4096, col_block=256, tk_want=256):
    t = q.shape[0]
    nb = t // seq
    groups = GLB_KV_HEADS
    gw = q.shape[1] // groups
    n_pairs = gw // LANES
    tq = _row_tile(seq, tq_want)
    nq = seq // tq
    col_block = min(col_block, tq)
    n_units = 2 * n_pairs * (tq // col_block)
    chunks_per_tile = max(1, min(tk_want, seq) // vt.shape[4])
    assert vt.shape[2] % chunks_per_tile == 0
    max_keys = max(vt.shape[4] * chunks_per_tile, vtc.shape[4])
    return pl.pallas_call(
        functools.partial(_glb_kernel, n_pairs=n_pairs, col_block=col_block, chunks_per_tile=chunks_per_tile),
        out_shape=jax.ShapeDtypeStruct(q.shape, BF16),
        grid=(nb, groups, nq),
        in_specs=[
            pl.BlockSpec((tq, gw), lambda b, g, i: (b * nq + i, g)),
            pl.BlockSpec((seq, LANES), lambda b, g, i: (b, g)),
            pl.BlockSpec((1, 1) + vt.shape[2:], lambda b, g, i: (b, g, 0, 0, 0)),
            pl.BlockSpec((ctx_len, LANES), lambda b, g, i: (b, g)),
            pl.BlockSpec((1, 1) + vtc.shape[2:], lambda b, g, i: (b, g, 0, 0, 0)),
        ],
        out_specs=pl.BlockSpec((tq, gw), lambda b, g, i: (b * nq + i, g)),
        scratch_shapes=[pltpu.VMEM((2 * n_pairs, HEAD_DIM, tq), BF16),
                        pltpu.VMEM((2 * n_pairs, 1, tq), F32),
                        pltpu.VMEM((2 * n_pairs, VT_ROWS, tq), F32),
                        pltpu.VMEM((n_units, max_keys, col_block), F32)],
        compiler_params=_cparams("parallel", "parallel", "arbitrary"),
        name="glb_attn",
    )(q, k, vt, kc, vtc)


def _softmax_pair(q_pair, keys, values, masks, sink_pair, hm, low):
    out = None
    inv = []
    for half in range(2):
        qh = q_pair * hm[half]
        ss = []
        for kk, mk in zip(keys, masks):
            s = _dot_nt(qh, kk)
            ss.append(s if mk is None else jnp.where(mk, s, MASK_VALUE))
        m = ss[0].max(axis=1, keepdims=True)
        for s in ss[1:]:
            m = jnp.maximum(m, s.max(axis=1, keepdims=True))
        if sink_pair is not None:
            m = jnp.maximum(m, sink_pair[half])
        den = None
        for s, vv in zip(ss, values):
            e = jnp.exp(s - m)
            sm = jnp.sum(e, axis=1, keepdims=True)
            den = sm if den is None else den + sm
            part = _dot(e.astype(BF16), vv * hm[half])
            out = part if out is None else out + part
        if sink_pair is not None:
            den = den + jnp.exp(sink_pair[half] - m)
        inv.append(1.0 / den)
    return out * jnp.where(low, inv[0], inv[1])


def _win_kernel(sink_ref, q_ref, k_ref, vt_ref, kc_ref, vtc_ref, o_ref, sloc_ref, sctx_ref, *,
                tq, span, blocks_per_step):
    g = pl.program_id(1)
    seq = k_ref.shape[0]
    n_blocks = seq // tq
    n_steps = n_blocks // blocks_per_step
    wchunk = vt_ref.shape[4]
    kc = kc_ref[...][:, 0:HEAD_DIM]
    vt_ctx = jnp.concatenate([vtc_ref[0, 0, c] for c in range(vtc_ref.shape[2])], axis=1)
    col_minus_row = (lax.broadcasted_iota(jnp.int32, (span, tq), 1)
                     - lax.broadcasted_iota(jnp.int32, (span, tq), 0))

    def band_start(qb):
        return pl.multiple_of(jnp.clip(qb * tq - WINDOW, 0, seq - span), wchunk)

    def block_scores(qb):
        k_loc = k_ref[pl.ds(band_start(qb), span), :][:, 0:HEAD_DIM]
        qt = (q_ref[pl.ds(pl.multiple_of(qb * tq, tq), tq), :].astype(F32) * LOG2_E).T
        for half in range(2):
            qt_h = qt[half * HEAD_DIM:(half + 1) * HEAD_DIM].astype(BF16)
            yield _dot(k_loc, qt_h), _dot(kc, qt_h)

    for j in range(blocks_per_step):
        for half, (s_loc, s_ctx) in enumerate(block_scores(j)):
            sloc_ref[2 * j + half] = s_loc
            sctx_ref[2 * j + half] = s_ctx

    def step(i, carry):
        for j in range(blocks_per_step):
            qb = i * blocks_per_step + j
            q0 = pl.multiple_of(qb * tq, tq)
            start = band_start(qb)
            c0 = start // wchunk
            bias = jnp.where(jnp.abs(col_minus_row + (q0 - start)) <= WINDOW, 0.0, MASK_VALUE)
            vt_loc = jnp.concatenate([vt_ref[0, 0, c0 + c] for c in range(span // wchunk)], axis=1)
            ahead = block_scores(jnp.minimum(qb + blocks_per_step, n_blocks - 1))
            halves = []
            for half in range(2):
                u = 2 * j + half
                sink2 = sink_ref[2 * g + half] * LOG2_E
                s_loc = sloc_ref[u] + bias
                s_ctx = sctx_ref[u]
                sloc_ref[u], sctx_ref[u] = next(ahead)
                m = jnp.maximum(jnp.max(s_loc, axis=0, keepdims=True), jnp.max(s_ctx, axis=0, keepdims=True))
                m = jnp.maximum(m, sink2)
                e_loc = jnp.exp2((s_loc - m).astype(BF16))
                e_ctx = jnp.exp2((s_ctx - m).astype(BF16))
                acc = _dot(vt_loc, e_loc) + _dot(vt_ctx, e_ctx)
                den = acc[HEAD_DIM:HEAD_DIM + 1, :] + jnp.exp2(sink2 - m)
                halves.append(acc[0:HEAD_DIM, :] / den)
            o_ref[pl.ds(q0, tq), :] = jnp.concatenate(halves, axis=0).T.astype(o_ref.dtype)
        return carry

    lax.fori_loop(0, n_steps, step, 0)


def _win_attn(q, k, vt, kc, vtc, sink, seq, ctx_len, tq_want=256, blocks_per_step=4):
    t = q.shape[0]
    nb = t // seq
    groups = WIN_KV_HEADS
    assert q.shape[1] == groups * LANES
    tq = _row_tile(seq, min(tq_want, max(seq - 2 * WINDOW, WINDOW)))
    span = tq + 2 * WINDOW
    assert seq >= span and span % vt.shape[4] == 0 and WINDOW % vt.shape[4] == 0
    blocks_per_step = min(blocks_per_step, seq // tq)
    assert (seq // tq) % blocks_per_step == 0
    n_units = 2 * blocks_per_step
    return pl.pallas_call(
        functools.partial(_win_kernel, tq=tq, span=span, blocks_per_step=blocks_per_step),
        out_shape=jax.ShapeDtypeStruct(q.shape, BF16),
        grid=(nb, groups),
        in_specs=[
            pl.BlockSpec(memory_space=pltpu.SMEM),
            pl.BlockSpec((seq, LANES), lambda b, g: (b, g)),
            pl.BlockSpec((seq, LANES), lambda b, g: (b, g)),
            pl.BlockSpec((1, 1) + vt.shape[2:], lambda b, g: (b, g, 0, 0, 0)),
            pl.BlockSpec((ctx_len, LANES), lambda b, g: (b, g)),
            pl.BlockSpec((1, 1) + vtc.shape[2:], lambda b, g: (b, g, 0, 0, 0)),
        ],
        out_specs=pl.BlockSpec((seq, LANES), lambda b, g: (b, g)),
        scratch_shapes=[pltpu.VMEM((n_units, span, tq), F32), pltpu.VMEM((n_units, ctx_len, tq), F32)],
        compiler_params=_cparams("parallel", "parallel"),
        name="win_attn",
    )(sink, q, k, vt, kc, vtc)


def _ctx_attn_kernel(sink_ref, q_ref, k_ref, v_ref, o_ref, *, n_pairs, use_sink):
    g = pl.program_id(1)
    hm = _half_masks(BF16)
    low = _lane_iota((1, LANES)) < HEAD_DIM
    k = k_ref[...]
    v = v_ref[...]
    for p in range(n_pairs):
        heads = (g * n_pairs + p) * 2
        sink_pair = (sink_ref[heads], sink_ref[heads + 1]) if use_sink else None
        o = _softmax_pair(q_ref[:, p * LANES:(p + 1) * LANES], [k], [v], [None], sink_pair, hm, low)
        o_ref[:, p * LANES:(p + 1) * LANES] = o.astype(o_ref.dtype)


def _ctx_attn(q, k, v, sink, groups, ctx_len, use_sink):
    t = q.shape[0]
    nb = t // ctx_len
    gw = q.shape[1] // groups
    n_pairs = gw // LANES
    return pl.pallas_call(
        functools.partial(_ctx_attn_kernel, n_pairs=n_pairs, use_sink=use_sink),
        out_shape=jax.ShapeDtypeStruct(q.shape, BF16),
        grid=(nb, groups),
        in_specs=[
            pl.BlockSpec(memory_space=pltpu.SMEM),
            pl.BlockSpec((ctx_len, gw), lambda b, g: (b, g)),
            pl.BlockSpec((ctx_len, LANES), lambda b, g: (b, g)),
            pl.BlockSpec((ctx_len, LANES), lambda b, g: (b, g)),
        ],
        out_specs=pl.BlockSpec((ctx_len, gw), lambda b, g: (b, g)),
        compiler_params=_cparams("parallel", "parallel"),
        name="ctx_attn",
    )(sink, q, k, v)


def _outproj_ffn_kernel(x_ref, mod_ref, of_ref, ob_ref, r_ref, og_ref, ow_ref, gain_ref, bd_ref, w_ref,
                        g2_ref, win_ref, wout_ref, o_ref, *, f):
    o = of_ref[...] + ob_ref[...]
    r = r_ref[...].astype(F32)
    gate = r * jax.nn.sigmoid(r)
    ss = _dot((o * o).astype(BF16), bd_ref[...])
    on = o * lax.rsqrt(ss * (1.0 / GLA_DV) + EPS) * gain_ref[...]
    cat = jnp.concatenate([(on * gate).astype(BF16), og_ref[...], ow_ref[...]], axis=-1)
    x = x_ref[...] + mod_ref[0, 5:6, :] * _dot(cat, w_ref[...])
    o_ref[...] = _ffn_body(x, mod_ref, g2_ref, win_ref, wout_ref, 6, f)


def _outproj_ffn(x, mods, o_gla_f, o_gla_b, r, o_glb, o_win, gla_gain, bd, w_out, gain2, w2_in, w2_out,
                 rows_per_batch, tm_want=512):
    t, d = x.shape
    f = w2_out.shape[0]
    tm = _row_tile(rows_per_batch, tm_want)
    return pl.pallas_call(
        functools.partial(_outproj_ffn_kernel, f=f),
        out_shape=jax.ShapeDtypeStruct((t, d), F32),
        grid=(t // tm,),
        in_specs=[
            pl.BlockSpec((tm, d), lambda i: (i, 0)),
            _mod_spec(mods, rows_per_batch // tm),
            pl.BlockSpec((tm, GLA_V), lambda i: (i, 0)),
            pl.BlockSpec((tm, GLA_V), lambda i: (i, 0)),
            pl.BlockSpec((tm, GLA_V), lambda i: (i, 0)),
            pl.BlockSpec((tm, o_glb.shape[1]), lambda i: (i, 0)),
            pl.BlockSpec((tm, o_win.shape[1]), lambda i: (i, 0)),
            _const_spec(gla_gain.shape),
            _const_spec(bd.shape),
            _const_spec(w_out.shape),
            _const_spec((1, d)),
            _const_spec((d, 2 * f)),
            _const_spec((f, d)),
        ],
        out_specs=pl.BlockSpec((tm, d), lambda i: (i, 0)),
        compiler_params=_cparams("parallel"),
        name="mix_outproj_ffn",
    )(x, mods, o_gla_f, o_gla_b, r, o_glb, o_win, gla_gain, bd, w_out, gain2.reshape(1, d), w2_in, w2_out)


def _rope_tables(seq):
    rows = seq // GRID_W
    row = jnp.repeat(jnp.arange(rows, dtype=F32), GRID_W)
    col = (jnp.arange(rows * GRID_W) % GRID_W).astype(F32)
    n_freq = HEAD_DIM // 4
    inv = jnp.power(ROPE_BASE, -jnp.arange(n_freq, dtype=F32) / n_freq)
    ang = jnp.concatenate([row[:, None] * inv, col[:, None] * inv], axis=-1)
    cos, sin = jnp.cos(ang), jnp.sin(ang)
    cos_t = jnp.concatenate([cos, cos] * (LANES // HEAD_DIM), axis=-1)
    sin_t = jnp.concatenate([-sin, sin] * (LANES // HEAD_DIM), axis=-1)
    return cos_t, sin_t


def _block_diag_ones(n, block):
    idx = np.arange(n) // block
    return jnp.asarray(idx[:, None] == idx[None, :], dtype=BF16)


def kernel(x, c, ctx, c_ctx, mod_w, mod_b, norm_ffn1, ffn1_w_in, ffn1_w_out, norm_mix, mix_w_in, mix_w_out,
           gla_wg_f, gla_bg_f, gla_wg_b, gla_bg_b, gla_out_norm, glb_q_norm, glb_k_norm,
           win_q_norm, win_k_norm, win_sink, norm_ffn2, ffn2_w_in, ffn2_w_out):
    bsz, seq, d = x.shape
    ctx_len = ctx.shape[1]
    depth = mod_w.shape[0]
    in_splits = (GLA_QK, GLA_QK, GLA_V, GLA_V, 2 * GLA_GATE_RANK,
                 GLB_HEADS * HEAD_DIM, GLB_KV_HEADS * HEAD_DIM, GLB_KV_HEADS * HEAD_DIM,
                 WIN_HEADS * HEAD_DIM, WIN_KV_HEADS * HEAD_DIM, WIN_KV_HEADS * HEAD_DIM)

    n_rows = -(-(bsz + 1) // 8) * 8
    c_rows = jnp.concatenate([c, c_ctx[None, :], jnp.zeros((n_rows - bsz - 1, d), F32)], axis=0)
    mods = _modvec(c_rows, mod_w, mod_b)

    cos_l, sin_l = _rope_tables(seq)
    cos_c = jnp.ones((ctx_len, LANES), F32)
    sin_c = jnp.zeros((ctx_len, LANES), F32)
    bd = _block_diag_ones(SLAB, HEAD_DIM)
    zero_state = jnp.zeros((bsz, 2, GLA_V, GLA_QK), F32)

    xl = x.reshape(bsz * seq, d)
    xc = ctx.reshape(bsz * ctx_len, d)

    for l in range(depth):
        need_ctx = l < depth - 1
        mods_l = mods[l, :bsz].reshape(bsz, N_MOD, d)
        mods_c = mods[l, bsz:bsz + 1].reshape(1, N_MOD, d)
        w1_in, w1_out = ffn1_w_in[l].astype(BF16), ffn1_w_out[l].astype(BF16)
        w2_in, w2_out = ffn2_w_in[l].astype(BF16), ffn2_w_out[l].astype(BF16)
        wm = mix_w_in[l]
        src = dict(zip(("aq", "ak", "av", "ar", "ad", "gq", "gk", "gv", "wq", "wk", "wv"),
                       jnp.split(wm, np.cumsum(in_splits)[:-1], axis=1)))
        w_mix = jnp.concatenate([src[n] for n in ("aq", "ak", "av", "ar", "gq", "gk", "wk", "gv", "wv", "wq", "ad")]
                                + [jnp.zeros((d, GATE_PAD - 2 * GLA_GATE_RANK), F32)], axis=1).astype(BF16)
        assert w_mix.shape[1] == _C_END
        wg = jnp.zeros((GATE_PAD, 2 * GLA_QK), F32)
        wg = wg.at[:GLA_GATE_RANK, :GLA_QK].set(gla_wg_f[l])
        wg = wg.at[GLA_GATE_RANK:2 * GLA_GATE_RANK, GLA_QK:].set(gla_wg_b[l]).astype(BF16)
        bg = jnp.concatenate([gla_bg_f[l], gla_bg_b[l]])[None, :]
        qk_gains = jnp.stack([jnp.tile(gn[l], LANES // HEAD_DIM)
                              for gn in (glb_q_norm, glb_k_norm, win_q_norm, win_k_norm)])
        gla_gain = jnp.tile(gla_out_norm[l], GLA_HEADS)[None, :]
        w_out = mix_w_out[l].astype(BF16)
        sink = win_sink[l]

        xl = _ffn(xl, mods_l, 0, norm_ffn1[l], w1_in, w1_out, seq)
        xc = _ffn(xc, mods_c, 0, norm_ffn1[l], w1_in, w1_out, bsz * ctx_len)

        pc = _inproj(xc, mods_c, norm_mix[l], w_mix, wg, bg, qk_gains, cos_c, sin_c, bd, ctx_len)
        pq = _inproj(xl, mods_l, norm_mix[l], w_mix, wg, bg, qk_gains, cos_l, sin_l, bd, seq)
        aq, ak, av, ar, ag, gq, gk, gv, wq, wk, wv, gvt, wvt = pq
        aqc, akc, avc, arc, agc, gqc, gkc, gvc, wqc, wkc, wvc, gvtc, wvtc = pc

        oc_gla_f, oc_gla_b, states = _gla(aqc, akc, avc, agc, zero_state, ctx_len)
        o_gla_f, o_gla_b, _ = _gla(aq, ak, av, ag, states, seq)
        o_glb = _glb_attn(gq, gk, gvt, gkc, gvtc, seq, ctx_len)
        o_win = _win_attn(wq, wk, wvt, wkc, wvtc, sink, seq, ctx_len)
        xl = _outproj_ffn(xl, mods_l, o_gla_f, o_gla_b, ar, o_glb, o_win, gla_gain, bd, w_out, norm_ffn2[l],
                          w2_in, w2_out, seq)

        if need_ctx:
            oc_glb = _ctx_attn(gqc, gkc, gvc, sink, GLB_KV_HEADS, ctx_len, use_sink=False)
            oc_win = _ctx_attn(wqc, wkc, wvc, sink, WIN_KV_HEADS, ctx_len, use_sink=True)
            xc = _outproj_ffn(xc, mods_c, oc_gla_f, oc_gla_b, arc, oc_glb, oc_win, gla_gain, bd, w_out, norm_ffn2[l],
                              w2_in, w2_out, bsz * ctx_len)

    return xl.reshape(bsz, seq, d)
```

```python
import functools

import numpy as np
import jax
import jax.numpy as jnp
from jax import lax
from jax.experimental import pallas as pl
from jax.experimental.pallas import tpu as pltpu

GRID_W = 64
HEAD_DIM = 64
GLA_HEADS = 4
GLA_DK = 32
GLA_DV = 64
GLA_GATE_RANK = 16
GLA_GATE_TAU = 16.0
GLA_CHUNK = 64
GLB_HEADS = 8
GLB_KV_HEADS = 2
WIN_HEADS = 4
WIN_KV_HEADS = 2
WINDOW = 128
ROPE_BASE = 10000.0
N_MOD = 9
EPS = 1e-6

LANES = 128
VMEM_LIMIT_BYTES = 56 * 1024 * 1024

GLA_QK = GLA_HEADS * GLA_DK
GLA_V = GLA_HEADS * GLA_DV
GATE_PAD = LANES
MASK_VALUE = -1e30
LOG2_E = 1.4426950408889634
KEY_CHUNK = 256
ROW_TILE = 512
FFN_HIDDEN_CHUNK = 1536
GLA_ROW_TILE = 2048
GLB_Q_TILE = 4096
GLB_COL_BLOCK = 256
WIN_Q_BLOCK = 256
WIN_BLOCKS_PER_STEP = 4
GLA_GROUP = 4
WIN_CHUNK = 128
VT_ROWS = HEAD_DIM + 16

BF16 = jnp.bfloat16
F32 = jnp.float32


def _cparams(*sem):
    return pltpu.CompilerParams(dimension_semantics=sem, vmem_limit_bytes=VMEM_LIMIT_BYTES)


def _dot(a, b):
    return jnp.dot(a, b, preferred_element_type=F32)


def _dot_nt(a, b):
    return lax.dot_general(a, b, (((1,), (1,)), ((), ())), preferred_element_type=F32)


def _dot_tn(a, b):
    return lax.dot_general(a, b, (((0,), (0,)), ((), ())), preferred_element_type=F32)


def _lane_iota(shape):
    return lax.broadcasted_iota(jnp.int32, shape, len(shape) - 1)


def _modvec_kernel(c_ref, w_ref, b_ref, o_ref):
    c = c_ref[...]
    s = (c * jax.nn.sigmoid(c)).astype(BF16)
    o_ref[0] = _dot(s, w_ref[0].astype(BF16)) + b_ref[0]


def _modvec(c_rows, mod_w, mod_b):
    depth, d, n = mod_w.shape
    rows = c_rows.shape[0]
    tn = n // N_MOD
    return pl.pallas_call(
        _modvec_kernel,
        out_shape=jax.ShapeDtypeStruct((depth, rows, n), F32),
        grid=(depth, n // tn),
        in_specs=[
            pl.BlockSpec((rows, d), lambda l, j: (0, 0)),
            pl.BlockSpec((1, d, tn), lambda l, j: (l, 0, j)),
            pl.BlockSpec((1, 1, tn), lambda l, j: (l, 0, j)),
        ],
        out_specs=pl.BlockSpec((1, rows, tn), lambda l, j: (l, 0, j)),
        compiler_params=_cparams("arbitrary", "arbitrary"),
        name="modvec",
    )(c_rows, mod_w, mod_b.reshape(depth, 1, n))


def _norm_modulate(x, gain, mod_ref, k_shift):
    shift = mod_ref[0, k_shift:k_shift + 1, :]
    scale = mod_ref[0, k_shift + 1:k_shift + 2, :]
    y = x * lax.rsqrt(jnp.mean(x * x, axis=-1, keepdims=True) + EPS)
    return (y * gain) * (1.0 + scale) + shift


def _mod_spec(mods, tiles_per_batch):
    nb, nm, d = mods.shape
    if nb == 1:
        return pl.BlockSpec((1, nm, d), lambda i: (0, 0, 0))
    return pl.BlockSpec((1, nm, d), lambda i: (i // tiles_per_batch, 0, 0))


def _const_spec(shape):
    nd = len(shape)
    return pl.BlockSpec(shape, lambda *_: (0,) * nd, pipeline_mode=pl.Buffered(1))


def _row_tile(n_rows_per_batch, want):
    t = min(want, n_rows_per_batch)
    assert n_rows_per_batch % t == 0
    return t


def _ffn_chunks(f):
    step = min(FFN_HIDDEN_CHUNK, f)
    return tuple((lo, min(lo + step, f)) for lo in range(0, f, step))


def _ffn_kernel(x_ref, mod_ref, g_ref, win_ref, wout_ref, o_ref, *, k0, f):
    o_ref[...] = _ffn_body(x_ref[...], mod_ref, g_ref, win_ref, wout_ref, k0, f)


def _ffn_body(x, mod_ref, g_ref, win_ref, wout_ref, k0, f):
    hb = _norm_modulate(x, g_ref[...], mod_ref, k0).astype(BF16)
    gate = mod_ref[0, k0 + 2:k0 + 3, :]
    acc = None
    for lo, hi in _ffn_chunks(f):
        a = _dot(hb, win_ref[:, lo:hi])
        b = _dot(hb, win_ref[:, f + lo:f + hi])
        u = ((a * jax.nn.sigmoid(a)) * b).astype(BF16)
        part = _dot(u, wout_ref[lo:hi, :])
        acc = part if acc is None else acc + part
    return x + (0.5 * gate) * acc


def _ffn(x, mods, k0, gain, w_in, w_out, rows_per_batch, tm_want=ROW_TILE):
    t, d = x.shape
    f = w_out.shape[0]
    tm = _row_tile(rows_per_batch, tm_want)
    return pl.pallas_call(
        functools.partial(_ffn_kernel, k0=k0, f=f),
        out_shape=jax.ShapeDtypeStruct((t, d), F32),
        grid=(t // tm,),
        in_specs=[
            pl.BlockSpec((tm, d), lambda i: (i, 0)),
            _mod_spec(mods, rows_per_batch // tm),
            _const_spec((1, d)),
            _const_spec((d, 2 * f)),
            _const_spec((f, d)),
        ],
        out_specs=pl.BlockSpec((tm, d), lambda i: (i, 0)),
        compiler_params=_cparams("parallel"),
        name="ffn",
    )(x, mods, gain.reshape(1, d), w_in, w_out)


_C_AQ = 0
_C_AK = _C_AQ + GLA_QK
_C_AV = _C_AK + GLA_QK
_C_AR = _C_AV + GLA_V
_C_GQ = _C_AR + GLA_V
_C_GK = _C_GQ + GLB_HEADS * HEAD_DIM
_C_WK = _C_GK + GLB_KV_HEADS * HEAD_DIM
_C_GV = _C_WK + WIN_KV_HEADS * HEAD_DIM
_C_WV = _C_GV + GLB_KV_HEADS * HEAD_DIM
_C_WQ = _C_WV + WIN_KV_HEADS * HEAD_DIM
_C_AD = _C_WQ + WIN_HEADS * HEAD_DIM
_C_END = _C_AD + GATE_PAD
SLAB = 2 * LANES


def _norm_rope_slab(x, bd_ref, gains, cos, sin_signed, out_scales):
    ss = _dot((x * x).astype(BF16), bd_ref[...])
    first_half = (_lane_iota((x.shape[0], LANES)) % HEAD_DIM) < (HEAD_DIM // 2)
    outs = []
    for j in range(2):
        sl = slice(j * LANES, (j + 1) * LANES)
        xn = x[:, sl] * lax.rsqrt(ss[:, sl] * (1.0 / HEAD_DIM) + EPS) * gains[j]
        partner = jnp.where(first_half, pltpu.roll(xn, LANES - HEAD_DIM // 2, 1), pltpu.roll(xn, HEAD_DIM // 2, 1))
        out = xn * cos + partner * sin_signed
        outs.append(out * out_scales[j] if out_scales[j] != 1.0 else out)
    return outs


def _dup_heads(x):
    swapped = pltpu.roll(x, HEAD_DIM, 1)
    low = _lane_iota(x.shape) < HEAD_DIM
    return jnp.where(low, x, swapped), jnp.where(low, swapped, x)


def _store_vt(vt_ref, v):
    chunk = vt_ref.shape[4]
    for c in range(vt_ref.shape[2]):
        vt = v[c * chunk:(c + 1) * chunk, :].T
        for kv in range(vt_ref.shape[1]):
            vt_ref[0, kv, c, 0:HEAD_DIM, :] = vt[kv * HEAD_DIM:(kv + 1) * HEAD_DIM].astype(BF16)
            vt_ref[0, kv, c, HEAD_DIM:VT_ROWS, :] = jnp.ones((VT_ROWS - HEAD_DIM, chunk), BF16)


def _inproj_kernel(x_ref, mod_ref, g_ref, w_ref, wg_ref, bg_ref, qkg_ref, cos_ref, sin_ref, bd_ref,
                   aq_ref, ak_ref, av_ref, ar_ref, ag_ref, gq_ref, gk_ref, wq_ref, wk_ref, gvt_ref, wvt_ref):
    hb = _norm_modulate(x_ref[...], g_ref[...], mod_ref, 3).astype(BF16)
    cos = cos_ref[...]
    sin = sin_ref[...]
    tiles = {}

    def proj(lo, width):
        j, off = divmod(lo, SLAB)
        assert off + width <= SLAB
        if j not in tiles:
            tiles[j] = _dot(hb, w_ref[:, j * SLAB:min((j + 1) * SLAB, _C_END)])
        return tiles[j][:, off:off + width]

    q_scale = HEAD_DIM ** -0.5
    g_glb_q, g_glb_k, g_win_q, g_win_k = (qkg_ref[r:r + 1, :] for r in range(4))

    order = (_C_GQ, _C_GQ + SLAB, _C_WQ, _C_GK, _C_GV, _C_AD, _C_AQ, _C_AV, _C_AR)

    def issue_ahead(n):
        for lo in order[:n + 3]:
            proj(lo, LANES)

    for j in range(GLB_HEADS * HEAD_DIM // SLAB):
        issue_ahead(j)
        halves = _norm_rope_slab(proj(_C_GQ + j * SLAB, SLAB), bd_ref, (g_glb_q, g_glb_q), cos, sin,
                                 (q_scale, q_scale))
        for i, q in enumerate(halves):
            gq_ref[:, j * SLAB + i * LANES:j * SLAB + (i + 1) * LANES] = q.astype(BF16)
    issue_ahead(2)
    halves = _norm_rope_slab(proj(_C_WQ, SLAB), bd_ref, (g_win_q, g_win_q), cos, sin, (q_scale, q_scale))
    for i, q in enumerate(halves):
        wq_ref[:, i * LANES:(i + 1) * LANES] = q.astype(BF16)
    issue_ahead(3)
    k_glb, k_win = _norm_rope_slab(proj(_C_GK, SLAB), bd_ref, (g_glb_k, g_win_k), cos, sin, (1.0, 1.0))
    k0, k1 = _dup_heads(k_glb)
    gk_ref[:, 0:LANES] = k0.astype(BF16)
    gk_ref[:, LANES:2 * LANES] = k1.astype(BF16)
    k0, k1 = _dup_heads(k_win)
    wk_ref[:, 0:LANES] = k0.astype(BF16)
    wk_ref[:, LANES:2 * LANES] = k1.astype(BF16)

    issue_ahead(4)
    _store_vt(gvt_ref, proj(_C_GV, LANES))
    _store_vt(wvt_ref, proj(_C_WV, LANES))

    issue_ahead(len(order))
    z = _dot(proj(_C_AD, GATE_PAD).astype(BF16), wg_ref[...]) + bg_ref[...]
    log_sig = jnp.minimum(z, 0.0) - jnp.log1p(jnp.exp(-jnp.abs(z)))
    ag_ref[...] = log_sig * (1.0 / GLA_GATE_TAU)
    aq_ref[...] = (proj(_C_AQ, GLA_QK) * (GLA_DK ** -0.5)).astype(BF16)
    ak_ref[...] = proj(_C_AK, GLA_QK).astype(BF16)
    av_ref[...] = proj(_C_AV, GLA_V).astype(BF16)
    ar_ref[...] = proj(_C_AR, GLA_V).astype(BF16)


def _inproj(x, mods, gain, w, wg, bg, qk_gains, cos, sin, bd, rows_per_batch, tm_want=ROW_TILE):
    t, d = x.shape
    tm = _row_tile(rows_per_batch, tm_want)
    tpb = rows_per_batch // tm
    widths = (GLA_QK, GLA_QK, GLA_V, GLA_V, 2 * GLA_QK, GLB_HEADS * HEAD_DIM, 2 * LANES,
              WIN_HEADS * HEAD_DIM, 2 * LANES)
    dtypes = (BF16, BF16, BF16, BF16, F32, BF16, BF16, BF16, BF16)
    nb = t // rows_per_batch
    vt_shapes, vt_specs = [], []
    for kv_heads, want in ((GLB_KV_HEADS, KEY_CHUNK), (WIN_KV_HEADS, WIN_CHUNK)):
        chunk = min(want, tm)
        vt_shapes.append(jax.ShapeDtypeStruct((nb, kv_heads, rows_per_batch // chunk, VT_ROWS, chunk), BF16))
        vt_specs.append(pl.BlockSpec((1, kv_heads, tm // chunk, VT_ROWS, chunk),
                                     lambda i: (i // tpb, 0, i % tpb, 0, 0)))
    return pl.pallas_call(
        _inproj_kernel,
        out_shape=tuple(jax.ShapeDtypeStruct((t, wd), dt) for wd, dt in zip(widths, dtypes)) + tuple(vt_shapes),
        grid=(t // tm,),
        in_specs=[
            pl.BlockSpec((tm, d), lambda i: (i, 0)),
            _mod_spec(mods, tpb),
            _const_spec((1, d)),
            _const_spec(w.shape),
            _const_spec(wg.shape),
            _const_spec(bg.shape),
            _const_spec(qk_gains.shape),
            pl.BlockSpec((tm, LANES), lambda i: (i % tpb, 0)),
            pl.BlockSpec((tm, LANES), lambda i: (i % tpb, 0)),
            _const_spec(bd.shape),
        ],
        out_specs=tuple(pl.BlockSpec((tm, wd), lambda i: (i, 0)) for wd in widths) + tuple(vt_specs),
        compiler_params=_cparams("parallel"),
        name="mix_inproj",
    )(x, mods, gain.reshape(1, d), w, wg, bg, qk_gains, cos, sin, bd)


def _gla_kernel(qf_ref, kf_ref, vf_ref, gf_ref, qb_ref, kb_ref, vb_ref, gb_ref, s0_ref,
                of_ref, ob_ref, sfin_ref, st_ref, *, n_chunks):
    c_len = GLA_CHUNK
    i = pl.program_id(1)
    dir_refs = ((qf_ref, kf_ref, vf_ref, gf_ref, of_ref), (qb_ref, kb_ref, vb_ref, gb_ref, ob_ref))
    signs = (1, -1)

    @pl.when(i == 0)
    def _():
        st_ref[...] = s0_ref[0]

    group = min(GLA_GROUP, n_chunks)
    g_len = group * c_len
    row = lax.broadcasted_iota(jnp.int32, (g_len, g_len), 0)
    col = lax.broadcasted_iota(jnp.int32, (g_len, g_len), 1)
    same_chunk = row // c_len == col // c_len
    cum_ops = [jnp.where(((row - col) * sg >= 0) & same_chunk, 1.0, 0.0).astype(BF16) for sg in signs]
    row4 = lax.broadcasted_iota(jnp.int32, (c_len, GLA_HEADS * c_len), 0)
    col4 = lax.broadcasted_iota(jnp.int32, (c_len, GLA_HEADS * c_len), 1) % c_len
    keep4s = [(row4 - col4) * sg >= 0 for sg in signs]
    qk_lane_head = _lane_iota((1, GLA_QK)) // GLA_DK
    v_lane_head = _lane_iota((1, GLA_V)) // GLA_DV
    qk_head_mask = [(qk_lane_head == h).astype(F32) for h in range(GLA_HEADS)]
    v_head_mask = [(v_lane_head == h).astype(BF16) for h in range(GLA_HEADS)]
    st_row_head = lax.broadcasted_iota(jnp.int32, (GLA_V, GLA_QK), 0) // GLA_DV
    st_col_head = lax.broadcasted_iota(jnp.int32, (GLA_V, GLA_QK), 1) // GLA_DK
    st_mask = st_row_head == st_col_head

    n_groups = n_chunks // group

    def group_starts(d, j):
        starts = []
        for p in range(group):
            c = j * group + p
            if d == 1:
                c = n_chunks - 1 - c
            starts.append(pl.multiple_of(c * c_len, c_len))
        return starts

    def decay_stage(d, j):
        q_ref, k_ref, _, g_ref, _ = dir_refs[d]
        starts = group_starts(d, j)
        q = jnp.concatenate([q_ref[pl.ds(r, c_len), :] for r in starts], axis=0).astype(F32)
        k = jnp.concatenate([k_ref[pl.ds(r, c_len), :] for r in starts], axis=0).astype(F32)
        g = jnp.concatenate([g_ref[pl.ds(r, c_len), :] for r in starts], axis=0)
        g_hi = g.astype(BF16)
        r1 = g - g_hi.astype(F32)
        g_mid = r1.astype(BF16)
        g_lo = (r1 - g_mid.astype(F32)).astype(BF16)
        b = _dot(cum_ops[d], g_hi) + _dot(cum_ops[d], g_mid) + _dot(cum_ops[d], g_lo)
        q_in = (q * jnp.exp(b)).astype(BF16)
        k_out = k * jnp.exp(-b)
        decays = jnp.concatenate([jnp.exp(jnp.sum(g[p * c_len:(p + 1) * c_len], axis=0, keepdims=True))
                                  for p in range(group)], axis=0)
        return q_in, k_out, decays

    def chunk_group(j, staged):
        j_next = jnp.minimum(j + 1, n_groups - 1)
        staged_next = tuple(decay_stage(d, j_next) for d in range(2))
        starts = [group_starts(d, j) for d in range(2)]
        vs = [[dir_refs[d][2][pl.ds(r, c_len), :] for r in starts[d]] for d in range(2)]
        sts = [st_ref[0], st_ref[1]]
        for p in range(group):
            rows = slice(p * c_len, (p + 1) * c_len)
            for d in range(2):
                q_in, k_out, decays = staged[d]
                v_p = vs[d][p]
                decay = decays[p:p + 1, :]
                k_out_p = k_out[rows]
                k_dec = (k_out_p * decay).astype(BF16)
                k_stack = jnp.concatenate([(k_out_p * qk_head_mask[h]).astype(BF16) for h in range(GLA_HEADS)],
                                          axis=0)
                a = _dot_nt(q_in[rows], k_stack)
                a = jnp.where(keep4s[d], a, 0.0).astype(BF16)
                v_bd = jnp.concatenate([v_p * v_head_mask[h] for h in range(GLA_HEADS)], axis=0)
                o = _dot(a, v_bd) + _dot_nt(q_in[rows], sts[d].astype(BF16))
                dir_refs[d][4][pl.ds(starts[d][p], c_len), :] = o
                ds_t = _dot_tn(v_p, k_dec)
                sts[d] = sts[d] * decay + jnp.where(st_mask, ds_t, 0.0)
        st_ref[0] = sts[0]
        st_ref[1] = sts[1]
        return staged_next

    lax.fori_loop(0, n_groups, chunk_group, tuple(decay_stage(d, 0) for d in range(2)))

    @pl.when(i == pl.num_programs(1) - 1)
    def _():
        sfin_ref[0] = st_ref[...]


def _gla(q, k, v, g, s0, rows_per_batch, tt_want=GLA_ROW_TILE):
    t = q.shape[0]
    nb = t // rows_per_batch
    tt = _row_tile(rows_per_batch, tt_want)
    nt = rows_per_batch // tt

    def fwd(lane_block):
        return lambda b, i: (b * nt + i, lane_block)

    def bwd(lane_block):
        return lambda b, i: (b * nt + nt - 1 - i, lane_block)

    state_spec = pl.BlockSpec((1, 2, GLA_V, GLA_QK), lambda b, i: (b, 0, 0, 0))
    return pl.pallas_call(
        functools.partial(_gla_kernel, n_chunks=tt // GLA_CHUNK),
        out_shape=(jax.ShapeDtypeStruct((t, GLA_V), F32), jax.ShapeDtypeStruct((t, GLA_V), F32),
                   jax.ShapeDtypeStruct((nb, 2, GLA_V, GLA_QK), F32)),
        grid=(nb, nt),
        in_specs=[
            pl.BlockSpec((tt, GLA_QK), fwd(0)), pl.BlockSpec((tt, GLA_QK), fwd(0)),
            pl.BlockSpec((tt, GLA_V), fwd(0)), pl.BlockSpec((tt, GLA_QK), fwd(0)),
            pl.BlockSpec((tt, GLA_QK), bwd(0)), pl.BlockSpec((tt, GLA_QK), bwd(0)),
            pl.BlockSpec((tt, GLA_V), bwd(0)), pl.BlockSpec((tt, GLA_QK), bwd(1)),
            state_spec,
        ],
        out_specs=(pl.BlockSpec((tt, GLA_V), fwd(0)), pl.BlockSpec((tt, GLA_V), bwd(0)), state_spec),
        scratch_shapes=[pltpu.VMEM((2, GLA_V, GLA_QK), F32)],
        compiler_params=_cparams("parallel", "arbitrary"),
        name="gla_scan",
    )(q, k, v, g, q, k, v, g, s0)


def _glb_kernel(q_ref, k_ref, vt_ref, kc_ref, vtc_ref, o_ref, qt_ref, m_ref, acc_ref, s_ref, *,
                n_pairs, col_block, chunks_per_tile):
    tq = q_ref.shape[0]
    n_lat = vt_ref.shape[2] // chunks_per_tile
    tk = vt_ref.shape[4] * chunks_per_tile
    n_heads = 2 * n_pairs
    for p in range(n_pairs):
        qt = (q_ref[:, p * LANES:(p + 1) * LANES].astype(F32) * LOG2_E).T
        qt_ref[2 * p] = qt[0:HEAD_DIM].astype(BF16)
        qt_ref[2 * p + 1] = qt[HEAD_DIM:2 * HEAD_DIM].astype(BF16)

    m_ref[...] = jnp.full(m_ref.shape, MASK_VALUE, F32)
    acc_ref[...] = jnp.zeros(acc_ref.shape, F32)

    units = [(h, cb) for h in range(n_heads) for cb in range(tq // col_block)]

    def scores(k, u):
        h, cb = units[u]
        return _dot(k[:, 0:HEAD_DIM], qt_ref[h, :, cb * col_block:(cb + 1) * col_block])

    def step(k, vt, k_next):
        w = k.shape[0]
        for u, (h, cb) in enumerate(units):
            s = s_ref[u, 0:w, :]
            if k_next is not None:
                s_ref[u, 0:k_next.shape[0], :] = scores(k_next, u)
            cols = slice(cb * col_block, (cb + 1) * col_block)
            m_prev = m_ref[h, :, cols]
            m_new = jnp.maximum(m_prev, jnp.max(s, axis=0, keepdims=True))
            alpha = jnp.exp2(m_prev - m_new)
            e = jnp.exp2((s - m_new).astype(BF16))
            m_ref[h, :, cols] = m_new
            ch = w // len(vt)
            pv = _dot(vt[0], e[0:ch, :])
            for c in range(1, len(vt)):
                pv = pv + _dot(vt[c], e[c * ch:(c + 1) * ch, :])
            acc_ref[h, :, cols] = acc_ref[h, :, cols] * alpha + pv

    def lat_tile(j):
        return k_ref[pl.ds(pl.multiple_of(j * tk, tk), tk), :]

    def lat_vt(j):
        return [vt_ref[0, 0, j * chunks_per_tile + c] for c in range(chunks_per_tile)]

    wc = vtc_ref.shape[4]
    ctx_tiles = [kc_ref[j * wc:(j + 1) * wc, :] for j in range(vtc_ref.shape[2])]

    for u in range(len(units)):
        s_ref[u, 0:tk, :] = scores(lat_tile(0), u)

    def body(j, carry):
        step(lat_tile(j), lat_vt(j), lat_tile(j + 1))
        return carry

    lax.fori_loop(0, n_lat - 1, body, 0)
    step(lat_tile(n_lat - 1), lat_vt(n_lat - 1), ctx_tiles[0])
    for j, kc in enumerate(ctx_tiles):
        step(kc, [vtc_ref[0, 0, j]], ctx_tiles[j + 1] if j + 1 < len(ctx_tiles) else None)

    for p in range(n_pairs):
        halves = [acc_ref[h, 0:HEAD_DIM, :] / acc_ref[h, HEAD_DIM:HEAD_DIM + 1, :] for h in (2 * p, 2 * p + 1)]
        o_ref[:, p * LANES:(p + 1) * LANES] = jnp.concatenate(halves, axis=0).T.astype(o_ref.dtype)


def _glb_attn(q, k, vt, kc, vtc, seq, ctx_len, tq_want=GLB_Q_TILE, col_block=GLB_COL_BLOCK, tk_want=KEY_CHUNK):
    t = q.shape[0]
    nb = t // seq
    groups = GLB_KV_HEADS
    gw = q.shape[1] // groups
    n_pairs = gw // LANES
    tq = _row_tile(seq, tq_want)
    nq = seq // tq
    col_block = min(col_block, tq)
    n_units = 2 * n_pairs * (tq // col_block)
    chunks_per_tile = max(1, min(tk_want, seq) // vt.shape[4])
    assert vt.shape[2] % chunks_per_tile == 0
    max_keys = max(vt.shape[4] * chunks_per_tile, vtc.shape[4])
    return pl.pallas_call(
        functools.partial(_glb_kernel, n_pairs=n_pairs, col_block=col_block, chunks_per_tile=chunks_per_tile),
        out_shape=jax.ShapeDtypeStruct(q.shape, BF16),
        grid=(nb, groups, nq),
        in_specs=[
            pl.BlockSpec((tq, gw), lambda b, g, i: (b * nq + i, g)),
            pl.BlockSpec((seq, LANES), lambda b, g, i: (b, g)),
            pl.BlockSpec((1, 1) + vt.shape[2:], lambda b, g, i: (b, g, 0, 0, 0)),
            pl.BlockSpec((ctx_len, LANES), lambda b, g, i: (b, g)),
            pl.BlockSpec((1, 1) + vtc.shape[2:], lambda b, g, i: (b, g, 0, 0, 0)),
        ],
        out_specs=pl.BlockSpec((tq, gw), lambda b, g, i: (b * nq + i, g)),
        scratch_shapes=[pltpu.VMEM((2 * n_pairs, HEAD_DIM, tq), BF16),
                        pltpu.VMEM((2 * n_pairs, 1, tq), F32),
                        pltpu.VMEM((2 * n_pairs, VT_ROWS, tq), F32),
                        pltpu.VMEM((n_units, max_keys, col_block), F32)],
        compiler_params=_cparams("parallel", "parallel", "arbitrary"),
        name="glb_attn",
    )(q, k, vt, kc, vtc)


def _win_kernel(sink_ref, q_ref, k_ref, vt_ref, kc_ref, vtc_ref, o_ref, sloc_ref, sctx_ref, *,
                tq, span, blocks_per_step):
    g = pl.program_id(1)
    seq = k_ref.shape[0]
    n_blocks = seq // tq
    n_steps = n_blocks // blocks_per_step
    wchunk = vt_ref.shape[4]
    kc = kc_ref[...][:, 0:HEAD_DIM]
    vt_ctx = jnp.concatenate([vtc_ref[0, 0, c] for c in range(vtc_ref.shape[2])], axis=1)
    col_minus_row = (lax.broadcasted_iota(jnp.int32, (span, tq), 1)
                     - lax.broadcasted_iota(jnp.int32, (span, tq), 0))

    def band_start(qb):
        return pl.multiple_of(jnp.clip(qb * tq - WINDOW, 0, seq - span), wchunk)

    def block_scores(qb):
        k_loc = k_ref[pl.ds(band_start(qb), span), :][:, 0:HEAD_DIM]
        qt = (q_ref[pl.ds(pl.multiple_of(qb * tq, tq), tq), :].astype(F32) * LOG2_E).T
        for half in range(2):
            qt_h = qt[half * HEAD_DIM:(half + 1) * HEAD_DIM].astype(BF16)
            yield _dot(k_loc, qt_h), _dot(kc, qt_h)

    for j in range(blocks_per_step):
        for half, (s_loc, s_ctx) in enumerate(block_scores(j)):
            sloc_ref[2 * j + half] = s_loc
            sctx_ref[2 * j + half] = s_ctx

    def step(i, carry):
        for j in range(blocks_per_step):
            qb = i * blocks_per_step + j
            q0 = pl.multiple_of(qb * tq, tq)
            start = band_start(qb)
            c0 = start // wchunk
            bias = jnp.where(jnp.abs(col_minus_row + (q0 - start)) <= WINDOW, 0.0, MASK_VALUE)
            vt_loc = jnp.concatenate([vt_ref[0, 0, c0 + c] for c in range(span // wchunk)], axis=1)
            ahead = block_scores(jnp.minimum(qb + blocks_per_step, n_blocks - 1))
            halves = []
            for half in range(2):
                u = 2 * j + half
                sink2 = sink_ref[2 * g + half] * LOG2_E
                s_loc = sloc_ref[u] + bias
                s_ctx = sctx_ref[u]
                sloc_ref[u], sctx_ref[u] = next(ahead)
                m = jnp.maximum(jnp.max(s_loc, axis=0, keepdims=True), jnp.max(s_ctx, axis=0, keepdims=True))
                m = jnp.maximum(m, sink2)
                e_loc = jnp.exp2((s_loc - m).astype(BF16))
                e_ctx = jnp.exp2((s_ctx - m).astype(BF16))
                acc = _dot(vt_loc, e_loc) + _dot(vt_ctx, e_ctx)
                den = acc[HEAD_DIM:HEAD_DIM + 1, :] + jnp.exp2(sink2 - m)
                halves.append(acc[0:HEAD_DIM, :] / den)
            o_ref[pl.ds(q0, tq), :] = jnp.concatenate(halves, axis=0).T.astype(o_ref.dtype)
        return carry

    lax.fori_loop(0, n_steps, step, 0)


def _win_attn(q, k, vt, kc, vtc, sink, seq, ctx_len, tq_want=WIN_Q_BLOCK, blocks_per_step=WIN_BLOCKS_PER_STEP):
    t = q.shape[0]
    nb = t // seq
    groups = WIN_KV_HEADS
    assert q.shape[1] == groups * LANES
    tq = _row_tile(seq, min(tq_want, max(seq - 2 * WINDOW, WINDOW)))
    span = tq + 2 * WINDOW
    assert seq >= span and span % vt.shape[4] == 0 and WINDOW % vt.shape[4] == 0
    blocks_per_step = min(blocks_per_step, seq // tq)
    assert (seq // tq) % blocks_per_step == 0
    n_units = 2 * blocks_per_step
    return pl.pallas_call(
        functools.partial(_win_kernel, tq=tq, span=span, blocks_per_step=blocks_per_step),
        out_shape=jax.ShapeDtypeStruct(q.shape, BF16),
        grid=(nb, groups),
        in_specs=[
            pl.BlockSpec(memory_space=pltpu.SMEM),
            pl.BlockSpec((seq, LANES), lambda b, g: (b, g)),
            pl.BlockSpec((seq, LANES), lambda b, g: (b, g)),
            pl.BlockSpec((1, 1) + vt.shape[2:], lambda b, g: (b, g, 0, 0, 0)),
            pl.BlockSpec((ctx_len, LANES), lambda b, g: (b, g)),
            pl.BlockSpec((1, 1) + vtc.shape[2:], lambda b, g: (b, g, 0, 0, 0)),
        ],
        out_specs=pl.BlockSpec((seq, LANES), lambda b, g: (b, g)),
        scratch_shapes=[pltpu.VMEM((n_units, span, tq), F32), pltpu.VMEM((n_units, ctx_len, tq), F32)],
        compiler_params=_cparams("parallel", "parallel"),
        name="win_attn",
    )(sink, q, k, vt, kc, vtc)


def _ctx_attn_kernel(sink_ref, q_ref, k_ref, vt_ref, o_ref, *, n_pairs, use_sink):
    g = pl.program_id(1)
    k = k_ref[...][:, 0:HEAD_DIM]
    vt = jnp.concatenate([vt_ref[0, 0, c] for c in range(vt_ref.shape[2])], axis=1)
    for p in range(n_pairs):
        qt = (q_ref[:, p * LANES:(p + 1) * LANES].astype(F32) * LOG2_E).T
        halves = []
        for half in range(2):
            s = _dot(k, qt[half * HEAD_DIM:(half + 1) * HEAD_DIM].astype(BF16))
            m = jnp.max(s, axis=0, keepdims=True)
            if use_sink:
                sink2 = sink_ref[(g * n_pairs + p) * 2 + half] * LOG2_E
                m = jnp.maximum(m, sink2)
            acc = _dot(vt, jnp.exp2((s - m).astype(BF16)))
            den = acc[HEAD_DIM:HEAD_DIM + 1, :]
            if use_sink:
                den = den + jnp.exp2(sink2 - m)
            halves.append(acc[0:HEAD_DIM, :] / den)
        o_ref[:, p * LANES:(p + 1) * LANES] = jnp.concatenate(halves, axis=0).T.astype(o_ref.dtype)


def _ctx_attn(q, k, vt, sink, groups, ctx_len, use_sink):
    t = q.shape[0]
    nb = t // ctx_len
    gw = q.shape[1] // groups
    n_pairs = gw // LANES
    return pl.pallas_call(
        functools.partial(_ctx_attn_kernel, n_pairs=n_pairs, use_sink=use_sink),
        out_shape=jax.ShapeDtypeStruct(q.shape, BF16),
        grid=(nb, groups),
        in_specs=[
            pl.BlockSpec(memory_space=pltpu.SMEM),
            pl.BlockSpec((ctx_len, gw), lambda b, g: (b, g)),
            pl.BlockSpec((ctx_len, LANES), lambda b, g: (b, g)),
            pl.BlockSpec((1, 1) + vt.shape[2:], lambda b, g: (b, g, 0, 0, 0)),
        ],
        out_specs=pl.BlockSpec((ctx_len, gw), lambda b, g: (b, g)),
        compiler_params=_cparams("parallel", "parallel"),
        name="ctx_attn",
    )(sink, q, k, vt)


def _outproj_ffn_kernel(x_ref, mod_ref, of_ref, ob_ref, r_ref, og_ref, ow_ref, gain_ref, bd_ref, w_ref,
                        g2_ref, win_ref, wout_ref, o_ref, *, f):
    o = of_ref[...] + ob_ref[...]
    r = r_ref[...].astype(F32)
    gate = r * jax.nn.sigmoid(r)
    ss = _dot((o * o).astype(BF16), bd_ref[...])
    on = o * lax.rsqrt(ss * (1.0 / GLA_DV) + EPS) * gain_ref[...]
    cat = jnp.concatenate([(on * gate).astype(BF16), og_ref[...], ow_ref[...]], axis=-1)
    x = x_ref[...] + mod_ref[0, 5:6, :] * _dot(cat, w_ref[...])
    o_ref[...] = _ffn_body(x, mod_ref, g2_ref, win_ref, wout_ref, 6, f)


def _outproj_ffn(x, mods, o_gla_f, o_gla_b, r, o_glb, o_win, gla_gain, bd, w_out, gain2, w2_in, w2_out,
                 rows_per_batch, tm_want=ROW_TILE):
    t, d = x.shape
    f = w2_out.shape[0]
    tm = _row_tile(rows_per_batch, tm_want)
    return pl.pallas_call(
        functools.partial(_outproj_ffn_kernel, f=f),
        out_shape=jax.ShapeDtypeStruct((t, d), F32),
        grid=(t // tm,),
        in_specs=[
            pl.BlockSpec((tm, d), lambda i: (i, 0)),
            _mod_spec(mods, rows_per_batch // tm),
            pl.BlockSpec((tm, GLA_V), lambda i: (i, 0)),
            pl.BlockSpec((tm, GLA_V), lambda i: (i, 0)),
            pl.BlockSpec((tm, GLA_V), lambda i: (i, 0)),
            pl.BlockSpec((tm, o_glb.shape[1]), lambda i: (i, 0)),
            pl.BlockSpec((tm, o_win.shape[1]), lambda i: (i, 0)),
            _const_spec(gla_gain.shape),
            _const_spec(bd.shape),
            _const_spec(w_out.shape),
            _const_spec((1, d)),
            _const_spec((d, 2 * f)),
            _const_spec((f, d)),
        ],
        out_specs=pl.BlockSpec((tm, d), lambda i: (i, 0)),
        compiler_params=_cparams("parallel"),
        name="mix_outproj_ffn",
    )(x, mods, o_gla_f, o_gla_b, r, o_glb, o_win, gla_gain, bd, w_out, gain2.reshape(1, d), w2_in, w2_out)


def _rope_tables(seq):
    rows = seq // GRID_W
    row = jnp.repeat(jnp.arange(rows, dtype=F32), GRID_W)
    col = (jnp.arange(rows * GRID_W) % GRID_W).astype(F32)
    n_freq = HEAD_DIM // 4
    inv = jnp.power(ROPE_BASE, -jnp.arange(n_freq, dtype=F32) / n_freq)
    ang = jnp.concatenate([row[:, None] * inv, col[:, None] * inv], axis=-1)
    cos, sin = jnp.cos(ang), jnp.sin(ang)
    cos_t = jnp.concatenate([cos, cos] * (LANES // HEAD_DIM), axis=-1)
    sin_t = jnp.concatenate([-sin, sin] * (LANES // HEAD_DIM), axis=-1)
    return cos_t, sin_t


def _block_diag_ones(n, block):
    idx = np.arange(n) // block
    return jnp.asarray(idx[:, None] == idx[None, :], dtype=BF16)


def kernel(x, c, ctx, c_ctx, mod_w, mod_b, norm_ffn1, ffn1_w_in, ffn1_w_out, norm_mix, mix_w_in, mix_w_out,
           gla_wg_f, gla_bg_f, gla_wg_b, gla_bg_b, gla_out_norm, glb_q_norm, glb_k_norm,
           win_q_norm, win_k_norm, win_sink, norm_ffn2, ffn2_w_in, ffn2_w_out):
    bsz, seq, d = x.shape
    ctx_len = ctx.shape[1]
    depth = mod_w.shape[0]
    in_splits = (GLA_QK, GLA_QK, GLA_V, GLA_V, 2 * GLA_GATE_RANK,
                 GLB_HEADS * HEAD_DIM, GLB_KV_HEADS * HEAD_DIM, GLB_KV_HEADS * HEAD_DIM,
                 WIN_HEADS * HEAD_DIM, WIN_KV_HEADS * HEAD_DIM, WIN_KV_HEADS * HEAD_DIM)

    n_rows = -(-(bsz + 1) // 8) * 8
    c_rows = jnp.concatenate([c, c_ctx[None, :], jnp.zeros((n_rows - bsz - 1, d), F32)], axis=0)
    mods = _modvec(c_rows, mod_w, mod_b)

    cos_l, sin_l = _rope_tables(seq)
    cos_c = jnp.ones((ctx_len, LANES), F32)
    sin_c = jnp.zeros((ctx_len, LANES), F32)
    bd = _block_diag_ones(SLAB, HEAD_DIM)
    zero_state = jnp.zeros((bsz, 2, GLA_V, GLA_QK), F32)

    xl = x.reshape(bsz * seq, d)
    xc = ctx.reshape(bsz * ctx_len, d)

    for l in range(depth):
        need_ctx = l < depth - 1
        mods_l = mods[l, :bsz].reshape(bsz, N_MOD, d)
        mods_c = mods[l, bsz:bsz + 1].reshape(1, N_MOD, d)
        w1_in, w1_out = ffn1_w_in[l].astype(BF16), ffn1_w_out[l].astype(BF16)
        w2_in, w2_out = ffn2_w_in[l].astype(BF16), ffn2_w_out[l].astype(BF16)
        wm = mix_w_in[l]
        src = dict(zip(("aq", "ak", "av", "ar", "ad", "gq", "gk", "gv", "wq", "wk", "wv"),
                       jnp.split(wm, np.cumsum(in_splits)[:-1], axis=1)))
        w_mix = jnp.concatenate([src[n] for n in ("aq", "ak", "av", "ar", "gq", "gk", "wk", "gv", "wv", "wq", "ad")]
                                + [jnp.zeros((d, GATE_PAD - 2 * GLA_GATE_RANK), F32)], axis=1).astype(BF16)
        assert w_mix.shape[1] == _C_END
        wg = jnp.zeros((GATE_PAD, 2 * GLA_QK), F32)
        wg = wg.at[:GLA_GATE_RANK, :GLA_QK].set(gla_wg_f[l])
        wg = wg.at[GLA_GATE_RANK:2 * GLA_GATE_RANK, GLA_QK:].set(gla_wg_b[l]).astype(BF16)
        bg = jnp.concatenate([gla_bg_f[l], gla_bg_b[l]])[None, :]
        qk_gains = jnp.stack([jnp.tile(gn[l], LANES // HEAD_DIM)
                              for gn in (glb_q_norm, glb_k_norm, win_q_norm, win_k_norm)])
        gla_gain = jnp.tile(gla_out_norm[l], GLA_HEADS)[None, :]
        w_out = mix_w_out[l].astype(BF16)
        sink = win_sink[l]

        xl = _ffn(xl, mods_l, 0, norm_ffn1[l], w1_in, w1_out, seq)
        xc = _ffn(xc, mods_c, 0, norm_ffn1[l], w1_in, w1_out, bsz * ctx_len)

        pc = _inproj(xc, mods_c, norm_mix[l], w_mix, wg, bg, qk_gains, cos_c, sin_c, bd, ctx_len)
        pq = _inproj(xl, mods_l, norm_mix[l], w_mix, wg, bg, qk_gains, cos_l, sin_l, bd, seq)
        aq, ak, av, ar, ag, gq, gk, wq, wk, gvt, wvt = pq
        aqc, akc, avc, arc, agc, gqc, gkc, wqc, wkc, gvtc, wvtc = pc

        oc_gla_f, oc_gla_b, states = _gla(aqc, akc, avc, agc, zero_state, ctx_len)
        o_gla_f, o_gla_b, _ = _gla(aq, ak, av, ag, states, seq)
        o_glb = _glb_attn(gq, gk, gvt, gkc, gvtc, seq, ctx_len)
        o_win = _win_attn(wq, wk, wvt, wkc, wvtc, sink, seq, ctx_len)
        xl = _outproj_ffn(xl, mods_l, o_gla_f, o_gla_b, ar, o_glb, o_win, gla_gain, bd, w_out, norm_ffn2[l],
                          w2_in, w2_out, seq)

        if need_ctx:
            oc_glb = _ctx_attn(gqc, gkc, gvtc, sink, GLB_KV_HEADS, ctx_len, use_sink=False)
            oc_win = _ctx_attn(wqc, wkc, wvtc, sink, WIN_KV_HEADS, ctx_len, use_sink=True)
            xc = _outproj_ffn(xc, mods_c, oc_gla_f, oc_gla_b, arc, oc_glb, oc_win, gla_gain, bd, w_out, norm_ffn2[l],
                              w2_in, w2_out, bsz * ctx_len)

    return xl.reshape(bsz, seq, d)
```

```python
import functools

import numpy as np
import jax
import jax.numpy as jnp
from jax import lax
from jax.experimental import pallas as pl
from jax.experimental.pallas import tpu as pltpu

GRID_W = 64
HEAD_DIM = 64
GLA_HEADS = 4
GLA_DK = 32
GLA_DV = 64
GLA_GATE_RANK = 16
GLA_GATE_TAU = 16.0
GLA_CHUNK = 64
GLB_HEADS = 8
GLB_KV_HEADS = 2
WIN_HEADS = 4
WIN_KV_HEADS = 2
WINDOW = 128
ROPE_BASE = 10000.0
N_MOD = 9
EPS = 1e-6

LANES = 128
VMEM_LIMIT_BYTES = 56 * 1024 * 1024

GLA_QK = GLA_HEADS * GLA_DK
GLA_V = GLA_HEADS * GLA_DV
GATE_PAD = LANES
MASK_VALUE = -1e30
LOG2_E = 1.4426950408889634
KEY_CHUNK = 256
ROW_TILE = 512
FFN_HIDDEN_CHUNK = 1536
GLA_ROW_TILE = 2048
GLB_Q_TILE = 4096
GLB_COL_BLOCK = 256
WIN_Q_BLOCK = 256
WIN_BLOCKS_PER_STEP = 4
GLA_GROUP = 4
WIN_CHUNK = 128
VT_ROWS = HEAD_DIM + 16

BF16 = jnp.bfloat16
F32 = jnp.float32


def _cparams(*sem):
    return pltpu.CompilerParams(dimension_semantics=sem, vmem_limit_bytes=VMEM_LIMIT_BYTES)


def _dot(a, b):
    return jnp.dot(a, b, preferred_element_type=F32)


def _dot_nt(a, b):
    return lax.dot_general(a, b, (((1,), (1,)), ((), ())), preferred_element_type=F32)


def _dot_tn(a, b):
    return lax.dot_general(a, b, (((0,), (0,)), ((), ())), preferred_element_type=F32)


def _lane_iota(shape):
    return lax.broadcasted_iota(jnp.int32, shape, len(shape) - 1)


def _modvec_kernel(c_ref, w_ref, b_ref, o_ref):
    c = c_ref[...]
    s = (c * jax.nn.sigmoid(c)).astype(BF16)
    o_ref[0] = _dot(s, w_ref[0].astype(BF16)) + b_ref[0]


def _modvec(c_rows, mod_w, mod_b):
    depth, d, n = mod_w.shape
    rows = c_rows.shape[0]
    tn = n // N_MOD
    return pl.pallas_call(
        _modvec_kernel,
        out_shape=jax.ShapeDtypeStruct((depth, rows, n), F32),
        grid=(depth, n // tn),
        in_specs=[
            pl.BlockSpec((rows, d), lambda l, j: (0, 0)),
            pl.BlockSpec((1, d, tn), lambda l, j: (l, 0, j)),
            pl.BlockSpec((1, 1, tn), lambda l, j: (l, 0, j)),
        ],
        out_specs=pl.BlockSpec((1, rows, tn), lambda l, j: (l, 0, j)),
        compiler_params=_cparams("arbitrary", "arbitrary"),
        name="modvec",
    )(c_rows, mod_w, mod_b.reshape(depth, 1, n))


def _norm_modulate(x, gain, mod_ref, k_shift):
    shift = mod_ref[0, k_shift:k_shift + 1, :]
    scale = mod_ref[0, k_shift + 1:k_shift + 2, :]
    y = x * lax.rsqrt(jnp.mean(x * x, axis=-1, keepdims=True) + EPS)
    return (y * gain) * (1.0 + scale) + shift


def _mod_spec(mods, tiles_per_batch):
    nb, nm, d = mods.shape
    if nb == 1:
        return pl.BlockSpec((1, nm, d), lambda i: (0, 0, 0))
    return pl.BlockSpec((1, nm, d), lambda i: (i // tiles_per_batch, 0, 0))


def _const_spec(shape):
    nd = len(shape)
    return pl.BlockSpec(shape, lambda *_: (0,) * nd, pipeline_mode=pl.Buffered(1))


def _row_tile(n_rows_per_batch, want):
    t = min(want, n_rows_per_batch)
    assert n_rows_per_batch % t == 0
    return t


def _ffn_chunks(f):
    step = min(FFN_HIDDEN_CHUNK, f)
    return tuple((lo, min(lo + step, f)) for lo in range(0, f, step))


def _ffn_kernel(x_ref, mod_ref, g_ref, win_ref, wout_ref, o_ref, *, k0, f):
    o_ref[...] = _ffn_body(x_ref[...], mod_ref, g_ref, win_ref, wout_ref, k0, f)


def _ffn_body(x, mod_ref, g_ref, win_ref, wout_ref, k0, f):
    halves = 2 if x.shape[0] % 16 == 0 else 1
    rows = x.shape[0] // halves
    return jnp.concatenate([_ffn_rows(x[h * rows:(h + 1) * rows], mod_ref, g_ref, win_ref, wout_ref, k0, f)
                            for h in range(halves)], axis=0)


def _ffn_rows(x, mod_ref, g_ref, win_ref, wout_ref, k0, f):
    hb = _norm_modulate(x, g_ref[...], mod_ref, k0).astype(BF16)
    gate = mod_ref[0, k0 + 2:k0 + 3, :]
    acc = None
    for lo, hi in _ffn_chunks(f):
        a = _dot(hb, win_ref[:, lo:hi])
        b = _dot(hb, win_ref[:, f + lo:f + hi])
        u = ((a * jax.nn.sigmoid(a)) * b).astype(BF16)
        part = _dot(u, wout_ref[lo:hi, :])
        acc = part if acc is None else acc + part
    return x + (0.5 * gate) * acc


def _ffn(x, mods, k0, gain, w_in, w_out, rows_per_batch, tm_want=ROW_TILE):
    t, d = x.shape
    f = w_out.shape[0]
    tm = _row_tile(rows_per_batch, tm_want)
    return pl.pallas_call(
        functools.partial(_ffn_kernel, k0=k0, f=f),
        out_shape=jax.ShapeDtypeStruct((t, d), F32),
        grid=(t // tm,),
        in_specs=[
            pl.BlockSpec((tm, d), lambda i: (i, 0)),
            _mod_spec(mods, rows_per_batch // tm),
            _const_spec((1, d)),
            _const_spec((d, 2 * f)),
            _const_spec((f, d)),
        ],
        out_specs=pl.BlockSpec((tm, d), lambda i: (i, 0)),
        compiler_params=_cparams("parallel"),
        name="ffn",
    )(x, mods, gain.reshape(1, d), w_in, w_out)


_C_AQ = 0
_C_AK = _C_AQ + GLA_QK
_C_AV = _C_AK + GLA_QK
_C_AR = _C_AV + GLA_V
_C_GQ = _C_AR + GLA_V
_C_GK = _C_GQ + GLB_HEADS * HEAD_DIM
_C_WK = _C_GK + GLB_KV_HEADS * HEAD_DIM
_C_GV = _C_WK + WIN_KV_HEADS * HEAD_DIM
_C_WV = _C_GV + GLB_KV_HEADS * HEAD_DIM
_C_WQ = _C_WV + WIN_KV_HEADS * HEAD_DIM
_C_AD = _C_WQ + WIN_HEADS * HEAD_DIM
_C_END = _C_AD + GATE_PAD
SLAB = 2 * LANES


def _norm_rope_slab(x, bd_ref, gains, cos, sin_signed, out_scales):
    ss = _dot((x * x).astype(BF16), bd_ref[...])
    first_half = (_lane_iota((x.shape[0], LANES)) % HEAD_DIM) < (HEAD_DIM // 2)
    outs = []
    for j in range(2):
        sl = slice(j * LANES, (j + 1) * LANES)
        xn = x[:, sl] * lax.rsqrt(ss[:, sl] * (1.0 / HEAD_DIM) + EPS) * gains[j]
        partner = jnp.where(first_half, pltpu.roll(xn, LANES - HEAD_DIM // 2, 1), pltpu.roll(xn, HEAD_DIM // 2, 1))
        out = xn * cos + partner * sin_signed
        outs.append(out * out_scales[j] if out_scales[j] != 1.0 else out)
    return outs


def _dup_heads(x):
    swapped = pltpu.roll(x, HEAD_DIM, 1)
    low = _lane_iota(x.shape) < HEAD_DIM
    return jnp.where(low, x, swapped), jnp.where(low, swapped, x)


def _store_vt(vt_ref, v):
    chunk = vt_ref.shape[4]
    for c in range(vt_ref.shape[2]):
        vt = v[c * chunk:(c + 1) * chunk, :].T
        for kv in range(vt_ref.shape[1]):
            vt_ref[0, kv, c, 0:HEAD_DIM, :] = vt[kv * HEAD_DIM:(kv + 1) * HEAD_DIM].astype(BF16)
            vt_ref[0, kv, c, HEAD_DIM:VT_ROWS, :] = jnp.ones((VT_ROWS - HEAD_DIM, chunk), BF16)


def _inproj_kernel(x_ref, mod_ref, g_ref, w_ref, wg_ref, bg_ref, qkg_ref, cos_ref, sin_ref, bd_ref,
                   aq_ref, ak_ref, av_ref, ar_ref, ag_ref, gq_ref, gk_ref, wq_ref, wk_ref, gvt_ref, wvt_ref):
    hb = _norm_modulate(x_ref[...], g_ref[...], mod_ref, 3).astype(BF16)
    cos = cos_ref[...]
    sin = sin_ref[...]
    tiles = {}

    def proj(lo, width):
        j, off = divmod(lo, SLAB)
        assert off + width <= SLAB
        if j not in tiles:
            tiles[j] = _dot(hb, w_ref[:, j * SLAB:min((j + 1) * SLAB, _C_END)])
        return tiles[j][:, off:off + width]

    q_scale = HEAD_DIM ** -0.5
    g_glb_q, g_glb_k, g_win_q, g_win_k = (qkg_ref[r:r + 1, :] for r in range(4))

    order = (_C_GQ, _C_GQ + SLAB, _C_WQ, _C_GK, _C_GV, _C_AD, _C_AQ, _C_AV, _C_AR)

    def issue_ahead(n):
        for lo in order[:n + 3]:
            proj(lo, LANES)

    for j in range(GLB_HEADS * HEAD_DIM // SLAB):
        issue_ahead(j)
        halves = _norm_rope_slab(proj(_C_GQ + j * SLAB, SLAB), bd_ref, (g_glb_q, g_glb_q), cos, sin,
                                 (q_scale, q_scale))
        for i, q in enumerate(halves):
            gq_ref[:, j * SLAB + i * LANES:j * SLAB + (i + 1) * LANES] = q.astype(BF16)
    issue_ahead(2)
    halves = _norm_rope_slab(proj(_C_WQ, SLAB), bd_ref, (g_win_q, g_win_q), cos, sin, (q_scale, q_scale))
    for i, q in enumerate(halves):
        wq_ref[:, i * LANES:(i + 1) * LANES] = q.astype(BF16)
    issue_ahead(3)
    k_glb, k_win = _norm_rope_slab(proj(_C_GK, SLAB), bd_ref, (g_glb_k, g_win_k), cos, sin, (1.0, 1.0))
    k0, k1 = _dup_heads(k_glb)
    gk_ref[:, 0:LANES] = k0.astype(BF16)
    gk_ref[:, LANES:2 * LANES] = k1.astype(BF16)
    k0, k1 = _dup_heads(k_win)
    wk_ref[:, 0:LANES] = k0.astype(BF16)
    wk_ref[:, LANES:2 * LANES] = k1.astype(BF16)

    issue_ahead(4)
    _store_vt(gvt_ref, proj(_C_GV, LANES))
    _store_vt(wvt_ref, proj(_C_WV, LANES))

    issue_ahead(len(order))
    z = _dot(proj(_C_AD, GATE_PAD).astype(BF16), wg_ref[...]) + bg_ref[...]
    log_sig = jnp.minimum(z, 0.0) - jnp.log1p(jnp.exp(-jnp.abs(z)))
    ag_ref[...] = log_sig * (1.0 / GLA_GATE_TAU)
    aq_ref[...] = (proj(_C_AQ, GLA_QK) * (GLA_DK ** -0.5)).astype(BF16)
    ak_ref[...] = proj(_C_AK, GLA_QK).astype(BF16)
    av_ref[...] = proj(_C_AV, GLA_V).astype(BF16)
    ar_ref[...] = proj(_C_AR, GLA_V).astype(BF16)


def _inproj(x, mods, gain, w, wg, bg, qk_gains, cos, sin, bd, rows_per_batch, tm_want=ROW_TILE):
    t, d = x.shape
    tm = _row_tile(rows_per_batch, tm_want)
    tpb = rows_per_batch // tm
    widths = (GLA_QK, GLA_QK, GLA_V, GLA_V, 2 * GLA_QK, GLB_HEADS * HEAD_DIM, 2 * LANES,
              WIN_HEADS * HEAD_DIM, 2 * LANES)
    dtypes = (BF16, BF16, BF16, BF16, F32, BF16, BF16, BF16, BF16)
    nb = t // rows_per_batch
    vt_shapes, vt_specs = [], []
    for kv_heads, want in ((GLB_KV_HEADS, KEY_CHUNK), (WIN_KV_HEADS, WIN_CHUNK)):
        chunk = min(want, tm)
        vt_shapes.append(jax.ShapeDtypeStruct((nb, kv_heads, rows_per_batch // chunk, VT_ROWS, chunk), BF16))
        vt_specs.append(pl.BlockSpec((1, kv_heads, tm // chunk, VT_ROWS, chunk),
                                     lambda i: (i // tpb, 0, i % tpb, 0, 0)))
    return pl.pallas_call(
        _inproj_kernel,
        out_shape=tuple(jax.ShapeDtypeStruct((t, wd), dt) for wd, dt in zip(widths, dtypes)) + tuple(vt_shapes),
        grid=(t // tm,),
        in_specs=[
            pl.BlockSpec((tm, d), lambda i: (i, 0)),
            _mod_spec(mods, tpb),
            _const_spec((1, d)),
            _const_spec(w.shape),
            _const_spec(wg.shape),
            _const_spec(bg.shape),
            _const_spec(qk_gains.shape),
            pl.BlockSpec((tm, LANES), lambda i: (i % tpb, 0)),
            pl.BlockSpec((tm, LANES), lambda i: (i % tpb, 0)),
            _const_spec(bd.shape),
        ],
        out_specs=tuple(pl.BlockSpec((tm, wd), lambda i: (i, 0)) for wd in widths) + tuple(vt_specs),
        compiler_params=_cparams("parallel"),
        name="mix_inproj",
    )(x, mods, gain.reshape(1, d), w, wg, bg, qk_gains, cos, sin, bd)


def _gla_kernel(qf_ref, kf_ref, vf_ref, gf_ref, qb_ref, kb_ref, vb_ref, gb_ref, s0_ref,
                of_ref, ob_ref, sfin_ref, st_ref, *, n_chunks):
    c_len = GLA_CHUNK
    i = pl.program_id(1)
    dir_refs = ((qf_ref, kf_ref, vf_ref, gf_ref, of_ref), (qb_ref, kb_ref, vb_ref, gb_ref, ob_ref))
    signs = (1, -1)

    @pl.when(i == 0)
    def _():
        st_ref[...] = s0_ref[0]

    group = min(GLA_GROUP, n_chunks)
    g_len = group * c_len
    row = lax.broadcasted_iota(jnp.int32, (g_len, g_len), 0)
    col = lax.broadcasted_iota(jnp.int32, (g_len, g_len), 1)
    same_chunk = row // c_len == col // c_len
    cum_ops = [jnp.where(((row - col) * sg >= 0) & same_chunk, 1.0, 0.0).astype(BF16) for sg in signs]
    row4 = lax.broadcasted_iota(jnp.int32, (c_len, GLA_HEADS * c_len), 0)
    col4 = lax.broadcasted_iota(jnp.int32, (c_len, GLA_HEADS * c_len), 1) % c_len
    keep4s = [(row4 - col4) * sg >= 0 for sg in signs]
    qk_lane_head = _lane_iota((1, GLA_QK)) // GLA_DK
    v_lane_head = _lane_iota((1, GLA_V)) // GLA_DV
    qk_head_mask = [(qk_lane_head == h).astype(F32) for h in range(GLA_HEADS)]
    v_head_mask = [(v_lane_head == h).astype(BF16) for h in range(GLA_HEADS)]
    st_row_head = lax.broadcasted_iota(jnp.int32, (GLA_V, GLA_QK), 0) // GLA_DV
    st_col_head = lax.broadcasted_iota(jnp.int32, (GLA_V, GLA_QK), 1) // GLA_DK
    st_mask = st_row_head == st_col_head

    n_groups = n_chunks // group

    def group_starts(d, j):
        starts = []
        for p in range(group):
            c = j * group + p
            if d == 1:
                c = n_chunks - 1 - c
            starts.append(pl.multiple_of(c * c_len, c_len))
        return starts

    def decay_stage(d, j):
        q_ref, k_ref, _, g_ref, _ = dir_refs[d]
        starts = group_starts(d, j)
        q = jnp.concatenate([q_ref[pl.ds(r, c_len), :] for r in starts], axis=0).astype(F32)
        k = jnp.concatenate([k_ref[pl.ds(r, c_len), :] for r in starts], axis=0).astype(F32)
        g = jnp.concatenate([g_ref[pl.ds(r, c_len), :] for r in starts], axis=0)
        g_hi = g.astype(BF16)
        r1 = g - g_hi.astype(F32)
        g_mid = r1.astype(BF16)
        g_lo = (r1 - g_mid.astype(F32)).astype(BF16)
        b = _dot(cum_ops[d], g_hi) + _dot(cum_ops[d], g_mid) + _dot(cum_ops[d], g_lo)
        q_in = (q * jnp.exp(b)).astype(BF16)
        k_out = k * jnp.exp(-b)
        decays = jnp.concatenate([jnp.exp(jnp.sum(g[p * c_len:(p + 1) * c_len], axis=0, keepdims=True))
                                  for p in range(group)], axis=0)
        return q_in, k_out, decays

    def chunk_group(j, staged):
        j_next = jnp.minimum(j + 1, n_groups - 1)
        staged_next = tuple(decay_stage(d, j_next) for d in range(2))
        starts = [group_starts(d, j) for d in range(2)]
        vs = [[dir_refs[d][2][pl.ds(r, c_len), :] for r in starts[d]] for d in range(2)]
        sts = [st_ref[0], st_ref[1]]
        for p in range(group):
            rows = slice(p * c_len, (p + 1) * c_len)
            for d in range(2):
                q_in, k_out, decays = staged[d]
                v_p = vs[d][p]
                decay = decays[p:p + 1, :]
                k_out_p = k_out[rows]
                k_dec = (k_out_p * decay).astype(BF16)
                k_stack = jnp.concatenate([(k_out_p * qk_head_mask[h]).astype(BF16) for h in range(GLA_HEADS)],
                                          axis=0)
                a = _dot_nt(q_in[rows], k_stack)
                a = jnp.where(keep4s[d], a, 0.0).astype(BF16)
                v_bd = jnp.concatenate([v_p * v_head_mask[h] for h in range(GLA_HEADS)], axis=0)
                o = _dot(a, v_bd) + _dot_nt(q_in[rows], sts[d].astype(BF16))
                dir_refs[d][4][pl.ds(starts[d][p], c_len), :] = o
                ds_t = _dot_tn(v_p, k_dec)
                sts[d] = sts[d] * decay + jnp.where(st_mask, ds_t, 0.0)
        st_ref[0] = sts[0]
        st_ref[1] = sts[1]
        return staged_next

    lax.fori_loop(0, n_groups, chunk_group, tuple(decay_stage(d, 0) for d in range(2)))

    @pl.when(i == pl.num_programs(1) - 1)
    def _():
        sfin_ref[0] = st_ref[...]


def _gla(q, k, v, g, s0, rows_per_batch, tt_want=GLA_ROW_TILE):
    t = q.shape[0]
    nb = t // rows_per_batch
    tt = _row_tile(rows_per_batch, tt_want)
    nt = rows_per_batch // tt

    def fwd(lane_block):
        return lambda b, i: (b * nt + i, lane_block)

    def bwd(lane_block):
        return lambda b, i: (b * nt + nt - 1 - i, lane_block)

    state_spec = pl.BlockSpec((1, 2, GLA_V, GLA_QK), lambda b, i: (b, 0, 0, 0))
    return pl.pallas_call(
        functools.partial(_gla_kernel, n_chunks=tt // GLA_CHUNK),
        out_shape=(jax.ShapeDtypeStruct((t, GLA_V), F32), jax.ShapeDtypeStruct((t, GLA_V), F32),
                   jax.ShapeDtypeStruct((nb, 2, GLA_V, GLA_QK), F32)),
        grid=(nb, nt),
        in_specs=[
            pl.BlockSpec((tt, GLA_QK), fwd(0)), pl.BlockSpec((tt, GLA_QK), fwd(0)),
            pl.BlockSpec((tt, GLA_V), fwd(0)), pl.BlockSpec((tt, GLA_QK), fwd(0)),
            pl.BlockSpec((tt, GLA_QK), bwd(0)), pl.BlockSpec((tt, GLA_QK), bwd(0)),
            pl.BlockSpec((tt, GLA_V), bwd(0)), pl.BlockSpec((tt, GLA_QK), bwd(1)),
            state_spec,
        ],
        out_specs=(pl.BlockSpec((tt, GLA_V), fwd(0)), pl.BlockSpec((tt, GLA_V), bwd(0)), state_spec),
        scratch_shapes=[pltpu.VMEM((2, GLA_V, GLA_QK), F32)],
        compiler_params=_cparams("parallel", "arbitrary"),
        name="gla_scan",
    )(q, k, v, g, q, k, v, g, s0)


def _glb_kernel(q_ref, k_ref, vt_ref, kc_ref, vtc_ref, o_ref, qt_ref, m_ref, acc_ref, s_ref, *,
                n_pairs, col_block, chunks_per_tile):
    tq = q_ref.shape[0]
    n_lat = vt_ref.shape[2] // chunks_per_tile
    tk = vt_ref.shape[4] * chunks_per_tile
    n_heads = 2 * n_pairs
    for p in range(n_pairs):
        qt = (q_ref[:, p * LANES:(p + 1) * LANES].astype(F32) * LOG2_E).T
        qt_ref[2 * p] = qt[0:HEAD_DIM].astype(BF16)
        qt_ref[2 * p + 1] = qt[HEAD_DIM:2 * HEAD_DIM].astype(BF16)

    m_ref[...] = jnp.full(m_ref.shape, MASK_VALUE, F32)
    acc_ref[...] = jnp.zeros(acc_ref.shape, F32)

    units = [(h, cb) for h in range(n_heads) for cb in range(tq // col_block)]

    def scores(k, u):
        h, cb = units[u]
        return _dot(k[:, 0:HEAD_DIM], qt_ref[h, :, cb * col_block:(cb + 1) * col_block])

    def step(k, vt, k_next):
        w = k.shape[0]
        for u, (h, cb) in enumerate(units):
            s = s_ref[u, 0:w, :]
            if k_next is not None:
                s_ref[u, 0:k_next.shape[0], :] = scores(k_next, u)
            cols = slice(cb * col_block, (cb + 1) * col_block)
            m_prev = m_ref[h, :, cols]
            m_new = jnp.maximum(m_prev, jnp.max(s, axis=0, keepdims=True))
            alpha = jnp.exp2(m_prev - m_new)
            e = jnp.exp2((s - m_new).astype(BF16))
            m_ref[h, :, cols] = m_new
            ch = w // len(vt)
            pv = _dot(vt[0], e[0:ch, :])
            for c in range(1, len(vt)):
                pv = pv + _dot(vt[c], e[c * ch:(c + 1) * ch, :])
            acc_ref[h, :, cols] = acc_ref[h, :, cols] * alpha + pv

    def lat_tile(j):
        return k_ref[pl.ds(pl.multiple_of(j * tk, tk), tk), :]

    def lat_vt(j):
        return [vt_ref[0, 0, j * chunks_per_tile + c] for c in range(chunks_per_tile)]

    wc = vtc_ref.shape[4]
    ctx_tiles = [kc_ref[j * wc:(j + 1) * wc, :] for j in range(vtc_ref.shape[2])]

    for u in range(len(units)):
        s_ref[u, 0:tk, :] = scores(lat_tile(0), u)

    def body(j, carry):
        step(lat_tile(j), lat_vt(j), lat_tile(j + 1))
        return carry

    lax.fori_loop(0, n_lat - 1, body, 0)
    step(lat_tile(n_lat - 1), lat_vt(n_lat - 1), ctx_tiles[0])
    for j, kc in enumerate(ctx_tiles):
        step(kc, [vtc_ref[0, 0, j]], ctx_tiles[j + 1] if j + 1 < len(ctx_tiles) else None)

    for p in range(n_pairs):
        halves = [acc_ref[h, 0:HEAD_DIM, :] / acc_ref[h, HEAD_DIM:HEAD_DIM + 1, :] for h in (2 * p, 2 * p + 1)]
        o_ref[:, p * LANES:(p + 1) * LANES] = jnp.concatenate(halves, axis=0).T.astype(o_ref.dtype)


def _glb_attn(q, k, vt, kc, vtc, seq, ctx_len, tq_want=GLB_Q_TILE, col_block=GLB_COL_BLOCK, tk_want=KEY_CHUNK):
    t = q.shape[0]
    nb = t // seq
    groups = GLB_KV_HEADS
    gw = q.shape[1] // groups
    n_pairs = gw // LANES
    tq = _row_tile(seq, tq_want)
    nq = seq // tq
    col_block = min(col_block, tq)
    n_units = 2 * n_pairs * (tq // col_block)
    chunks_per_tile = max(1, min(tk_want, seq) // vt.shape[4])
    assert vt.shape[2] % chunks_per_tile == 0
    max_keys = max(vt.shape[4] * chunks_per_tile, vtc.shape[4])
    return pl.pallas_call(
        functools.partial(_glb_kernel, n_pairs=n_pairs, col_block=col_block, chunks_per_tile=chunks_per_tile),
        out_shape=jax.ShapeDtypeStruct(q.shape, BF16),
        grid=(nb, groups, nq),
        in_specs=[
            pl.BlockSpec((tq, gw), lambda b, g, i: (b * nq + i, g)),
            pl.BlockSpec((seq, LANES), lambda b, g, i: (b, g)),
            pl.BlockSpec((1, 1) + vt.shape[2:], lambda b, g, i: (b, g, 0, 0, 0)),
            pl.BlockSpec((ctx_len, LANES), lambda b, g, i: (b, g)),
            pl.BlockSpec((1, 1) + vtc.shape[2:], lambda b, g, i: (b, g, 0, 0, 0)),
        ],
        out_specs=pl.BlockSpec((tq, gw), lambda b, g, i: (b * nq + i, g)),
        scratch_shapes=[pltpu.VMEM((2 * n_pairs, HEAD_DIM, tq), BF16),
                        pltpu.VMEM((2 * n_pairs, 1, tq), F32),
                        pltpu.VMEM((2 * n_pairs, VT_ROWS, tq), F32),
                        pltpu.VMEM((n_units, max_keys, col_block), F32)],
        compiler_params=_cparams("parallel", "parallel", "arbitrary"),
        name="glb_attn",
    )(q, k, vt, kc, vtc)


def _win_kernel(sink_ref, q_ref, k_ref, vt_ref, kc_ref, vtc_ref, o_ref, sloc_ref, sctx_ref, *,
                tq, span, blocks_per_step):
    g = pl.program_id(1)
    seq = k_ref.shape[0]
    n_blocks = seq // tq
    n_steps = n_blocks // blocks_per_step
    wchunk = vt_ref.shape[4]
    kc = kc_ref[...][:, 0:HEAD_DIM]
    vt_ctx = jnp.concatenate([vtc_ref[0, 0, c] for c in range(vtc_ref.shape[2])], axis=1)
    col_minus_row = (lax.broadcasted_iota(jnp.int32, (span, tq), 1)
                     - lax.broadcasted_iota(jnp.int32, (span, tq), 0))

    def band_start(qb):
        return pl.multiple_of(jnp.clip(qb * tq - WINDOW, 0, seq - span), wchunk)

    def block_scores(qb):
        k_loc = k_ref[pl.ds(band_start(qb), span), :][:, 0:HEAD_DIM]
        qt = (q_ref[pl.ds(pl.multiple_of(qb * tq, tq), tq), :].astype(F32) * LOG2_E).T
        for half in range(2):
            qt_h = qt[half * HEAD_DIM:(half + 1) * HEAD_DIM].astype(BF16)
            yield _dot(k_loc, qt_h), _dot(kc, qt_h)

    for j in range(blocks_per_step):
        for half, (s_loc, s_ctx) in enumerate(block_scores(j)):
            sloc_ref[2 * j + half] = s_loc
            sctx_ref[2 * j + half] = s_ctx

    def step(i, carry):
        for j in range(blocks_per_step):
            qb = i * blocks_per_step + j
            q0 = pl.multiple_of(qb * tq, tq)
            start = band_start(qb)
            c0 = start // wchunk
            bias = jnp.where(jnp.abs(col_minus_row + (q0 - start)) <= WINDOW, 0.0, MASK_VALUE)
            vt_loc = jnp.concatenate([vt_ref[0, 0, c0 + c] for c in range(span // wchunk)], axis=1)
            ahead = block_scores(jnp.minimum(qb + blocks_per_step, n_blocks - 1))
            halves = []
            for half in range(2):
                u = 2 * j + half
                sink2 = sink_ref[2 * g + half] * LOG2_E
                s_loc = sloc_ref[u] + bias
                s_ctx = sctx_ref[u]
                sloc_ref[u], sctx_ref[u] = next(ahead)
                m = jnp.maximum(jnp.max(s_loc, axis=0, keepdims=True), jnp.max(s_ctx, axis=0, keepdims=True))
                m = jnp.maximum(m, sink2)
                e_loc = jnp.exp2((s_loc - m).astype(BF16))
                e_ctx = jnp.exp2((s_ctx - m).astype(BF16))
                acc = _dot(vt_loc, e_loc) + _dot(vt_ctx, e_ctx)
                den = acc[HEAD_DIM:HEAD_DIM + 1, :] + jnp.exp2(sink2 - m)
                halves.append(acc[0:HEAD_DIM, :] / den)
            o_ref[pl.ds(q0, tq), :] = jnp.concatenate(halves, axis=0).T.astype(o_ref.dtype)
        return carry

    lax.fori_loop(0, n_steps, step, 0)


def _win_attn(q, k, vt, kc, vtc, sink, seq, ctx_len, tq_want=WIN_Q_BLOCK, blocks_per_step=WIN_BLOCKS_PER_STEP):
    t = q.shape[0]
    nb = t // seq
    groups = WIN_KV_HEADS
    assert q.shape[1] == groups * LANES
    tq = _row_tile(seq, min(tq_want, max(seq - 2 * WINDOW, WINDOW)))
    span = tq + 2 * WINDOW
    assert seq >= span and span % vt.shape[4] == 0 and WINDOW % vt.shape[4] == 0
    blocks_per_step = min(blocks_per_step, seq // tq)
    assert (seq // tq) % blocks_per_step == 0
    n_units = 2 * blocks_per_step
    return pl.pallas_call(
        functools.partial(_win_kernel, tq=tq, span=span, blocks_per_step=blocks_per_step),
        out_shape=jax.ShapeDtypeStruct(q.shape, BF16),
        grid=(nb, groups),
        in_specs=[
            pl.BlockSpec(memory_space=pltpu.SMEM),
            pl.BlockSpec((seq, LANES), lambda b, g: (b, g)),
            pl.BlockSpec((seq, LANES), lambda b, g: (b, g)),
            pl.BlockSpec((1, 1) + vt.shape[2:], lambda b, g: (b, g, 0, 0, 0)),
            pl.BlockSpec((ctx_len, LANES), lambda b, g: (b, g)),
            pl.BlockSpec((1, 1) + vtc.shape[2:], lambda b, g: (b, g, 0, 0, 0)),
        ],
        out_specs=pl.BlockSpec((seq, LANES), lambda b, g: (b, g)),
        scratch_shapes=[pltpu.VMEM((n_units, span, tq), F32), pltpu.VMEM((n_units, ctx_len, tq), F32)],
        compiler_params=_cparams("parallel", "parallel"),
        name="win_attn",
    )(sink, q, k, vt, kc, vtc)


def _ctx_attn_kernel(sink_ref, q_ref, k_ref, vt_ref, o_ref, *, n_pairs, use_sink):
    g = pl.program_id(1)
    k = k_ref[...][:, 0:HEAD_DIM]
    vt = jnp.concatenate([vt_ref[0, 0, c] for c in range(vt_ref.shape[2])], axis=1)
    for p in range(n_pairs):
        qt = (q_ref[:, p * LANES:(p + 1) * LANES].astype(F32) * LOG2_E).T
        halves = []
        for half in range(2):
            s = _dot(k, qt[half * HEAD_DIM:(half + 1) * HEAD_DIM].astype(BF16))
            m = jnp.max(s, axis=0, keepdims=True)
            if use_sink:
                sink2 = sink_ref[(g * n_pairs + p) * 2 + half] * LOG2_E
                m = jnp.maximum(m, sink2)
            acc = _dot(vt, jnp.exp2((s - m).astype(BF16)))
            den = acc[HEAD_DIM:HEAD_DIM + 1, :]
            if use_sink:
                den = den + jnp.exp2(sink2 - m)
            halves.append(acc[0:HEAD_DIM, :] / den)
        o_ref[:, p * LANES:(p + 1) * LANES] = jnp.concatenate(halves, axis=0).T.astype(o_ref.dtype)


def _ctx_attn(q, k, vt, sink, groups, ctx_len, use_sink):
    t = q.shape[0]
    nb = t // ctx_len
    gw = q.shape[1] // groups
    n_pairs = gw // LANES
    return pl.pallas_call(
        functools.partial(_ctx_attn_kernel, n_pairs=n_pairs, use_sink=use_sink),
        out_shape=jax.ShapeDtypeStruct(q.shape, BF16),
        grid=(nb, groups),
        in_specs=[
            pl.BlockSpec(memory_space=pltpu.SMEM),
            pl.BlockSpec((ctx_len, gw), lambda b, g: (b, g)),
            pl.BlockSpec((ctx_len, LANES), lambda b, g: (b, g)),
            pl.BlockSpec((1, 1) + vt.shape[2:], lambda b, g: (b, g, 0, 0, 0)),
        ],
        out_specs=pl.BlockSpec((ctx_len, gw), lambda b, g: (b, g)),
        compiler_params=_cparams("parallel", "parallel"),
        name="ctx_attn",
    )(sink, q, k, vt)


def _outproj_ffn_kernel(x_ref, mod_ref, of_ref, ob_ref, r_ref, og_ref, ow_ref, gain_ref, bd_ref, w_ref,
                        g2_ref, win_ref, wout_ref, o_ref, *, f):
    o = of_ref[...] + ob_ref[...]
    r = r_ref[...].astype(F32)
    gate = r * jax.nn.sigmoid(r)
    ss = _dot((o * o).astype(BF16), bd_ref[...])
    on = o * lax.rsqrt(ss * (1.0 / GLA_DV) + EPS) * gain_ref[...]
    cat = jnp.concatenate([(on * gate).astype(BF16), og_ref[...], ow_ref[...]], axis=-1)
    x = x_ref[...] + mod_ref[0, 5:6, :] * _dot(cat, w_ref[...])
    o_ref[...] = _ffn_body(x, mod_ref, g2_ref, win_ref, wout_ref, 6, f)


def _outproj_ffn(x, mods, o_gla_f, o_gla_b, r, o_glb, o_win, gla_gain, bd, w_out, gain2, w2_in, w2_out,
                 rows_per_batch, tm_want=ROW_TILE):
    t, d = x.shape
    f = w2_out.shape[0]
    tm = _row_tile(rows_per_batch, tm_want)
    return pl.pallas_call(
        functools.partial(_outproj_ffn_kernel, f=f),
        out_shape=jax.ShapeDtypeStruct((t, d), F32),
        grid=(t // tm,),
        in_specs=[
            pl.BlockSpec((tm, d), lambda i: (i, 0)),
            _mod_spec(mods, rows_per_batch // tm),
            pl.BlockSpec((tm, GLA_V), lambda i: (i, 0)),
            pl.BlockSpec((tm, GLA_V), lambda i: (i, 0)),
            pl.BlockSpec((tm, GLA_V), lambda i: (i, 0)),
            pl.BlockSpec((tm, o_glb.shape[1]), lambda i: (i, 0)),
            pl.BlockSpec((tm, o_win.shape[1]), lambda i: (i, 0)),
            _const_spec(gla_gain.shape),
            _const_spec(bd.shape),
            _const_spec(w_out.shape),
            _const_spec((1, d)),
            _const_spec((d, 2 * f)),
            _const_spec((f, d)),
        ],
        out_specs=pl.BlockSpec((tm, d), lambda i: (i, 0)),
        compiler_params=_cparams("parallel"),
        name="mix_outproj_ffn",
    )(x, mods, o_gla_f, o_gla_b, r, o_glb, o_win, gla_gain, bd, w_out, gain2.reshape(1, d), w2_in, w2_out)


def _rope_tables(seq):
    rows = seq // GRID_W
    row = jnp.repeat(jnp.arange(rows, dtype=F32), GRID_W)
    col = (jnp.arange(rows * GRID_W) % GRID_W).astype(F32)
    n_freq = HEAD_DIM // 4
    inv = jnp.power(ROPE_BASE, -jnp.arange(n_freq, dtype=F32) / n_freq)
    ang = jnp.concatenate([row[:, None] * inv, col[:, None] * inv], axis=-1)
    cos, sin = jnp.cos(ang), jnp.sin(ang)
    cos_t = jnp.concatenate([cos, cos] * (LANES // HEAD_DIM), axis=-1)
    sin_t = jnp.concatenate([-sin, sin] * (LANES // HEAD_DIM), axis=-1)
    return cos_t, sin_t


def _block_diag_ones(n, block):
    idx = np.arange(n) // block
    return jnp.asarray(idx[:, None] == idx[None, :], dtype=BF16)


def kernel(x, c, ctx, c_ctx, mod_w, mod_b, norm_ffn1, ffn1_w_in, ffn1_w_out, norm_mix, mix_w_in, mix_w_out,
           gla_wg_f, gla_bg_f, gla_wg_b, gla_bg_b, gla_out_norm, glb_q_norm, glb_k_norm,
           win_q_norm, win_k_norm, win_sink, norm_ffn2, ffn2_w_in, ffn2_w_out):
    bsz, seq, d = x.shape
    ctx_len = ctx.shape[1]
    depth = mod_w.shape[0]
    in_splits = (GLA_QK, GLA_QK, GLA_V, GLA_V, 2 * GLA_GATE_RANK,
                 GLB_HEADS * HEAD_DIM, GLB_KV_HEADS * HEAD_DIM, GLB_KV_HEADS * HEAD_DIM,
                 WIN_HEADS * HEAD_DIM, WIN_KV_HEADS * HEAD_DIM, WIN_KV_HEADS * HEAD_DIM)

    n_rows = -(-(bsz + 1) // 8) * 8
    c_rows = jnp.concatenate([c, c_ctx[None, :], jnp.zeros((n_rows - bsz - 1, d), F32)], axis=0)
    mods = _modvec(c_rows, mod_w, mod_b)

    cos_l, sin_l = _rope_tables(seq)
    cos_c = jnp.ones((ctx_len, LANES), F32)
    sin_c = jnp.zeros((ctx_len, LANES), F32)
    bd = _block_diag_ones(SLAB, HEAD_DIM)
    zero_state = jnp.zeros((bsz, 2, GLA_V, GLA_QK), F32)

    xl = x.reshape(bsz * seq, d)
    xc = ctx.reshape(bsz * ctx_len, d)

    for l in range(depth):
        need_ctx = l < depth - 1
        mods_l = mods[l, :bsz].reshape(bsz, N_MOD, d)
        mods_c = mods[l, bsz:bsz + 1].reshape(1, N_MOD, d)
        w1_in, w1_out = ffn1_w_in[l].astype(BF16), ffn1_w_out[l].astype(BF16)
        w2_in, w2_out = ffn2_w_in[l].astype(BF16), ffn2_w_out[l].astype(BF16)
        wm = mix_w_in[l]
        src = dict(zip(("aq", "ak", "av", "ar", "ad", "gq", "gk", "gv", "wq", "wk", "wv"),
                       jnp.split(wm, np.cumsum(in_splits)[:-1], axis=1)))
        w_mix = jnp.concatenate([src[n] for n in ("aq", "ak", "av", "ar", "gq", "gk", "wk", "gv", "wv", "wq", "ad")]
                                + [jnp.zeros((d, GATE_PAD - 2 * GLA_GATE_RANK), F32)], axis=1).astype(BF16)
        assert w_mix.shape[1] == _C_END
        wg = jnp.zeros((GATE_PAD, 2 * GLA_QK), F32)
        wg = wg.at[:GLA_GATE_RANK, :GLA_QK].set(gla_wg_f[l])
        wg = wg.at[GLA_GATE_RANK:2 * GLA_GATE_RANK, GLA_QK:].set(gla_wg_b[l]).astype(BF16)
        bg = jnp.concatenate([gla_bg_f[l], gla_bg_b[l]])[None, :]
        qk_gains = jnp.stack([jnp.tile(gn[l], LANES // HEAD_DIM)
                              for gn in (glb_q_norm, glb_k_norm, win_q_norm, win_k_norm)])
        gla_gain = jnp.tile(gla_out_norm[l], GLA_HEADS)[None, :]
        w_out = mix_w_out[l].astype(BF16)
        sink = win_sink[l]

        xl = _ffn(xl, mods_l, 0, norm_ffn1[l], w1_in, w1_out, seq)
        xc = _ffn(xc, mods_c, 0, norm_ffn1[l], w1_in, w1_out, bsz * ctx_len)

        pc = _inproj(xc, mods_c, norm_mix[l], w_mix, wg, bg, qk_gains, cos_c, sin_c, bd, ctx_len)
        pq = _inproj(xl, mods_l, norm_mix[l], w_mix, wg, bg, qk_gains, cos_l, sin_l, bd, seq)
        aq, ak, av, ar, ag, gq, gk, wq, wk, gvt, wvt = pq
        aqc, akc, avc, arc, agc, gqc, gkc, wqc, wkc, gvtc, wvtc = pc

        oc_gla_f, oc_gla_b, states = _gla(aqc, akc, avc, agc, zero_state, ctx_len)
        o_gla_f, o_gla_b, _ = _gla(aq, ak, av, ag, states, seq)
        o_glb = _glb_attn(gq, gk, gvt, gkc, gvtc, seq, ctx_len)
        o_win = _win_attn(wq, wk, wvt, wkc, wvtc, sink, seq, ctx_len)
        xl = _outproj_ffn(xl, mods_l, o_gla_f, o_gla_b, ar, o_glb, o_win, gla_gain, bd, w_out, norm_ffn2[l],
                          w2_in, w2_out, seq)

        if need_ctx:
            oc_glb = _ctx_attn(gqc, gkc, gvtc, sink, GLB_KV_HEADS, ctx_len, use_sink=False)
            oc_win = _ctx_attn(wqc, wkc, wvtc, sink, WIN_KV_HEADS, ctx_len, use_sink=True)
            xc = _outproj_ffn(xc, mods_c, oc_gla_f, oc_gla_b, arc, oc_glb, oc_win, gla_gain, bd, w_out, norm_ffn2[l],
                              w2_in, w2_out, bsz * ctx_len)

    return xl.reshape(bsz, seq, d)
```

```python
import functools

import numpy as np
import jax
import jax.numpy as jnp
from jax import lax
from jax.experimental import pallas as pl
from jax.experimental.pallas import tpu as pltpu

GRID_W = 64
HEAD_DIM = 64
GLA_HEADS = 4
GLA_DK = 32
GLA_DV = 64
GLA_GATE_RANK = 16
GLA_GATE_TAU = 16.0
GLA_CHUNK = 64
GLB_HEADS = 8
GLB_KV_HEADS = 2
WIN_HEADS = 4
WIN_KV_HEADS = 2
WINDOW = 128
ROPE_BASE = 10000.0
N_MOD = 9
EPS = 1e-6

LANES = 128
VMEM_LIMIT_BYTES = 56 * 1024 * 1024

GLA_QK = GLA_HEADS * GLA_DK
GLA_V = GLA_HEADS * GLA_DV
GATE_PAD = LANES
MASK_VALUE = -1e30
LOG2_E = 1.4426950408889634
KEY_CHUNK = 256
ROW_TILE = 512
FFN_HIDDEN_CHUNK = 1536
GLA_ROW_TILE = 2048
GLB_Q_TILE = 4096
GLB_COL_BLOCK = 256
WIN_Q_BLOCK = 256
WIN_BLOCKS_PER_STEP = 4
GLA_GROUP = 4
WIN_CHUNK = 128
VT_ROWS = HEAD_DIM + 16

BF16 = jnp.bfloat16
F32 = jnp.float32


def _cparams(*sem):
    return pltpu.CompilerParams(dimension_semantics=sem, vmem_limit_bytes=VMEM_LIMIT_BYTES)


def _dot(a, b):
    return jnp.dot(a, b, preferred_element_type=F32)


def _dot_nt(a, b):
    return lax.dot_general(a, b, (((1,), (1,)), ((), ())), preferred_element_type=F32)


def _dot_tn(a, b):
    return lax.dot_general(a, b, (((0,), (0,)), ((), ())), preferred_element_type=F32)


def _lane_iota(shape):
    return lax.broadcasted_iota(jnp.int32, shape, len(shape) - 1)


def _modvec_kernel(c_ref, w_ref, b_ref, o_ref):
    c = c_ref[...]
    s = (c * jax.nn.sigmoid(c)).astype(BF16)
    o_ref[0] = _dot(s, w_ref[0].astype(BF16)) + b_ref[0]


def _modvec(c_rows, mod_w, mod_b):
    depth, d, n = mod_w.shape
    rows = c_rows.shape[0]
    tn = n // N_MOD
    return pl.pallas_call(
        _modvec_kernel,
        out_shape=jax.ShapeDtypeStruct((depth, rows, n), F32),
        grid=(depth, n // tn),
        in_specs=[
            pl.BlockSpec((rows, d), lambda l, j: (0, 0)),
            pl.BlockSpec((1, d, tn), lambda l, j: (l, 0, j)),
            pl.BlockSpec((1, 1, tn), lambda l, j: (l, 0, j)),
        ],
        out_specs=pl.BlockSpec((1, rows, tn), lambda l, j: (l, 0, j)),
        compiler_params=_cparams("arbitrary", "arbitrary"),
        name="modvec",
    )(c_rows, mod_w, mod_b.reshape(depth, 1, n))


def _norm_modulate(x, gain, mod_ref, k_shift):
    shift = mod_ref[0, k_shift:k_shift + 1, :]
    scale = mod_ref[0, k_shift + 1:k_shift + 2, :]
    y = x * lax.rsqrt(jnp.mean(x * x, axis=-1, keepdims=True) + EPS)
    return (y * gain) * (1.0 + scale) + shift


def _mod_spec(mods, tiles_per_batch):
    nb, nm, d = mods.shape
    if nb == 1:
        return pl.BlockSpec((1, nm, d), lambda i: (0, 0, 0))
    return pl.BlockSpec((1, nm, d), lambda i: (i // tiles_per_batch, 0, 0))


def _const_spec(shape):
    nd = len(shape)
    return pl.BlockSpec(shape, lambda *_: (0,) * nd, pipeline_mode=pl.Buffered(1))


def _row_tile(n_rows_per_batch, want):
    t = min(want, n_rows_per_batch)
    assert n_rows_per_batch % t == 0
    return t


def _ffn_chunks(f):
    step = min(FFN_HIDDEN_CHUNK, f)
    return tuple((lo, min(lo + step, f)) for lo in range(0, f, step))


def _ffn_kernel(x_ref, mod_ref, g_ref, win_ref, wout_ref, o_ref, *, k0, f):
    o_ref[...] = _ffn_body(x_ref[...], mod_ref, g_ref, win_ref, wout_ref, k0, f)


def _ffn_body(x, mod_ref, g_ref, win_ref, wout_ref, k0, f):
    halves = 2 if x.shape[0] % 16 == 0 else 1
    rows = x.shape[0] // halves
    return jnp.concatenate([_ffn_rows(x[h * rows:(h + 1) * rows], mod_ref, g_ref, win_ref, wout_ref, k0, f)
                            for h in range(halves)], axis=0)


def _ffn_rows(x, mod_ref, g_ref, win_ref, wout_ref, k0, f):
    hb = _norm_modulate(x, g_ref[...], mod_ref, k0).astype(BF16)
    gate = mod_ref[0, k0 + 2:k0 + 3, :]
    acc = None
    for lo, hi in _ffn_chunks(f):
        a = _dot(hb, win_ref[:, lo:hi])
        b = _dot(hb, win_ref[:, f + lo:f + hi])
        u = ((a * jax.nn.sigmoid(a)) * b).astype(BF16)
        part = _dot(u, wout_ref[lo:hi, :])
        acc = part if acc is None else acc + part
    return x + (0.5 * gate) * acc


def _ffn(x, mods, k0, gain, w_in, w_out, rows_per_batch, tm_want=ROW_TILE):
    t, d = x.shape
    f = w_out.shape[0]
    tm = _row_tile(rows_per_batch, tm_want)
    return pl.pallas_call(
        functools.partial(_ffn_kernel, k0=k0, f=f),
        out_shape=jax.ShapeDtypeStruct((t, d), F32),
        grid=(t // tm,),
        in_specs=[
            pl.BlockSpec((tm, d), lambda i: (i, 0)),
            _mod_spec(mods, rows_per_batch // tm),
            _const_spec((1, d)),
            _const_spec((d, 2 * f)),
            _const_spec((f, d)),
        ],
        out_specs=pl.BlockSpec((tm, d), lambda i: (i, 0)),
        compiler_params=_cparams("parallel"),
        name="ffn",
    )(x, mods, gain.reshape(1, d), w_in, w_out)


_C_AQ = 0
_C_AK = _C_AQ + GLA_QK
_C_AV = _C_AK + GLA_QK
_C_AR = _C_AV + GLA_V
_C_GQ = _C_AR + GLA_V
_C_GK = _C_GQ + GLB_HEADS * HEAD_DIM
_C_WK = _C_GK + GLB_KV_HEADS * HEAD_DIM
_C_GV = _C_WK + WIN_KV_HEADS * HEAD_DIM
_C_WV = _C_GV + GLB_KV_HEADS * HEAD_DIM
_C_WQ = _C_WV + WIN_KV_HEADS * HEAD_DIM
_C_AD = _C_WQ + WIN_HEADS * HEAD_DIM
_C_END = _C_AD + GATE_PAD
SLAB = 2 * LANES


def _norm_rope_slab(x, bd_ref, gains, cos, sin_signed, out_scales):
    ss = _dot((x * x).astype(BF16), bd_ref[...])
    first_half = (_lane_iota((x.shape[0], LANES)) % HEAD_DIM) < (HEAD_DIM // 2)
    outs = []
    for j in range(2):
        sl = slice(j * LANES, (j + 1) * LANES)
        xn = x[:, sl] * lax.rsqrt(ss[:, sl] * (1.0 / HEAD_DIM) + EPS) * gains[j]
        partner = jnp.where(first_half, pltpu.roll(xn, LANES - HEAD_DIM // 2, 1), pltpu.roll(xn, HEAD_DIM // 2, 1))
        out = xn * cos + partner * sin_signed
        outs.append(out * out_scales[j] if out_scales[j] != 1.0 else out)
    return outs


def _dup_heads(x):
    swapped = pltpu.roll(x, HEAD_DIM, 1)
    low = _lane_iota(x.shape) < HEAD_DIM
    return jnp.where(low, x, swapped), jnp.where(low, swapped, x)


def _store_vt(vt_ref, v):
    chunk = vt_ref.shape[4]
    for c in range(vt_ref.shape[2]):
        vt = v[c * chunk:(c + 1) * chunk, :].T
        for kv in range(vt_ref.shape[1]):
            vt_ref[0, kv, c, 0:HEAD_DIM, :] = vt[kv * HEAD_DIM:(kv + 1) * HEAD_DIM].astype(BF16)
            vt_ref[0, kv, c, HEAD_DIM:VT_ROWS, :] = jnp.ones((VT_ROWS - HEAD_DIM, chunk), BF16)


def _inproj_kernel(x_ref, mod_ref, g_ref, w_ref, wg_ref, bg_ref, qkg_ref, cos_ref, sin_ref, bd_ref,
                   aq_ref, ak_ref, av_ref, ar_ref, ag_ref, gq_ref, gk_ref, wq_ref, wk_ref, gvt_ref, wvt_ref):
    hb = _norm_modulate(x_ref[...], g_ref[...], mod_ref, 3).astype(BF16)
    cos = cos_ref[...]
    sin = sin_ref[...]
    tiles = {}

    def proj(lo, width):
        j, off = divmod(lo, SLAB)
        assert off + width <= SLAB
        if j not in tiles:
            tiles[j] = _dot(hb, w_ref[:, j * SLAB:min((j + 1) * SLAB, _C_END)])
        return tiles[j][:, off:off + width]

    q_scale = HEAD_DIM ** -0.5
    g_glb_q, g_glb_k, g_win_q, g_win_k = (qkg_ref[r:r + 1, :] for r in range(4))

    order = (_C_GQ, _C_GQ + SLAB, _C_WQ, _C_GK, _C_GV, _C_AD, _C_AQ, _C_AV, _C_AR)

    def issue_ahead(n):
        for lo in order[:n + 3]:
            proj(lo, LANES)

    for j in range(GLB_HEADS * HEAD_DIM // SLAB):
        issue_ahead(j)
        halves = _norm_rope_slab(proj(_C_GQ + j * SLAB, SLAB), bd_ref, (g_glb_q, g_glb_q), cos, sin,
                                 (q_scale, q_scale))
        for i, q in enumerate(halves):
            gq_ref[:, j * SLAB + i * LANES:j * SLAB + (i + 1) * LANES] = q.astype(BF16)
    issue_ahead(2)
    halves = _norm_rope_slab(proj(_C_WQ, SLAB), bd_ref, (g_win_q, g_win_q), cos, sin, (q_scale, q_scale))
    for i, q in enumerate(halves):
        wq_ref[:, i * LANES:(i + 1) * LANES] = q.astype(BF16)
    issue_ahead(3)
    k_glb, k_win = _norm_rope_slab(proj(_C_GK, SLAB), bd_ref, (g_glb_k, g_win_k), cos, sin, (1.0, 1.0))
    k0, k1 = _dup_heads(k_glb)
    gk_ref[:, 0:LANES] = k0.astype(BF16)
    gk_ref[:, LANES:2 * LANES] = k1.astype(BF16)
    k0, k1 = _dup_heads(k_win)
    wk_ref[:, 0:LANES] = k0.astype(BF16)
    wk_ref[:, LANES:2 * LANES] = k1.astype(BF16)

    issue_ahead(4)
    _store_vt(gvt_ref, proj(_C_GV, LANES))
    _store_vt(wvt_ref, proj(_C_WV, LANES))

    issue_ahead(len(order))
    z = _dot(proj(_C_AD, GATE_PAD).astype(BF16), wg_ref[...]) + bg_ref[...]
    log_sig = jnp.minimum(z, 0.0) - jnp.log1p(jnp.exp(-jnp.abs(z)))
    ag_ref[...] = log_sig * (1.0 / GLA_GATE_TAU)
    aq_ref[...] = (proj(_C_AQ, GLA_QK) * (GLA_DK ** -0.5)).astype(BF16)
    ak_ref[...] = proj(_C_AK, GLA_QK).astype(BF16)
    av_ref[...] = proj(_C_AV, GLA_V).astype(BF16)
    ar_ref[...] = proj(_C_AR, GLA_V).astype(BF16)


def _inproj(x, mods, gain, w, wg, bg, qk_gains, cos, sin, bd, rows_per_batch, tm_want=ROW_TILE):
    t, d = x.shape
    tm = _row_tile(rows_per_batch, tm_want)
    tpb = rows_per_batch // tm
    widths = (GLA_QK, GLA_QK, GLA_V, GLA_V, 2 * GLA_QK, GLB_HEADS * HEAD_DIM, 2 * LANES,
              WIN_HEADS * HEAD_DIM, 2 * LANES)
    dtypes = (BF16, BF16, BF16, BF16, F32, BF16, BF16, BF16, BF16)
    nb = t // rows_per_batch
    vt_shapes, vt_specs = [], []
    for kv_heads, want in ((GLB_KV_HEADS, KEY_CHUNK), (WIN_KV_HEADS, WIN_CHUNK)):
        chunk = min(want, tm)
        vt_shapes.append(jax.ShapeDtypeStruct((nb, kv_heads, rows_per_batch // chunk, VT_ROWS, chunk), BF16))
        vt_specs.append(pl.BlockSpec((1, kv_heads, tm // chunk, VT_ROWS, chunk),
                                     lambda i: (i // tpb, 0, i % tpb, 0, 0)))
    return pl.pallas_call(
        _inproj_kernel,
        out_shape=tuple(jax.ShapeDtypeStruct((t, wd), dt) for wd, dt in zip(widths, dtypes)) + tuple(vt_shapes),
        grid=(t // tm,),
        in_specs=[
            pl.BlockSpec((tm, d), lambda i: (i, 0)),
            _mod_spec(mods, tpb),
            _const_spec((1, d)),
            _const_spec(w.shape),
            _const_spec(wg.shape),
            _const_spec(bg.shape),
            _const_spec(qk_gains.shape),
            pl.BlockSpec((tm, LANES), lambda i: (i % tpb, 0)),
            pl.BlockSpec((tm, LANES), lambda i: (i % tpb, 0)),
            _const_spec(bd.shape),
        ],
        out_specs=tuple(pl.BlockSpec((tm, wd), lambda i: (i, 0)) for wd in widths) + tuple(vt_specs),
        compiler_params=_cparams("parallel"),
        name="mix_inproj",
    )(x, mods, gain.reshape(1, d), w, wg, bg, qk_gains, cos, sin, bd)


def _gla_kernel(qf_ref, kf_ref, vf_ref, gf_ref, qb_ref, kb_ref, vb_ref, gb_ref, s0_ref,
                of_ref, ob_ref, sfin_ref, st_ref, *, n_chunks):
    c_len = GLA_CHUNK
    i = pl.program_id(1)
    dir_refs = ((qf_ref, kf_ref, vf_ref, gf_ref, of_ref), (qb_ref, kb_ref, vb_ref, gb_ref, ob_ref))
    signs = (1, -1)

    @pl.when(i == 0)
    def _():
        st_ref[...] = s0_ref[0]

    group = min(GLA_GROUP, n_chunks)
    g_len = group * c_len
    row = lax.broadcasted_iota(jnp.int32, (g_len, g_len), 0)
    col = lax.broadcasted_iota(jnp.int32, (g_len, g_len), 1)
    same_chunk = row // c_len == col // c_len
    cum_ops = [jnp.where(((row - col) * sg >= 0) & same_chunk, 1.0, 0.0).astype(BF16) for sg in signs]
    row4 = lax.broadcasted_iota(jnp.int32, (c_len, GLA_HEADS * c_len), 0)
    col4 = lax.broadcasted_iota(jnp.int32, (c_len, GLA_HEADS * c_len), 1) % c_len
    keep4s = [(row4 - col4) * sg >= 0 for sg in signs]
    qk_lane_head = _lane_iota((1, GLA_QK)) // GLA_DK
    v_lane_head = _lane_iota((1, GLA_V)) // GLA_DV
    qk_head_mask = [(qk_lane_head == h).astype(F32) for h in range(GLA_HEADS)]
    v_head_mask = [(v_lane_head == h).astype(BF16) for h in range(GLA_HEADS)]
    st_row_head = lax.broadcasted_iota(jnp.int32, (GLA_V, GLA_QK), 0) // GLA_DV
    st_col_head = lax.broadcasted_iota(jnp.int32, (GLA_V, GLA_QK), 1) // GLA_DK
    st_mask = st_row_head == st_col_head

    n_groups = n_chunks // group

    def group_starts(d, j):
        starts = []
        for p in range(group):
            c = j * group + p
            if d == 1:
                c = n_chunks - 1 - c
            starts.append(pl.multiple_of(c * c_len, c_len))
        return starts

    def decay_stage(d, j):
        q_ref, k_ref, _, g_ref, _ = dir_refs[d]
        starts = group_starts(d, j)
        q = jnp.concatenate([q_ref[pl.ds(r, c_len), :] for r in starts], axis=0).astype(F32)
        k = jnp.concatenate([k_ref[pl.ds(r, c_len), :] for r in starts], axis=0).astype(F32)
        g = jnp.concatenate([g_ref[pl.ds(r, c_len), :] for r in starts], axis=0)
        g_hi = g.astype(BF16)
        r1 = g - g_hi.astype(F32)
        g_mid = r1.astype(BF16)
        g_lo = (r1 - g_mid.astype(F32)).astype(BF16)
        b = _dot(cum_ops[d], g_hi) + _dot(cum_ops[d], g_mid) + _dot(cum_ops[d], g_lo)
        q_in = (q * jnp.exp(b)).astype(BF16)
        k_out = k * jnp.exp(-b)
        decays = jnp.concatenate([jnp.exp(jnp.sum(g[p * c_len:(p + 1) * c_len], axis=0, keepdims=True))
                                  for p in range(group)], axis=0)
        return q_in, k_out, decays

    def chunk_group(j, staged):
        j_next = jnp.minimum(j + 1, n_groups - 1)
        staged_next = tuple(decay_stage(d, j_next) for d in range(2))
        starts = [group_starts(d, j) for d in range(2)]
        vs = [[dir_refs[d][2][pl.ds(r, c_len), :] for r in starts[d]] for d in range(2)]
        sts = [st_ref[0], st_ref[1]]
        for p in range(group):
            rows = slice(p * c_len, (p + 1) * c_len)
            for d in range(2):
                q_in, k_out, decays = staged[d]
                v_p = vs[d][p]
                decay = decays[p:p + 1, :]
                k_out_p = k_out[rows]
                k_dec = (k_out_p * decay).astype(BF16)
                k_stack = jnp.concatenate([(k_out_p * qk_head_mask[h]).astype(BF16) for h in range(GLA_HEADS)],
                                          axis=0)
                a = _dot_nt(q_in[rows], k_stack)
                a = jnp.where(keep4s[d], a, 0.0).astype(BF16)
                v_bd = jnp.concatenate([v_p * v_head_mask[h] for h in range(GLA_HEADS)], axis=0)
                o = _dot(a, v_bd) + _dot_nt(q_in[rows], sts[d].astype(BF16))
                dir_refs[d][4][pl.ds(starts[d][p], c_len), :] = o
                ds_t = _dot_tn(v_p, k_dec)
                sts[d] = sts[d] * decay + jnp.where(st_mask, ds_t, 0.0)
        st_ref[0] = sts[0]
        st_ref[1] = sts[1]
        return staged_next

    lax.fori_loop(0, n_groups, chunk_group, tuple(decay_stage(d, 0) for d in range(2)))

    @pl.when(i == pl.num_programs(1) - 1)
    def _():
        sfin_ref[0] = st_ref[...]


def _gla(q, k, v, g, s0, rows_per_batch, tt_want=GLA_ROW_TILE):
    t = q.shape[0]
    nb = t // rows_per_batch
    tt = _row_tile(rows_per_batch, tt_want)
    nt = rows_per_batch // tt

    def fwd(lane_block):
        return lambda b, i: (b * nt + i, lane_block)

    def bwd(lane_block):
        return lambda b, i: (b * nt + nt - 1 - i, lane_block)

    state_spec = pl.BlockSpec((1, 2, GLA_V, GLA_QK), lambda b, i: (b, 0, 0, 0))
    return pl.pallas_call(
        functools.partial(_gla_kernel, n_chunks=tt // GLA_CHUNK),
        out_shape=(jax.ShapeDtypeStruct((t, GLA_V), F32), jax.ShapeDtypeStruct((t, GLA_V), F32),
                   jax.ShapeDtypeStruct((nb, 2, GLA_V, GLA_QK), F32)),
        grid=(nb, nt),
        in_specs=[
            pl.BlockSpec((tt, GLA_QK), fwd(0)), pl.BlockSpec((tt, GLA_QK), fwd(0)),
            pl.BlockSpec((tt, GLA_V), fwd(0)), pl.BlockSpec((tt, GLA_QK), fwd(0)),
            pl.BlockSpec((tt, GLA_QK), bwd(0)), pl.BlockSpec((tt, GLA_QK), bwd(0)),
            pl.BlockSpec((tt, GLA_V), bwd(0)), pl.BlockSpec((tt, GLA_QK), bwd(1)),
            state_spec,
        ],
        out_specs=(pl.BlockSpec((tt, GLA_V), fwd(0)), pl.BlockSpec((tt, GLA_V), bwd(0)), state_spec),
        scratch_shapes=[pltpu.VMEM((2, GLA_V, GLA_QK), F32)],
        compiler_params=_cparams("parallel", "arbitrary"),
        name="gla_scan",
    )(q, k, v, g, q, k, v, g, s0)


def _glb_kernel(q_ref, k_ref, vt_ref, kc_ref, vtc_ref, o_ref, qt_ref, m_ref, acc_ref, s_ref, *,
                n_pairs, col_block, chunks_per_tile):
    tq = q_ref.shape[0]
    n_lat = vt_ref.shape[2] // chunks_per_tile
    tk = vt_ref.shape[4] * chunks_per_tile
    n_heads = 2 * n_pairs
    for p in range(n_pairs):
        qt = (q_ref[:, p * LANES:(p + 1) * LANES].astype(F32) * LOG2_E).T
        qt_ref[2 * p] = qt[0:HEAD_DIM].astype(BF16)
        qt_ref[2 * p + 1] = qt[HEAD_DIM:2 * HEAD_DIM].astype(BF16)

    m_ref[...] = jnp.full(m_ref.shape, MASK_VALUE, F32)
    acc_ref[...] = jnp.zeros(acc_ref.shape, F32)

    units = [(h, cb) for h in range(n_heads) for cb in range(tq // col_block)]

    def scores(k, u):
        h, cb = units[u]
        return _dot(k[:, 0:HEAD_DIM], qt_ref[h, :, cb * col_block:(cb + 1) * col_block])

    def step(k, vt, k_next):
        w = k.shape[0]
        for u, (h, cb) in enumerate(units):
            s = s_ref[u, 0:w, :]
            if k_next is not None:
                s_ref[u, 0:k_next.shape[0], :] = scores(k_next, u)
            cols = slice(cb * col_block, (cb + 1) * col_block)
            m_prev = m_ref[h, :, cols]
            m_new = jnp.maximum(m_prev, jnp.max(s, axis=0, keepdims=True))
            alpha = jnp.exp2(m_prev - m_new)
            e = jnp.exp2((s - m_new).astype(BF16))
            m_ref[h, :, cols] = m_new
            ch = w // len(vt)
            pv = _dot(vt[0], e[0:ch, :])
            for c in range(1, len(vt)):
                pv = pv + _dot(vt[c], e[c * ch:(c + 1) * ch, :])
            acc_ref[h, :, cols] = acc_ref[h, :, cols] * alpha + pv

    def lat_tile(j):
        return k_ref[pl.ds(pl.multiple_of(j * tk, tk), tk), :]

    def lat_vt(j):
        return [vt_ref[0, 0, j * chunks_per_tile + c] for c in range(chunks_per_tile)]

    wc = vtc_ref.shape[4]
    ctx_tiles = [kc_ref[j * wc:(j + 1) * wc, :] for j in range(vtc_ref.shape[2])]

    for u in range(len(units)):
        s_ref[u, 0:tk, :] = scores(lat_tile(0), u)

    def body(j, carry):
        step(lat_tile(j), lat_vt(j), lat_tile(j + 1))
        return carry

    lax.fori_loop(0, n_lat - 1, body, 0)
    step(lat_tile(n_lat - 1), lat_vt(n_lat - 1), ctx_tiles[0])
    for j, kc in enumerate(ctx_tiles):
        step(kc, [vtc_ref[0, 0, j]], ctx_tiles[j + 1] if j + 1 < len(ctx_tiles) else None)

    for p in range(n_pairs):
        halves = [acc_ref[h, 0:HEAD_DIM, :] / acc_ref[h, HEAD_DIM:HEAD_DIM + 1, :] for h in (2 * p, 2 * p + 1)]
        o_ref[:, p * LANES:(p + 1) * LANES] = jnp.concatenate(halves, axis=0).T.astype(o_ref.dtype)


def _glb_attn(q, k, vt, kc, vtc, seq, ctx_len, tq_want=GLB_Q_TILE, col_block=GLB_COL_BLOCK, tk_want=KEY_CHUNK):
    t = q.shape[0]
    nb = t // seq
    groups = GLB_KV_HEADS
    gw = q.shape[1] // groups
    n_pairs = gw // LANES
    tq = _row_tile(seq, tq_want)
    nq = seq // tq
    col_block = min(col_block, tq)
    n_units = 2 * n_pairs * (tq // col_block)
    chunks_per_tile = max(1, min(tk_want, seq) // vt.shape[4])
    assert vt.shape[2] % chunks_per_tile == 0
    max_keys = max(vt.shape[4] * chunks_per_tile, vtc.shape[4])
    return pl.pallas_call(
        functools.partial(_glb_kernel, n_pairs=n_pairs, col_block=col_block, chunks_per_tile=chunks_per_tile),
        out_shape=jax.ShapeDtypeStruct(q.shape, BF16),
        grid=(nb, groups, nq),
        in_specs=[
            pl.BlockSpec((tq, gw), lambda b, g, i: (b * nq + i, g)),
            pl.BlockSpec((seq, LANES), lambda b, g, i: (b, g)),
            pl.BlockSpec((1, 1) + vt.shape[2:], lambda b, g, i: (b, g, 0, 0, 0)),
            pl.BlockSpec((ctx_len, LANES), lambda b, g, i: (b, g)),
            pl.BlockSpec((1, 1) + vtc.shape[2:], lambda b, g, i: (b, g, 0, 0, 0)),
        ],
        out_specs=pl.BlockSpec((tq, gw), lambda b, g, i: (b * nq + i, g)),
        scratch_shapes=[pltpu.VMEM((2 * n_pairs, HEAD_DIM, tq), BF16),
                        pltpu.VMEM((2 * n_pairs, 1, tq), F32),
                        pltpu.VMEM((2 * n_pairs, VT_ROWS, tq), F32),
                        pltpu.VMEM((n_units, max_keys, col_block), F32)],
        compiler_params=_cparams("parallel", "parallel", "arbitrary"),
        name="glb_attn",
    )(q, k, vt, kc, vtc)


def _win_kernel(sink_ref, q_ref, k_ref, vt_ref, kc_ref, vtc_ref, o_ref, sloc_ref, sctx_ref, bias_ref, *,
                tq, span, blocks_per_step):
    g = pl.program_id(1)
    seq = k_ref.shape[0]
    n_blocks = seq // tq
    n_steps = n_blocks // blocks_per_step
    wchunk = vt_ref.shape[4]
    kc = kc_ref[...][:, 0:HEAD_DIM]
    vt_ctx = jnp.concatenate([vtc_ref[0, 0, c] for c in range(vtc_ref.shape[2])], axis=1)
    col_minus_row = (lax.broadcasted_iota(jnp.int32, (span, tq), 1)
                     - lax.broadcasted_iota(jnp.int32, (span, tq), 0))
    for n in range(3):
        bias_ref[n] = jnp.where(jnp.abs(col_minus_row + n * WINDOW) <= WINDOW, 0.0, MASK_VALUE)

    def band_start(qb):
        return pl.multiple_of(jnp.clip(qb * tq - WINDOW, 0, seq - span), wchunk)

    def block_scores(qb):
        k_loc = k_ref[pl.ds(band_start(qb), span), :][:, 0:HEAD_DIM]
        qt = (q_ref[pl.ds(pl.multiple_of(qb * tq, tq), tq), :].astype(F32) * LOG2_E).T
        for half in range(2):
            qt_h = qt[half * HEAD_DIM:(half + 1) * HEAD_DIM].astype(BF16)
            yield _dot(k_loc, qt_h), _dot(kc, qt_h)

    for j in range(blocks_per_step):
        for half, (s_loc, s_ctx) in enumerate(block_scores(j)):
            sloc_ref[2 * j + half] = s_loc
            sctx_ref[2 * j + half] = s_ctx

    def step(i, carry):
        for j in range(blocks_per_step):
            qb = i * blocks_per_step + j
            q0 = pl.multiple_of(qb * tq, tq)
            start = band_start(qb)
            c0 = start // wchunk
            bias = bias_ref[(q0 - start) // WINDOW]
            vt_loc = jnp.concatenate([vt_ref[0, 0, c0 + c] for c in range(span // wchunk)], axis=1)
            ahead = block_scores(jnp.minimum(qb + blocks_per_step, n_blocks - 1))
            halves = []
            for half in range(2):
                u = 2 * j + half
                sink2 = sink_ref[2 * g + half] * LOG2_E
                s_loc = sloc_ref[u] + bias
                s_ctx = sctx_ref[u]
                sloc_ref[u], sctx_ref[u] = next(ahead)
                m = jnp.maximum(jnp.max(s_loc, axis=0, keepdims=True), jnp.max(s_ctx, axis=0, keepdims=True))
                m = jnp.maximum(m, sink2)
                e_loc = jnp.exp2((s_loc - m).astype(BF16))
                e_ctx = jnp.exp2((s_ctx - m).astype(BF16))
                acc = _dot(vt_loc, e_loc) + _dot(vt_ctx, e_ctx)
                den = acc[HEAD_DIM:HEAD_DIM + 1, :] + jnp.exp2(sink2 - m)
                halves.append(acc[0:HEAD_DIM, :] / den)
            o_ref[pl.ds(q0, tq), :] = jnp.concatenate(halves, axis=0).T.astype(o_ref.dtype)
        return carry

    lax.fori_loop(0, n_steps, step, 0)


def _win_attn(q, k, vt, kc, vtc, sink, seq, ctx_len, tq_want=WIN_Q_BLOCK, blocks_per_step=WIN_BLOCKS_PER_STEP):
    t = q.shape[0]
    nb = t // seq
    groups = WIN_KV_HEADS
    assert q.shape[1] == groups * LANES
    tq = _row_tile(seq, min(tq_want, max(seq - 2 * WINDOW, WINDOW)))
    span = tq + 2 * WINDOW
    assert seq >= span and span % vt.shape[4] == 0 and WINDOW % vt.shape[4] == 0 and tq % WINDOW == 0
    blocks_per_step = min(blocks_per_step, seq // tq)
    assert (seq // tq) % blocks_per_step == 0
    n_units = 2 * blocks_per_step
    return pl.pallas_call(
        functools.partial(_win_kernel, tq=tq, span=span, blocks_per_step=blocks_per_step),
        out_shape=jax.ShapeDtypeStruct(q.shape, BF16),
        grid=(nb, groups),
        in_specs=[
            pl.BlockSpec(memory_space=pltpu.SMEM),
            pl.BlockSpec((seq, LANES), lambda b, g: (b, g)),
            pl.BlockSpec((seq, LANES), lambda b, g: (b, g)),
            pl.BlockSpec((1, 1) + vt.shape[2:], lambda b, g: (b, g, 0, 0, 0)),
            pl.BlockSpec((ctx_len, LANES), lambda b, g: (b, g)),
            pl.BlockSpec((1, 1) + vtc.shape[2:], lambda b, g: (b, g, 0, 0, 0)),
        ],
        out_specs=pl.BlockSpec((seq, LANES), lambda b, g: (b, g)),
        scratch_shapes=[pltpu.VMEM((n_units, span, tq), F32), pltpu.VMEM((n_units, ctx_len, tq), F32),
                        pltpu.VMEM((3, span, tq), F32)],
        compiler_params=_cparams("parallel", "parallel"),
        name="win_attn",
    )(sink, q, k, vt, kc, vtc)


def _ctx_attn_kernel(sink_ref, q_ref, k_ref, vt_ref, o_ref, *, n_pairs, use_sink):
    g = pl.program_id(1)
    k = k_ref[...][:, 0:HEAD_DIM]
    vt = jnp.concatenate([vt_ref[0, 0, c] for c in range(vt_ref.shape[2])], axis=1)
    for p in range(n_pairs):
        qt = (q_ref[:, p * LANES:(p + 1) * LANES].astype(F32) * LOG2_E).T
        halves = []
        for half in range(2):
            s = _dot(k, qt[half * HEAD_DIM:(half + 1) * HEAD_DIM].astype(BF16))
            m = jnp.max(s, axis=0, keepdims=True)
            if use_sink:
                sink2 = sink_ref[(g * n_pairs + p) * 2 + half] * LOG2_E
                m = jnp.maximum(m, sink2)
            acc = _dot(vt, jnp.exp2((s - m).astype(BF16)))
            den = acc[HEAD_DIM:HEAD_DIM + 1, :]
            if use_sink:
                den = den + jnp.exp2(sink2 - m)
            halves.append(acc[0:HEAD_DIM, :] / den)
        o_ref[:, p * LANES:(p + 1) * LANES] = jnp.concatenate(halves, axis=0).T.astype(o_ref.dtype)


def _ctx_attn(q, k, vt, sink, groups, ctx_len, use_sink):
    t = q.shape[0]
    nb = t // ctx_len
    gw = q.shape[1] // groups
    n_pairs = gw // LANES
    return pl.pallas_call(
        functools.partial(_ctx_attn_kernel, n_pairs=n_pairs, use_sink=use_sink),
        out_shape=jax.ShapeDtypeStruct(q.shape, BF16),
        grid=(nb, groups),
        in_specs=[
            pl.BlockSpec(memory_space=pltpu.SMEM),
            pl.BlockSpec((ctx_len, gw), lambda b, g: (b, g)),
            pl.BlockSpec((ctx_len, LANES), lambda b, g: (b, g)),
            pl.BlockSpec((1, 1) + vt.shape[2:], lambda b, g: (b, g, 0, 0, 0)),
        ],
        out_specs=pl.BlockSpec((ctx_len, gw), lambda b, g: (b, g)),
        compiler_params=_cparams("parallel", "parallel"),
        name="ctx_attn",
    )(sink, q, k, vt)


def _outproj_ffn_kernel(x_ref, mod_ref, of_ref, ob_ref, r_ref, og_ref, ow_ref, gain_ref, bd_ref, w_ref,
                        g2_ref, win_ref, wout_ref, o_ref, *, f):
    o = of_ref[...] + ob_ref[...]
    r = r_ref[...].astype(F32)
    gate = r * jax.nn.sigmoid(r)
    ss = _dot((o * o).astype(BF16), bd_ref[...])
    on = o * lax.rsqrt(ss * (1.0 / GLA_DV) + EPS) * gain_ref[...]
    cat = jnp.concatenate([(on * gate).astype(BF16), og_ref[...], ow_ref[...]], axis=-1)
    x = x_ref[...] + mod_ref[0, 5:6, :] * _dot(cat, w_ref[...])
    o_ref[...] = _ffn_body(x, mod_ref, g2_ref, win_ref, wout_ref, 6, f)


def _outproj_ffn(x, mods, o_gla_f, o_gla_b, r, o_glb, o_win, gla_gain, bd, w_out, gain2, w2_in, w2_out,
                 rows_per_batch, tm_want=ROW_TILE):
    t, d = x.shape
    f = w2_out.shape[0]
    tm = _row_tile(rows_per_batch, tm_want)
    return pl.pallas_call(
        functools.partial(_outproj_ffn_kernel, f=f),
        out_shape=jax.ShapeDtypeStruct((t, d), F32),
        grid=(t // tm,),
        in_specs=[
            pl.BlockSpec((tm, d), lambda i: (i, 0)),
            _mod_spec(mods, rows_per_batch // tm),
            pl.BlockSpec((tm, GLA_V), lambda i: (i, 0)),
            pl.BlockSpec((tm, GLA_V), lambda i: (i, 0)),
            pl.BlockSpec((tm, GLA_V), lambda i: (i, 0)),
            pl.BlockSpec((tm, o_glb.shape[1]), lambda i: (i, 0)),
            pl.BlockSpec((tm, o_win.shape[1]), lambda i: (i, 0)),
            _const_spec(gla_gain.shape),
            _const_spec(bd.shape),
            _const_spec(w_out.shape),
            _const_spec((1, d)),
            _const_spec((d, 2 * f)),
            _const_spec((f, d)),
        ],
        out_specs=pl.BlockSpec((tm, d), lambda i: (i, 0)),
        compiler_params=_cparams("parallel"),
        name="mix_outproj_ffn",
    )(x, mods, o_gla_f, o_gla_b, r, o_glb, o_win, gla_gain, bd, w_out, gain2.reshape(1, d), w2_in, w2_out)


def _rope_tables(seq):
    rows = seq // GRID_W
    row = jnp.repeat(jnp.arange(rows, dtype=F32), GRID_W)
    col = (jnp.arange(rows * GRID_W) % GRID_W).astype(F32)
    n_freq = HEAD_DIM // 4
    inv = jnp.power(ROPE_BASE, -jnp.arange(n_freq, dtype=F32) / n_freq)
    ang = jnp.concatenate([row[:, None] * inv, col[:, None] * inv], axis=-1)
    cos, sin = jnp.cos(ang), jnp.sin(ang)
    cos_t = jnp.concatenate([cos, cos] * (LANES // HEAD_DIM), axis=-1)
    sin_t = jnp.concatenate([-sin, sin] * (LANES // HEAD_DIM), axis=-1)
    return cos_t, sin_t


def _block_diag_ones(n, block):
    idx = np.arange(n) // block
    return jnp.asarray(idx[:, None] == idx[None, :], dtype=BF16)


def kernel(x, c, ctx, c_ctx, mod_w, mod_b, norm_ffn1, ffn1_w_in, ffn1_w_out, norm_mix, mix_w_in, mix_w_out,
           gla_wg_f, gla_bg_f, gla_wg_b, gla_bg_b, gla_out_norm, glb_q_norm, glb_k_norm,
           win_q_norm, win_k_norm, win_sink, norm_ffn2, ffn2_w_in, ffn2_w_out):
    bsz, seq, d = x.shape
    ctx_len = ctx.shape[1]
    depth = mod_w.shape[0]
    in_splits = (GLA_QK, GLA_QK, GLA_V, GLA_V, 2 * GLA_GATE_RANK,
                 GLB_HEADS * HEAD_DIM, GLB_KV_HEADS * HEAD_DIM, GLB_KV_HEADS * HEAD_DIM,
                 WIN_HEADS * HEAD_DIM, WIN_KV_HEADS * HEAD_DIM, WIN_KV_HEADS * HEAD_DIM)

    n_rows = -(-(bsz + 1) // 8) * 8
    c_rows = jnp.concatenate([c, c_ctx[None, :], jnp.zeros((n_rows - bsz - 1, d), F32)], axis=0)
    mods = _modvec(c_rows, mod_w, mod_b)

    cos_l, sin_l = _rope_tables(seq)
    cos_c = jnp.ones((ctx_len, LANES), F32)
    sin_c = jnp.zeros((ctx_len, LANES), F32)
    bd = _block_diag_ones(SLAB, HEAD_DIM)
    zero_state = jnp.zeros((bsz, 2, GLA_V, GLA_QK), F32)

    xl = x.reshape(bsz * seq, d)
    xc = ctx.reshape(bsz * ctx_len, d)

    for l in range(depth):
        need_ctx = l < depth - 1
        mods_l = mods[l, :bsz].reshape(bsz, N_MOD, d)
        mods_c = mods[l, bsz:bsz + 1].reshape(1, N_MOD, d)
        w1_in, w1_out = ffn1_w_in[l].astype(BF16), ffn1_w_out[l].astype(BF16)
        w2_in, w2_out = ffn2_w_in[l].astype(BF16), ffn2_w_out[l].astype(BF16)
        wm = mix_w_in[l]
        src = dict(zip(("aq", "ak", "av", "ar", "ad", "gq", "gk", "gv", "wq", "wk", "wv"),
                       jnp.split(wm, np.cumsum(in_splits)[:-1], axis=1)))
        w_mix = jnp.concatenate([src[n] for n in ("aq", "ak", "av", "ar", "gq", "gk", "wk", "gv", "wv", "wq", "ad")]
                                + [jnp.zeros((d, GATE_PAD - 2 * GLA_GATE_RANK), F32)], axis=1).astype(BF16)
        assert w_mix.shape[1] == _C_END
        wg = jnp.zeros((GATE_PAD, 2 * GLA_QK), F32)
        wg = wg.at[:GLA_GATE_RANK, :GLA_QK].set(gla_wg_f[l])
        wg = wg.at[GLA_GATE_RANK:2 * GLA_GATE_RANK, GLA_QK:].set(gla_wg_b[l]).astype(BF16)
        bg = jnp.concatenate([gla_bg_f[l], gla_bg_b[l]])[None, :]
        qk_gains = jnp.stack([jnp.tile(gn[l], LANES // HEAD_DIM)
                              for gn in (glb_q_norm, glb_k_norm, win_q_norm, win_k_norm)])
        gla_gain = jnp.tile(gla_out_norm[l], GLA_HEADS)[None, :]
        w_out = mix_w_out[l].astype(BF16)
        sink = win_sink[l]

        xl = _ffn(xl, mods_l, 0, norm_ffn1[l], w1_in, w1_out, seq)
        xc = _ffn(xc, mods_c, 0, norm_ffn1[l], w1_in, w1_out, bsz * ctx_len)

        pc = _inproj(xc, mods_c, norm_mix[l], w_mix, wg, bg, qk_gains, cos_c, sin_c, bd, ctx_len)
        pq = _inproj(xl, mods_l, norm_mix[l], w_mix, wg, bg, qk_gains, cos_l, sin_l, bd, seq)
        aq, ak, av, ar, ag, gq, gk, wq, wk, gvt, wvt = pq
        aqc, akc, avc, arc, agc, gqc, gkc, wqc, wkc, gvtc, wvtc = pc

        oc_gla_f, oc_gla_b, states = _gla(aqc, akc, avc, agc, zero_state, ctx_len)
        o_gla_f, o_gla_b, _ = _gla(aq, ak, av, ag, states, seq)
        o_glb = _glb_attn(gq, gk, gvt, gkc, gvtc, seq, ctx_len)
        o_win = _win_attn(wq, wk, wvt, wkc, wvtc, sink, seq, ctx_len)
        xl = _outproj_ffn(xl, mods_l, o_gla_f, o_gla_b, ar, o_glb, o_win, gla_gain, bd, w_out, norm_ffn2[l],
                          w2_in, w2_out, seq)

        if need_ctx:
            oc_glb = _ctx_attn(gqc, gkc, gvtc, sink, GLB_KV_HEADS, ctx_len, use_sink=False)
            oc_win = _ctx_attn(wqc, wkc, wvtc, sink, WIN_KV_HEADS, ctx_len, use_sink=True)
            xc = _outproj_ffn(xc, mods_c, oc_gla_f, oc_gla_b, arc, oc_glb, oc_win, gla_gain, bd, w_out, norm_ffn2[l],
                              w2_in, w2_out, bsz * ctx_len)

    return xl.reshape(bsz, seq, d)
```

```python
import functools

import numpy as np
import jax
import jax.numpy as jnp
from jax import lax
from jax.experimental import pallas as pl
from jax.experimental.pallas import tpu as pltpu

GRID_W = 64
HEAD_DIM = 64
GLA_HEADS = 4
GLA_DK = 32
GLA_DV = 64
GLA_GATE_RANK = 16
GLA_GATE_TAU = 16.0
GLA_CHUNK = 64
GLB_HEADS = 8
GLB_KV_HEADS = 2
WIN_HEADS = 4
WIN_KV_HEADS = 2
WINDOW = 128
ROPE_BASE = 10000.0
N_MOD = 9
EPS = 1e-6

LANES = 128
VMEM_LIMIT_BYTES = 56 * 1024 * 1024

GLA_QK = GLA_HEADS * GLA_DK
GLA_V = GLA_HEADS * GLA_DV
GATE_PAD = LANES
MASK_VALUE = -1e30
LOG2_E = 1.4426950408889634
KEY_CHUNK = 256
ROW_TILE = 512
INPROJ_ROW_TILE = 1024
FFN_HIDDEN_CHUNK = 1536
GLA_ROW_TILE = 2048
GLB_Q_TILE = 4096
GLB_COL_BLOCK = 256
WIN_Q_BLOCK = 256
WIN_BLOCKS_PER_STEP = 4
GLA_GROUP = 4
WIN_CHUNK = 128
VT_ROWS = HEAD_DIM + 16

BF16 = jnp.bfloat16
F32 = jnp.float32


def _cparams(*sem):
    return pltpu.CompilerParams(dimension_semantics=sem, vmem_limit_bytes=VMEM_LIMIT_BYTES)


def _dot(a, b):
    return jnp.dot(a, b, preferred_element_type=F32)


def _dot_nt(a, b):
    return lax.dot_general(a, b, (((1,), (1,)), ((), ())), preferred_element_type=F32)


def _dot_tn(a, b):
    return lax.dot_general(a, b, (((0,), (0,)), ((), ())), preferred_element_type=F32)


def _lane_iota(shape):
    return lax.broadcasted_iota(jnp.int32, shape, len(shape) - 1)


def _modvec_kernel(c_ref, w_ref, b_ref, o_ref):
    c = c_ref[...]
    s = (c * jax.nn.sigmoid(c)).astype(BF16)
    o_ref[0] = _dot(s, w_ref[0].astype(BF16)) + b_ref[0]


def _modvec(c_rows, mod_w, mod_b):
    depth, d, n = mod_w.shape
    rows = c_rows.shape[0]
    tn = n // N_MOD
    return pl.pallas_call(
        _modvec_kernel,
        out_shape=jax.ShapeDtypeStruct((depth, rows, n), F32),
        grid=(depth, n // tn),
        in_specs=[
            pl.BlockSpec((rows, d), lambda l, j: (0, 0)),
            pl.BlockSpec((1, d, tn), lambda l, j: (l, 0, j)),
            pl.BlockSpec((1, 1, tn), lambda l, j: (l, 0, j)),
        ],
        out_specs=pl.BlockSpec((1, rows, tn), lambda l, j: (l, 0, j)),
        compiler_params=_cparams("arbitrary", "arbitrary"),
        name="modvec",
    )(c_rows, mod_w, mod_b.reshape(depth, 1, n))


def _norm_modulate(x, gain, mod_ref, k_shift):
    shift = mod_ref[0, k_shift:k_shift + 1, :]
    scale = mod_ref[0, k_shift + 1:k_shift + 2, :]
    y = x * lax.rsqrt(jnp.mean(x * x, axis=-1, keepdims=True) + EPS)
    return (y * gain) * (1.0 + scale) + shift


def _mod_spec(mods, tiles_per_batch):
    nb, nm, d = mods.shape
    if nb == 1:
        return pl.BlockSpec((1, nm, d), lambda i: (0, 0, 0))
    return pl.BlockSpec((1, nm, d), lambda i: (i // tiles_per_batch, 0, 0))


def _const_spec(shape):
    nd = len(shape)
    return pl.BlockSpec(shape, lambda *_: (0,) * nd, pipeline_mode=pl.Buffered(1))


def _row_tile(n_rows_per_batch, want):
    t = min(want, n_rows_per_batch)
    assert n_rows_per_batch % t == 0
    return t


def _ffn_chunks(f):
    step = min(FFN_HIDDEN_CHUNK, f)
    return tuple((lo, min(lo + step, f)) for lo in range(0, f, step))


def _ffn_kernel(x_ref, mod_ref, g_ref, win_ref, wout_ref, o_ref, *, k0, f):
    o_ref[...] = _ffn_body(x_ref[...], mod_ref, g_ref, win_ref, wout_ref, k0, f)


def _ffn_body(x, mod_ref, g_ref, win_ref, wout_ref, k0, f):
    halves = 2 if x.shape[0] % 16 == 0 else 1
    rows = x.shape[0] // halves
    return jnp.concatenate([_ffn_rows(x[h * rows:(h + 1) * rows], mod_ref, g_ref, win_ref, wout_ref, k0, f)
                            for h in range(halves)], axis=0)


def _ffn_rows(x, mod_ref, g_ref, win_ref, wout_ref, k0, f):
    hb = _norm_modulate(x, g_ref[...], mod_ref, k0).astype(BF16)
    gate = mod_ref[0, k0 + 2:k0 + 3, :]
    acc = None
    for lo, hi in _ffn_chunks(f):
        a = _dot(hb, win_ref[:, lo:hi])
        b = _dot(hb, win_ref[:, f + lo:f + hi])
        u = ((a * jax.nn.sigmoid(a)) * b).astype(BF16)
        part = _dot(u, wout_ref[lo:hi, :])
        acc = part if acc is None else acc + part
    return x + (0.5 * gate) * acc


def _ffn(x, mods, k0, gain, w_in, w_out, rows_per_batch, tm_want=ROW_TILE):
    t, d = x.shape
    f = w_out.shape[0]
    tm = _row_tile(rows_per_batch, tm_want)
    return pl.pallas_call(
        functools.partial(_ffn_kernel, k0=k0, f=f),
        out_shape=jax.ShapeDtypeStruct((t, d), F32),
        grid=(t // tm,),
        in_specs=[
            pl.BlockSpec((tm, d), lambda i: (i, 0)),
            _mod_spec(mods, rows_per_batch // tm),
            _const_spec((1, d)),
            _const_spec((d, 2 * f)),
            _const_spec((f, d)),
        ],
        out_specs=pl.BlockSpec((tm, d), lambda i: (i, 0)),
        compiler_params=_cparams("parallel"),
        name="ffn",
    )(x, mods, gain.reshape(1, d), w_in, w_out)


_C_AQ = 0
_C_AK = _C_AQ + GLA_QK
_C_AV = _C_AK + GLA_QK
_C_AR = _C_AV + GLA_V
_C_GQ = _C_AR + GLA_V
_C_GK = _C_GQ + GLB_HEADS * HEAD_DIM
_C_WK = _C_GK + GLB_KV_HEADS * HEAD_DIM
_C_GV = _C_WK + WIN_KV_HEADS * HEAD_DIM
_C_WV = _C_GV + GLB_KV_HEADS * HEAD_DIM
_C_WQ = _C_WV + WIN_KV_HEADS * HEAD_DIM
_C_AD = _C_WQ + WIN_HEADS * HEAD_DIM
_C_END = _C_AD + GATE_PAD
SLAB = 2 * LANES


def _norm_rope_slab(x, bd_ref, gains, cos, sin_signed, out_scales):
    ss = _dot((x * x).astype(BF16), bd_ref[...])
    first_half = (_lane_iota((x.shape[0], LANES)) % HEAD_DIM) < (HEAD_DIM // 2)
    outs = []
    for j in range(2):
        sl = slice(j * LANES, (j + 1) * LANES)
        xn = x[:, sl] * lax.rsqrt(ss[:, sl] * (1.0 / HEAD_DIM) + EPS) * gains[j]
        partner = jnp.where(first_half, pltpu.roll(xn, LANES - HEAD_DIM // 2, 1), pltpu.roll(xn, HEAD_DIM // 2, 1))
        out = xn * cos + partner * sin_signed
        outs.append(out * out_scales[j] if out_scales[j] != 1.0 else out)
    return outs


def _dup_heads(x):
    swapped = pltpu.roll(x, HEAD_DIM, 1)
    low = _lane_iota(x.shape) < HEAD_DIM
    return jnp.where(low, x, swapped), jnp.where(low, swapped, x)


def _store_vt(vt_ref, v):
    chunk = vt_ref.shape[4]
    for c in range(vt_ref.shape[2]):
        vt = v[c * chunk:(c + 1) * chunk, :].T
        for kv in range(vt_ref.shape[1]):
            vt_ref[0, kv, c, 0:HEAD_DIM, :] = vt[kv * HEAD_DIM:(kv + 1) * HEAD_DIM].astype(BF16)
            vt_ref[0, kv, c, HEAD_DIM:VT_ROWS, :] = jnp.ones((VT_ROWS - HEAD_DIM, chunk), BF16)


def _inproj_kernel(x_ref, mod_ref, g_ref, w_ref, wg_ref, bg_ref, qkg_ref, cos_ref, sin_ref, bd_ref,
                   aq_ref, ak_ref, av_ref, ar_ref, ag_ref, gq_ref, gk_ref, wq_ref, wk_ref, gvt_ref, wvt_ref):
    hb = _norm_modulate(x_ref[...], g_ref[...], mod_ref, 3).astype(BF16)
    cos = cos_ref[...]
    sin = sin_ref[...]
    tiles = {}

    def proj(lo, width):
        j, off = divmod(lo, SLAB)
        assert off + width <= SLAB
        if j not in tiles:
            tiles[j] = _dot(hb, w_ref[:, j * SLAB:min((j + 1) * SLAB, _C_END)])
        return tiles[j][:, off:off + width]

    q_scale = HEAD_DIM ** -0.5
    g_glb_q, g_glb_k, g_win_q, g_win_k = (qkg_ref[r:r + 1, :] for r in range(4))

    order = (_C_GQ, _C_GQ + SLAB, _C_WQ, _C_GK, _C_GV, _C_AD, _C_AQ, _C_AV, _C_AR)

    def issue_ahead(n):
        for lo in order[:n + 3]:
            proj(lo, LANES)

    for j in range(GLB_HEADS * HEAD_DIM // SLAB):
        issue_ahead(j)
        halves = _norm_rope_slab(proj(_C_GQ + j * SLAB, SLAB), bd_ref, (g_glb_q, g_glb_q), cos, sin,
                                 (q_scale, q_scale))
        for i, q in enumerate(halves):
            gq_ref[:, j * SLAB + i * LANES:j * SLAB + (i + 1) * LANES] = q.astype(BF16)
    issue_ahead(2)
    halves = _norm_rope_slab(proj(_C_WQ, SLAB), bd_ref, (g_win_q, g_win_q), cos, sin, (q_scale, q_scale))
    for i, q in enumerate(halves):
        wq_ref[:, i * LANES:(i + 1) * LANES] = q.astype(BF16)
    issue_ahead(3)
    k_glb, k_win = _norm_rope_slab(proj(_C_GK, SLAB), bd_ref, (g_glb_k, g_win_k), cos, sin, (1.0, 1.0))
    k0, k1 = _dup_heads(k_glb)
    gk_ref[:, 0:LANES] = k0.astype(BF16)
    gk_ref[:, LANES:2 * LANES] = k1.astype(BF16)
    k0, k1 = _dup_heads(k_win)
    wk_ref[:, 0:LANES] = k0.astype(BF16)
    wk_ref[:, LANES:2 * LANES] = k1.astype(BF16)

    issue_ahead(4)
    _store_vt(gvt_ref, proj(_C_GV, LANES))
    _store_vt(wvt_ref, proj(_C_WV, LANES))

    issue_ahead(len(order))
    z = _dot(proj(_C_AD, GATE_PAD).astype(BF16), wg_ref[...]) + bg_ref[...]
    log_sig = jnp.minimum(z, 0.0) - jnp.log1p(jnp.exp(-jnp.abs(z)))
    ag_ref[...] = log_sig * (1.0 / GLA_GATE_TAU)
    aq_ref[...] = (proj(_C_AQ, GLA_QK) * (GLA_DK ** -0.5)).astype(BF16)
    ak_ref[...] = proj(_C_AK, GLA_QK).astype(BF16)
    av_ref[...] = proj(_C_AV, GLA_V).astype(BF16)
    ar_ref[...] = proj(_C_AR, GLA_V).astype(BF16)


def _inproj(x, mods, gain, w, wg, bg, qk_gains, cos, sin, bd, rows_per_batch, tm_want=INPROJ_ROW_TILE):
    t, d = x.shape
    tm = _row_tile(rows_per_batch, tm_want)
    tpb = rows_per_batch // tm
    widths = (GLA_QK, GLA_QK, GLA_V, GLA_V, 2 * GLA_QK, GLB_HEADS * HEAD_DIM, 2 * LANES,
              WIN_HEADS * HEAD_DIM, 2 * LANES)
    dtypes = (BF16, BF16, BF16, BF16, F32, BF16, BF16, BF16, BF16)
    nb = t // rows_per_batch
    vt_shapes, vt_specs = [], []
    for kv_heads, want in ((GLB_KV_HEADS, KEY_CHUNK), (WIN_KV_HEADS, WIN_CHUNK)):
        chunk = min(want, tm)
        vt_shapes.append(jax.ShapeDtypeStruct((nb, kv_heads, rows_per_batch // chunk, VT_ROWS, chunk), BF16))
        vt_specs.append(pl.BlockSpec((1, kv_heads, tm // chunk, VT_ROWS, chunk),
                                     lambda i: (i // tpb, 0, i % tpb, 0, 0)))
    return pl.pallas_call(
        _inproj_kernel,
        out_shape=tuple(jax.ShapeDtypeStruct((t, wd), dt) for wd, dt in zip(widths, dtypes)) + tuple(vt_shapes),
        grid=(t // tm,),
        in_specs=[
            pl.BlockSpec((tm, d), lambda i: (i, 0)),
            _mod_spec(mods, tpb),
            _const_spec((1, d)),
            _const_spec(w.shape),
            _const_spec(wg.shape),
            _const_spec(bg.shape),
            _const_spec(qk_gains.shape),
            pl.BlockSpec((tm, LANES), lambda i: (i % tpb, 0)),
            pl.BlockSpec((tm, LANES), lambda i: (i % tpb, 0)),
            _const_spec(bd.shape),
        ],
        out_specs=tuple(pl.BlockSpec((tm, wd), lambda i: (i, 0)) for wd in widths) + tuple(vt_specs),
        compiler_params=_cparams("parallel"),
        name="mix_inproj",
    )(x, mods, gain.reshape(1, d), w, wg, bg, qk_gains, cos, sin, bd)


def _gla_kernel(qf_ref, kf_ref, vf_ref, gf_ref, qb_ref, kb_ref, vb_ref, gb_ref, s0_ref,
                of_ref, ob_ref, sfin_ref, st_ref, *, n_chunks):
    c_len = GLA_CHUNK
    i = pl.program_id(1)
    dir_refs = ((qf_ref, kf_ref, vf_ref, gf_ref, of_ref), (qb_ref, kb_ref, vb_ref, gb_ref, ob_ref))
    signs = (1, -1)

    @pl.when(i == 0)
    def _():
        st_ref[...] = s0_ref[0]

    group = min(GLA_GROUP, n_chunks)
    g_len = group * c_len
    row = lax.broadcasted_iota(jnp.int32, (g_len, g_len), 0)
    col = lax.broadcasted_iota(jnp.int32, (g_len, g_len), 1)
    same_chunk = row // c_len == col // c_len
    cum_ops = [jnp.where(((row - col) * sg >= 0) & same_chunk, 1.0, 0.0).astype(BF16) for sg in signs]
    row4 = lax.broadcasted_iota(jnp.int32, (c_len, GLA_HEADS * c_len), 0)
    col4 = lax.broadcasted_iota(jnp.int32, (c_len, GLA_HEADS * c_len), 1) % c_len
    keep4s = [(row4 - col4) * sg >= 0 for sg in signs]
    qk_lane_head = _lane_iota((1, GLA_QK)) // GLA_DK
    v_lane_head = _lane_iota((1, GLA_V)) // GLA_DV
    qk_head_mask = [(qk_lane_head == h).astype(F32) for h in range(GLA_HEADS)]
    v_head_mask = [(v_lane_head == h).astype(BF16) for h in range(GLA_HEADS)]
    st_row_head = lax.broadcasted_iota(jnp.int32, (GLA_V, GLA_QK), 0) // GLA_DV
    st_col_head = lax.broadcasted_iota(jnp.int32, (GLA_V, GLA_QK), 1) // GLA_DK
    st_mask = st_row_head == st_col_head

    n_groups = n_chunks // group

    def group_starts(d, j):
        starts = []
        for p in range(group):
            c = j * group + p
            if d == 1:
                c = n_chunks - 1 - c
            starts.append(pl.multiple_of(c * c_len, c_len))
        return starts

    def decay_stage(d, j):
        q_ref, k_ref, _, g_ref, _ = dir_refs[d]
        starts = group_starts(d, j)
        q = jnp.concatenate([q_ref[pl.ds(r, c_len), :] for r in starts], axis=0).astype(F32)
        k = jnp.concatenate([k_ref[pl.ds(r, c_len), :] for r in starts], axis=0).astype(F32)
        g = jnp.concatenate([g_ref[pl.ds(r, c_len), :] for r in starts], axis=0)
        g_hi = g.astype(BF16)
        r1 = g - g_hi.astype(F32)
        g_mid = r1.astype(BF16)
        g_lo = (r1 - g_mid.astype(F32)).astype(BF16)
        b = _dot(cum_ops[d], g_hi) + _dot(cum_ops[d], g_mid) + _dot(cum_ops[d], g_lo)
        q_in = (q * jnp.exp(b)).astype(BF16)
        k_out = k * jnp.exp(-b)
        decays = jnp.concatenate([jnp.exp(jnp.sum(g[p * c_len:(p + 1) * c_len], axis=0, keepdims=True))
                                  for p in range(group)], axis=0)
        return q_in, k_out, decays

    def chunk_group(j, staged):
        j_next = jnp.minimum(j + 1, n_groups - 1)
        staged_next = tuple(decay_stage(d, j_next) for d in range(2))
        starts = [group_starts(d, j) for d in range(2)]
        vs = [[dir_refs[d][2][pl.ds(r, c_len), :] for r in starts[d]] for d in range(2)]
        sts = [st_ref[0], st_ref[1]]
        for p in range(group):
            rows = slice(p * c_len, (p + 1) * c_len)
            for d in range(2):
                q_in, k_out, decays = staged[d]
                v_p = vs[d][p]
                decay = decays[p:p + 1, :]
                k_out_p = k_out[rows]
                k_dec = (k_out_p * decay).astype(BF16)
                k_stack = jnp.concatenate([(k_out_p * qk_head_mask[h]).astype(BF16) for h in range(GLA_HEADS)],
                                          axis=0)
                a = _dot_nt(q_in[rows], k_stack)
                a = jnp.where(keep4s[d], a, 0.0).astype(BF16)
                v_bd = jnp.concatenate([v_p * v_head_mask[h] for h in range(GLA_HEADS)], axis=0)
                o = _dot(a, v_bd) + _dot_nt(q_in[rows], sts[d].astype(BF16))
                dir_refs[d][4][pl.ds(starts[d][p], c_len), :] = o
                ds_t = _dot_tn(v_p, k_dec)
                sts[d] = sts[d] * decay + jnp.where(st_mask, ds_t, 0.0)
        st_ref[0] = sts[0]
        st_ref[1] = sts[1]
        return staged_next

    lax.fori_loop(0, n_groups, chunk_group, tuple(decay_stage(d, 0) for d in range(2)))

    @pl.when(i == pl.num_programs(1) - 1)
    def _():
        sfin_ref[0] = st_ref[...]


def _gla(q, k, v, g, s0, rows_per_batch, tt_want=GLA_ROW_TILE):
    t = q.shape[0]
    nb = t // rows_per_batch
    tt = _row_tile(rows_per_batch, tt_want)
    nt = rows_per_batch // tt

    def fwd(lane_block):
        return lambda b, i: (b * nt + i, lane_block)

    def bwd(lane_block):
        return lambda b, i: (b * nt + nt - 1 - i, lane_block)

    state_spec = pl.BlockSpec((1, 2, GLA_V, GLA_QK), lambda b, i: (b, 0, 0, 0))
    return pl.pallas_call(
        functools.partial(_gla_kernel, n_chunks=tt // GLA_CHUNK),
        out_shape=(jax.ShapeDtypeStruct((t, GLA_V), F32), jax.ShapeDtypeStruct((t, GLA_V), F32),
                   jax.ShapeDtypeStruct((nb, 2, GLA_V, GLA_QK), F32)),
        grid=(nb, nt),
        in_specs=[
            pl.BlockSpec((tt, GLA_QK), fwd(0)), pl.BlockSpec((tt, GLA_QK), fwd(0)),
            pl.BlockSpec((tt, GLA_V), fwd(0)), pl.BlockSpec((tt, GLA_QK), fwd(0)),
            pl.BlockSpec((tt, GLA_QK), bwd(0)), pl.BlockSpec((tt, GLA_QK), bwd(0)),
            pl.BlockSpec((tt, GLA_V), bwd(0)), pl.BlockSpec((tt, GLA_QK), bwd(1)),
            state_spec,
        ],
        out_specs=(pl.BlockSpec((tt, GLA_V), fwd(0)), pl.BlockSpec((tt, GLA_V), bwd(0)), state_spec),
        scratch_shapes=[pltpu.VMEM((2, GLA_V, GLA_QK), F32)],
        compiler_params=_cparams("parallel", "arbitrary"),
        name="gla_scan",
    )(q, k, v, g, q, k, v, g, s0)


def _glb_kernel(q_ref, k_ref, vt_ref, kc_ref, vtc_ref, o_ref, qt_ref, m_ref, acc_ref, s_ref, *,
                n_pairs, col_block, chunks_per_tile):
    tq = q_ref.shape[0]
    n_lat = vt_ref.shape[2] // chunks_per_tile
    tk = vt_ref.shape[4] * chunks_per_tile
    n_heads = 2 * n_pairs
    for p in range(n_pairs):
        qt = (q_ref[:, p * LANES:(p + 1) * LANES].astype(F32) * LOG2_E).T
        qt_ref[2 * p] = qt[0:HEAD_DIM].astype(BF16)
        qt_ref[2 * p + 1] = qt[HEAD_DIM:2 * HEAD_DIM].astype(BF16)

    m_ref[...] = jnp.full(m_ref.shape, MASK_VALUE, F32)
    acc_ref[...] = jnp.zeros(acc_ref.shape, F32)

    units = [(h, cb) for h in range(n_heads) for cb in range(tq // col_block)]

    def scores(k, u):
        h, cb = units[u]
        return _dot(k[:, 0:HEAD_DIM], qt_ref[h, :, cb * col_block:(cb + 1) * col_block])

    def step(k, vt, k_next):
        w = k.shape[0]
        for u, (h, cb) in enumerate(units):
            s = s_ref[u, 0:w, :]
            if k_next is not None:
                s_ref[u, 0:k_next.shape[0], :] = scores(k_next, u)
            cols = slice(cb * col_block, (cb + 1) * col_block)
            m_prev = m_ref[h, :, cols]
            m_new = jnp.maximum(m_prev, jnp.max(s, axis=0, keepdims=True))
            alpha = jnp.exp2(m_prev - m_new)
            e = jnp.exp2((s - m_new).astype(BF16))
            m_ref[h, :, cols] = m_new
            ch = w // len(vt)
            pv = _dot(vt[0], e[0:ch, :])
            for c in range(1, len(vt)):
                pv = pv + _dot(vt[c], e[c * ch:(c + 1) * ch, :])
            acc_ref[h, :, cols] = acc_ref[h, :, cols] * alpha + pv

    def lat_tile(j):
        return k_ref[pl.ds(pl.multiple_of(j * tk, tk), tk), :]

    def lat_vt(j):
        return [vt_ref[0, 0, j * chunks_per_tile + c] for c in range(chunks_per_tile)]

    wc = vtc_ref.shape[4]
    ctx_tiles = [kc_ref[j * wc:(j + 1) * wc, :] for j in range(vtc_ref.shape[2])]

    for u in range(len(units)):
        s_ref[u, 0:tk, :] = scores(lat_tile(0), u)

    def body(j, carry):
        step(lat_tile(j), lat_vt(j), lat_tile(j + 1))
        return carry

    lax.fori_loop(0, n_lat - 1, body, 0)
    step(lat_tile(n_lat - 1), lat_vt(n_lat - 1), ctx_tiles[0])
    for j, kc in enumerate(ctx_tiles):
        step(kc, [vtc_ref[0, 0, j]], ctx_tiles[j + 1] if j + 1 < len(ctx_tiles) else None)

    for p in range(n_pairs):
        halves = [acc_ref[h, 0:HEAD_DIM, :] / acc_ref[h, HEAD_DIM:HEAD_DIM + 1, :] for h in (2 * p, 2 * p + 1)]
        o_ref[:, p * LANES:(p + 1) * LANES] = jnp.concatenate(halves, axis=0).T.astype(o_ref.dtype)


def _glb_attn(q, k, vt, kc, vtc, seq, ctx_len, tq_want=GLB_Q_TILE, col_block=GLB_COL_BLOCK, tk_want=KEY_CHUNK):
    t = q.shape[0]
    nb = t // seq
    groups = GLB_KV_HEADS
    gw = q.shape[1] // groups
    n_pairs = gw // LANES
    tq = _row_tile(seq, tq_want)
    nq = seq // tq
    col_block = min(col_block, tq)
    n_units = 2 * n_pairs * (tq // col_block)
    chunks_per_tile = max(1, min(tk_want, seq) // vt.shape[4])
    assert vt.shape[2] % chunks_per_tile == 0
    max_keys = max(vt.shape[4] * chunks_per_tile, vtc.shape[4])
    return pl.pallas_call(
        functools.partial(_glb_kernel, n_pairs=n_pairs, col_block=col_block, chunks_per_tile=chunks_per_tile),
        out_shape=jax.ShapeDtypeStruct(q.shape, BF16),
        grid=(nb, groups, nq),
        in_specs=[
            pl.BlockSpec((tq, gw), lambda b, g, i: (b * nq + i, g)),
            pl.BlockSpec((seq, LANES), lambda b, g, i: (b, g)),
            pl.BlockSpec((1, 1) + vt.shape[2:], lambda b, g, i: (b, g, 0, 0, 0)),
            pl.BlockSpec((ctx_len, LANES), lambda b, g, i: (b, g)),
            pl.BlockSpec((1, 1) + vtc.shape[2:], lambda b, g, i: (b, g, 0, 0, 0)),
        ],
        out_specs=pl.BlockSpec((tq, gw), lambda b, g, i: (b * nq + i, g)),
        scratch_shapes=[pltpu.VMEM((2 * n_pairs, HEAD_DIM, tq), BF16),
                        pltpu.VMEM((2 * n_pairs, 1, tq), F32),
                        pltpu.VMEM((2 * n_pairs, VT_ROWS, tq), F32),
                        pltpu.VMEM((n_units, max_keys, col_block), F32)],
        compiler_params=_cparams("parallel", "parallel", "arbitrary"),
        name="glb_attn",
    )(q, k, vt, kc, vtc)


def _win_kernel(sink_ref, q_ref, k_ref, vt_ref, kc_ref, vtc_ref, o_ref, sloc_ref, sctx_ref, bias_ref, *,
                tq, span, blocks_per_step):
    g = pl.program_id(1)
    seq = k_ref.shape[0]
    n_blocks = seq // tq
    n_steps = n_blocks // blocks_per_step
    wchunk = vt_ref.shape[4]
    kc = kc_ref[...][:, 0:HEAD_DIM]
    vt_ctx = jnp.concatenate([vtc_ref[0, 0, c] for c in range(vtc_ref.shape[2])], axis=1)
    col_minus_row = (lax.broadcasted_iota(jnp.int32, (span, tq), 1)
                     - lax.broadcasted_iota(jnp.int32, (span, tq), 0))
    for n in range(3):
        bias_ref[n] = jnp.where(jnp.abs(col_minus_row + n * WINDOW) <= WINDOW, 0.0, MASK_VALUE)

    def band_start(qb):
        return pl.multiple_of(jnp.clip(qb * tq - WINDOW, 0, seq - span), wchunk)

    def block_scores(qb):
        k_loc = k_ref[pl.ds(band_start(qb), span), :][:, 0:HEAD_DIM]
        qt = (q_ref[pl.ds(pl.multiple_of(qb * tq, tq), tq), :].astype(F32) * LOG2_E).T
        for half in range(2):
            qt_h = qt[half * HEAD_DIM:(half + 1) * HEAD_DIM].astype(BF16)
            yield _dot(k_loc, qt_h), _dot(kc, qt_h)

    for j in range(blocks_per_step):
        for half, (s_loc, s_ctx) in enumerate(block_scores(j)):
            sloc_ref[2 * j + half] = s_loc
            sctx_ref[2 * j + half] = s_ctx

    def step(i, carry):
        for j in range(blocks_per_step):
            qb = i * blocks_per_step + j
            q0 = pl.multiple_of(qb * tq, tq)
            start = band_start(qb)
            c0 = start // wchunk
            bias = bias_ref[(q0 - start) // WINDOW]
            vt_loc = jnp.concatenate([vt_ref[0, 0, c0 + c] for c in range(span // wchunk)], axis=1)
            ahead = block_scores(jnp.minimum(qb + blocks_per_step, n_blocks - 1))
            halves = []
            for half in range(2):
                u = 2 * j + half
                sink2 = sink_ref[2 * g + half] * LOG2_E
                s_loc = sloc_ref[u] + bias
                s_ctx = sctx_ref[u]
                sloc_ref[u], sctx_ref[u] = next(ahead)
                m = jnp.maximum(jnp.max(s_loc, axis=0, keepdims=True), jnp.max(s_ctx, axis=0, keepdims=True))
                m = jnp.maximum(m, sink2)
                e_loc = jnp.exp2((s_loc - m).astype(BF16))
                e_ctx = jnp.exp2((s_ctx - m).astype(BF16))
                acc = _dot(vt_loc, e_loc) + _dot(vt_ctx, e_ctx)
                den = acc[HEAD_DIM:HEAD_DIM + 1, :] + jnp.exp2(sink2 - m)
                halves.append(acc[0:HEAD_DIM, :] / den)
            o_ref[pl.ds(q0, tq), :] = jnp.concatenate(halves, axis=0).T.astype(o_ref.dtype)
        return carry

    lax.fori_loop(0, n_steps, step, 0)


def _win_attn(q, k, vt, kc, vtc, sink, seq, ctx_len, tq_want=WIN_Q_BLOCK, blocks_per_step=WIN_BLOCKS_PER_STEP):
    t = q.shape[0]
    nb = t // seq
    groups = WIN_KV_HEADS
    assert q.shape[1] == groups * LANES
    tq = _row_tile(seq, min(tq_want, max(seq - 2 * WINDOW, WINDOW)))
    span = tq + 2 * WINDOW
    assert seq >= span and span % vt.shape[4] == 0 and WINDOW % vt.shape[4] == 0 and tq % WINDOW == 0
    blocks_per_step = min(blocks_per_step, seq // tq)
    assert (seq // tq) % blocks_per_step == 0
    n_units = 2 * blocks_per_step
    return pl.pallas_call(
        functools.partial(_win_kernel, tq=tq, span=span, blocks_per_step=blocks_per_step),
        out_shape=jax.ShapeDtypeStruct(q.shape, BF16),
        grid=(nb, groups),
        in_specs=[
            pl.BlockSpec(memory_space=pltpu.SMEM),
            pl.BlockSpec((seq, LANES), lambda b, g: (b, g)),
            pl.BlockSpec((seq, LANES), lambda b, g: (b, g)),
            pl.BlockSpec((1, 1) + vt.shape[2:], lambda b, g: (b, g, 0, 0, 0)),
            pl.BlockSpec((ctx_len, LANES), lambda b, g: (b, g)),
            pl.BlockSpec((1, 1) + vtc.shape[2:], lambda b, g: (b, g, 0, 0, 0)),
        ],
        out_specs=pl.BlockSpec((seq, LANES), lambda b, g: (b, g)),
        scratch_shapes=[pltpu.VMEM((n_units, span, tq), F32), pltpu.VMEM((n_units, ctx_len, tq), F32),
                        pltpu.VMEM((3, span, tq), F32)],
        compiler_params=_cparams("parallel", "parallel"),
        name="win_attn",
    )(sink, q, k, vt, kc, vtc)


def _ctx_attn_kernel(sink_ref, q_ref, k_ref, vt_ref, o_ref, *, n_pairs, use_sink):
    g = pl.program_id(1)
    k = k_ref[...][:, 0:HEAD_DIM]
    vt = jnp.concatenate([vt_ref[0, 0, c] for c in range(vt_ref.shape[2])], axis=1)
    for p in range(n_pairs):
        qt = (q_ref[:, p * LANES:(p + 1) * LANES].astype(F32) * LOG2_E).T
        halves = []
        for half in range(2):
            s = _dot(k, qt[half * HEAD_DIM:(half + 1) * HEAD_DIM].astype(BF16))
            m = jnp.max(s, axis=0, keepdims=True)
            if use_sink:
                sink2 = sink_ref[(g * n_pairs + p) * 2 + half] * LOG2_E
                m = jnp.maximum(m, sink2)
            acc = _dot(vt, jnp.exp2((s - m).astype(BF16)))
            den = acc[HEAD_DIM:HEAD_DIM + 1, :]
            if use_sink:
                den = den + jnp.exp2(sink2 - m)
            halves.append(acc[0:HEAD_DIM, :] / den)
        o_ref[:, p * LANES:(p + 1) * LANES] = jnp.concatenate(halves, axis=0).T.astype(o_ref.dtype)


def _ctx_attn(q, k, vt, sink, groups, ctx_len, use_sink):
    t = q.shape[0]
    nb = t // ctx_len
    gw = q.shape[1] // groups
    n_pairs = gw // LANES
    return pl.pallas_call(
        functools.partial(_ctx_attn_kernel, n_pairs=n_pairs, use_sink=use_sink),
        out_shape=jax.ShapeDtypeStruct(q.shape, BF16),
        grid=(nb, groups),
        in_specs=[
            pl.BlockSpec(memory_space=pltpu.SMEM),
            pl.BlockSpec((ctx_len, gw), lambda b, g: (b, g)),
            pl.BlockSpec((ctx_len, LANES), lambda b, g: (b, g)),
            pl.BlockSpec((1, 1) + vt.shape[2:], lambda b, g: (b, g, 0, 0, 0)),
        ],
        out_specs=pl.BlockSpec((ctx_len, gw), lambda b, g: (b, g)),
        compiler_params=_cparams("parallel", "parallel"),
        name="ctx_attn",
    )(sink, q, k, vt)


def _outproj_ffn_kernel(x_ref, mod_ref, of_ref, ob_ref, r_ref, og_ref, ow_ref, gain_ref, bd_ref, w_ref,
                        g2_ref, win_ref, wout_ref, o_ref, *, f):
    o = of_ref[...] + ob_ref[...]
    r = r_ref[...].astype(F32)
    gate = r * jax.nn.sigmoid(r)
    ss = _dot((o * o).astype(BF16), bd_ref[...])
    on = o * lax.rsqrt(ss * (1.0 / GLA_DV) + EPS) * gain_ref[...]
    cat = jnp.concatenate([(on * gate).astype(BF16), og_ref[...], ow_ref[...]], axis=-1)
    x = x_ref[...] + mod_ref[0, 5:6, :] * _dot(cat, w_ref[...])
    o_ref[...] = _ffn_body(x, mod_ref, g2_ref, win_ref, wout_ref, 6, f)


def _outproj_ffn(x, mods, o_gla_f, o_gla_b, r, o_glb, o_win, gla_gain, bd, w_out, gain2, w2_in, w2_out,
                 rows_per_batch, tm_want=ROW_TILE):
    t, d = x.shape
    f = w2_out.shape[0]
    tm = _row_tile(rows_per_batch, tm_want)
    return pl.pallas_call(
        functools.partial(_outproj_ffn_kernel, f=f),
        out_shape=jax.ShapeDtypeStruct((t, d), F32),
        grid=(t // tm,),
        in_specs=[
            pl.BlockSpec((tm, d), lambda i: (i, 0)),
            _mod_spec(mods, rows_per_batch // tm),
            pl.BlockSpec((tm, GLA_V), lambda i: (i, 0)),
            pl.BlockSpec((tm, GLA_V), lambda i: (i, 0)),
            pl.BlockSpec((tm, GLA_V), lambda i: (i, 0)),
            pl.BlockSpec((tm, o_glb.shape[1]), lambda i: (i, 0)),
            pl.BlockSpec((tm, o_win.shape[1]), lambda i: (i, 0)),
            _const_spec(gla_gain.shape),
            _const_spec(bd.shape),
            _const_spec(w_out.shape),
            _const_spec((1, d)),
            _const_spec((d, 2 * f)),
            _const_spec((f, d)),
        ],
        out_specs=pl.BlockSpec((tm, d), lambda i: (i, 0)),
        compiler_params=_cparams("parallel"),
        name="mix_outproj_ffn",
    )(x, mods, o_gla_f, o_gla_b, r, o_glb, o_win, gla_gain, bd, w_out, gain2.reshape(1, d), w2_in, w2_out)


def _rope_tables(seq):
    rows = seq // GRID_W
    row = jnp.repeat(jnp.arange(rows, dtype=F32), GRID_W)
    col = (jnp.arange(rows * GRID_W) % GRID_W).astype(F32)
    n_freq = HEAD_DIM // 4
    inv = jnp.power(ROPE_BASE, -jnp.arange(n_freq, dtype=F32) / n_freq)
    ang = jnp.concatenate([row[:, None] * inv, col[:, None] * inv], axis=-1)
    cos, sin = jnp.cos(ang), jnp.sin(ang)
    cos_t = jnp.concatenate([cos, cos] * (LANES // HEAD_DIM), axis=-1)
    sin_t = jnp.concatenate([-sin, sin] * (LANES // HEAD_DIM), axis=-1)
    return cos_t, sin_t


def _block_diag_ones(n, block):
    idx = np.arange(n) // block
    return jnp.asarray(idx[:, None] == idx[None, :], dtype=BF16)


def kernel(x, c, ctx, c_ctx, mod_w, mod_b, norm_ffn1, ffn1_w_in, ffn1_w_out, norm_mix, mix_w_in, mix_w_out,
           gla_wg_f, gla_bg_f, gla_wg_b, gla_bg_b, gla_out_norm, glb_q_norm, glb_k_norm,
           win_q_norm, win_k_norm, win_sink, norm_ffn2, ffn2_w_in, ffn2_w_out):
    bsz, seq, d = x.shape
    ctx_len = ctx.shape[1]
    depth = mod_w.shape[0]
    in_splits = (GLA_QK, GLA_QK, GLA_V, GLA_V, 2 * GLA_GATE_RANK,
                 GLB_HEADS * HEAD_DIM, GLB_KV_HEADS * HEAD_DIM, GLB_KV_HEADS * HEAD_DIM,
                 WIN_HEADS * HEAD_DIM, WIN_KV_HEADS * HEAD_DIM, WIN_KV_HEADS * HEAD_DIM)

    n_rows = -(-(bsz + 1) // 8) * 8
    c_rows = jnp.concatenate([c, c_ctx[None, :], jnp.zeros((n_rows - bsz - 1, d), F32)], axis=0)
    mods = _modvec(c_rows, mod_w, mod_b)

    cos_l, sin_l = _rope_tables(seq)
    cos_c = jnp.ones((ctx_len, LANES), F32)
    sin_c = jnp.zeros((ctx_len, LANES), F32)
    bd = _block_diag_ones(SLAB, HEAD_DIM)
    zero_state = jnp.zeros((bsz, 2, GLA_V, GLA_QK), F32)

    xl = x.reshape(bsz * seq, d)
    xc = ctx.reshape(bsz * ctx_len, d)

    for l in range(depth):
        need_ctx = l < depth - 1
        mods_l = mods[l, :bsz].reshape(bsz, N_MOD, d)
        mods_c = mods[l, bsz:bsz + 1].reshape(1, N_MOD, d)
        w1_in, w1_out = ffn1_w_in[l].astype(BF16), ffn1_w_out[l].astype(BF16)
        w2_in, w2_out = ffn2_w_in[l].astype(BF16), ffn2_w_out[l].astype(BF16)
        wm = mix_w_in[l]
        src = dict(zip(("aq", "ak", "av", "ar", "ad", "gq", "gk", "gv", "wq", "wk", "wv"),
                       jnp.split(wm, np.cumsum(in_splits)[:-1], axis=1)))
        w_mix = jnp.concatenate([src[n] for n in ("aq", "ak", "av", "ar", "gq", "gk", "wk", "gv", "wv", "wq", "ad")]
                                + [jnp.zeros((d, GATE_PAD - 2 * GLA_GATE_RANK), F32)], axis=1).astype(BF16)
        assert w_mix.shape[1] == _C_END
        wg = jnp.zeros((GATE_PAD, 2 * GLA_QK), F32)
        wg = wg.at[:GLA_GATE_RANK, :GLA_QK].set(gla_wg_f[l])
        wg = wg.at[GLA_GATE_RANK:2 * GLA_GATE_RANK, GLA_QK:].set(gla_wg_b[l]).astype(BF16)
        bg = jnp.concatenate([gla_bg_f[l], gla_bg_b[l]])[None, :]
        qk_gains = jnp.stack([jnp.tile(gn[l], LANES // HEAD_DIM)
                              for gn in (glb_q_norm, glb_k_norm, win_q_norm, win_k_norm)])
        gla_gain = jnp.tile(gla_out_norm[l], GLA_HEADS)[None, :]
        w_out = mix_w_out[l].astype(BF16)
        sink = win_sink[l]

        xl = _ffn(xl, mods_l, 0, norm_ffn1[l], w1_in, w1_out, seq)
        xc = _ffn(xc, mods_c, 0, norm_ffn1[l], w1_in, w1_out, bsz * ctx_len)

        pc = _inproj(xc, mods_c, norm_mix[l], w_mix, wg, bg, qk_gains, cos_c, sin_c, bd, ctx_len)
        pq = _inproj(xl, mods_l, norm_mix[l], w_mix, wg, bg, qk_gains, cos_l, sin_l, bd, seq)
        aq, ak, av, ar, ag, gq, gk, wq, wk, gvt, wvt = pq
        aqc, akc, avc, arc, agc, gqc, gkc, wqc, wkc, gvtc, wvtc = pc

        oc_gla_f, oc_gla_b, states = _gla(aqc, akc, avc, agc, zero_state, ctx_len)
        o_gla_f, o_gla_b, _ = _gla(aq, ak, av, ag, states, seq)
        o_glb = _glb_attn(gq, gk, gvt, gkc, gvtc, seq, ctx_len)
        o_win = _win_attn(wq, wk, wvt, wkc, wvtc, sink, seq, ctx_len)
        xl = _outproj_ffn(xl, mods_l, o_gla_f, o_gla_b, ar, o_glb, o_win, gla_gain, bd, w_out, norm_ffn2[l],
                          w2_in, w2_out, seq)

        if need_ctx:
            oc_glb = _ctx_attn(gqc, gkc, gvtc, sink, GLB_KV_HEADS, ctx_len, use_sink=False)
            oc_win = _ctx_attn(wqc, wkc, wvtc, sink, WIN_KV_HEADS, ctx_len, use_sink=True)
            xc = _outproj_ffn(xc, mods_c, oc_gla_f, oc_gla_b, arc, oc_glb, oc_win, gla_gain, bd, w_out, norm_ffn2[l],
                              w2_in, w2_out, bsz * ctx_len)

    return xl.reshape(bsz, seq, d)
```

```python
import functools

import numpy as np
import jax
import jax.numpy as jnp
from jax import lax
from jax.experimental import pallas as pl
from jax.experimental.pallas import tpu as pltpu

GRID_W = 64
HEAD_DIM = 64
GLA_HEADS = 4
GLA_DK = 32
GLA_DV = 64
GLA_GATE_RANK = 16
GLA_GATE_TAU = 16.0
GLA_CHUNK = 64
GLB_HEADS = 8
GLB_KV_HEADS = 2
WIN_HEADS = 4
WIN_KV_HEADS = 2
WINDOW = 128
ROPE_BASE = 10000.0
N_MOD = 9
EPS = 1e-6

LANES = 128
VMEM_LIMIT_BYTES = 56 * 1024 * 1024

GLA_QK = GLA_HEADS * GLA_DK
GLA_V = GLA_HEADS * GLA_DV
GATE_PAD = LANES
MASK_VALUE = -1e30
LOG2_E = 1.4426950408889634
KEY_CHUNK = 256
ROW_TILE = 512
INPROJ_ROW_TILE = 1024
FFN_HIDDEN_CHUNK = 1536
GLA_ROW_TILE = 2048
GLB_Q_TILE = 4096
GLB_COL_BLOCK = 256
WIN_Q_BLOCK = 256
WIN_BLOCKS_PER_STEP = 4
GLA_GROUP = 4
WIN_CHUNK = 128
VT_ROWS = HEAD_DIM + 16

BF16 = jnp.bfloat16
F32 = jnp.float32


def _cparams(*sem):
    return pltpu.CompilerParams(dimension_semantics=sem, vmem_limit_bytes=VMEM_LIMIT_BYTES)


def _dot(a, b):
    return jnp.dot(a, b, preferred_element_type=F32)


def _dot_nt(a, b):
    return lax.dot_general(a, b, (((1,), (1,)), ((), ())), preferred_element_type=F32)


def _dot_tn(a, b):
    return lax.dot_general(a, b, (((0,), (0,)), ((), ())), preferred_element_type=F32)


def _lane_iota(shape):
    return lax.broadcasted_iota(jnp.int32, shape, len(shape) - 1)


def _modvec_kernel(c_ref, w_ref, b_ref, o_ref):
    c = c_ref[...]
    s = (c * jax.nn.sigmoid(c)).astype(BF16)
    o_ref[0] = _dot(s, w_ref[0].astype(BF16)) + b_ref[0]


def _modvec(c_rows, mod_w, mod_b):
    depth, d, n = mod_w.shape
    rows = c_rows.shape[0]
    tn = n // N_MOD
    return pl.pallas_call(
        _modvec_kernel,
        out_shape=jax.ShapeDtypeStruct((depth, rows, n), F32),
        grid=(depth, n // tn),
        in_specs=[
            pl.BlockSpec((rows, d), lambda l, j: (0, 0)),
            pl.BlockSpec((1, d, tn), lambda l, j: (l, 0, j)),
            pl.BlockSpec((1, 1, tn), lambda l, j: (l, 0, j)),
        ],
        out_specs=pl.BlockSpec((1, rows, tn), lambda l, j: (l, 0, j)),
        compiler_params=_cparams("arbitrary", "arbitrary"),
        name="modvec",
    )(c_rows, mod_w, mod_b.reshape(depth, 1, n))


def _norm_modulate(x, gain, mod_ref, k_shift):
    shift = mod_ref[0, k_shift:k_shift + 1, :]
    scale = mod_ref[0, k_shift + 1:k_shift + 2, :]
    y = x * lax.rsqrt(jnp.mean(x * x, axis=-1, keepdims=True) + EPS)
    return (y * gain) * (1.0 + scale) + shift


def _mod_spec(mods, tiles_per_batch):
    nb, nm, d = mods.shape
    if nb == 1:
        return pl.BlockSpec((1, nm, d), lambda i: (0, 0, 0))
    return pl.BlockSpec((1, nm, d), lambda i: (i // tiles_per_batch, 0, 0))


def _const_spec(shape):
    nd = len(shape)
    return pl.BlockSpec(shape, lambda *_: (0,) * nd, pipeline_mode=pl.Buffered(1))


def _row_tile(n_rows_per_batch, want):
    t = min(want, n_rows_per_batch)
    assert n_rows_per_batch % t == 0
    return t


def _ffn_chunks(f):
    step = min(FFN_HIDDEN_CHUNK, f)
    return tuple((lo, min(lo + step, f)) for lo in range(0, f, step))


def _ffn_kernel(x_ref, mod_ref, g_ref, win_ref, wout_ref, o_ref, *, k0, f):
    o_ref[...] = _ffn_body(x_ref[...], mod_ref, g_ref, win_ref, wout_ref, k0, f)


def _ffn_body(x, mod_ref, g_ref, win_ref, wout_ref, k0, f):
    halves = 2 if x.shape[0] % 16 == 0 else 1
    rows = x.shape[0] // halves
    return jnp.concatenate([_ffn_rows(x[h * rows:(h + 1) * rows], mod_ref, g_ref, win_ref, wout_ref, k0, f)
                            for h in range(halves)], axis=0)


def _ffn_rows(x, mod_ref, g_ref, win_ref, wout_ref, k0, f):
    hb = _norm_modulate(x, g_ref[...], mod_ref, k0).astype(BF16)
    gate = mod_ref[0, k0 + 2:k0 + 3, :]
    acc = None
    for lo, hi in _ffn_chunks(f):
        a = _dot(hb, win_ref[:, lo:hi])
        b = _dot(hb, win_ref[:, f + lo:f + hi])
        u = ((a * jax.nn.sigmoid(a)) * b).astype(BF16)
        part = _dot(u, wout_ref[lo:hi, :])
        acc = part if acc is None else acc + part
    return x + (0.5 * gate) * acc


def _ffn(x, mods, k0, gain, w_in, w_out, rows_per_batch, tm_want=ROW_TILE):
    t, d = x.shape
    f = w_out.shape[0]
    tm = _row_tile(rows_per_batch, tm_want)
    return pl.pallas_call(
        functools.partial(_ffn_kernel, k0=k0, f=f),
        out_shape=jax.ShapeDtypeStruct((t, d), F32),
        grid=(t // tm,),
        in_specs=[
            pl.BlockSpec((tm, d), lambda i: (i, 0)),
            _mod_spec(mods, rows_per_batch // tm),
            _const_spec((1, d)),
            _const_spec((d, 2 * f)),
            _const_spec((f, d)),
        ],
        out_specs=pl.BlockSpec((tm, d), lambda i: (i, 0)),
        compiler_params=_cparams("parallel"),
        name="ffn",
    )(x, mods, gain.reshape(1, d), w_in, w_out)


_C_AQ = 0
_C_AK = _C_AQ + GLA_QK
_C_AV = _C_AK + GLA_QK
_C_AR = _C_AV + GLA_V
_C_GQ = _C_AR + GLA_V
_C_GK = _C_GQ + GLB_HEADS * HEAD_DIM
_C_WK = _C_GK + GLB_KV_HEADS * HEAD_DIM
_C_GV = _C_WK + WIN_KV_HEADS * HEAD_DIM
_C_WV = _C_GV + GLB_KV_HEADS * HEAD_DIM
_C_WQ = _C_WV + WIN_KV_HEADS * HEAD_DIM
_C_AD = _C_WQ + WIN_HEADS * HEAD_DIM
_C_END = _C_AD + GATE_PAD
SLAB = 2 * LANES


def _norm_rope_slab(x, bd_ref, gains, cos, sin_signed, out_scales):
    ss = _dot((x * x).astype(BF16), bd_ref[...])
    first_half = (_lane_iota((x.shape[0], LANES)) % HEAD_DIM) < (HEAD_DIM // 2)
    outs = []
    for j in range(2):
        sl = slice(j * LANES, (j + 1) * LANES)
        xn = x[:, sl] * lax.rsqrt(ss[:, sl] * (1.0 / HEAD_DIM) + EPS) * gains[j]
        partner = jnp.where(first_half, pltpu.roll(xn, LANES - HEAD_DIM // 2, 1), pltpu.roll(xn, HEAD_DIM // 2, 1))
        out = xn * cos + partner * sin_signed
        outs.append(out * out_scales[j] if out_scales[j] != 1.0 else out)
    return outs


def _dup_heads(x):
    swapped = pltpu.roll(x, HEAD_DIM, 1)
    low = _lane_iota(x.shape) < HEAD_DIM
    return jnp.where(low, x, swapped), jnp.where(low, swapped, x)


def _store_vt(vt_ref, v):
    chunk = vt_ref.shape[4]
    for c in range(vt_ref.shape[2]):
        vt = v[c * chunk:(c + 1) * chunk, :].T
        for kv in range(vt_ref.shape[1]):
            vt_ref[0, kv, c, 0:HEAD_DIM, :] = vt[kv * HEAD_DIM:(kv + 1) * HEAD_DIM].astype(BF16)
            vt_ref[0, kv, c, HEAD_DIM:VT_ROWS, :] = jnp.ones((VT_ROWS - HEAD_DIM, chunk), BF16)


def _inproj_kernel(x_ref, mod_ref, g_ref, w_ref, wg_ref, bg_ref, qkg_ref, cos_ref, sin_ref, bd_ref,
                   aq_ref, ak_ref, av_ref, ar_ref, ag_ref, gq_ref, gk_ref, wq_ref, wk_ref, gvt_ref, wvt_ref):
    hb = _norm_modulate(x_ref[...], g_ref[...], mod_ref, 3).astype(BF16)
    cos = cos_ref[...]
    sin = sin_ref[...]
    tiles = {}

    def proj(lo, width):
        j, off = divmod(lo, SLAB)
        assert off + width <= SLAB
        if j not in tiles:
            tiles[j] = _dot(hb, w_ref[:, j * SLAB:min((j + 1) * SLAB, _C_END)])
        return tiles[j][:, off:off + width]

    q_scale = 1.0
    g_glb_q, g_glb_k, g_win_q, g_win_k = (qkg_ref[r:r + 1, :] for r in range(4))

    order = (_C_GQ, _C_GQ + SLAB, _C_WQ, _C_GK, _C_GV, _C_AD, _C_AQ, _C_AV, _C_AR)

    def issue_ahead(n):
        for lo in order[:n + 3]:
            proj(lo, LANES)

    for j in range(GLB_HEADS * HEAD_DIM // SLAB):
        issue_ahead(j)
        halves = _norm_rope_slab(proj(_C_GQ + j * SLAB, SLAB), bd_ref, (g_glb_q, g_glb_q), cos, sin,
                                 (q_scale, q_scale))
        for i, q in enumerate(halves):
            gq_ref[:, j * SLAB + i * LANES:j * SLAB + (i + 1) * LANES] = q.astype(BF16)
    issue_ahead(2)
    halves = _norm_rope_slab(proj(_C_WQ, SLAB), bd_ref, (g_win_q, g_win_q), cos, sin, (q_scale, q_scale))
    for i, q in enumerate(halves):
        wq_ref[:, i * LANES:(i + 1) * LANES] = q.astype(BF16)
    issue_ahead(3)
    k_glb, k_win = _norm_rope_slab(proj(_C_GK, SLAB), bd_ref, (g_glb_k, g_win_k), cos, sin, (1.0, 1.0))
    k0, k1 = _dup_heads(k_glb)
    gk_ref[:, 0:LANES] = k0.astype(BF16)
    gk_ref[:, LANES:2 * LANES] = k1.astype(BF16)
    k0, k1 = _dup_heads(k_win)
    wk_ref[:, 0:LANES] = k0.astype(BF16)
    wk_ref[:, LANES:2 * LANES] = k1.astype(BF16)

    issue_ahead(4)
    _store_vt(gvt_ref, proj(_C_GV, LANES))
    _store_vt(wvt_ref, proj(_C_WV, LANES))

    issue_ahead(len(order))
    z = _dot(proj(_C_AD, GATE_PAD).astype(BF16), wg_ref[...]) + bg_ref[...]
    log_sig = jnp.minimum(z, 0.0) - jnp.log1p(jnp.exp(-jnp.abs(z)))
    ag_ref[...] = log_sig * (1.0 / GLA_GATE_TAU)
    aq_ref[...] = (proj(_C_AQ, GLA_QK) * (GLA_DK ** -0.5)).astype(BF16)
    ak_ref[...] = proj(_C_AK, GLA_QK).astype(BF16)
    av_ref[...] = proj(_C_AV, GLA_V).astype(BF16)
    ar_ref[...] = proj(_C_AR, GLA_V).astype(BF16)


def _inproj(x, mods, gain, w, wg, bg, qk_gains, cos, sin, bd, rows_per_batch, tm_want=INPROJ_ROW_TILE):
    t, d = x.shape
    tm = _row_tile(rows_per_batch, tm_want)
    tpb = rows_per_batch // tm
    widths = (GLA_QK, GLA_QK, GLA_V, GLA_V, 2 * GLA_QK, GLB_HEADS * HEAD_DIM, 2 * LANES,
              WIN_HEADS * HEAD_DIM, 2 * LANES)
    dtypes = (BF16, BF16, BF16, BF16, F32, BF16, BF16, BF16, BF16)
    nb = t // rows_per_batch
    vt_shapes, vt_specs = [], []
    for kv_heads, want in ((GLB_KV_HEADS, KEY_CHUNK), (WIN_KV_HEADS, WIN_CHUNK)):
        chunk = min(want, tm)
        vt_shapes.append(jax.ShapeDtypeStruct((nb, kv_heads, rows_per_batch // chunk, VT_ROWS, chunk), BF16))
        vt_specs.append(pl.BlockSpec((1, kv_heads, tm // chunk, VT_ROWS, chunk),
                                     lambda i: (i // tpb, 0, i % tpb, 0, 0)))
    return pl.pallas_call(
        _inproj_kernel,
        out_shape=tuple(jax.ShapeDtypeStruct((t, wd), dt) for wd, dt in zip(widths, dtypes)) + tuple(vt_shapes),
        grid=(t // tm,),
        in_specs=[
            pl.BlockSpec((tm, d), lambda i: (i, 0)),
            _mod_spec(mods, tpb),
            _const_spec((1, d)),
            _const_spec(w.shape),
            _const_spec(wg.shape),
            _const_spec(bg.shape),
            _const_spec(qk_gains.shape),
            pl.BlockSpec((tm, LANES), lambda i: (i % tpb, 0)),
            pl.BlockSpec((tm, LANES), lambda i: (i % tpb, 0)),
            _const_spec(bd.shape),
        ],
        out_specs=tuple(pl.BlockSpec((tm, wd), lambda i: (i, 0)) for wd in widths) + tuple(vt_specs),
        compiler_params=_cparams("parallel"),
        name="mix_inproj",
    )(x, mods, gain.reshape(1, d), w, wg, bg, qk_gains, cos, sin, bd)


def _gla_kernel(qf_ref, kf_ref, vf_ref, gf_ref, qb_ref, kb_ref, vb_ref, gb_ref, s0_ref,
                of_ref, ob_ref, sfin_ref, st_ref, *, n_chunks):
    c_len = GLA_CHUNK
    i = pl.program_id(1)
    dir_refs = ((qf_ref, kf_ref, vf_ref, gf_ref, of_ref), (qb_ref, kb_ref, vb_ref, gb_ref, ob_ref))
    signs = (1, -1)

    @pl.when(i == 0)
    def _():
        st_ref[...] = s0_ref[0]

    group = min(GLA_GROUP, n_chunks)
    g_len = group * c_len
    row = lax.broadcasted_iota(jnp.int32, (g_len, g_len), 0)
    col = lax.broadcasted_iota(jnp.int32, (g_len, g_len), 1)
    same_chunk = row // c_len == col // c_len
    cum_ops = [jnp.where(((row - col) * sg >= 0) & same_chunk, 1.0, 0.0).astype(BF16) for sg in signs]
    row4 = lax.broadcasted_iota(jnp.int32, (c_len, GLA_HEADS * c_len), 0)
    col4 = lax.broadcasted_iota(jnp.int32, (c_len, GLA_HEADS * c_len), 1) % c_len
    keep4s = [(row4 - col4) * sg >= 0 for sg in signs]
    qk_lane_head = _lane_iota((1, GLA_QK)) // GLA_DK
    v_lane_head = _lane_iota((1, GLA_V)) // GLA_DV
    qk_head_mask = [(qk_lane_head == h).astype(F32) for h in range(GLA_HEADS)]
    v_head_mask = [(v_lane_head == h).astype(BF16) for h in range(GLA_HEADS)]
    st_row_head = lax.broadcasted_iota(jnp.int32, (GLA_V, GLA_QK), 0) // GLA_DV
    st_col_head = lax.broadcasted_iota(jnp.int32, (GLA_V, GLA_QK), 1) // GLA_DK
    st_mask = st_row_head == st_col_head

    n_groups = n_chunks // group

    def group_starts(d, j):
        starts = []
        for p in range(group):
            c = j * group + p
            if d == 1:
                c = n_chunks - 1 - c
            starts.append(pl.multiple_of(c * c_len, c_len))
        return starts

    def decay_stage(d, j):
        q_ref, k_ref, _, g_ref, _ = dir_refs[d]
        starts = group_starts(d, j)
        q = jnp.concatenate([q_ref[pl.ds(r, c_len), :] for r in starts], axis=0).astype(F32)
        k = jnp.concatenate([k_ref[pl.ds(r, c_len), :] for r in starts], axis=0).astype(F32)
        g = jnp.concatenate([g_ref[pl.ds(r, c_len), :] for r in starts], axis=0)
        g_hi = g.astype(BF16)
        r1 = g - g_hi.astype(F32)
        g_mid = r1.astype(BF16)
        g_lo = (r1 - g_mid.astype(F32)).astype(BF16)
        b = _dot(cum_ops[d], g_hi) + _dot(cum_ops[d], g_mid) + _dot(cum_ops[d], g_lo)
        q_in = (q * jnp.exp(b)).astype(BF16)
        k_out = k * jnp.exp(-b)
        decays = jnp.concatenate([jnp.exp(jnp.sum(g[p * c_len:(p + 1) * c_len], axis=0, keepdims=True))
                                  for p in range(group)], axis=0)
        return q_in, k_out, decays

    def chunk_group(j, staged):
        j_next = jnp.minimum(j + 1, n_groups - 1)
        staged_next = tuple(decay_stage(d, j_next) for d in range(2))
        starts = [group_starts(d, j) for d in range(2)]
        vs = [[dir_refs[d][2][pl.ds(r, c_len), :] for r in starts[d]] for d in range(2)]
        sts = [st_ref[0], st_ref[1]]
        for p in range(group):
            rows = slice(p * c_len, (p + 1) * c_len)
            for d in range(2):
                q_in, k_out, decays = staged[d]
                v_p = vs[d][p]
                decay = decays[p:p + 1, :]
                k_out_p = k_out[rows]
                k_dec = (k_out_p * decay).astype(BF16)
                k_stack = jnp.concatenate([(k_out_p * qk_head_mask[h]).astype(BF16) for h in range(GLA_HEADS)],
                                          axis=0)
                a = _dot_nt(q_in[rows], k_stack)
                a = jnp.where(keep4s[d], a, 0.0).astype(BF16)
                v_bd = jnp.concatenate([v_p * v_head_mask[h] for h in range(GLA_HEADS)], axis=0)
                o = _dot(a, v_bd) + _dot_nt(q_in[rows], sts[d].astype(BF16))
                dir_refs[d][4][pl.ds(starts[d][p], c_len), :] = o
                ds_t = _dot_tn(v_p, k_dec)
                sts[d] = sts[d] * decay + jnp.where(st_mask, ds_t, 0.0)
        st_ref[0] = sts[0]
        st_ref[1] = sts[1]
        return staged_next

    lax.fori_loop(0, n_groups, chunk_group, tuple(decay_stage(d, 0) for d in range(2)))

    @pl.when(i == pl.num_programs(1) - 1)
    def _():
        sfin_ref[0] = st_ref[...]


def _gla(q, k, v, g, s0, rows_per_batch, tt_want=GLA_ROW_TILE):
    t = q.shape[0]
    nb = t // rows_per_batch
    tt = _row_tile(rows_per_batch, tt_want)
    nt = rows_per_batch // tt

    def fwd(lane_block):
        return lambda b, i: (b * nt + i, lane_block)

    def bwd(lane_block):
        return lambda b, i: (b * nt + nt - 1 - i, lane_block)

    state_spec = pl.BlockSpec((1, 2, GLA_V, GLA_QK), lambda b, i: (b, 0, 0, 0))
    return pl.pallas_call(
        functools.partial(_gla_kernel, n_chunks=tt // GLA_CHUNK),
        out_shape=(jax.ShapeDtypeStruct((t, GLA_V), F32), jax.ShapeDtypeStruct((t, GLA_V), F32),
                   jax.ShapeDtypeStruct((nb, 2, GLA_V, GLA_QK), F32)),
        grid=(nb, nt),
        in_specs=[
            pl.BlockSpec((tt, GLA_QK), fwd(0)), pl.BlockSpec((tt, GLA_QK), fwd(0)),
            pl.BlockSpec((tt, GLA_V), fwd(0)), pl.BlockSpec((tt, GLA_QK), fwd(0)),
            pl.BlockSpec((tt, GLA_QK), bwd(0)), pl.BlockSpec((tt, GLA_QK), bwd(0)),
            pl.BlockSpec((tt, GLA_V), bwd(0)), pl.BlockSpec((tt, GLA_QK), bwd(1)),
            state_spec,
        ],
        out_specs=(pl.BlockSpec((tt, GLA_V), fwd(0)), pl.BlockSpec((tt, GLA_V), bwd(0)), state_spec),
        scratch_shapes=[pltpu.VMEM((2, GLA_V, GLA_QK), F32)],
        compiler_params=_cparams("parallel", "arbitrary"),
        name="gla_scan",
    )(q, k, v, g, q, k, v, g, s0)


def _glb_kernel(q_ref, k_ref, vt_ref, kc_ref, vtc_ref, o_ref, qt_ref, m_ref, acc_ref, s_ref, *,
                n_pairs, col_block, chunks_per_tile):
    tq = q_ref.shape[0]
    n_lat = vt_ref.shape[2] // chunks_per_tile
    tk = vt_ref.shape[4] * chunks_per_tile
    n_heads = 2 * n_pairs
    for p in range(n_pairs):
        qt = q_ref[:, p * LANES:(p + 1) * LANES].astype(F32).T
        qt_ref[2 * p] = qt[0:HEAD_DIM].astype(BF16)
        qt_ref[2 * p + 1] = qt[HEAD_DIM:2 * HEAD_DIM].astype(BF16)

    m_ref[...] = jnp.full(m_ref.shape, MASK_VALUE, F32)
    acc_ref[...] = jnp.zeros(acc_ref.shape, F32)

    units = [(h, cb) for h in range(n_heads) for cb in range(tq // col_block)]

    def scores(k, u):
        h, cb = units[u]
        return _dot(k[:, 0:HEAD_DIM], qt_ref[h, :, cb * col_block:(cb + 1) * col_block])

    def step(k, vt, k_next):
        w = k.shape[0]
        for u, (h, cb) in enumerate(units):
            s = s_ref[u, 0:w, :]
            if k_next is not None:
                s_ref[u, 0:k_next.shape[0], :] = scores(k_next, u)
            cols = slice(cb * col_block, (cb + 1) * col_block)
            m_prev = m_ref[h, :, cols]
            m_new = jnp.maximum(m_prev, jnp.max(s, axis=0, keepdims=True))
            alpha = jnp.exp2(m_prev - m_new)
            e = jnp.exp2((s - m_new).astype(BF16))
            m_ref[h, :, cols] = m_new
            ch = w // len(vt)
            pv = _dot(vt[0], e[0:ch, :])
            for c in range(1, len(vt)):
                pv = pv + _dot(vt[c], e[c * ch:(c + 1) * ch, :])
            acc_ref[h, :, cols] = acc_ref[h, :, cols] * alpha + pv

    def lat_tile(j):
        return k_ref[pl.ds(pl.multiple_of(j * tk, tk), tk), :]

    def lat_vt(j):
        return [vt_ref[0, 0, j * chunks_per_tile + c] for c in range(chunks_per_tile)]

    wc = vtc_ref.shape[4]
    ctx_tiles = [kc_ref[j * wc:(j + 1) * wc, :] for j in range(vtc_ref.shape[2])]

    for u in range(len(units)):
        s_ref[u, 0:tk, :] = scores(lat_tile(0), u)

    def body(j, carry):
        step(lat_tile(j), lat_vt(j), lat_tile(j + 1))
        return carry

    lax.fori_loop(0, n_lat - 1, body, 0)
    step(lat_tile(n_lat - 1), lat_vt(n_lat - 1), ctx_tiles[0])
    for j, kc in enumerate(ctx_tiles):
        step(kc, [vtc_ref[0, 0, j]], ctx_tiles[j + 1] if j + 1 < len(ctx_tiles) else None)

    for p in range(n_pairs):
        halves = [acc_ref[h, 0:HEAD_DIM, :] / acc_ref[h, HEAD_DIM:HEAD_DIM + 1, :] for h in (2 * p, 2 * p + 1)]
        o_ref[:, p * LANES:(p + 1) * LANES] = jnp.concatenate(halves, axis=0).T.astype(o_ref.dtype)


def _glb_attn(q, k, vt, kc, vtc, seq, ctx_len, tq_want=GLB_Q_TILE, col_block=GLB_COL_BLOCK, tk_want=KEY_CHUNK):
    t = q.shape[0]
    nb = t // seq
    groups = GLB_KV_HEADS
    gw = q.shape[1] // groups
    n_pairs = gw // LANES
    tq = _row_tile(seq, tq_want)
    nq = seq // tq
    col_block = min(col_block, tq)
    n_units = 2 * n_pairs * (tq // col_block)
    chunks_per_tile = max(1, min(tk_want, seq) // vt.shape[4])
    assert vt.shape[2] % chunks_per_tile == 0
    max_keys = max(vt.shape[4] * chunks_per_tile, vtc.shape[4])
    return pl.pallas_call(
        functools.partial(_glb_kernel, n_pairs=n_pairs, col_block=col_block, chunks_per_tile=chunks_per_tile),
        out_shape=jax.ShapeDtypeStruct(q.shape, BF16),
        grid=(nb, groups, nq),
        in_specs=[
            pl.BlockSpec((tq, gw), lambda b, g, i: (b * nq + i, g)),
            pl.BlockSpec((seq, LANES), lambda b, g, i: (b, g)),
            pl.BlockSpec((1, 1) + vt.shape[2:], lambda b, g, i: (b, g, 0, 0, 0)),
            pl.BlockSpec((ctx_len, LANES), lambda b, g, i: (b, g)),
            pl.BlockSpec((1, 1) + vtc.shape[2:], lambda b, g, i: (b, g, 0, 0, 0)),
        ],
        out_specs=pl.BlockSpec((tq, gw), lambda b, g, i: (b * nq + i, g)),
        scratch_shapes=[pltpu.VMEM((2 * n_pairs, HEAD_DIM, tq), BF16),
                        pltpu.VMEM((2 * n_pairs, 1, tq), F32),
                        pltpu.VMEM((2 * n_pairs, VT_ROWS, tq), F32),
                        pltpu.VMEM((n_units, max_keys, col_block), F32)],
        compiler_params=_cparams("parallel", "parallel", "arbitrary"),
        name="glb_attn",
    )(q, k, vt, kc, vtc)


def _win_kernel(sink_ref, q_ref, k_ref, vt_ref, kc_ref, vtc_ref, o_ref, sloc_ref, sctx_ref, bias_ref, *,
                tq, span, blocks_per_step):
    g = pl.program_id(1)
    seq = k_ref.shape[0]
    n_blocks = seq // tq
    n_steps = n_blocks // blocks_per_step
    wchunk = vt_ref.shape[4]
    kc = kc_ref[...][:, 0:HEAD_DIM]
    vt_ctx = jnp.concatenate([vtc_ref[0, 0, c] for c in range(vtc_ref.shape[2])], axis=1)
    col_minus_row = (lax.broadcasted_iota(jnp.int32, (span, tq), 1)
                     - lax.broadcasted_iota(jnp.int32, (span, tq), 0))
    for n in range(3):
        bias_ref[n] = jnp.where(jnp.abs(col_minus_row + n * WINDOW) <= WINDOW, 0.0, MASK_VALUE)

    def band_start(qb):
        return pl.multiple_of(jnp.clip(qb * tq - WINDOW, 0, seq - span), wchunk)

    def block_scores(qb):
        k_loc = k_ref[pl.ds(band_start(qb), span), :][:, 0:HEAD_DIM]
        qt = q_ref[pl.ds(pl.multiple_of(qb * tq, tq), tq), :].astype(F32).T
        for half in range(2):
            qt_h = qt[half * HEAD_DIM:(half + 1) * HEAD_DIM].astype(BF16)
            yield _dot(k_loc, qt_h), _dot(kc, qt_h)

    for j in range(blocks_per_step):
        for half, (s_loc, s_ctx) in enumerate(block_scores(j)):
            sloc_ref[2 * j + half] = s_loc
            sctx_ref[2 * j + half] = s_ctx

    def step(i, carry):
        for j in range(blocks_per_step):
            qb = i * blocks_per_step + j
            q0 = pl.multiple_of(qb * tq, tq)
            start = band_start(qb)
            c0 = start // wchunk
            bias = bias_ref[(q0 - start) // WINDOW]
            vt_loc = jnp.concatenate([vt_ref[0, 0, c0 + c] for c in range(span // wchunk)], axis=1)
            ahead = block_scores(jnp.minimum(qb + blocks_per_step, n_blocks - 1))
            halves = []
            for half in range(2):
                u = 2 * j + half
                sink2 = sink_ref[2 * g + half] * LOG2_E
                s_loc = sloc_ref[u] + bias
                s_ctx = sctx_ref[u]
                sloc_ref[u], sctx_ref[u] = next(ahead)
                m = jnp.maximum(jnp.max(s_loc, axis=0, keepdims=True), jnp.max(s_ctx, axis=0, keepdims=True))
                m = jnp.maximum(m, sink2)
                e_loc = jnp.exp2((s_loc - m).astype(BF16))
                e_ctx = jnp.exp2((s_ctx - m).astype(BF16))
                acc = _dot(vt_loc, e_loc) + _dot(vt_ctx, e_ctx)
                den = acc[HEAD_DIM:HEAD_DIM + 1, :] + jnp.exp2(sink2 - m)
                halves.append(acc[0:HEAD_DIM, :] / den)
            o_ref[pl.ds(q0, tq), :] = jnp.concatenate(halves, axis=0).T.astype(o_ref.dtype)
        return carry

    lax.fori_loop(0, n_steps, step, 0)


def _win_attn(q, k, vt, kc, vtc, sink, seq, ctx_len, tq_want=WIN_Q_BLOCK, blocks_per_step=WIN_BLOCKS_PER_STEP):
    t = q.shape[0]
    nb = t // seq
    groups = WIN_KV_HEADS
    assert q.shape[1] == groups * LANES
    tq = _row_tile(seq, min(tq_want, max(seq - 2 * WINDOW, WINDOW)))
    span = tq + 2 * WINDOW
    assert seq >= span and span % vt.shape[4] == 0 and WINDOW % vt.shape[4] == 0 and tq % WINDOW == 0
    blocks_per_step = min(blocks_per_step, seq // tq)
    assert (seq // tq) % blocks_per_step == 0
    n_units = 2 * blocks_per_step
    return pl.pallas_call(
        functools.partial(_win_kernel, tq=tq, span=span, blocks_per_step=blocks_per_step),
        out_shape=jax.ShapeDtypeStruct(q.shape, BF16),
        grid=(nb, groups),
        in_specs=[
            pl.BlockSpec(memory_space=pltpu.SMEM),
            pl.BlockSpec((seq, LANES), lambda b, g: (b, g)),
            pl.BlockSpec((seq, LANES), lambda b, g: (b, g)),
            pl.BlockSpec((1, 1) + vt.shape[2:], lambda b, g: (b, g, 0, 0, 0)),
            pl.BlockSpec((ctx_len, LANES), lambda b, g: (b, g)),
            pl.BlockSpec((1, 1) + vtc.shape[2:], lambda b, g: (b, g, 0, 0, 0)),
        ],
        out_specs=pl.BlockSpec((seq, LANES), lambda b, g: (b, g)),
        scratch_shapes=[pltpu.VMEM((n_units, span, tq), F32), pltpu.VMEM((n_units, ctx_len, tq), F32),
                        pltpu.VMEM((3, span, tq), F32)],
        compiler_params=_cparams("parallel", "parallel"),
        name="win_attn",
    )(sink, q, k, vt, kc, vtc)


def _ctx_attn_kernel(sink_ref, q_ref, k_ref, vt_ref, o_ref, *, n_pairs, use_sink):
    g = pl.program_id(1)
    k = k_ref[...][:, 0:HEAD_DIM]
    vt = jnp.concatenate([vt_ref[0, 0, c] for c in range(vt_ref.shape[2])], axis=1)
    for p in range(n_pairs):
        qt = q_ref[:, p * LANES:(p + 1) * LANES].astype(F32).T
        halves = []
        for half in range(2):
            s = _dot(k, qt[half * HEAD_DIM:(half + 1) * HEAD_DIM].astype(BF16))
            m = jnp.max(s, axis=0, keepdims=True)
            if use_sink:
                sink2 = sink_ref[(g * n_pairs + p) * 2 + half] * LOG2_E
                m = jnp.maximum(m, sink2)
            acc = _dot(vt, jnp.exp2((s - m).astype(BF16)))
            den = acc[HEAD_DIM:HEAD_DIM + 1, :]
            if use_sink:
                den = den + jnp.exp2(sink2 - m)
            halves.append(acc[0:HEAD_DIM, :] / den)
        o_ref[:, p * LANES:(p + 1) * LANES] = jnp.concatenate(halves, axis=0).T.astype(o_ref.dtype)


def _ctx_attn(q, k, vt, sink, groups, ctx_len, use_sink):
    t = q.shape[0]
    nb = t // ctx_len
    gw = q.shape[1] // groups
    n_pairs = gw // LANES
    return pl.pallas_call(
        functools.partial(_ctx_attn_kernel, n_pairs=n_pairs, use_sink=use_sink),
        out_shape=jax.ShapeDtypeStruct(q.shape, BF16),
        grid=(nb, groups),
        in_specs=[
            pl.BlockSpec(memory_space=pltpu.SMEM),
            pl.BlockSpec((ctx_len, gw), lambda b, g: (b, g)),
            pl.BlockSpec((ctx_len, LANES), lambda b, g: (b, g)),
            pl.BlockSpec((1, 1) + vt.shape[2:], lambda b, g: (b, g, 0, 0, 0)),
        ],
        out_specs=pl.BlockSpec((ctx_len, gw), lambda b, g: (b, g)),
        compiler_params=_cparams("parallel", "parallel"),
        name="ctx_attn",
    )(sink, q, k, vt)


def _outproj_ffn_kernel(x_ref, mod_ref, of_ref, ob_ref, r_ref, og_ref, ow_ref, gain_ref, bd_ref, w_ref,
                        g2_ref, win_ref, wout_ref, o_ref, *, f):
    o = of_ref[...] + ob_ref[...]
    r = r_ref[...].astype(F32)
    gate = r * jax.nn.sigmoid(r)
    ss = _dot((o * o).astype(BF16), bd_ref[...])
    on = o * lax.rsqrt(ss * (1.0 / GLA_DV) + EPS) * gain_ref[...]
    cat = jnp.concatenate([(on * gate).astype(BF16), og_ref[...], ow_ref[...]], axis=-1)
    x = x_ref[...] + mod_ref[0, 5:6, :] * _dot(cat, w_ref[...])
    o_ref[...] = _ffn_body(x, mod_ref, g2_ref, win_ref, wout_ref, 6, f)


def _outproj_ffn(x, mods, o_gla_f, o_gla_b, r, o_glb, o_win, gla_gain, bd, w_out, gain2, w2_in, w2_out,
                 rows_per_batch, tm_want=ROW_TILE):
    t, d = x.shape
    f = w2_out.shape[0]
    tm = _row_tile(rows_per_batch, tm_want)
    return pl.pallas_call(
        functools.partial(_outproj_ffn_kernel, f=f),
        out_shape=jax.ShapeDtypeStruct((t, d), F32),
        grid=(t // tm,),
        in_specs=[
            pl.BlockSpec((tm, d), lambda i: (i, 0)),
            _mod_spec(mods, rows_per_batch // tm),
            pl.BlockSpec((tm, GLA_V), lambda i: (i, 0)),
            pl.BlockSpec((tm, GLA_V), lambda i: (i, 0)),
            pl.BlockSpec((tm, GLA_V), lambda i: (i, 0)),
            pl.BlockSpec((tm, o_glb.shape[1]), lambda i: (i, 0)),
            pl.BlockSpec((tm, o_win.shape[1]), lambda i: (i, 0)),
            _const_spec(gla_gain.shape),
            _const_spec(bd.shape),
            _const_spec(w_out.shape),
            _const_spec((1, d)),
            _const_spec((d, 2 * f)),
            _const_spec((f, d)),
        ],
        out_specs=pl.BlockSpec((tm, d), lambda i: (i, 0)),
        compiler_params=_cparams("parallel"),
        name="mix_outproj_ffn",
    )(x, mods, o_gla_f, o_gla_b, r, o_glb, o_win, gla_gain, bd, w_out, gain2.reshape(1, d), w2_in, w2_out)


def _rope_tables(seq):
    rows = seq // GRID_W
    row = jnp.repeat(jnp.arange(rows, dtype=F32), GRID_W)
    col = (jnp.arange(rows * GRID_W) % GRID_W).astype(F32)
    n_freq = HEAD_DIM // 4
    inv = jnp.power(ROPE_BASE, -jnp.arange(n_freq, dtype=F32) / n_freq)
    ang = jnp.concatenate([row[:, None] * inv, col[:, None] * inv], axis=-1)
    cos, sin = jnp.cos(ang), jnp.sin(ang)
    cos_t = jnp.concatenate([cos, cos] * (LANES // HEAD_DIM), axis=-1)
    sin_t = jnp.concatenate([-sin, sin] * (LANES // HEAD_DIM), axis=-1)
    return cos_t, sin_t


def _block_diag_ones(n, block):
    idx = np.arange(n) // block
    return jnp.asarray(idx[:, None] == idx[None, :], dtype=BF16)


def kernel(x, c, ctx, c_ctx, mod_w, mod_b, norm_ffn1, ffn1_w_in, ffn1_w_out, norm_mix, mix_w_in, mix_w_out,
           gla_wg_f, gla_bg_f, gla_wg_b, gla_bg_b, gla_out_norm, glb_q_norm, glb_k_norm,
           win_q_norm, win_k_norm, win_sink, norm_ffn2, ffn2_w_in, ffn2_w_out):
    bsz, seq, d = x.shape
    ctx_len = ctx.shape[1]
    depth = mod_w.shape[0]
    in_splits = (GLA_QK, GLA_QK, GLA_V, GLA_V, 2 * GLA_GATE_RANK,
                 GLB_HEADS * HEAD_DIM, GLB_KV_HEADS * HEAD_DIM, GLB_KV_HEADS * HEAD_DIM,
                 WIN_HEADS * HEAD_DIM, WIN_KV_HEADS * HEAD_DIM, WIN_KV_HEADS * HEAD_DIM)

    n_rows = -(-(bsz + 1) // 8) * 8
    c_rows = jnp.concatenate([c, c_ctx[None, :], jnp.zeros((n_rows - bsz - 1, d), F32)], axis=0)
    mods = _modvec(c_rows, mod_w, mod_b)

    cos_l, sin_l = _rope_tables(seq)
    cos_c = jnp.ones((ctx_len, LANES), F32)
    sin_c = jnp.zeros((ctx_len, LANES), F32)
    bd = _block_diag_ones(SLAB, HEAD_DIM)
    zero_state = jnp.zeros((bsz, 2, GLA_V, GLA_QK), F32)

    xl = x.reshape(bsz * seq, d)
    xc = ctx.reshape(bsz * ctx_len, d)

    for l in range(depth):
        need_ctx = l < depth - 1
        mods_l = mods[l, :bsz].reshape(bsz, N_MOD, d)
        mods_c = mods[l, bsz:bsz + 1].reshape(1, N_MOD, d)
        w1_in, w1_out = ffn1_w_in[l].astype(BF16), ffn1_w_out[l].astype(BF16)
        w2_in, w2_out = ffn2_w_in[l].astype(BF16), ffn2_w_out[l].astype(BF16)
        wm = mix_w_in[l]
        src = dict(zip(("aq", "ak", "av", "ar", "ad", "gq", "gk", "gv", "wq", "wk", "wv"),
                       jnp.split(wm, np.cumsum(in_splits)[:-1], axis=1)))
        w_mix = jnp.concatenate([src[n] for n in ("aq", "ak", "av", "ar", "gq", "gk", "wk", "gv", "wv", "wq", "ad")]
                                + [jnp.zeros((d, GATE_PAD - 2 * GLA_GATE_RANK), F32)], axis=1).astype(BF16)
        assert w_mix.shape[1] == _C_END
        wg = jnp.zeros((GATE_PAD, 2 * GLA_QK), F32)
        wg = wg.at[:GLA_GATE_RANK, :GLA_QK].set(gla_wg_f[l])
        wg = wg.at[GLA_GATE_RANK:2 * GLA_GATE_RANK, GLA_QK:].set(gla_wg_b[l]).astype(BF16)
        bg = jnp.concatenate([gla_bg_f[l], gla_bg_b[l]])[None, :]
        q_fold = HEAD_DIM ** -0.5 * LOG2_E
        qk_gains = jnp.stack([jnp.tile(gn[l] * sc, LANES // HEAD_DIM)
                              for gn, sc in ((glb_q_norm, q_fold), (glb_k_norm, 1.0),
                                             (win_q_norm, q_fold), (win_k_norm, 1.0))])
        gla_gain = jnp.tile(gla_out_norm[l], GLA_HEADS)[None, :]
        w_out = mix_w_out[l].astype(BF16)
        sink = win_sink[l]

        xl = _ffn(xl, mods_l, 0, norm_ffn1[l], w1_in, w1_out, seq)
        xc = _ffn(xc, mods_c, 0, norm_ffn1[l], w1_in, w1_out, bsz * ctx_len)

        pc = _inproj(xc, mods_c, norm_mix[l], w_mix, wg, bg, qk_gains, cos_c, sin_c, bd, ctx_len)
        pq = _inproj(xl, mods_l, norm_mix[l], w_mix, wg, bg, qk_gains, cos_l, sin_l, bd, seq)
        aq, ak, av, ar, ag, gq, gk, wq, wk, gvt, wvt = pq
        aqc, akc, avc, arc, agc, gqc, gkc, wqc, wkc, gvtc, wvtc = pc

        oc_gla_f, oc_gla_b, states = _gla(aqc, akc, avc, agc, zero_state, ctx_len)
        o_gla_f, o_gla_b, _ = _gla(aq, ak, av, ag, states, seq)
        o_glb = _glb_attn(gq, gk, gvt, gkc, gvtc, seq, ctx_len)
        o_win = _win_attn(wq, wk, wvt, wkc, wvtc, sink, seq, ctx_len)
        xl = _outproj_ffn(xl, mods_l, o_gla_f, o_gla_b, ar, o_glb, o_win, gla_gain, bd, w_out, norm_ffn2[l],
                          w2_in, w2_out, seq)

        if need_ctx:
            oc_glb = _ctx_attn(gqc, gkc, gvtc, sink, GLB_KV_HEADS, ctx_len, use_sink=False)
            oc_win = _ctx_attn(wqc, wkc, wvtc, sink, WIN_KV_HEADS, ctx_len, use_sink=True)
            xc = _outproj_ffn(xc, mods_c, oc_gla_f, oc_gla_b, arc, oc_glb, oc_win, gla_gain, bd, w_out, norm_ffn2[l],
                              w2_in, w2_out, bsz * ctx_len)

    return xl.reshape(bsz, seq, d)
```
